```python
import math
import jax, jax.numpy as jnp
from jax import lax
import numpy as np

D_MODEL = 1024
BATCH = 16
SEQ = 4096
DEPTH = 2

CHUNK = 64
EPS = 1e-6
GN_EPS = 1e-5
RET_HEADS = 8
RET_DK = 64
RET_DV = 64
RET_W = RET_HEADS * RET_DV
ROPE_BASE = 10000.0
DSA_HEADS = 8
DSA_DH = 64
DSA_W = DSA_HEADS * DSA_DH
DSA_RQ = 256
DSA_RKV = 128
IDX_HEADS = 8
IDX_DIM = 32
TOPK_MAX = 256
Q_BLOCK = 128
MEM_LEN = 256
MEM_HEADS = 4
MEM_DH = 128
MEM_W = MEM_HEADS * MEM_DH
REL_BUCKETS = 32
REL_MAX_DIST = 128
D_FF = 4 * D_MODEL
N_BRANCH = 3
IN_SIZES = (RET_HEADS * RET_DK, RET_HEADS * RET_DK, RET_W, RET_W,
            DSA_RQ, DSA_RKV, IDX_DIM, IDX_HEADS, MEM_W, N_BRANCH * D_MODEL)
N_IN = sum(IN_SIZES)

kernel_name = "chunk_causal_hybrid_retention_dsa_memory"


def rmsnorm(x, g):
    xf = x.astype(jnp.float32)
    y = xf * lax.rsqrt(jnp.mean(xf * xf, axis=-1, keepdims=True) + EPS)
    return (y * g.astype(jnp.float32)).astype(x.dtype)


def rope(x, pos):
    half = x.shape[-1] // 2
    freqs = ROPE_BASE ** (-jnp.arange(half, dtype=jnp.float32) / half)
    ang = pos[:, None] * freqs[None, :]
    cos = jnp.cos(ang)[None, :, None, :]
    sin = jnp.sin(ang)[None, :, None, :]
    xf = x.astype(jnp.float32)
    x1, x2 = xf[..., :half], xf[..., half:]
    return jnp.concatenate([x1 * cos - x2 * sin, x2 * cos + x1 * sin], axis=-1)


def retention(q, k, v):
    B, S, H, dk = q.shape
    dv = v.shape[-1]
    nc = S // CHUNK
    log_g = jnp.log1p(-jnp.exp2(-5.0 - jnp.arange(H, dtype=jnp.float32)))
    i = jnp.arange(CHUNK, dtype=jnp.float32)
    intra_decay = jnp.exp(log_g[:, None, None] * jnp.abs(i[:, None] - i[None, :]))
    key_decay = jnp.exp(log_g[:, None] * (CHUNK - 1 - i)[None, :])
    query_decay = jnp.exp(log_g[:, None] * (i + 1)[None, :])
    chunk_decay = jnp.exp(log_g * CHUNK)
    qc = q.reshape(B, nc, CHUNK, H, dk)
    kc = k.reshape(B, nc, CHUNK, H, dk) * (RET_DK ** -0.5)
    vc = v.astype(jnp.float32).reshape(B, nc, CHUNK, H, dv)
    scores = jnp.einsum('bnihd,bnjhd->bnhij', qc, kc) * intra_decay
    intra = jnp.einsum('bnhij,bnjhe->bnihe', scores, vc)
    kv = jnp.einsum('bnjhd,hj,bnjhe->nbhde', kc, key_decay, vc)

    def step(state, kv_n):
        return chunk_decay[None, :, None, None] * state + kv_n, state

    _, prev = lax.scan(step, jnp.zeros((B, H, dk, dv), jnp.float32), kv)
    cross = jnp.einsum('bnihd,nbhde,hi->bnihe', qc, prev, query_decay)
    o = (intra + cross).reshape(B, S, H, dv)
    mu = jnp.mean(o, axis=-1, keepdims=True)
    var = jnp.mean(jnp.square(o - mu), axis=-1, keepdims=True)
    return (o - mu) * lax.rsqrt(var + GN_EPS)


def t5_bucket(rel):
    nb = REL_BUCKETS // 2
    max_exact = nb // 2
    base = jnp.where(rel > 0, nb, 0)
    n = jnp.abs(rel)
    nf = jnp.maximum(n, 1).astype(jnp.float32)
    large = max_exact + (jnp.log(nf / max_exact) / math.log(REL_MAX_DIST / max_exact)
                         * (nb - max_exact)).astype(jnp.int32)
    large = jnp.minimum(large, nb - 1)
    return base + jnp.where(n < max_exact, n, large)


def dsa_attention(c_q, c_kv, k_idx, w_idx, w_uq, w_iq, w_uk, w_uv, rel_bias):
    B, S, _ = c_q.shape
    k_sel = min(TOPK_MAX, S // 4)
    nblk = S // Q_BLOCK
    q = jnp.einsum('btr,rhd->bthd', c_q, w_uq)
    q_lat = jnp.einsum('bthd,rhd->bthr', q, w_uk) * (DSA_DH ** -0.5)
    q_idx = jnp.einsum('btr,rhd->bthd', c_q, w_iq)
    w = w_idx * ((IDX_HEADS ** -0.5) * (IDX_DIM ** -0.5))
    key_pos = jnp.arange(S, dtype=jnp.int32)
    k_idx_f = k_idx.astype(jnp.float32)
    c_kv_f = c_kv.astype(jnp.float32)

    def to_blocks(a):
        return jnp.moveaxis(a.reshape((B, nblk, Q_BLOCK) + a.shape[2:]), 1, 0)

    def block(args):
        ql, qi, wb, t0 = args
        t = t0 + jnp.arange(Q_BLOCK, dtype=jnp.int32)
        end = (t // CHUNK + 1) * CHUNK - 1
        admissible = key_pos[None, :] <= end[:, None]
        s_idx = jnp.einsum('bqhd,bsd->bqhs', qi.astype(jnp.float32), k_idx_f)
        score = jnp.einsum('bqhs,bqh->bqs', jax.nn.relu(s_idx), wb.astype(jnp.float32))
        score = jnp.where(admissible[None], score, -jnp.inf)
        _, sel = lax.top_k(score, k_sel)
        valid = sel <= end[None, :, None]
        c_sel = jax.vmap(lambda c, ix: c[ix])(c_kv_f, sel)
        logits = jnp.einsum('bqhr,bqkr->bqhk', ql.astype(jnp.float32), c_sel)
        bias = rel_bias.astype(jnp.float32)[t5_bucket(sel - t[None, :, None])]
        logits = logits + jnp.moveaxis(bias, -1, -2)
        logits = jnp.where(valid[:, :, None, :], logits, -jnp.inf)
        p = jax.nn.softmax(logits, axis=-1)
        return jnp.einsum('bqhk,bqkr->bqhr', p, c_sel)

    starts = jnp.arange(nblk, dtype=jnp.int32) * Q_BLOCK
    o_lat = lax.map(block, (to_blocks(q_lat), to_blocks(q_idx), to_blocks(w), starts))
    o_lat = jnp.moveaxis(o_lat, 0, 1).reshape(B, S, DSA_HEADS, DSA_RKV)
    o = jnp.einsum('bthr,rhd->bthd', o_lat, w_uv.astype(jnp.float32))
    return o.reshape(B, S, DSA_W)


def memory_attention(q, mem_n, w_mem_kv):
    B, S, _ = q.shape
    kv = mem_n @ w_mem_kv
    k, v = jnp.split(kv, 2, axis=-1)
    M = mem_n.shape[1]
    qh = q.reshape(B, S, MEM_HEADS, MEM_DH).astype(jnp.float32)
    kh = k.reshape(B, M, MEM_HEADS, MEM_DH).astype(jnp.float32)
    vh = v.reshape(B, M, MEM_HEADS, MEM_DH).astype(jnp.float32)
    logits = jnp.einsum('bthd,bmhd->bhtm', qh, kh) * (MEM_DH ** -0.5)
    p = jax.nn.softmax(logits, axis=-1)
    return jnp.einsum('bhtm,bmhd->bthd', p, vh).reshape(B, S, MEM_W)


def setup_inputs(seed: int = 0) -> dict:
    key = jax.random.key(seed)
    ks = jax.random.split(key, 24)
    f32 = jnp.float32

    def nrm(k, shape, fan_in):
        return jax.random.normal(k, shape, f32) * (fan_in ** -0.5)

    def gain(k, shape):
        return 1.0 + 0.01 * jax.random.normal(k, shape, f32)

    L = DEPTH
    return {
        "x": jax.random.normal(ks[0], (BATCH, SEQ, D_MODEL), f32),
        "mem": jax.random.normal(ks[1], (BATCH, MEM_LEN, D_MODEL), f32),
        "norm1": gain(ks[2], (L, D_MODEL)),
        "w_in": nrm(ks[3], (L, D_MODEL, N_IN), D_MODEL),
        "q_norm": gain(ks[4], (L, DSA_RQ)),
        "kv_norm": gain(ks[5], (L, DSA_RKV)),
        "w_uq": nrm(ks[6], (L, DSA_RQ, DSA_HEADS, DSA_DH), DSA_RQ),
        "w_iq": nrm(ks[7], (L, DSA_RQ, IDX_HEADS, IDX_DIM), DSA_RQ),
        "w_uk": nrm(ks[8], (L, DSA_RKV, DSA_HEADS, DSA_DH), DSA_RKV),
        "w_uv": nrm(ks[9], (L, DSA_RKV, DSA_HEADS, DSA_DH), DSA_RKV),
        "mem_norm": gain(ks[10], (L, D_MODEL)),
        "w_mem_kv": nrm(ks[11], (L, D_MODEL, 2 * MEM_W), D_MODEL),
        "w_ret_o": nrm(ks[12], (L, RET_W, D_MODEL), RET_W),
        "w_dsa_o": nrm(ks[13], (L, DSA_W, D_MODEL), DSA_W),
        "w_mem_o": nrm(ks[14], (L, MEM_W, D_MODEL), MEM_W),
        "w_out": nrm(ks[15], (L, D_MODEL, D_MODEL), D_MODEL),
        "norm2": gain(ks[16], (L, D_MODEL)),
        "w_ff1": nrm(ks[17], (L, D_MODEL, D_FF), D_MODEL),
        "w_ff2": nrm(ks[18], (L, D_FF, D_MODEL), D_FF),
        "rel_bias": 0.5 * jax.random.normal(ks[19], (REL_BUCKETS, DSA_HEADS), f32),
        "final_norm": gain(ks[20], (D_MODEL,)),
    }


def reference(x, mem, norm1, w_in, q_norm, kv_norm, w_uq, w_iq, w_uk, w_uv, mem_norm, w_mem_kv,
              w_ret_o, w_dsa_o, w_mem_o, w_out, norm2, w_ff1, w_ff2, rel_bias, final_norm):
    B, S, D = x.shape
    pos = jnp.arange(S, dtype=jnp.float32)
    split_at = list(np.cumsum(IN_SIZES)[:-1])
    for l in range(DEPTH):
        h = rmsnorm(x, norm1[l])
        z = h @ w_in[l]
        (r_q, r_k, r_v, r_g, c_q, c_kv, i_k, i_w, m_q, gates) = jnp.split(z, split_at, axis=-1)
        rq = rope(r_q.reshape(B, S, RET_HEADS, RET_DK), pos)
        rk = rope(r_k.reshape(B, S, RET_HEADS, RET_DK), pos)
        ret = retention(rq, rk, r_v.reshape(B, S, RET_HEADS, RET_DV)).reshape(B, S, RET_W)
        ret = (jax.nn.silu(r_g.astype(jnp.float32)) * ret).astype(x.dtype)
        ret_b = ret @ w_ret_o[l]
        dsa = dsa_attention(rmsnorm(c_q, q_norm[l]), rmsnorm(c_kv, kv_norm[l]), i_k, i_w,
                            w_uq[l], w_iq[l], w_uk[l], w_uv[l], rel_bias)
        dsa_b = dsa.astype(x.dtype) @ w_dsa_o[l]
        memo = memory_attention(m_q, rmsnorm(mem, mem_norm[l]), w_mem_kv[l])
        mem_b = memo.astype(x.dtype) @ w_mem_o[l]
        g = jax.nn.sigmoid(gates.astype(jnp.float32)).reshape(B, S, N_BRANCH, D)
        merged = (g[:, :, 0] * ret_b + g[:, :, 1] * dsa_b + g[:, :, 2] * mem_b).astype(x.dtype)
        x = x + merged @ w_out[l]
        h2 = rmsnorm(x, norm2[l])
        x = x + jnp.square(jax.nn.relu(h2 @ w_ff1[l])) @ w_ff2[l]
    return rmsnorm(x, final_norm)
```

```python
import functools
import math

import numpy as np
import jax
import jax.numpy as jnp
from jax import lax
from jax.experimental import pallas as pl
from jax.experimental.pallas import tpu as pltpu

F32 = jnp.float32
BF16 = jnp.bfloat16

D_MODEL = 1024
CHUNK = 64
EPS = 1e-6
GN_EPS = 1e-5
RET_HEADS = 8
RET_DK = 64
RET_W = 512
ROPE_BASE = 10000.0
DSA_HEADS = 8
DSA_DH = 64
DSA_RQ = 256
DSA_RKV = 128
IDX_HEADS = 8
IDX_DIM = 32
TOPK_MAX = 256
MEM_LEN = 256
MEM_HEADS = 4
MEM_DH = 128
MEM_W = 512
REL_BUCKETS = 32
REL_MAX_DIST = 128
D_FF = 4096
IN_SIZES = (512, 512, 512, 512, DSA_RQ, DSA_RKV, IDX_DIM, IDX_HEADS, MEM_W, 3 * D_MODEL)

LANES = 128
IDX_COLS = 640
NEG = -1e30
F32_MAX = float(np.finfo(np.float32).max)
VMEM_LIMIT = 56 * 1024 * 1024

TM_IN = 256
T_RET = 256
TQ = 256
TK = 256
TM_MERGE = 256
TM_MLP = 512


def _dot(a, b):
    return jnp.dot(a, b, preferred_element_type=F32)


def _dot_nt(a, b):
    return lax.dot_general(a, b, (((1,), (1,)), ((), ())), preferred_element_type=F32)


def _split(a):
    hi = a.astype(BF16)
    lo = (a - hi.astype(F32)).astype(BF16)
    return hi, lo


def _rms(x, g):
    return x * lax.rsqrt(jnp.mean(x * x, axis=-1, keepdims=True) + EPS) * g


def _params(sem):
    return pltpu.CompilerParams(dimension_semantics=sem, vmem_limit_bytes=VMEM_LIMIT)


def _fold_kernel(uq_ref, uk_ref, uv_ref, wo_ref, wql_ref, wdsa_ref):
    uq_h, uq_l = _split(uq_ref[0, 0])
    uk_h, uk_l = _split(uk_ref[0, 0])
    ql = _dot_nt(uq_h, uk_h) + _dot_nt(uq_h, uk_l) + _dot_nt(uq_l, uk_h)
    wql_ref[0] = (ql * (DSA_DH ** -0.5)).astype(BF16)
    uv_h, uv_l = _split(uv_ref[0, 0])
    wo_h, wo_l = _split(wo_ref[0, 0])
    wdsa_ref[0] = (_dot(uv_h, wo_h) + _dot(uv_h, wo_l) + _dot(uv_l, wo_h)).astype(BF16)


def _fold_weights(w_uq, w_uk, w_uv, w_dsa_o):
    L = w_uq.shape[0]
    uq = jnp.transpose(w_uq, (0, 2, 1, 3))
    uk = jnp.transpose(w_uk, (0, 2, 1, 3))
    uv = jnp.transpose(w_uv, (0, 2, 1, 3))
    wo = w_dsa_o.reshape(L, DSA_HEADS, DSA_DH, D_MODEL)
    return pl.pallas_call(
        _fold_kernel,
        grid=(L, DSA_HEADS),
        in_specs=[
            pl.BlockSpec((1, 1, DSA_RQ, DSA_DH), lambda l, h: (l, h, 0, 0)),
            pl.BlockSpec((1, 1, DSA_RKV, DSA_DH), lambda l, h: (l, h, 0, 0)),
            pl.BlockSpec((1, 1, DSA_RKV, DSA_DH), lambda l, h: (l, h, 0, 0)),
            pl.BlockSpec((1, 1, DSA_DH, D_MODEL), lambda l, h: (l, h, 0, 0)),
        ],
        out_specs=[
            pl.BlockSpec((1, DSA_RQ, DSA_RKV), lambda l, h: (l, 0, h)),
            pl.BlockSpec((1, DSA_RKV, D_MODEL), lambda l, h: (l, h, 0)),
        ],
        out_shape=[
            jax.ShapeDtypeStruct((L, DSA_RQ, DSA_HEADS * DSA_RKV), BF16),
            jax.ShapeDtypeStruct((L, DSA_HEADS * DSA_RKV, D_MODEL), BF16),
        ],
        compiler_params=_params(("parallel", "parallel")),
        name="fold_weights",
    )(uq, uk, uv, wo)


def _in_kernel(x_ref, g_ref, cos_ref, sin_ref, wa_ref, wih_ref, wil_ref,
               rq_ref, rk_ref, rv_ref, rg_ref, mq_ref, gates_ref, idx_ref):
    h = _rms(x_ref[...], g_ref[...])
    hb, hl = _split(h)
    c = cos_ref[...]
    s = sin_ref[...]
    half = RET_W // 2

    def rope_store(ref, z, scale):
        x1 = z[:, :half]
        x2 = z[:, half:]
        ref[:, :half] = ((x1 * c - x2 * s) * scale).astype(BF16)
        ref[:, half:] = ((x2 * c + x1 * s) * scale).astype(BF16)

    rope_store(rq_ref, _dot(hb, wa_ref[:, 0:512]), 1.0)
    rope_store(rk_ref, _dot(hb, wa_ref[:, 512:1024]), RET_DK ** -0.5)
    rv_ref[...] = _dot(hb, wa_ref[:, 1024:1536]).astype(BF16)
    rg_ref[...] = _dot(hb, wa_ref[:, 1536:2048]).astype(BF16)
    mq_ref[...] = _dot(hb, wa_ref[:, 2048:2560]).astype(BF16)
    for j in range(6):
        gates_ref[:, j * 512:(j + 1) * 512] = _dot(hb, wa_ref[:, 2560 + j * 512:3072 + j * 512]).astype(BF16)
    idx_ref[...] = _dot(hb, wih_ref[...]) + _dot(hl, wih_ref[...]) + _dot(hb, wil_ref[...])


def _in_proj(x2d, g, cos_t, sin_t, wa, wih, wil, S):
    N = x2d.shape[0]
    n_pos = S // TM_IN
    row = lambda i: (i, 0)
    const = lambda i: (0, 0)
    pos = lambda i: (i % n_pos, 0)
    bf = lambda w: jax.ShapeDtypeStruct((N, w), BF16)
    return pl.pallas_call(
        _in_kernel,
        grid=(N // TM_IN,),
        in_specs=[
            pl.BlockSpec((TM_IN, D_MODEL), row),
            pl.BlockSpec((1, D_MODEL), const),
            pl.BlockSpec((TM_IN, RET_W // 2), pos),
            pl.BlockSpec((TM_IN, RET_W // 2), pos),
            pl.BlockSpec(wa.shape, const),
            pl.BlockSpec(wih.shape, const),
            pl.BlockSpec(wil.shape, const),
        ],
        out_specs=[
            pl.BlockSpec((TM_IN, 512), row), pl.BlockSpec((TM_IN, 512), row),
            pl.BlockSpec((TM_IN, 512), row), pl.BlockSpec((TM_IN, 512), row),
            pl.BlockSpec((TM_IN, 512), row), pl.BlockSpec((TM_IN, 3 * D_MODEL), row),
            pl.BlockSpec((TM_IN, IDX_COLS), row),
        ],
        out_shape=[bf(512), bf(512), bf(512), bf(512), bf(512), bf(3 * D_MODEL),
                   jax.ShapeDtypeStruct((N, IDX_COLS), F32)],
        compiler_params=_params(("parallel",)),
        name="in_proj",
    )(x2d, g, cos_t, sin_t, wa, wih, wil)


def _ret_kernel(rq_ref, rk_ref, rv_ref, rg_ref, dmat_ref, qd_ref, kd_ref, hq_ref, hv_ref,
                cdec_ref, bd_ref, p_ref, o_ref, s_ref):
    @pl.when(pl.program_id(1) == 0)
    def _():
        s_ref[...] = jnp.zeros_like(s_ref)

    q = rq_ref[...]
    k = rk_ref[...]
    v = rv_ref[...]
    o = jnp.zeros(o_ref.shape, F32)
    for h in range(RET_HEADS):
        sc = _dot_nt(q * hq_ref[h], k) * dmat_ref[h]
        o = o + _dot(sc.astype(BF16), v) * hv_ref[h]
    state = s_ref[...]
    qf = q.astype(F32) * qd_ref[...]
    o = o + _dot(qf.astype(BF16), state.astype(BF16))
    kf = k.astype(F32) * kd_ref[...]
    kv = _dot(kf.T.astype(BF16), v)
    s_ref[...] = state * cdec_ref[...] + kv * bd_ref[...]

    p = p_ref[...]

    def group_mean(a):
        ah, al = _split(a)
        return _dot(ah, p) + _dot(al, p)

    d = o - group_mean(o)
    y = d * lax.rsqrt(group_mean(d * d) + GN_EPS)
    g = rg_ref[...].astype(F32)
    o_ref[...] = (g * jax.nn.sigmoid(g) * y).astype(BF16)


def _ret_consts():
    T = T_RET
    hh = np.arange(RET_HEADS, dtype=np.float64)
    log_g = np.log1p(-np.exp2(-5.0 - hh))
    t = np.arange(T)
    ct = t // CHUNK
    diff = (t[:, None] - t[None, :]).astype(np.float64)
    same = ct[:, None] == ct[None, :]
    past = ct[None, :] < ct[:, None]
    expo = np.where(same, np.abs(diff), diff)
    dmat = np.where((same | past)[None], np.exp(log_g[:, None, None] * expo[None]), 0.0)
    lane = np.arange(RET_W)
    hk = (lane % (RET_W // 2)) // (RET_DK // 2)
    hv = lane // RET_DK
    qd = np.exp(log_g[hk][None, :] * (t[:, None] + 1.0))
    kd = np.exp(log_g[hk][None, :] * (T - 1.0 - t[:, None]))
    hq_mask = (hk[None, :] == np.arange(RET_HEADS)[:, None]).astype(np.float32)[:, None, :]
    hv_mask = (hv[None, :] == np.arange(RET_HEADS)[:, None]).astype(np.float32)[:, None, :]
    bd = (hk[:, None] == hv[None, :]).astype(np.float32)
    cdec = np.broadcast_to(np.exp(log_g[hk] * T)[:, None], (RET_W, RET_W))
    pmat = (hv[:, None] == hv[None, :]).astype(np.float32) / RET_DK
    f = lambda a: jnp.asarray(np.asarray(a, dtype=np.float32))
    return (f(dmat), f(qd), f(kd), jnp.asarray(hq_mask, dtype=BF16), f(hv_mask), f(cdec), f(bd),
            jnp.asarray(pmat, dtype=BF16))


def _retention(rq, rk, rv, rg, B, S):
    consts = _ret_consts()
    nb = S // T_RET
    tok = lambda b, j: (b * nb + j, 0)
    full = lambda a: pl.BlockSpec(a.shape, lambda b, j: (0,) * a.ndim)
    return pl.pallas_call(
        _ret_kernel,
        grid=(B, nb),
        in_specs=[pl.BlockSpec((T_RET, RET_W), tok)] * 4 + [full(a) for a in consts],
        out_specs=pl.BlockSpec((T_RET, RET_W), tok),
        out_shape=jax.ShapeDtypeStruct(rq.shape, BF16),
        scratch_shapes=[pltpu.VMEM((RET_W, RET_W), F32)],
        compiler_params=_params(("parallel", "arbitrary")),
        name="retention",
    )(rq, rk, rv, rg, *consts)


def _memkv_kernel(mem_ref, g_ref, w_ref, k_ref, v_ref):
    mn = _rms(mem_ref[0], g_ref[...]).astype(BF16)
    kv = _dot(mn, w_ref[...])
    k_ref[0] = kv[:, :MEM_W].astype(BF16)
    v_ref[0] = kv[:, MEM_W:].astype(BF16)


def _mem_kv(mem, g, w):
    B = mem.shape[0]
    return pl.pallas_call(
        _memkv_kernel,
        grid=(B,),
        in_specs=[pl.BlockSpec((1, MEM_LEN, D_MODEL), lambda b: (b, 0, 0)),
                  pl.BlockSpec((1, D_MODEL), lambda b: (0, 0)),
                  pl.BlockSpec(w.shape, lambda b: (0, 0))],
        out_specs=[pl.BlockSpec((1, MEM_LEN, MEM_W), lambda b: (b, 0, 0))] * 2,
        out_shape=[jax.ShapeDtypeStruct((B, MEM_LEN, MEM_W), BF16)] * 2,
        compiler_params=_params(("parallel",)),
        name="mem_kv",
    )(mem, g, w)


def _prep_kernel(idx_ref, qn_ref, kvn_ref, wql_ref, wiqh_ref, wiql_ref,
                 qlat_ref, qidx_ref, w_ref, ckvn_ref, kidx_ref):
    z = idx_ref[...]
    cqn = _rms(z[:, :DSA_RQ], qn_ref[...])
    cb, cl = _split(cqn)
    ql = _dot(cb, wql_ref[...])
    y = _dot(cb, wiqh_ref[...]) + _dot(cl, wiqh_ref[...]) + _dot(cb, wiql_ref[...])
    yh = y.astype(BF16).astype(F32)
    lane = lax.broadcasted_iota(jnp.int32, y.shape, 1)
    qsel = jnp.where((lane % LANES) < 2 * IDX_DIM, yh, y - yh).astype(BF16)
    for h in range(DSA_HEADS):
        qlat_ref[0, h] = ql[:, h * LANES:(h + 1) * LANES].astype(BF16)
        qidx_ref[0, h] = qsel[:, h * LANES:(h + 1) * LANES]
    ckvn_ref[...] = _rms(z[:, DSA_RQ:DSA_RQ + DSA_RKV], kvn_ref[...]).astype(BF16)
    ik = z[:, 384:512]
    ikh = ik.astype(BF16).astype(F32)
    lane1 = lax.broadcasted_iota(jnp.int32, ik.shape, 1)
    kidx_ref[...] = jnp.where((lane1 // IDX_DIM) % 2 == 0, ikh, ik - ikh).astype(BF16)
    w_ref[...] = z[:, 512:640] * ((IDX_HEADS ** -0.5) * (IDX_DIM ** -0.5))


def _dsa_prep(idx, qn, kvn, wql, wiqh, wiql):
    N = idx.shape[0]
    nt = N // TQ
    row = lambda i: (i, 0)
    const = lambda i: (0, 0)
    stacked = pl.BlockSpec((1, DSA_HEADS, TQ, LANES), lambda i: (i, 0, 0, 0))
    return pl.pallas_call(
        _prep_kernel,
        grid=(nt,),
        in_specs=[pl.BlockSpec((TQ, IDX_COLS), row),
                  pl.BlockSpec((1, DSA_RQ), const), pl.BlockSpec((1, DSA_RKV), const),
                  pl.BlockSpec(wql.shape, const), pl.BlockSpec(wiqh.shape, const),
                  pl.BlockSpec(wiql.shape, const)],
        out_specs=[stacked, stacked, pl.BlockSpec((TQ, LANES), row),
                   pl.BlockSpec((TQ, LANES), row), pl.BlockSpec((TQ, LANES), row)],
        out_shape=[jax.ShapeDtypeStruct((nt, DSA_HEADS, TQ, LANES), BF16),
                   jax.ShapeDtypeStruct((nt, DSA_HEADS, TQ, LANES), BF16),
                   jax.ShapeDtypeStruct((N, LANES), F32),
                   jax.ShapeDtypeStruct((N, LANES), BF16),
                   jax.ShapeDtypeStruct((N, LANES), BF16)],
        compiler_params=_params(("parallel",)),
        name="dsa_prep",
    )(idx, qn, kvn, wql, wiqh, wiql)


def _key_to_f32(key):
    bits = jnp.where(key >= 0, key, key ^ jnp.int32(0x7FFFFFFF))
    return lax.bitcast_convert_type(bits, F32)


def _dsa_kernel(qlat_ref, qidx_ref, w_ref, ckvn_ref, kidx_ref, bnear_ref, bfar_ref, o_ref,
                sc_ref, m_ref, l_ref, acc_ref, *, k_sel):
    i = pl.program_id(1)
    wv = w_ref[...]

    def score_tile(j):
        kt = kidx_ref[pl.ds(pl.multiple_of(j * TK, TK), TK), :]
        tot = jnp.zeros((TQ, TK), F32)
        for h in range(IDX_HEADS):
            tot = tot + jnp.maximum(_dot_nt(qidx_ref[0, h], kt), 0.0) * wv[:, h:h + 1]
        return tot

    def fill(j, carry):
        sc_ref[j] = score_tile(j)
        return carry

    lax.fori_loop(0, i, fill, 0)
    row = lax.broadcasted_iota(jnp.int32, (TQ, TK), 0)
    col = lax.broadcasted_iota(jnp.int32, (TQ, TK), 1)
    admissible = col <= (row // CHUNK) * CHUNK + (CHUNK - 1)
    sc_ref[i] = jnp.where(admissible, score_tile(i), -jnp.inf)

    def count_ge(thr):
        def body(j, c):
            return c + jnp.sum(jnp.where(sc_ref[j] >= thr, 1.0, 0.0), axis=1, keepdims=True)
        return lax.fori_loop(0, i + 1, body, jnp.zeros((TQ, 1), F32))

    rowc = lax.broadcasted_iota(jnp.int32, (TQ, 1), 0)
    n_adm = (i * TQ + (rowc // CHUNK + 1) * CHUNK).astype(F32)
    kf = jnp.float32(k_sel)
    lo0 = jnp.full((TQ, 1), np.int32(np.array(-F32_MAX, np.float32).view(np.int32)) ^ np.int32(0x7FFFFFFF), jnp.int32)
    hi0 = jnp.full((TQ, 1), np.int32(np.array(np.inf, np.float32).view(np.int32)), jnp.int32)

    def not_done(lo, hi, cnt_lo):
        mid = (lo & hi) + ((lo ^ hi) >> 1)
        return jnp.where((cnt_lo == kf) | (mid == lo) | (n_adm <= kf), 0.0, 1.0)

    def cond(st):
        it, lo, hi, cnt_lo = st
        return jnp.logical_and(it < 34, jnp.sum(not_done(lo, hi, cnt_lo)) > 0.0)

    def step(st):
        it, lo, hi, cnt_lo = st
        mid = (lo & hi) + ((lo ^ hi) >> 1)
        c = count_ge(_key_to_f32(mid))
        ge = c >= kf
        return it + 1, jnp.where(ge, mid, lo), jnp.where(ge, hi, mid), jnp.where(ge, c, cnt_lo)

    _, lo, _, _ = lax.while_loop(cond, step, (jnp.int32(0), lo0, hi0, n_adm))
    thr = _key_to_f32(lo)

    m_ref[...] = jnp.full(m_ref.shape, NEG, F32)
    l_ref[...] = jnp.zeros(l_ref.shape, F32)
    acc_ref[...] = jnp.zeros(acc_ref.shape, F32)

    def attend(j, bias_of):
        ck = ckvn_ref[pl.ds(pl.multiple_of(j * TK, TK), TK), :]
        sel = sc_ref[j] >= thr
        for h in range(DSA_HEADS):
            lg = jnp.where(sel, _dot_nt(qlat_ref[0, h], ck) + bias_of(h), NEG)
            m_old = m_ref[h]
            m_new = jnp.maximum(m_old, jnp.max(lg, axis=1, keepdims=True))
            alpha = jnp.exp(m_old - m_new)
            p = jnp.exp(lg - m_new)
            l_ref[h] = alpha * l_ref[h] + jnp.sum(p, axis=1, keepdims=True)
            acc_ref[h] = alpha * acc_ref[h] + _dot(p.astype(BF16), ck)
            m_ref[h] = m_new

    def far(j, carry):
        attend(j, lambda h: bfar_ref[h:h + 1, :])
        return carry

    lax.fori_loop(0, jnp.maximum(i - 1, 0), far, 0)

    @pl.when(i >= 1)
    def _():
        attend(i - 1, lambda h: bnear_ref[h, 0])

    attend(i, lambda h: bnear_ref[h, 1])
    for h in range(DSA_HEADS):
        o_ref[:, h * LANES:(h + 1) * LANES] = (acc_ref[h] / l_ref[h]).astype(BF16)


def _t5_bucket(rel):
    nb = REL_BUCKETS // 2
    max_exact = nb // 2
    base = jnp.where(rel > 0, nb, 0)
    n = jnp.abs(rel)
    nf = jnp.maximum(n, 1).astype(jnp.float32)
    large = max_exact + (jnp.log(nf / max_exact) / math.log(REL_MAX_DIST / max_exact)
                         * (nb - max_exact)).astype(jnp.int32)
    large = jnp.minimum(large, nb - 1)
    return base + jnp.where(n < max_exact, n, large)


def _bias_tables(rel_bias):
    t = jnp.arange(TQ, dtype=jnp.int32)[:, None]
    s = jnp.arange(TK, dtype=jnp.int32)[None, :]
    rel = jnp.stack([s - TK - t, s - t])
    near = jnp.transpose(rel_bias.astype(F32)[_t5_bucket(rel)], (3, 0, 1, 2))
    far_row = rel_bias.astype(F32)[_t5_bucket(jnp.full((1,), -REL_MAX_DIST, jnp.int32))[0]]
    far = jnp.broadcast_to(far_row[:, None], (DSA_HEADS, TK))
    return near, far


def _dsa_attention(qlat, qidx, w, ckvn, kidx, bnear, bfar, B, S):
    nq = S // TQ
    N = B * S
    k_sel = min(TOPK_MAX, S // 4)
    stacked = pl.BlockSpec((1, DSA_HEADS, TQ, LANES), lambda b, i: (b * nq + i, 0, 0, 0))
    keys = pl.BlockSpec((S, LANES), lambda b, i: (b, 0))
    return pl.pallas_call(
        functools.partial(_dsa_kernel, k_sel=k_sel),
        grid=(B, nq),
        in_specs=[stacked, stacked, pl.BlockSpec((TQ, LANES), lambda b, i: (b * nq + i, 0)),
                  keys, keys,
                  pl.BlockSpec(bnear.shape, lambda b, i: (0, 0, 0, 0)),
                  pl.BlockSpec(bfar.shape, lambda b, i: (0, 0))],
        out_specs=pl.BlockSpec((TQ, DSA_HEADS * DSA_RKV), lambda b, i: (b * nq + i, 0)),
        out_shape=jax.ShapeDtypeStruct((N, DSA_HEADS * DSA_RKV), BF16),
        scratch_shapes=[pltpu.VMEM((nq, TQ, TK), F32),
                        pltpu.VMEM((DSA_HEADS, TQ, 1), F32),
                        pltpu.VMEM((DSA_HEADS, TQ, 1), F32),
                        pltpu.VMEM((DSA_HEADS, TQ, DSA_RKV), F32)],
        compiler_params=_params(("parallel", "arbitrary")),
        name="dsa_attention",
    )(qlat, qidx, w, ckvn, kidx, bnear, bfar)


def _merge_kernel(x_ref, ret_ref, olat_ref, mq_ref, gates_ref, km_ref, vm_ref,
                  wret_ref, wdsa_ref, wmem_ref, wout_ref, o_ref):
    mq = mq_ref[...]
    km = km_ref[0]
    vm = vm_ref[0]
    mem_b = jnp.zeros(o_ref.shape, F32)
    for h in range(MEM_HEADS):
        sl = slice(h * MEM_DH, (h + 1) * MEM_DH)
        lg = _dot_nt(mq[:, sl], km[:, sl]) * (MEM_DH ** -0.5)
        p = jnp.exp(lg - jnp.max(lg, axis=1, keepdims=True))
        p = p / jnp.sum(p, axis=1, keepdims=True)
        pv = _dot(p.astype(BF16), vm[:, sl])
        mem_b = mem_b + _dot(pv.astype(BF16), wmem_ref[sl, :])
    ret_b = _dot(ret_ref[...], wret_ref[...])
    dsa_b = _dot(olat_ref[...], wdsa_ref[...])
    g = jax.nn.sigmoid(gates_ref[...].astype(F32))
    merged = (g[:, :D_MODEL] * ret_b + g[:, D_MODEL:2 * D_MODEL] * dsa_b + g[:, 2 * D_MODEL:] * mem_b)
    o_ref[...] = x_ref[...] + _dot(merged.astype(BF16), wout_ref[...])


def _merge(x2d, ret, olat, mq, gates, km, vm, wret, wdsa, wmem, wout, S):
    N = x2d.shape[0]
    per_b = S // TM_MERGE
    row = lambda i: (i, 0)
    const = lambda i: (0, 0)
    memb = lambda i: (i // per_b, 0, 0)
    return pl.pallas_call(
        _merge_kernel,
        grid=(N // TM_MERGE,),
        in_specs=[pl.BlockSpec((TM_MERGE, D_MODEL), row), pl.BlockSpec((TM_MERGE, RET_W), row),
                  pl.BlockSpec((TM_MERGE, DSA_HEADS * DSA_RKV), row), pl.BlockSpec((TM_MERGE, MEM_W), row),
                  pl.BlockSpec((TM_MERGE, 3 * D_MODEL), row),
                  pl.BlockSpec((1, MEM_LEN, MEM_W), memb), pl.BlockSpec((1, MEM_LEN, MEM_W), memb),
                  pl.BlockSpec(wret.shape, const), pl.BlockSpec(wdsa.shape, const),
                  pl.BlockSpec(wmem.shape, const), pl.BlockSpec(wout.shape, const)],
        out_specs=pl.BlockSpec((TM_MERGE, D_MODEL), row),
        out_shape=jax.ShapeDtypeStruct(x2d.shape, F32),
        compiler_params=_params(("parallel",)),
        name="merge_out",
    )(x2d, ret, olat, mq, gates, km, vm, wret, wdsa, wmem, wout)


def _mlp_kernel(x_ref, g_ref, w1_ref, w2_ref, gf_ref, o_ref, *, final):
    x = x_ref[...]
    h = _rms(x, g_ref[...]).astype(BF16)
    acc = jnp.zeros(x.shape, F32)
    for c in range(D_FF // D_MODEL):
        sl = slice(c * D_MODEL, (c + 1) * D_MODEL)
        a = jnp.maximum(_dot(h, w1_ref[:, sl]), 0.0)
        acc = acc + _dot((a * a).astype(BF16), w2_ref[sl, :])
    y = x + acc
    if final:
        y = _rms(y, gf_ref[...])
    o_ref[...] = y


def _mlp(x2d, g, w1, w2, gf, final):
    N = x2d.shape[0]
    row = lambda i: (i, 0)
    const = lambda i: (0, 0)
    return pl.pallas_call(
        functools.partial(_mlp_kernel, final=final),
        grid=(N // TM_MLP,),
        in_specs=[pl.BlockSpec((TM_MLP, D_MODEL), row), pl.BlockSpec((1, D_MODEL), const),
                  pl.BlockSpec(w1.shape, const), pl.BlockSpec(w2.shape, const),
                  pl.BlockSpec((1, D_MODEL), const)],
        out_specs=pl.BlockSpec((TM_MLP, D_MODEL), row),
        out_shape=jax.ShapeDtypeStruct(x2d.shape, F32),
        compiler_params=_params(("parallel",)),
        name="mlp",
    )(x2d, g, w1, w2, gf)


def _rope_tables(S):
    half = RET_DK // 2
    pos = jnp.arange(S, dtype=F32)
    freqs = ROPE_BASE ** (-jnp.arange(half, dtype=F32) / half)
    ang = pos[:, None] * freqs[None, :]
    return jnp.tile(jnp.cos(ang), (1, RET_HEADS)), jnp.tile(jnp.sin(ang), (1, RET_HEADS))


def _in_weights(wi, w_iq_l):
    half = RET_DK // 2
    perm = (np.arange(RET_HEADS)[None, :, None] * RET_DK + np.arange(2)[:, None, None] * half
            + np.arange(half)[None, None, :]).reshape(-1)
    offs = np.concatenate([[0], np.cumsum(IN_SIZES)])
    seg = lambda k: wi[:, int(offs[k]):int(offs[k + 1])]
    wa = jnp.concatenate([seg(0)[:, perm], seg(1)[:, perm], seg(2), seg(3), seg(8), seg(9)], axis=1).astype(BF16)
    pad = jnp.zeros((D_MODEL, LANES - IDX_HEADS), F32)
    widx = jnp.concatenate([seg(4), seg(5), jnp.tile(seg(6), (1, 4)), seg(7), pad], axis=1)
    wih, wil = _split(widx)
    wiq = jnp.tile(w_iq_l[:, :, None, :], (1, 1, 4, 1)).reshape(DSA_RQ, IDX_HEADS * LANES)
    wiqh, wiql = _split(wiq)
    return wa, wih, wil, wiqh, wiql


def kernel(x, mem, norm1, w_in, q_norm, kv_norm, w_uq, w_iq, w_uk, w_uv, mem_norm, w_mem_kv,
           w_ret_o, w_dsa_o, w_mem_o, w_out, norm2, w_ff1, w_ff2, rel_bias, final_norm):
    B, S, D = x.shape
    depth = w_in.shape[0]
    assert D == D_MODEL and S % TM_MLP == 0 and S % TQ == 0 and TQ == TK
    cos_t, sin_t = _rope_tables(S)
    bnear, bfar = _bias_tables(rel_bias)
    wql_all, wdsa_all = _fold_weights(w_uq, w_uk, w_uv, w_dsa_o)
    x2d = x.reshape(B * S, D)
    row = lambda v: v.reshape(1, -1)
    for l in range(depth):
        wa, wih, wil, wiqh, wiql = _in_weights(w_in[l], w_iq[l])
        rq, rk, rv, rg, mq, gates, idx = _in_proj(x2d, row(norm1[l]), cos_t, sin_t, wa, wih, wil, S)
        ret = _retention(rq, rk, rv, rg, B, S)
        qlat, qidx, wsc, ckvn, kidx = _dsa_prep(idx, row(q_norm[l]), row(kv_norm[l]), wql_all[l], wiqh, wiql)
        olat = _dsa_attention(qlat, qidx, wsc, ckvn, kidx, bnear, bfar, B, S)
        km, vm = _mem_kv(mem, row(mem_norm[l]), w_mem_kv[l].astype(BF16))
        x2d = _merge(x2d, ret, olat, mq, gates, km, vm, w_ret_o[l].astype(BF16), wdsa_all[l],
                     w_mem_o[l].astype(BF16), w_out[l].astype(BF16), S)
        x2d = _mlp(x2d, row(norm2[l]), w_ff1[l].astype(BF16), w_ff2[l].astype(BF16), row(final_norm),
                   final=(l == depth - 1))
    return x2d.reshape(B, S, D)
```

```python
import functools
import math

import numpy as np
import jax
import jax.numpy as jnp
from jax import lax
from jax.experimental import pallas as pl
from jax.experimental.pallas import tpu as pltpu

F32 = jnp.float32
BF16 = jnp.bfloat16

D_MODEL = 1024
CHUNK = 64
EPS = 1e-6
GN_EPS = 1e-5
RET_HEADS = 8
RET_DK = 64
RET_W = 512
ROPE_BASE = 10000.0
DSA_HEADS = 8
DSA_DH = 64
DSA_RQ = 256
DSA_RKV = 128
IDX_HEADS = 8
IDX_DIM = 32
TOPK_MAX = 256
MEM_LEN = 256
MEM_HEADS = 4
MEM_DH = 128
MEM_W = 512
REL_BUCKETS = 32
REL_MAX_DIST = 128
D_FF = 4096
IN_SIZES = (512, 512, 512, 512, DSA_RQ, DSA_RKV, IDX_DIM, IDX_HEADS, MEM_W, 3 * D_MODEL)

LANES = 128
IDX_COLS = 640
NEG = -1e30
LOG2E = math.log2(math.e)
F32_MAX = float(np.finfo(np.float32).max)
VMEM_LIMIT = 56 * 1024 * 1024

TM_IN = 256
T_RET = 256
TQ = 256
TK = 256
TM_MERGE = 256
TM_MLP = 512


def _dot(a, b):
    return jnp.dot(a, b, preferred_element_type=F32)


def _dot_nt(a, b):
    return lax.dot_general(a, b, (((1,), (1,)), ((), ())), preferred_element_type=F32)


def _split(a):
    hi = a.astype(BF16)
    lo = (a - hi.astype(F32)).astype(BF16)
    return hi, lo


def _rms(x, g):
    return x * lax.rsqrt(jnp.mean(x * x, axis=-1, keepdims=True) + EPS) * g


def _params(sem):
    return pltpu.CompilerParams(dimension_semantics=sem, vmem_limit_bytes=VMEM_LIMIT)


def _fold_kernel(uq_ref, uk_ref, uv_ref, wo_ref, wql_ref, wdsa_ref):
    uq_h, uq_l = _split(uq_ref[0, 0])
    uk_h, uk_l = _split(uk_ref[0, 0])
    ql = _dot_nt(uq_h, uk_h) + _dot_nt(uq_h, uk_l) + _dot_nt(uq_l, uk_h)
    wql_ref[0] = (ql * (DSA_DH ** -0.5 * LOG2E)).astype(BF16)
    uv_h, uv_l = _split(uv_ref[0, 0])
    wo_h, wo_l = _split(wo_ref[0, 0])
    wdsa_ref[0] = (_dot(uv_h, wo_h) + _dot(uv_h, wo_l) + _dot(uv_l, wo_h)).astype(BF16)


def _fold_weights(w_uq, w_uk, w_uv, w_dsa_o):
    L = w_uq.shape[0]
    uq = jnp.transpose(w_uq, (0, 2, 1, 3))
    uk = jnp.transpose(w_uk, (0, 2, 1, 3))
    uv = jnp.transpose(w_uv, (0, 2, 1, 3))
    wo = w_dsa_o.reshape(L, DSA_HEADS, DSA_DH, D_MODEL)
    return pl.pallas_call(
        _fold_kernel,
        grid=(L, DSA_HEADS),
        in_specs=[
            pl.BlockSpec((1, 1, DSA_RQ, DSA_DH), lambda l, h: (l, h, 0, 0)),
            pl.BlockSpec((1, 1, DSA_RKV, DSA_DH), lambda l, h: (l, h, 0, 0)),
            pl.BlockSpec((1, 1, DSA_RKV, DSA_DH), lambda l, h: (l, h, 0, 0)),
            pl.BlockSpec((1, 1, DSA_DH, D_MODEL), lambda l, h: (l, h, 0, 0)),
        ],
        out_specs=[
            pl.BlockSpec((1, DSA_RQ, DSA_RKV), lambda l, h: (l, 0, h)),
            pl.BlockSpec((1, DSA_RKV, D_MODEL), lambda l, h: (l, h, 0)),
        ],
        out_shape=[
            jax.ShapeDtypeStruct((L, DSA_RQ, DSA_HEADS * DSA_RKV), BF16),
            jax.ShapeDtypeStruct((L, DSA_HEADS * DSA_RKV, D_MODEL), BF16),
        ],
        compiler_params=_params(("parallel", "parallel")),
        name="fold_weights",
    )(uq, uk, uv, wo)


def _in_kernel(x_ref, g_ref, cos_ref, sin_ref, wa_ref, wih_ref, wil_ref,
               rq_ref, rk_ref, rv_ref, rg_ref, mq_ref, gates_ref, idx_ref):
    h = _rms(x_ref[...], g_ref[...])
    hb, hl = _split(h)
    c = cos_ref[...]
    s = sin_ref[...]
    half = RET_W // 2

    def rope_store(ref, z, scale):
        x1 = z[:, :half]
        x2 = z[:, half:]
        ref[:, :half] = ((x1 * c - x2 * s) * scale).astype(BF16)
        ref[:, half:] = ((x2 * c + x1 * s) * scale).astype(BF16)

    rope_store(rq_ref, _dot(hb, wa_ref[:, 0:512]), 1.0)
    rope_store(rk_ref, _dot(hb, wa_ref[:, 512:1024]), RET_DK ** -0.5)
    rv_ref[...] = _dot(hb, wa_ref[:, 1024:1536]).astype(BF16)
    rg_ref[...] = _dot(hb, wa_ref[:, 1536:2048]).astype(BF16)
    mq_ref[...] = _dot(hb, wa_ref[:, 2048:2560]).astype(BF16)
    for j in range(6):
        gates_ref[:, j * 512:(j + 1) * 512] = _dot(hb, wa_ref[:, 2560 + j * 512:3072 + j * 512]).astype(BF16)
    idx_ref[...] = _dot(hb, wih_ref[...]) + _dot(hl, wih_ref[...]) + _dot(hb, wil_ref[...])


def _in_proj(x2d, g, cos_t, sin_t, wa, wih, wil, S):
    N = x2d.shape[0]
    n_pos = S // TM_IN
    row = lambda i: (i, 0)
    const = lambda i: (0, 0)
    pos = lambda i: (i % n_pos, 0)
    bf = lambda w: jax.ShapeDtypeStruct((N, w), BF16)
    return pl.pallas_call(
        _in_kernel,
        grid=(N // TM_IN,),
        in_specs=[
            pl.BlockSpec((TM_IN, D_MODEL), row),
            pl.BlockSpec((1, D_MODEL), const),
            pl.BlockSpec((TM_IN, RET_W // 2), pos),
            pl.BlockSpec((TM_IN, RET_W // 2), pos),
            pl.BlockSpec(wa.shape, const),
            pl.BlockSpec(wih.shape, const),
            pl.BlockSpec(wil.shape, const),
        ],
        out_specs=[
            pl.BlockSpec((TM_IN, 512), row), pl.BlockSpec((TM_IN, 512), row),
            pl.BlockSpec((TM_IN, 512), row), pl.BlockSpec((TM_IN, 512), row),
            pl.BlockSpec((TM_IN, 512), row), pl.BlockSpec((TM_IN, 3 * D_MODEL), row),
            pl.BlockSpec((TM_IN, IDX_COLS), row),
        ],
        out_shape=[bf(512), bf(512), bf(512), bf(512), bf(512), bf(3 * D_MODEL),
                   jax.ShapeDtypeStruct((N, IDX_COLS), F32)],
        compiler_params=_params(("parallel",)),
        name="in_proj",
    )(x2d, g, cos_t, sin_t, wa, wih, wil)


def _ret_kernel(rq_ref, rk_ref, rv_ref, rg_ref, dmat_ref, qd_ref, kd_ref, hq_ref, hv_ref,
                cdec_ref, bd_ref, p_ref, o_ref, s_ref):
    @pl.when(pl.program_id(1) == 0)
    def _():
        s_ref[...] = jnp.zeros_like(s_ref)

    q = rq_ref[...]
    k = rk_ref[...]
    v = rv_ref[...]
    o = jnp.zeros(o_ref.shape, F32)
    for h in range(RET_HEADS):
        sc = _dot_nt(q * hq_ref[h], k) * dmat_ref[h]
        o = o + _dot(sc.astype(BF16), v) * hv_ref[h]
    state = s_ref[...]
    qf = q.astype(F32) * qd_ref[...]
    o = o + _dot(qf.astype(BF16), state.astype(BF16))
    kf = k.astype(F32) * kd_ref[...]
    kv = _dot(kf.T.astype(BF16), v)
    s_ref[...] = state * cdec_ref[...] + kv * bd_ref[...]

    p = p_ref[...]

    def group_mean(a):
        ah, al = _split(a)
        return _dot(ah, p) + _dot(al, p)

    d = o - group_mean(o)
    y = d * lax.rsqrt(group_mean(d * d) + GN_EPS)
    g = rg_ref[...].astype(F32)
    o_ref[...] = (g * jax.nn.sigmoid(g) * y).astype(BF16)


def _ret_consts():
    T = T_RET
    hh = np.arange(RET_HEADS, dtype=np.float64)
    log_g = np.log1p(-np.exp2(-5.0 - hh))
    t = np.arange(T)
    ct = t // CHUNK
    diff = (t[:, None] - t[None, :]).astype(np.float64)
    same = ct[:, None] == ct[None, :]
    past = ct[None, :] < ct[:, None]
    expo = np.where(same, np.abs(diff), diff)
    dmat = np.where((same | past)[None], np.exp(log_g[:, None, None] * expo[None]), 0.0)
    lane = np.arange(RET_W)
    hk = (lane % (RET_W // 2)) // (RET_DK // 2)
    hv = lane // RET_DK
    qd = np.exp(log_g[hk][None, :] * (t[:, None] + 1.0))
    kd = np.exp(log_g[hk][None, :] * (T - 1.0 - t[:, None]))
    hq_mask = (hk[None, :] == np.arange(RET_HEADS)[:, None]).astype(np.float32)[:, None, :]
    hv_mask = (hv[None, :] == np.arange(RET_HEADS)[:, None]).astype(np.float32)[:, None, :]
    bd = (hk[:, None] == hv[None, :]).astype(np.float32)
    cdec = np.broadcast_to(np.exp(log_g[hk] * T)[:, None], (RET_W, RET_W))
    pmat = (hv[:, None] == hv[None, :]).astype(np.float32) / RET_DK
    f = lambda a: jnp.asarray(np.asarray(a, dtype=np.float32))
    return (f(dmat), f(qd), f(kd), jnp.asarray(hq_mask, dtype=BF16), f(hv_mask), f(cdec), f(bd),
            jnp.asarray(pmat, dtype=BF16))


def _retention(rq, rk, rv, rg, B, S):
    consts = _ret_consts()
    nb = S // T_RET
    tok = lambda b, j: (b * nb + j, 0)
    full = lambda a: pl.BlockSpec(a.shape, lambda b, j: (0,) * a.ndim)
    return pl.pallas_call(
        _ret_kernel,
        grid=(B, nb),
        in_specs=[pl.BlockSpec((T_RET, RET_W), tok)] * 4 + [full(a) for a in consts],
        out_specs=pl.BlockSpec((T_RET, RET_W), tok),
        out_shape=jax.ShapeDtypeStruct(rq.shape, BF16),
        scratch_shapes=[pltpu.VMEM((RET_W, RET_W), F32)],
        compiler_params=_params(("parallel", "arbitrary")),
        name="retention",
    )(rq, rk, rv, rg, *consts)


def _memkv_kernel(mem_ref, g_ref, w_ref, k_ref, v_ref):
    mn = _rms(mem_ref[0], g_ref[...]).astype(BF16)
    kv = _dot(mn, w_ref[...])
    k_ref[0] = kv[:, :MEM_W].astype(BF16)
    v_ref[0] = kv[:, MEM_W:].astype(BF16)


def _mem_kv(mem, g, w):
    B = mem.shape[0]
    return pl.pallas_call(
        _memkv_kernel,
        grid=(B,),
        in_specs=[pl.BlockSpec((1, MEM_LEN, D_MODEL), lambda b: (b, 0, 0)),
                  pl.BlockSpec((1, D_MODEL), lambda b: (0, 0)),
                  pl.BlockSpec(w.shape, lambda b: (0, 0))],
        out_specs=[pl.BlockSpec((1, MEM_LEN, MEM_W), lambda b: (b, 0, 0))] * 2,
        out_shape=[jax.ShapeDtypeStruct((B, MEM_LEN, MEM_W), BF16)] * 2,
        compiler_params=_params(("parallel",)),
        name="mem_kv",
    )(mem, g, w)


def _prep_kernel(idx_ref, qn_ref, kvn_ref, wql_ref, wiqh_ref, wiql_ref,
                 qlat_ref, qidx_ref, wt_ref, ckvn_ref, ckt_ref, kidx_ref):
    z = idx_ref[...]
    cqn = _rms(z[:, :DSA_RQ], qn_ref[...])
    cb, cl = _split(cqn)
    ql = _dot(cb, wql_ref[...])
    y = _dot(cb, wiqh_ref[...]) + _dot(cl, wiqh_ref[...]) + _dot(cb, wiql_ref[...])
    yh = y.astype(BF16).astype(F32)
    lane = lax.broadcasted_iota(jnp.int32, y.shape, 1)
    qsel = jnp.where((lane % LANES) < 2 * IDX_DIM, yh, y - yh).astype(BF16)
    for h in range(DSA_HEADS):
        qlat_ref[0, h] = ql[:, h * LANES:(h + 1) * LANES].astype(BF16)
        qidx_ref[0, h] = qsel[:, h * LANES:(h + 1) * LANES]
    ckn = _rms(z[:, DSA_RQ:DSA_RQ + DSA_RKV], kvn_ref[...])
    ckvn_ref[0] = ckn.astype(BF16)
    ckt_ref[0] = ckn.T.astype(BF16)
    ik = z[:, 384:512]
    ikh = ik.astype(BF16).astype(F32)
    lane1 = lax.broadcasted_iota(jnp.int32, ik.shape, 1)
    kidx_ref[0] = jnp.where((lane1 // IDX_DIM) % 2 == 0, ikh, ik - ikh).astype(BF16)
    wfull = z[:, 512:640] * ((IDX_HEADS ** -0.5) * (IDX_DIM ** -0.5))
    wt_ref[...] = wfull.T[:IDX_HEADS, :]


def _dsa_prep(idx, qn, kvn, wql, wiqh, wiql):
    N = idx.shape[0]
    nt = N // TQ
    const = lambda i: (0, 0)
    tile3 = lambda i: (i, 0, 0)
    stacked = pl.BlockSpec((1, DSA_HEADS, TQ, LANES), lambda i: (i, 0, 0, 0))
    return pl.pallas_call(
        _prep_kernel,
        grid=(nt,),
        in_specs=[pl.BlockSpec((TQ, IDX_COLS), lambda i: (i, 0)),
                  pl.BlockSpec((1, DSA_RQ), const), pl.BlockSpec((1, DSA_RKV), const),
                  pl.BlockSpec(wql.shape, const), pl.BlockSpec(wiqh.shape, const),
                  pl.BlockSpec(wiql.shape, const)],
        out_specs=[stacked, stacked, pl.BlockSpec((IDX_HEADS, TQ), lambda i: (0, i)),
                   pl.BlockSpec((1, TQ, LANES), tile3), pl.BlockSpec((1, LANES, TQ), tile3),
                   pl.BlockSpec((1, TQ, LANES), tile3)],
        out_shape=[jax.ShapeDtypeStruct((nt, DSA_HEADS, TQ, LANES), BF16),
                   jax.ShapeDtypeStruct((nt, DSA_HEADS, TQ, LANES), BF16),
                   jax.ShapeDtypeStruct((IDX_HEADS, N), F32),
                   jax.ShapeDtypeStruct((nt, TQ, LANES), BF16),
                   jax.ShapeDtypeStruct((nt, LANES, TQ), BF16),
                   jax.ShapeDtypeStruct((nt, TQ, LANES), BF16)],
        compiler_params=_params(("parallel",)),
        name="dsa_prep",
    )(idx, qn, kvn, wql, wiqh, wiql)


def _key_to_f32(key):
    bits = jnp.where(key >= 0, key, key ^ jnp.int32(0x7FFFFFFF))
    return lax.bitcast_convert_type(bits, F32)


def _sublane_all(a, op):
    for shift in (4, 2, 1):
        a = op(a, pltpu.roll(a, shift, 0))
    return a


def _dsa_kernel(qlat_ref, qidx_ref, wt_ref, ckvn_ref, ckt_ref, kidx_ref, bnear_ref, o_ref,
                sc_ref, m_ref, l_ref, acc_ref, *, k_sel):
    i = pl.program_id(1)
    wt = wt_ref[...]

    def score_tile(j):
        y_all = _dot_nt(kidx_ref[j], qidx_ref[0].reshape(IDX_HEADS * TQ, LANES))
        tot = jnp.zeros((TK, TQ), F32)
        for h in range(IDX_HEADS):
            tot = tot + jnp.maximum(y_all[:, h * TQ:(h + 1) * TQ], 0.0) * wt[h:h + 1, :]
        return tot

    def fill(j, carry):
        sc_ref[j] = score_tile(j)
        return carry

    lax.fori_loop(0, i, fill, 0)
    key_r = lax.broadcasted_iota(jnp.int32, (TK, TQ), 0)
    qry_c = lax.broadcasted_iota(jnp.int32, (TK, TQ), 1)
    admissible = key_r <= (qry_c // CHUNK) * CHUNK + (CHUNK - 1)
    sc_ref[i] = jnp.where(admissible, score_tile(i), -jnp.inf)

    def colsum8(a):
        return jnp.sum(a.reshape(TK // 8, 8, TQ), axis=0)

    def count_ge(thr):
        def body(j, c):
            return c + colsum8(jnp.where(sc_ref[j] >= thr, 1.0, 0.0))
        c8 = lax.fori_loop(0, i + 1, body, jnp.zeros((8, TQ), F32))
        return jnp.sum(c8, axis=0, keepdims=True)

    qc = lax.broadcasted_iota(jnp.int32, (1, TQ), 1)
    n_adm = (i * TQ + (qc // CHUNK + 1) * CHUNK).astype(F32)
    kf = jnp.float32(k_sel)
    lo0 = jnp.full((1, TQ), np.int32(np.array(-F32_MAX, np.float32).view(np.int32)) ^ np.int32(0x7FFFFFFF), jnp.int32)
    hi0 = jnp.full((1, TQ), np.int32(np.array(np.inf, np.float32).view(np.int32)), jnp.int32)

    def not_done(lo, hi, cnt_lo):
        mid = (lo & hi) + ((lo ^ hi) >> 1)
        return jnp.where((cnt_lo == kf) | (mid == lo) | (n_adm <= kf), 0.0, 1.0)

    def cond(st):
        it, lo, hi, cnt_lo = st
        return jnp.logical_and(it < 34, jnp.sum(not_done(lo, hi, cnt_lo)) > 0.0)

    def step(st):
        it, lo, hi, cnt_lo = st
        mid = (lo & hi) + ((lo ^ hi) >> 1)
        c = count_ge(_key_to_f32(mid))
        ge = c >= kf
        return it + 1, jnp.where(ge, mid, lo), jnp.where(ge, hi, mid), jnp.where(ge, c, cnt_lo)

    _, lo, _, _ = lax.while_loop(cond, step, (jnp.int32(0), lo0, hi0, n_adm))
    thr = _key_to_f32(lo)

    m_ref[...] = jnp.full(m_ref.shape, NEG, F32)
    l_ref[...] = jnp.zeros(l_ref.shape, F32)
    acc_ref[...] = jnp.zeros(acc_ref.shape, F32)

    def attend(j, bias_of):
        ck = ckvn_ref[j]
        ckt = ckt_ref[j]
        mask_bias = jnp.where(sc_ref[j] >= thr, 0.0, NEG)
        lg_all = _dot_nt(ck, qlat_ref[0].reshape(DSA_HEADS * TQ, LANES))
        ps, alphas = [], []
        for h in range(DSA_HEADS):
            lg = lg_all[:, h * TQ:(h + 1) * TQ] + mask_bias
            bias = bias_of(h)
            if bias is not None:
                lg = lg + bias
            lg3 = lg.reshape(TK // 8, 8, TQ)
            m_old = m_ref[h]
            m_new = jnp.maximum(m_old, _sublane_all(jnp.max(lg3, axis=0), jnp.maximum))
            alpha = jnp.exp2(m_old - m_new)
            p3 = jnp.exp2(lg3 - m_new[None])
            l_ref[h] = alpha * l_ref[h] + jnp.sum(p3, axis=0)
            m_ref[h] = m_new
            ps.append(p3.reshape(TK, TQ).astype(BF16))
            alphas.append(alpha)
        pv_all = _dot(ckt, jnp.concatenate(ps, axis=1))
        for h in range(DSA_HEADS):
            acc3 = acc_ref[h].reshape(DSA_RKV // 8, 8, TQ) * alphas[h][None]
            acc_ref[h] = acc3.reshape(DSA_RKV, TQ) + pv_all[:, h * TQ:(h + 1) * TQ]

    def far(j, carry):
        attend(j, lambda h: None)
        return carry

    lax.fori_loop(0, jnp.maximum(i - 1, 0), far, 0)

    @pl.when(i >= 1)
    def _():
        attend(i - 1, lambda h: bnear_ref[h, 0])

    attend(i, lambda h: bnear_ref[h, 1])
    for h in range(DSA_HEADS):
        l_tot = _sublane_all(l_ref[h], jnp.add)
        o = (acc_ref[h].reshape(DSA_RKV // 8, 8, TQ) / l_tot[None]).reshape(DSA_RKV, TQ)
        o_ref[:, h * LANES:(h + 1) * LANES] = o.T.astype(BF16)


def _t5_bucket(rel):
    nb = REL_BUCKETS // 2
    max_exact = nb // 2
    base = jnp.where(rel > 0, nb, 0)
    n = jnp.abs(rel)
    nf = jnp.maximum(n, 1).astype(jnp.float32)
    large = max_exact + (jnp.log(nf / max_exact) / math.log(REL_MAX_DIST / max_exact)
                         * (nb - max_exact)).astype(jnp.int32)
    large = jnp.minimum(large, nb - 1)
    return base + jnp.where(n < max_exact, n, large)


def _bias_tables(rel_bias):
    s = jnp.arange(TK, dtype=jnp.int32)[:, None]
    t = jnp.arange(TQ, dtype=jnp.int32)[None, :]
    rel = jnp.stack([s - TK - t, s - t])
    table = rel_bias.astype(F32)
    near = jnp.transpose(table[_t5_bucket(rel)], (3, 0, 1, 2))
    far = table[_t5_bucket(jnp.full((1,), -REL_MAX_DIST, jnp.int32))[0]]
    return (near - far[:, None, None, None]) * LOG2E


def _dsa_attention(qlat, qidx, wt, ckvn, ckt, kidx, bnear, B, S):
    nq = S // TQ
    N = B * S
    k_sel = min(TOPK_MAX, S // 4)
    stacked = pl.BlockSpec((1, DSA_HEADS, TQ, LANES), lambda b, i: (b * nq + i, 0, 0, 0))
    keys = pl.BlockSpec((nq, TK, LANES), lambda b, i: (b, 0, 0))
    keys_t = pl.BlockSpec((nq, LANES, TK), lambda b, i: (b, 0, 0))
    return pl.pallas_call(
        functools.partial(_dsa_kernel, k_sel=k_sel),
        grid=(B, nq),
        in_specs=[stacked, stacked, pl.BlockSpec((IDX_HEADS, TQ), lambda b, i: (0, b * nq + i)),
                  keys, keys_t, keys,
                  pl.BlockSpec(bnear.shape, lambda b, i: (0, 0, 0, 0))],
        out_specs=pl.BlockSpec((TQ, DSA_HEADS * DSA_RKV), lambda b, i: (b * nq + i, 0)),
        out_shape=jax.ShapeDtypeStruct((N, DSA_HEADS * DSA_RKV), BF16),
        scratch_shapes=[pltpu.VMEM((nq, TK, TQ), F32),
                        pltpu.VMEM((DSA_HEADS, 8, TQ), F32),
                        pltpu.VMEM((DSA_HEADS, 8, TQ), F32),
                        pltpu.VMEM((DSA_HEADS, DSA_RKV, TQ), F32)],
        compiler_params=_params(("parallel", "arbitrary")),
        name="dsa_attention",
    )(qlat, qidx, wt, ckvn, ckt, kidx, bnear)


def _merge_kernel(x_ref, ret_ref, olat_ref, mq_ref, gates_ref, km_ref, vm_ref,
                  wret_ref, wdsa_ref, wmem_ref, wout_ref, o_ref):
    mq = mq_ref[...]
    km = km_ref[0]
    vm = vm_ref[0]
    mem_b = jnp.zeros(o_ref.shape, F32)
    for h in range(MEM_HEADS):
        sl = slice(h * MEM_DH, (h + 1) * MEM_DH)
        lg = _dot_nt(mq[:, sl], km[:, sl]) * (MEM_DH ** -0.5)
        p = jnp.exp(lg - jnp.max(lg, axis=1, keepdims=True))
        p = p / jnp.sum(p, axis=1, keepdims=True)
        pv = _dot(p.astype(BF16), vm[:, sl])
        mem_b = mem_b + _dot(pv.astype(BF16), wmem_ref[sl, :])
    ret_b = _dot(ret_ref[...], wret_ref[...])
    dsa_b = _dot(olat_ref[...], wdsa_ref[...])
    g = jax.nn.sigmoid(gates_ref[...].astype(F32))
    merged = (g[:, :D_MODEL] * ret_b + g[:, D_MODEL:2 * D_MODEL] * dsa_b + g[:, 2 * D_MODEL:] * mem_b)
    o_ref[...] = x_ref[...] + _dot(merged.astype(BF16), wout_ref[...])


def _merge(x2d, ret, olat, mq, gates, km, vm, wret, wdsa, wmem, wout, S):
    N = x2d.shape[0]
    per_b = S // TM_MERGE
    row = lambda i: (i, 0)
    const = lambda i: (0, 0)
    memb = lambda i: (i // per_b, 0, 0)
    return pl.pallas_call(
        _merge_kernel,
        grid=(N // TM_MERGE,),
        in_specs=[pl.BlockSpec((TM_MERGE, D_MODEL), row), pl.BlockSpec((TM_MERGE, RET_W), row),
                  pl.BlockSpec((TM_MERGE, DSA_HEADS * DSA_RKV), row), pl.BlockSpec((TM_MERGE, MEM_W), row),
                  pl.BlockSpec((TM_MERGE, 3 * D_MODEL), row),
                  pl.BlockSpec((1, MEM_LEN, MEM_W), memb), pl.BlockSpec((1, MEM_LEN, MEM_W), memb),
                  pl.BlockSpec(wret.shape, const), pl.BlockSpec(wdsa.shape, const),
                  pl.BlockSpec(wmem.shape, const), pl.BlockSpec(wout.shape, const)],
        out_specs=pl.BlockSpec((TM_MERGE, D_MODEL), row),
        out_shape=jax.ShapeDtypeStruct(x2d.shape, F32),
        compiler_params=_params(("parallel",)),
        name="merge_out",
    )(x2d, ret, olat, mq, gates, km, vm, wret, wdsa, wmem, wout)


def _mlp_kernel(x_ref, g_ref, w1_ref, w2_ref, gf_ref, o_ref, *, final):
    x = x_ref[...]
    h = _rms(x, g_ref[...]).astype(BF16)
    acc = jnp.zeros(x.shape, F32)
    for c in range(D_FF // D_MODEL):
        sl = slice(c * D_MODEL, (c + 1) * D_MODEL)
        a = jnp.maximum(_dot(h, w1_ref[:, sl]), 0.0)
        acc = acc + _dot((a * a).astype(BF16), w2_ref[sl, :])
    y = x + acc
    if final:
        y = _rms(y, gf_ref[...])
    o_ref[...] = y


def _mlp(x2d, g, w1, w2, gf, final):
    N = x2d.shape[0]
    row = lambda i: (i, 0)
    const = lambda i: (0, 0)
    return pl.pallas_call(
        functools.partial(_mlp_kernel, final=final),
        grid=(N // TM_MLP,),
        in_specs=[pl.BlockSpec((TM_MLP, D_MODEL), row), pl.BlockSpec((1, D_MODEL), const),
                  pl.BlockSpec(w1.shape, const), pl.BlockSpec(w2.shape, const),
                  pl.BlockSpec((1, D_MODEL), const)],
        out_specs=pl.BlockSpec((TM_MLP, D_MODEL), row),
        out_shape=jax.ShapeDtypeStruct(x2d.shape, F32),
        compiler_params=_params(("parallel",)),
        name="mlp",
    )(x2d, g, w1, w2, gf)


def _rope_tables(S):
    half = RET_DK // 2
    pos = jnp.arange(S, dtype=F32)
    freqs = ROPE_BASE ** (-jnp.arange(half, dtype=F32) / half)
    ang = pos[:, None] * freqs[None, :]
    return jnp.tile(jnp.cos(ang), (1, RET_HEADS)), jnp.tile(jnp.sin(ang), (1, RET_HEADS))


def _in_weights(wi, w_iq_l):
    half = RET_DK // 2
    perm = (np.arange(RET_HEADS)[None, :, None] * RET_DK + np.arange(2)[:, None, None] * half
            + np.arange(half)[None, None, :]).reshape(-1)
    offs = np.concatenate([[0], np.cumsum(IN_SIZES)])
    seg = lambda k: wi[:, int(offs[k]):int(offs[k + 1])]
    wa = jnp.concatenate([seg(0)[:, perm], seg(1)[:, perm], seg(2), seg(3), seg(8), seg(9)], axis=1).astype(BF16)
    pad = jnp.zeros((D_MODEL, LANES - IDX_HEADS), F32)
    widx = jnp.concatenate([seg(4), seg(5), jnp.tile(seg(6), (1, 4)), seg(7), pad], axis=1)
    wih, wil = _split(widx)
    wiq = jnp.tile(w_iq_l[:, :, None, :], (1, 1, 4, 1)).reshape(DSA_RQ, IDX_HEADS * LANES)
    wiqh, wiql = _split(wiq)
    return wa, wih, wil, wiqh, wiql


def kernel(x, mem, norm1, w_in, q_norm, kv_norm, w_uq, w_iq, w_uk, w_uv, mem_norm, w_mem_kv,
           w_ret_o, w_dsa_o, w_mem_o, w_out, norm2, w_ff1, w_ff2, rel_bias, final_norm):
    B, S, D = x.shape
    depth = w_in.shape[0]
    assert D == D_MODEL and S % TM_MLP == 0 and S % TQ == 0 and TQ == TK
    cos_t, sin_t = _rope_tables(S)
    bnear = _bias_tables(rel_bias)
    wql_all, wdsa_all = _fold_weights(w_uq, w_uk, w_uv, w_dsa_o)
    x2d = x.reshape(B * S, D)
    row = lambda v: v.reshape(1, -1)
    for l in range(depth):
        wa, wih, wil, wiqh, wiql = _in_weights(w_in[l], w_iq[l])
        rq, rk, rv, rg, mq, gates, idx = _in_proj(x2d, row(norm1[l]), cos_t, sin_t, wa, wih, wil, S)
        ret = _retention(rq, rk, rv, rg, B, S)
        qlat, qidx, wt, ckvn, ckt, kidx = _dsa_prep(idx, row(q_norm[l]), row(kv_norm[l]), wql_all[l], wiqh, wiql)
        olat = _dsa_attention(qlat, qidx, wt, ckvn, ckt, kidx, bnear, B, S)
        km, vm = _mem_kv(mem, row(mem_norm[l]), w_mem_kv[l].astype(BF16))
        x2d = _merge(x2d, ret, olat, mq, gates, km, vm, w_ret_o[l].astype(BF16), wdsa_all[l],
                     w_mem_o[l].astype(BF16), w_out[l].astype(BF16), S)
        x2d = _mlp(x2d, row(norm2[l]), w_ff1[l].astype(BF16), w_ff2[l].astype(BF16), row(final_norm),
                   final=(l == depth - 1))
    return x2d.reshape(B, S, D)
```

```python
import functools
import math
import statistics

import numpy as np
import jax
import jax.numpy as jnp
from jax import lax
from jax.experimental import pallas as pl
from jax.experimental.pallas import tpu as pltpu

F32 = jnp.float32
BF16 = jnp.bfloat16

D_MODEL = 1024
CHUNK = 64
EPS = 1e-6
GN_EPS = 1e-5
RET_HEADS = 8
RET_DK = 64
RET_W = 512
ROPE_BASE = 10000.0
DSA_HEADS = 8
DSA_DH = 64
DSA_RQ = 256
DSA_RKV = 128
IDX_HEADS = 8
IDX_DIM = 32
TOPK_MAX = 256
MEM_LEN = 256
MEM_HEADS = 4
MEM_DH = 128
MEM_W = 512
REL_BUCKETS = 32
REL_MAX_DIST = 128
D_FF = 4096
IN_SIZES = (512, 512, 512, 512, DSA_RQ, DSA_RKV, IDX_DIM, IDX_HEADS, MEM_W, 3 * D_MODEL)

LANES = 128
IDX_COLS = 640
NEG = -1e30
LOG2E = math.log2(math.e)
F32_MAX = float(np.finfo(np.float32).max)
VMEM_LIMIT = 56 * 1024 * 1024

TM_IN = 256
T_RET = 256
TQ = 256
TK = 256
SEARCH_SWITCH = 12
SEARCH_CAP = SEARCH_SWITCH + 34
TM_MERGE = 256
TM_MLP = 512


def _dot(a, b):
    return jnp.dot(a, b, preferred_element_type=F32)


def _dot_nt(a, b):
    return lax.dot_general(a, b, (((1,), (1,)), ((), ())), preferred_element_type=F32)


def _split(a):
    hi = a.astype(BF16)
    lo = (a - hi.astype(F32)).astype(BF16)
    return hi, lo


def _rms(x, g):
    return x * lax.rsqrt(jnp.mean(x * x, axis=-1, keepdims=True) + EPS) * g


def _params(sem):
    return pltpu.CompilerParams(dimension_semantics=sem, vmem_limit_bytes=VMEM_LIMIT)


def _fold_kernel(uq_ref, uk_ref, uv_ref, wo_ref, wql_ref, wdsa_ref):
    uq_h, uq_l = _split(uq_ref[0, 0])
    uk_h, uk_l = _split(uk_ref[0, 0])
    ql = _dot_nt(uq_h, uk_h) + _dot_nt(uq_h, uk_l) + _dot_nt(uq_l, uk_h)
    wql_ref[0] = (ql * (DSA_DH ** -0.5 * LOG2E)).astype(BF16)
    uv_h, uv_l = _split(uv_ref[0, 0])
    wo_h, wo_l = _split(wo_ref[0, 0])
    wdsa_ref[0] = (_dot(uv_h, wo_h) + _dot(uv_h, wo_l) + _dot(uv_l, wo_h)).astype(BF16)


def _fold_weights(w_uq, w_uk, w_uv, w_dsa_o):
    L = w_uq.shape[0]
    uq = jnp.transpose(w_uq, (0, 2, 1, 3))
    uk = jnp.transpose(w_uk, (0, 2, 1, 3))
    uv = jnp.transpose(w_uv, (0, 2, 1, 3))
    wo = w_dsa_o.reshape(L, DSA_HEADS, DSA_DH, D_MODEL)
    return pl.pallas_call(
        _fold_kernel,
        grid=(L, DSA_HEADS),
        in_specs=[
            pl.BlockSpec((1, 1, DSA_RQ, DSA_DH), lambda l, h: (l, h, 0, 0)),
            pl.BlockSpec((1, 1, DSA_RKV, DSA_DH), lambda l, h: (l, h, 0, 0)),
            pl.BlockSpec((1, 1, DSA_RKV, DSA_DH), lambda l, h: (l, h, 0, 0)),
            pl.BlockSpec((1, 1, DSA_DH, D_MODEL), lambda l, h: (l, h, 0, 0)),
        ],
        out_specs=[
            pl.BlockSpec((1, DSA_RQ, DSA_RKV), lambda l, h: (l, 0, h)),
            pl.BlockSpec((1, DSA_RKV, D_MODEL), lambda l, h: (l, h, 0)),
        ],
        out_shape=[
            jax.ShapeDtypeStruct((L, DSA_RQ, DSA_HEADS * DSA_RKV), BF16),
            jax.ShapeDtypeStruct((L, DSA_HEADS * DSA_RKV, D_MODEL), BF16),
        ],
        compiler_params=_params(("parallel", "parallel")),
        name="fold_weights",
    )(uq, uk, uv, wo)


def _in_kernel(x_ref, g_ref, cos_ref, sin_ref, wa_ref, wih_ref, wil_ref,
               rq_ref, rk_ref, rv_ref, rg_ref, mq_ref, gates_ref, idx_ref):
    h = _rms(x_ref[...], g_ref[...])
    hb, hl = _split(h)
    c = cos_ref[...]
    s = sin_ref[...]
    half = RET_W // 2

    def rope_store(ref, z, scale):
        x1 = z[:, :half]
        x2 = z[:, half:]
        ref[:, :half] = ((x1 * c - x2 * s) * scale).astype(BF16)
        ref[:, half:] = ((x2 * c + x1 * s) * scale).astype(BF16)

    rope_store(rq_ref, _dot(hb, wa_ref[:, 0:512]), 1.0)
    rope_store(rk_ref, _dot(hb, wa_ref[:, 512:1024]), RET_DK ** -0.5)
    rv_ref[...] = _dot(hb, wa_ref[:, 1024:1536]).astype(BF16)
    rg_ref[...] = _dot(hb, wa_ref[:, 1536:2048]).astype(BF16)
    mq_ref[...] = _dot(hb, wa_ref[:, 2048:2560]).astype(BF16)
    for j in range(6):
        gates_ref[:, j * 512:(j + 1) * 512] = _dot(hb, wa_ref[:, 2560 + j * 512:3072 + j * 512]).astype(BF16)
    idx_ref[...] = _dot(hb, wih_ref[...]) + _dot(hl, wih_ref[...]) + _dot(hb, wil_ref[...])


def _in_proj(x2d, g, cos_t, sin_t, wa, wih, wil, S):
    N = x2d.shape[0]
    n_pos = S // TM_IN
    row = lambda i: (i, 0)
    const = lambda i: (0, 0)
    pos = lambda i: (i % n_pos, 0)
    bf = lambda w: jax.ShapeDtypeStruct((N, w), BF16)
    return pl.pallas_call(
        _in_kernel,
        grid=(N // TM_IN,),
        in_specs=[
            pl.BlockSpec((TM_IN, D_MODEL), row),
            pl.BlockSpec((1, D_MODEL), const),
            pl.BlockSpec((TM_IN, RET_W // 2), pos),
            pl.BlockSpec((TM_IN, RET_W // 2), pos),
            pl.BlockSpec(wa.shape, const),
            pl.BlockSpec(wih.shape, const),
            pl.BlockSpec(wil.shape, const),
        ],
        out_specs=[
            pl.BlockSpec((TM_IN, 512), row), pl.BlockSpec((TM_IN, 512), row),
            pl.BlockSpec((TM_IN, 512), row), pl.BlockSpec((TM_IN, 512), row),
            pl.BlockSpec((TM_IN, 512), row), pl.BlockSpec((TM_IN, 3 * D_MODEL), row),
            pl.BlockSpec((TM_IN, IDX_COLS), row),
        ],
        out_shape=[bf(512), bf(512), bf(512), bf(512), bf(512), bf(3 * D_MODEL),
                   jax.ShapeDtypeStruct((N, IDX_COLS), F32)],
        compiler_params=_params(("parallel",)),
        name="in_proj",
    )(x2d, g, cos_t, sin_t, wa, wih, wil)


def _ret_kernel(rq_ref, rk_ref, rv_ref, rg_ref, dmat_ref, qd_ref, kd_ref, hq_ref, hv_ref,
                cdec_ref, bd_ref, p_ref, o_ref, s_ref):
    @pl.when(pl.program_id(1) == 0)
    def _():
        s_ref[...] = jnp.zeros_like(s_ref)

    q = rq_ref[...]
    k = rk_ref[...]
    v = rv_ref[...]
    o = jnp.zeros(o_ref.shape, F32)
    for h in range(RET_HEADS):
        sc = _dot_nt(q * hq_ref[h], k) * dmat_ref[h]
        o = o + _dot(sc.astype(BF16), v) * hv_ref[h]
    state = s_ref[...]
    qf = q.astype(F32) * qd_ref[...]
    o = o + _dot(qf.astype(BF16), state.astype(BF16))
    kf = k.astype(F32) * kd_ref[...]
    kv = _dot(kf.T.astype(BF16), v)
    s_ref[...] = state * cdec_ref[...] + kv * bd_ref[...]

    p = p_ref[...]

    def group_mean(a):
        ah, al = _split(a)
        return _dot(ah, p) + _dot(al, p)

    d = o - group_mean(o)
    y = d * lax.rsqrt(group_mean(d * d) + GN_EPS)
    g = rg_ref[...].astype(F32)
    o_ref[...] = (g * jax.nn.sigmoid(g) * y).astype(BF16)


def _ret_consts():
    T = T_RET
    hh = np.arange(RET_HEADS, dtype=np.float64)
    log_g = np.log1p(-np.exp2(-5.0 - hh))
    t = np.arange(T)
    ct = t // CHUNK
    diff = (t[:, None] - t[None, :]).astype(np.float64)
    same = ct[:, None] == ct[None, :]
    past = ct[None, :] < ct[:, None]
    expo = np.where(same, np.abs(diff), diff)
    dmat = np.where((same | past)[None], np.exp(log_g[:, None, None] * expo[None]), 0.0)
    lane = np.arange(RET_W)
    hk = (lane % (RET_W // 2)) // (RET_DK // 2)
    hv = lane // RET_DK
    qd = np.exp(log_g[hk][None, :] * (t[:, None] + 1.0))
    kd = np.exp(log_g[hk][None, :] * (T - 1.0 - t[:, None]))
    hq_mask = (hk[None, :] == np.arange(RET_HEADS)[:, None]).astype(np.float32)[:, None, :]
    hv_mask = (hv[None, :] == np.arange(RET_HEADS)[:, None]).astype(np.float32)[:, None, :]
    bd = (hk[:, None] == hv[None, :]).astype(np.float32)
    cdec = np.broadcast_to(np.exp(log_g[hk] * T)[:, None], (RET_W, RET_W))
    pmat = (hv[:, None] == hv[None, :]).astype(np.float32) / RET_DK
    f = lambda a: jnp.asarray(np.asarray(a, dtype=np.float32))
    return (f(dmat), f(qd), f(kd), jnp.asarray(hq_mask, dtype=BF16), f(hv_mask), f(cdec), f(bd),
            jnp.asarray(pmat, dtype=BF16))


def _retention(rq, rk, rv, rg, B, S):
    consts = _ret_consts()
    nb = S // T_RET
    tok = lambda b, j: (b * nb + j, 0)
    full = lambda a: pl.BlockSpec(a.shape, lambda b, j: (0,) * a.ndim)
    return pl.pallas_call(
        _ret_kernel,
        grid=(B, nb),
        in_specs=[pl.BlockSpec((T_RET, RET_W), tok)] * 4 + [full(a) for a in consts],
        out_specs=pl.BlockSpec((T_RET, RET_W), tok),
        out_shape=jax.ShapeDtypeStruct(rq.shape, BF16),
        scratch_shapes=[pltpu.VMEM((RET_W, RET_W), F32)],
        compiler_params=_params(("parallel", "arbitrary")),
        name="retention",
    )(rq, rk, rv, rg, *consts)


def _memkv_kernel(mem_ref, g_ref, w_ref, k_ref, v_ref):
    mn = _rms(mem_ref[0], g_ref[...]).astype(BF16)
    kv = _dot(mn, w_ref[...])
    k_ref[0] = kv[:, :MEM_W].astype(BF16)
    v_ref[0] = kv[:, MEM_W:].astype(BF16)


def _mem_kv(mem, g, w):
    B = mem.shape[0]
    return pl.pallas_call(
        _memkv_kernel,
        grid=(B,),
        in_specs=[pl.BlockSpec((1, MEM_LEN, D_MODEL), lambda b: (b, 0, 0)),
                  pl.BlockSpec((1, D_MODEL), lambda b: (0, 0)),
                  pl.BlockSpec(w.shape, lambda b: (0, 0))],
        out_specs=[pl.BlockSpec((1, MEM_LEN, MEM_W), lambda b: (b, 0, 0))] * 2,
        out_shape=[jax.ShapeDtypeStruct((B, MEM_LEN, MEM_W), BF16)] * 2,
        compiler_params=_params(("parallel",)),
        name="mem_kv",
    )(mem, g, w)


def _prep_kernel(idx_ref, qn_ref, kvn_ref, wql_ref, wiqh_ref, wiql_ref,
                 qlat_ref, qidx_ref, wt_ref, ckvn_ref, ckt_ref, kidx_ref):
    z = idx_ref[...]
    cqn = _rms(z[:, :DSA_RQ], qn_ref[...])
    cb, cl = _split(cqn)
    ql = _dot(cb, wql_ref[...])
    y = _dot(cb, wiqh_ref[...]) + _dot(cl, wiqh_ref[...]) + _dot(cb, wiql_ref[...])
    yh = y.astype(BF16).astype(F32)
    lane = lax.broadcasted_iota(jnp.int32, y.shape, 1)
    qsel = jnp.where((lane % LANES) < 2 * IDX_DIM, yh, y - yh).astype(BF16)
    for h in range(DSA_HEADS):
        qlat_ref[0, h] = ql[:, h * LANES:(h + 1) * LANES].astype(BF16)
        qidx_ref[0, h] = qsel[:, h * LANES:(h + 1) * LANES]
    ckn = _rms(z[:, DSA_RQ:DSA_RQ + DSA_RKV], kvn_ref[...])
    ckvn_ref[0] = ckn.astype(BF16)
    ckt_ref[0] = ckn.T.astype(BF16)
    ik = z[:, 384:512]
    ikh = ik.astype(BF16).astype(F32)
    lane1 = lax.broadcasted_iota(jnp.int32, ik.shape, 1)
    kidx_ref[0] = jnp.where((lane1 // IDX_DIM) % 2 == 0, ikh, ik - ikh).astype(BF16)
    wfull = z[:, 512:640] * ((IDX_HEADS ** -0.5) * (IDX_DIM ** -0.5))
    wt_ref[...] = wfull.T[:IDX_HEADS, :]


def _dsa_prep(idx, qn, kvn, wql, wiqh, wiql):
    N = idx.shape[0]
    nt = N // TQ
    const = lambda i: (0, 0)
    tile3 = lambda i: (i, 0, 0)
    stacked = pl.BlockSpec((1, DSA_HEADS, TQ, LANES), lambda i: (i, 0, 0, 0))
    return pl.pallas_call(
        _prep_kernel,
        grid=(nt,),
        in_specs=[pl.BlockSpec((TQ, IDX_COLS), lambda i: (i, 0)),
                  pl.BlockSpec((1, DSA_RQ), const), pl.BlockSpec((1, DSA_RKV), const),
                  pl.BlockSpec(wql.shape, const), pl.BlockSpec(wiqh.shape, const),
                  pl.BlockSpec(wiql.shape, const)],
        out_specs=[stacked, stacked, pl.BlockSpec((IDX_HEADS, TQ), lambda i: (0, i)),
                   pl.BlockSpec((1, TQ, LANES), tile3), pl.BlockSpec((1, LANES, TQ), tile3),
                   pl.BlockSpec((1, TQ, LANES), tile3)],
        out_shape=[jax.ShapeDtypeStruct((nt, DSA_HEADS, TQ, LANES), BF16),
                   jax.ShapeDtypeStruct((nt, DSA_HEADS, TQ, LANES), BF16),
                   jax.ShapeDtypeStruct((IDX_HEADS, N), F32),
                   jax.ShapeDtypeStruct((nt, TQ, LANES), BF16),
                   jax.ShapeDtypeStruct((nt, LANES, TQ), BF16),
                   jax.ShapeDtypeStruct((nt, TQ, LANES), BF16)],
        compiler_params=_params(("parallel",)),
        name="dsa_prep",
    )(idx, qn, kvn, wql, wiqh, wiql)


def _key_to_f32(key):
    bits = jnp.where(key >= 0, key, key ^ jnp.int32(0x7FFFFFFF))
    return lax.bitcast_convert_type(bits, F32)


def _f32_to_key(x):
    bits = lax.bitcast_convert_type(x, jnp.int32)
    return jnp.where(bits >= 0, bits, bits ^ jnp.int32(0x7FFFFFFF))


def _sublane_all(a, op):
    for shift in (4, 2, 1):
        a = op(a, pltpu.roll(a, shift, 0))
    return a


def _dsa_kernel(qlat_ref, qidx_ref, wt_ref, ckvn_ref, ckt_ref, kidx_ref, bnear_ref, zq_ref, tri_ref, o_ref,
                sc_ref, m_ref, l_ref, acc_ref, *, k_sel):
    i = pl.program_id(1)
    wt = wt_ref[...]

    def score_tile(j):
        y_all = _dot_nt(kidx_ref[j], qidx_ref[0].reshape(IDX_HEADS * TQ, LANES))
        tot = jnp.zeros((TK, TQ), F32)
        for h in range(IDX_HEADS):
            tot = tot + jnp.maximum(y_all[:, h * TQ:(h + 1) * TQ], 0.0) * wt[h:h + 1, :]
        return tot

    def colsum8(a):
        return jnp.sum(a.reshape(TK // 8, 8, TQ), axis=0)

    def ones_where(cond_):
        return jnp.where(cond_, 1.0, 0.0)

    def stats(s_counted, s_summed):
        return (colsum8(s_summed), colsum8(s_summed * s_summed),
                colsum8(ones_where(s_counted >= 0.0)), colsum8(ones_where(s_counted > 0.0)))

    def fill(j, carry):
        s = score_tile(j)
        sc_ref[j] = s
        return tuple(a + b for a, b in zip(carry, stats(s, s)))

    zero8 = jnp.zeros((8, TQ), F32)
    carry = lax.fori_loop(0, i, fill, (zero8, zero8, zero8, zero8))
    key_r = lax.broadcasted_iota(jnp.int32, (TK, TQ), 0)
    qry_c = lax.broadcasted_iota(jnp.int32, (TK, TQ), 1)
    admissible = key_r <= (qry_c // CHUNK) * CHUNK + (CHUNK - 1)
    s_diag = score_tile(i)
    s_masked = jnp.where(admissible, s_diag, -jnp.inf)
    sc_ref[i] = s_masked
    carry = tuple(a + b for a, b in zip(carry, stats(s_masked, jnp.where(admissible, s_diag, 0.0))))
    s1, s2, c_ge0, c_gt0 = (_sublane_all(a, jnp.add)[0:1, :] for a in carry)

    def count_ge(thr):
        def body(j, c):
            return c + colsum8(ones_where(sc_ref[j] >= thr))
        c8 = lax.fori_loop(0, i + 1, body, zero8)
        return jnp.sum(c8, axis=0, keepdims=True)

    qc = lax.broadcasted_iota(jnp.int32, (1, TQ), 1)
    n_adm = (i * TQ + (qc // CHUNK + 1) * CHUNK).astype(F32)
    kf = jnp.float32(k_sel)
    key_min = jnp.int32(np.int32(np.array(-F32_MAX, np.float32).view(np.int32)) ^ np.int32(0x7FFFFFFF))
    key_max = jnp.int32(np.array(np.inf, np.float32).view(np.int32))
    inf = jnp.float32(np.inf)
    zq = zq_ref[0]
    mu = s1 / n_adm
    sd = jnp.sqrt(jnp.maximum(s2 / n_adm - mu * mu, 1e-30))
    dens = zq[1:2, :] / sd
    pos = c_gt0 >= kf
    tie0 = jnp.logical_and(jnp.logical_not(pos), c_ge0 >= kf)
    small = n_adm <= kf
    lo0 = jnp.where(pos | tie0, 0, key_min)
    hi0 = jnp.where(pos, key_max, jnp.where(tie0, 1, 0))
    lo0 = jnp.where(small, key_min, lo0)
    hi0 = jnp.where(small, key_min + 1, hi0)
    lf0 = jnp.where(pos | tie0, 0.0, -inf)
    hf0 = jnp.where(pos, inf, 0.0)
    cl0 = jnp.where(pos | tie0, c_ge0, n_adm)
    ch0 = jnp.where(pos, 0.0, jnp.where(tie0, c_gt0, c_ge0))
    t0 = mu + zq[0:1, :] * sd
    one = jnp.ones((1, TQ), F32)

    def mid_key(lo, hi):
        return (lo & hi) + ((lo ^ hi) >> 1)

    def unfinished(lo, hi, cl):
        return jnp.logical_not((cl == kf) | (mid_key(lo, hi) == lo))

    def cond(st):
        it, lo, hi, cl = st[0], st[1], st[2], st[5]
        return jnp.logical_and(it < SEARCH_CAP, jnp.sum(ones_where(unfinished(lo, hi, cl))) > 0.0)

    def step(st):
        it, lo, hi, lf, hf, cl, ch, t, last, wl, wh = st
        act = unfinished(lo, hi, cl)
        guided = (it < SEARCH_SWITCH).astype(jnp.int32)
        mid = mid_key(lo, hi)
        tk = mid + (_f32_to_key(t) - mid) * guided
        tk = jnp.minimum(jnp.maximum(tk, lo + 1), hi - 1)
        tf = _key_to_f32(tk)
        c = count_ge(tf)
        ge = c >= kf
        up_lo = act & ge
        up_hi = act & jnp.logical_not(ge)
        lo = jnp.where(up_lo, tk, lo)
        lf = jnp.where(up_lo, tf, lf)
        cl = jnp.where(up_lo, c, cl)
        hi = jnp.where(up_hi, tk, hi)
        hf = jnp.where(up_hi, tf, hf)
        ch = jnp.where(up_hi, c, ch)
        side = jnp.where(ge, 1.0, -1.0)
        same = side == last
        wh = jnp.where(ge, jnp.where(same, wh * 0.5, one), one)
        wl = jnp.where(ge, one, jnp.where(same, wl * 0.5, one))
        a = (cl - kf + 0.5) * wl
        b = (kf - 0.5 - ch) * wh
        t_bracket = lf + (hf - lf) * (a / (a + b))
        t_model = tf + 1.5 * (c - kf + jnp.where(ge, 0.5, -0.5)) / dens
        bracketed = (lf > -inf) & (hf < inf)
        t = jnp.where(bracketed, t_bracket, t_model)
        return it + 1, lo, hi, lf, hf, cl, ch, t, side, wl, wh

    st = lax.while_loop(cond, step, (jnp.int32(0), lo0, hi0, lf0, hf0, cl0, ch0, t0,
                                     jnp.zeros((1, TQ), F32), one, one))
    lo, cl, ch = st[1], st[5], st[6]
    thr = _key_to_f32(lo)
    tied = (cl != kf) & jnp.logical_not(small)
    need = jnp.where(tied, kf - ch, jnp.float32(1e9))

    def mask_plain(j, carry_):
        sc_ref[j] = jnp.where(sc_ref[j] >= thr, 0.0, NEG)
        return carry_

    def mask_ranked(j, before):
        s = sc_ref[j]
        eq = ones_where(s == thr)
        rank = _dot(tri_ref[...], eq.astype(BF16)) + before
        keep = (s > thr) | ((s == thr) & (rank <= need))
        sc_ref[j] = jnp.where(keep, 0.0, NEG)
        return before + jnp.sum(eq, axis=0, keepdims=True)

    any_tied = jnp.sum(ones_where(tied)) > 0.0

    @pl.when(any_tied)
    def _():
        lax.fori_loop(0, i + 1, mask_ranked, jnp.zeros((1, TQ), F32))

    @pl.when(jnp.logical_not(any_tied))
    def _():
        lax.fori_loop(0, i + 1, mask_plain, 0)

    m_ref[...] = jnp.full(m_ref.shape, NEG, F32)
    l_ref[...] = jnp.zeros(l_ref.shape, F32)
    acc_ref[...] = jnp.zeros(acc_ref.shape, F32)

    def attend(j, bias_of):
        ck = ckvn_ref[j]
        ckt = ckt_ref[j]
        mask_bias = sc_ref[j]
        lg_all = _dot_nt(ck, qlat_ref[0].reshape(DSA_HEADS * TQ, LANES))
        ps, alphas = [], []
        for h in range(DSA_HEADS):
            lg = lg_all[:, h * TQ:(h + 1) * TQ] + mask_bias
            bias = bias_of(h)
            if bias is not None:
                lg = lg + bias
            lg3 = lg.reshape(TK // 8, 8, TQ)
            m_old = m_ref[h]
            m_new = jnp.maximum(m_old, _sublane_all(jnp.max(lg3, axis=0), jnp.maximum))
            alpha = jnp.exp2(m_old - m_new)
            p3 = jnp.exp2(lg3 - m_new[None])
            l_ref[h] = alpha * l_ref[h] + jnp.sum(p3, axis=0)
            m_ref[h] = m_new
            ps.append(p3.reshape(TK, TQ).astype(BF16))
            alphas.append(alpha)
        pv_all = _dot(ckt, jnp.concatenate(ps, axis=1))
        for h in range(DSA_HEADS):
            acc3 = acc_ref[h].reshape(DSA_RKV // 8, 8, TQ) * alphas[h][None]
            acc_ref[h] = acc3.reshape(DSA_RKV, TQ) + pv_all[:, h * TQ:(h + 1) * TQ]

    def far(j, carry):
        attend(j, lambda h: None)
        return carry

    lax.fori_loop(0, jnp.maximum(i - 1, 0), far, 0)

    @pl.when(i >= 1)
    def _():
        attend(i - 1, lambda h: bnear_ref[h, 0])

    attend(i, lambda h: bnear_ref[h, 1])
    for h in range(DSA_HEADS):
        l_tot = _sublane_all(l_ref[h], jnp.add)
        o = (acc_ref[h].reshape(DSA_RKV // 8, 8, TQ) / l_tot[None]).reshape(DSA_RKV, TQ)
        o_ref[:, h * LANES:(h + 1) * LANES] = o.T.astype(BF16)


def _t5_bucket(rel):
    nb = REL_BUCKETS // 2
    max_exact = nb // 2
    base = jnp.where(rel > 0, nb, 0)
    n = jnp.abs(rel)
    nf = jnp.maximum(n, 1).astype(jnp.float32)
    large = max_exact + (jnp.log(nf / max_exact) / math.log(REL_MAX_DIST / max_exact)
                         * (nb - max_exact)).astype(jnp.int32)
    large = jnp.minimum(large, nb - 1)
    return base + jnp.where(n < max_exact, n, large)


def _bias_tables(rel_bias):
    s = jnp.arange(TK, dtype=jnp.int32)[:, None]
    t = jnp.arange(TQ, dtype=jnp.int32)[None, :]
    rel = jnp.stack([s - TK - t, s - t])
    table = rel_bias.astype(F32)
    near = jnp.transpose(table[_t5_bucket(rel)], (3, 0, 1, 2))
    far = table[_t5_bucket(jnp.full((1,), -REL_MAX_DIST, jnp.int32))[0]]
    return (near - far[:, None, None, None]) * LOG2E


def _search_tables(nq, k_sel):
    nd = statistics.NormalDist()
    tab = np.zeros((nq, 2, TQ), np.float32)
    for i in range(nq):
        for c in range(TQ // CHUNK):
            n = i * TQ + (c + 1) * CHUNK
            z = nd.inv_cdf(1.0 - (k_sel - 0.5) / n) if n > k_sel else 0.0
            tab[i, 0, c * CHUNK:(c + 1) * CHUNK] = z
            tab[i, 1, c * CHUNK:(c + 1) * CHUNK] = n * nd.pdf(z)
    return jnp.asarray(tab)


def _dsa_attention(qlat, qidx, wt, ckvn, ckt, kidx, bnear, B, S):
    nq = S // TQ
    N = B * S
    k_sel = min(TOPK_MAX, S // 4)
    zq = _search_tables(nq, k_sel)
    tri = jnp.asarray(np.tril(np.ones((TK, TK), np.float32)), dtype=BF16)
    stacked = pl.BlockSpec((1, DSA_HEADS, TQ, LANES), lambda b, i: (b * nq + i, 0, 0, 0))
    keys = pl.BlockSpec((nq, TK, LANES), lambda b, i: (b, 0, 0))
    keys_t = pl.BlockSpec((nq, LANES, TK), lambda b, i: (b, 0, 0))
    return pl.pallas_call(
        functools.partial(_dsa_kernel, k_sel=k_sel),
        grid=(B, nq),
        in_specs=[stacked, stacked, pl.BlockSpec((IDX_HEADS, TQ), lambda b, i: (0, b * nq + i)),
                  keys, keys_t, keys,
                  pl.BlockSpec(bnear.shape, lambda b, i: (0, 0, 0, 0)),
                  pl.BlockSpec((1, 2, TQ), lambda b, i: (i, 0, 0)),
                  pl.BlockSpec((TK, TK), lambda b, i: (0, 0))],
        out_specs=pl.BlockSpec((TQ, DSA_HEADS * DSA_RKV), lambda b, i: (b * nq + i, 0)),
        out_shape=jax.ShapeDtypeStruct((N, DSA_HEADS * DSA_RKV), BF16),
        scratch_shapes=[pltpu.VMEM((nq, TK, TQ), F32),
                        pltpu.VMEM((DSA_HEADS, 8, TQ), F32),
                        pltpu.VMEM((DSA_HEADS, 8, TQ), F32),
                        pltpu.VMEM((DSA_HEADS, DSA_RKV, TQ), F32)],
        compiler_params=_params(("parallel", "arbitrary")),
        name="dsa_attention",
    )(qlat, qidx, wt, ckvn, ckt, kidx, bnear, zq, tri)


def _merge_kernel(x_ref, ret_ref, olat_ref, mq_ref, gates_ref, km_ref, vm_ref,
                  wret_ref, wdsa_ref, wmem_ref, wout_ref, o_ref):
    mq = mq_ref[...]
    km = km_ref[0]
    vm = vm_ref[0]
    mem_b = jnp.zeros(o_ref.shape, F32)
    for h in range(MEM_HEADS):
        sl = slice(h * MEM_DH, (h + 1) * MEM_DH)
        lg = _dot_nt(mq[:, sl], km[:, sl]) * (MEM_DH ** -0.5)
        p = jnp.exp(lg - jnp.max(lg, axis=1, keepdims=True))
        p = p / jnp.sum(p, axis=1, keepdims=True)
        pv = _dot(p.astype(BF16), vm[:, sl])
        mem_b = mem_b + _dot(pv.astype(BF16), wmem_ref[sl, :])
    ret_b = _dot(ret_ref[...], wret_ref[...])
    dsa_b = _dot(olat_ref[...], wdsa_ref[...])
    g = jax.nn.sigmoid(gates_ref[...].astype(F32))
    merged = (g[:, :D_MODEL] * ret_b + g[:, D_MODEL:2 * D_MODEL] * dsa_b + g[:, 2 * D_MODEL:] * mem_b)
    o_ref[...] = x_ref[...] + _dot(merged.astype(BF16), wout_ref[...])


def _merge(x2d, ret, olat, mq, gates, km, vm, wret, wdsa, wmem, wout, S):
    N = x2d.shape[0]
    per_b = S // TM_MERGE
    row = lambda i: (i, 0)
    const = lambda i: (0, 0)
    memb = lambda i: (i // per_b, 0, 0)
    return pl.pallas_call(
        _merge_kernel,
        grid=(N // TM_MERGE,),
        in_specs=[pl.BlockSpec((TM_MERGE, D_MODEL), row), pl.BlockSpec((TM_MERGE, RET_W), row),
                  pl.BlockSpec((TM_MERGE, DSA_HEADS * DSA_RKV), row), pl.BlockSpec((TM_MERGE, MEM_W), row),
                  pl.BlockSpec((TM_MERGE, 3 * D_MODEL), row),
                  pl.BlockSpec((1, MEM_LEN, MEM_W), memb), pl.BlockSpec((1, MEM_LEN, MEM_W), memb),
                  pl.BlockSpec(wret.shape, const), pl.BlockSpec(wdsa.shape, const),
                  pl.BlockSpec(wmem.shape, const), pl.BlockSpec(wout.shape, const)],
        out_specs=pl.BlockSpec((TM_MERGE, D_MODEL), row),
        out_shape=jax.ShapeDtypeStruct(x2d.shape, F32),
        compiler_params=_params(("parallel",)),
        name="merge_out",
    )(x2d, ret, olat, mq, gates, km, vm, wret, wdsa, wmem, wout)


def _mlp_kernel(x_ref, g_ref, w1_ref, w2_ref, gf_ref, o_ref, *, final):
    x = x_ref[...]
    h = _rms(x, g_ref[...]).astype(BF16)
    acc = jnp.zeros(x.shape, F32)
    for c in range(D_FF // D_MODEL):
        sl = slice(c * D_MODEL, (c + 1) * D_MODEL)
        a = jnp.maximum(_dot(h, w1_ref[:, sl]), 0.0)
        acc = acc + _dot((a * a).astype(BF16), w2_ref[sl, :])
    y = x + acc
    if final:
        y = _rms(y, gf_ref[...])
    o_ref[...] = y


def _mlp(x2d, g, w1, w2, gf, final):
    N = x2d.shape[0]
    row = lambda i: (i, 0)
    const = lambda i: (0, 0)
    return pl.pallas_call(
        functools.partial(_mlp_kernel, final=final),
        grid=(N // TM_MLP,),
        in_specs=[pl.BlockSpec((TM_MLP, D_MODEL), row), pl.BlockSpec((1, D_MODEL), const),
                  pl.BlockSpec(w1.shape, const), pl.BlockSpec(w2.shape, const),
                  pl.BlockSpec((1, D_MODEL), const)],
        out_specs=pl.BlockSpec((TM_MLP, D_MODEL), row),
        out_shape=jax.ShapeDtypeStruct(x2d.shape, F32),
        compiler_params=_params(("parallel",)),
        name="mlp",
    )(x2d, g, w1, w2, gf)


def _rope_tables(S):
    half = RET_DK // 2
    pos = jnp.arange(S, dtype=F32)
    freqs = ROPE_BASE ** (-jnp.arange(half, dtype=F32) / half)
    ang = pos[:, None] * freqs[None, :]
    return jnp.tile(jnp.cos(ang), (1, RET_HEADS)), jnp.tile(jnp.sin(ang), (1, RET_HEADS))


def _in_weights(wi, w_iq_l):
    half = RET_DK // 2
    perm = (np.arange(RET_HEADS)[None, :, None] * RET_DK + np.arange(2)[:, None, None] * half
            + np.arange(half)[None, None, :]).reshape(-1)
    offs = np.concatenate([[0], np.cumsum(IN_SIZES)])
    seg = lambda k: wi[:, int(offs[k]):int(offs[k + 1])]
    wa = jnp.concatenate([seg(0)[:, perm], seg(1)[:, perm], seg(2), seg(3), seg(8), seg(9)], axis=1).astype(BF16)
    pad = jnp.zeros((D_MODEL, LANES - IDX_HEADS), F32)
    widx = jnp.concatenate([seg(4), seg(5), jnp.tile(seg(6), (1, 4)), seg(7), pad], axis=1)
    wih, wil = _split(widx)
    wiq = jnp.tile(w_iq_l[:, :, None, :], (1, 1, 4, 1)).reshape(DSA_RQ, IDX_HEADS * LANES)
    wiqh, wiql = _split(wiq)
    return wa, wih, wil, wiqh, wiql


def kernel(x, mem, norm1, w_in, q_norm, kv_norm, w_uq, w_iq, w_uk, w_uv, mem_norm, w_mem_kv,
           w_ret_o, w_dsa_o, w_mem_o, w_out, norm2, w_ff1, w_ff2, rel_bias, final_norm):
    B, S, D = x.shape
    depth = w_in.shape[0]
    assert D == D_MODEL and S % TM_MLP == 0 and S % TQ == 0 and TQ == TK
    cos_t, sin_t = _rope_tables(S)
    bnear = _bias_tables(rel_bias)
    wql_all, wdsa_all = _fold_weights(w_uq, w_uk, w_uv, w_dsa_o)
    x2d = x.reshape(B * S, D)
    row = lambda v: v.reshape(1, -1)
    for l in range(depth):
        wa, wih, wil, wiqh, wiql = _in_weights(w_in[l], w_iq[l])
        rq, rk, rv, rg, mq, gates, idx = _in_proj(x2d, row(norm1[l]), cos_t, sin_t, wa, wih, wil, S)
        ret = _retention(rq, rk, rv, rg, B, S)
        qlat, qidx, wt, ckvn, ckt, kidx = _dsa_prep(idx, row(q_norm[l]), row(kv_norm[l]), wql_all[l], wiqh, wiql)
        olat = _dsa_attention(qlat, qidx, wt, ckvn, ckt, kidx, bnear, B, S)
        km, vm = _mem_kv(mem, row(mem_norm[l]), w_mem_kv[l].astype(BF16))
        x2d = _merge(x2d, ret, olat, mq, gates, km, vm, w_ret_o[l].astype(BF16), wdsa_all[l],
                     w_mem_o[l].astype(BF16), w_out[l].astype(BF16), S)
        x2d = _mlp(x2d, row(norm2[l]), w_ff1[l].astype(BF16), w_ff2[l].astype(BF16), row(final_norm),
                   final=(l == depth - 1))
    return x2d.reshape(B, S, D)
```

```python
import functools
import math
import statistics

import numpy as np
import jax
import jax.numpy as jnp
from jax import lax
from jax.experimental import pallas as pl
from jax.experimental.pallas import tpu as pltpu

F32 = jnp.float32
BF16 = jnp.bfloat16

D_MODEL = 1024
CHUNK = 64
EPS = 1e-6
GN_EPS = 1e-5
RET_HEADS = 8
RET_DK = 64
RET_W = 512
ROPE_BASE = 10000.0
DSA_HEADS = 8
DSA_DH = 64
DSA_RQ = 256
DSA_RKV = 128
IDX_HEADS = 8
IDX_DIM = 32
TOPK_MAX = 256
MEM_LEN = 256
MEM_HEADS = 4
MEM_DH = 128
MEM_W = 512
REL_BUCKETS = 32
REL_MAX_DIST = 128
D_FF = 4096
IN_SIZES = (512, 512, 512, 512, DSA_RQ, DSA_RKV, IDX_DIM, IDX_HEADS, MEM_W, 3 * D_MODEL)

LANES = 128
IDX_COLS = 640
NEG = -1e30
LOG2E = math.log2(math.e)
F32_MAX = float(np.finfo(np.float32).max)
VMEM_LIMIT = 56 * 1024 * 1024

TM_IN = 256
T_RET = 256
TQ = 256
TK = 256
SEARCH_SWITCH = 12
SEARCH_CAP = SEARCH_SWITCH + 34
TM_MERGE = 512
TM_MLP = 512


def _dot(a, b):
    return jnp.dot(a, b, preferred_element_type=F32)


def _dot_nt(a, b):
    return lax.dot_general(a, b, (((1,), (1,)), ((), ())), preferred_element_type=F32)


def _split(a):
    hi = a.astype(BF16)
    lo = (a - hi.astype(F32)).astype(BF16)
    return hi, lo


def _rms(x, g):
    return x * lax.rsqrt(jnp.mean(x * x, axis=-1, keepdims=True) + EPS) * g


def _params(sem):
    return pltpu.CompilerParams(dimension_semantics=sem, vmem_limit_bytes=VMEM_LIMIT)


def _fold_kernel(uq_ref, uk_ref, uv_ref, wo_ref, wql_ref, wdsa_ref):
    uq_h, uq_l = _split(uq_ref[0, 0])
    uk_h, uk_l = _split(uk_ref[0, 0])
    ql = _dot_nt(uq_h, uk_h) + _dot_nt(uq_h, uk_l) + _dot_nt(uq_l, uk_h)
    wql_ref[0] = (ql * (DSA_DH ** -0.5 * LOG2E)).astype(BF16)
    uv_h, uv_l = _split(uv_ref[0, 0])
    wo_h, wo_l = _split(wo_ref[0, 0])
    wdsa_ref[0] = (_dot(uv_h, wo_h) + _dot(uv_h, wo_l) + _dot(uv_l, wo_h)).astype(BF16)


def _fold_weights(w_uq, w_uk, w_uv, w_dsa_o):
    L = w_uq.shape[0]
    uq = jnp.transpose(w_uq, (0, 2, 1, 3))
    uk = jnp.transpose(w_uk, (0, 2, 1, 3))
    uv = jnp.transpose(w_uv, (0, 2, 1, 3))
    wo = w_dsa_o.reshape(L, DSA_HEADS, DSA_DH, D_MODEL)
    return pl.pallas_call(
        _fold_kernel,
        grid=(L, DSA_HEADS),
        in_specs=[
            pl.BlockSpec((1, 1, DSA_RQ, DSA_DH), lambda l, h: (l, h, 0, 0)),
            pl.BlockSpec((1, 1, DSA_RKV, DSA_DH), lambda l, h: (l, h, 0, 0)),
            pl.BlockSpec((1, 1, DSA_RKV, DSA_DH), lambda l, h: (l, h, 0, 0)),
            pl.BlockSpec((1, 1, DSA_DH, D_MODEL), lambda l, h: (l, h, 0, 0)),
        ],
        out_specs=[
            pl.BlockSpec((1, DSA_RQ, DSA_RKV), lambda l, h: (l, 0, h)),
            pl.BlockSpec((1, DSA_RKV, D_MODEL), lambda l, h: (l, h, 0)),
        ],
        out_shape=[
            jax.ShapeDtypeStruct((L, DSA_RQ, DSA_HEADS * DSA_RKV), BF16),
            jax.ShapeDtypeStruct((L, DSA_HEADS * DSA_RKV, D_MODEL), BF16),
        ],
        compiler_params=_params(("parallel", "parallel")),
        name="fold_weights",
    )(uq, uk, uv, wo)


def _in_kernel(x_ref, g_ref, cos_ref, sin_ref, wa_ref, wih_ref, wil_ref,
               rq_ref, rk_ref, rv_ref, rg_ref, mq_ref, gates_ref, idx_ref):
    h = _rms(x_ref[...], g_ref[...])
    hb, hl = _split(h)
    c = cos_ref[...]
    s = sin_ref[...]
    half = RET_W // 2

    def rope_store(ref, z, scale):
        x1 = z[:, :half]
        x2 = z[:, half:]
        ref[:, :half] = ((x1 * c - x2 * s) * scale).astype(BF16)
        ref[:, half:] = ((x2 * c + x1 * s) * scale).astype(BF16)

    rope_store(rq_ref, _dot(hb, wa_ref[:, 0:512]), 1.0)
    rope_store(rk_ref, _dot(hb, wa_ref[:, 512:1024]), RET_DK ** -0.5)
    rv_ref[...] = _dot(hb, wa_ref[:, 1024:1536]).astype(BF16)
    rg_ref[...] = _dot(hb, wa_ref[:, 1536:2048]).astype(BF16)
    mq_ref[...] = _dot(hb, wa_ref[:, 2048:2560]).astype(BF16)
    for j in range(6):
        gates_ref[:, j * 512:(j + 1) * 512] = _dot(hb, wa_ref[:, 2560 + j * 512:3072 + j * 512]).astype(BF16)
    idx_ref[...] = _dot(hb, wih_ref[...]) + _dot(hl, wih_ref[...]) + _dot(hb, wil_ref[...])


def _in_proj(x2d, g, cos_t, sin_t, wa, wih, wil, S):
    N = x2d.shape[0]
    n_pos = S // TM_IN
    row = lambda i: (i, 0)
    const = lambda i: (0, 0)
    pos = lambda i: (i % n_pos, 0)
    bf = lambda w: jax.ShapeDtypeStruct((N, w), BF16)
    return pl.pallas_call(
        _in_kernel,
        grid=(N // TM_IN,),
        in_specs=[
            pl.BlockSpec((TM_IN, D_MODEL), row),
            pl.BlockSpec((1, D_MODEL), const),
            pl.BlockSpec((TM_IN, RET_W // 2), pos),
            pl.BlockSpec((TM_IN, RET_W // 2), pos),
            pl.BlockSpec(wa.shape, const),
            pl.BlockSpec(wih.shape, const),
            pl.BlockSpec(wil.shape, const),
        ],
        out_specs=[
            pl.BlockSpec((TM_IN, 512), row), pl.BlockSpec((TM_IN, 512), row),
            pl.BlockSpec((TM_IN, 512), row), pl.BlockSpec((TM_IN, 512), row),
            pl.BlockSpec((TM_IN, 512), row), pl.BlockSpec((TM_IN, 3 * D_MODEL), row),
            pl.BlockSpec((TM_IN, IDX_COLS), row),
        ],
        out_shape=[bf(512), bf(512), bf(512), bf(512), bf(512), bf(3 * D_MODEL),
                   jax.ShapeDtypeStruct((N, IDX_COLS), F32)],
        compiler_params=_params(("parallel",)),
        name="in_proj",
    )(x2d, g, cos_t, sin_t, wa, wih, wil)


def _ret_kernel(rq_ref, rk_ref, rv_ref, rg_ref, dmat_ref, qd_ref, kd_ref, hq_ref, hv_ref,
                cdec_ref, bd_ref, p_ref, o_ref, s_ref):
    @pl.when(pl.program_id(1) == 0)
    def _():
        s_ref[...] = jnp.zeros_like(s_ref)

    q = rq_ref[...]
    k = rk_ref[...]
    v = rv_ref[...]
    def head_lane_tiles(a, h):
        c = h // (LANES // (RET_DK // 2))
        return jnp.concatenate([a[:, c * LANES:(c + 1) * LANES], a[:, (2 + c) * LANES:(3 + c) * LANES]], axis=1)

    o_cols = []
    for pair in range(RET_HEADS // 2):
        cols = slice(pair * LANES, (pair + 1) * LANES)
        o_pair = jnp.zeros((q.shape[0], LANES), F32)
        for h in (2 * pair, 2 * pair + 1):
            sc = _dot_nt(head_lane_tiles(q * hq_ref[h], h), head_lane_tiles(k, h)) * dmat_ref[h]
            o_pair = o_pair + _dot(sc.astype(BF16), v[:, cols]) * hv_ref[h][:, cols]
        o_cols.append(o_pair)
    o = jnp.concatenate(o_cols, axis=1)
    state = s_ref[...]
    qf = q.astype(F32) * qd_ref[...]
    o = o + _dot(qf.astype(BF16), state.astype(BF16))
    kf = k.astype(F32) * kd_ref[...]
    kv = _dot(kf.T.astype(BF16), v)
    s_ref[...] = state * cdec_ref[...] + kv * bd_ref[...]

    p = p_ref[...]

    oh, ol = _split(o)
    d = o - (_dot(oh, p) + _dot(ol, p))
    y = d * lax.rsqrt(_dot((d * d).astype(BF16), p) + GN_EPS)
    g = rg_ref[...].astype(F32)
    o_ref[...] = (g * jax.nn.sigmoid(g) * y).astype(BF16)


def _ret_consts():
    T = T_RET
    hh = np.arange(RET_HEADS, dtype=np.float64)
    log_g = np.log1p(-np.exp2(-5.0 - hh))
    t = np.arange(T)
    ct = t // CHUNK
    diff = (t[:, None] - t[None, :]).astype(np.float64)
    same = ct[:, None] == ct[None, :]
    past = ct[None, :] < ct[:, None]
    expo = np.where(same, np.abs(diff), diff)
    dmat = np.where((same | past)[None], np.exp(log_g[:, None, None] * expo[None]), 0.0)
    lane = np.arange(RET_W)
    hk = (lane % (RET_W // 2)) // (RET_DK // 2)
    hv = lane // RET_DK
    qd = np.exp(log_g[hk][None, :] * (t[:, None] + 1.0))
    kd = np.exp(log_g[hk][None, :] * (T - 1.0 - t[:, None]))
    hq_mask = (hk[None, :] == np.arange(RET_HEADS)[:, None]).astype(np.float32)[:, None, :]
    hv_mask = (hv[None, :] == np.arange(RET_HEADS)[:, None]).astype(np.float32)[:, None, :]
    bd = (hk[:, None] == hv[None, :]).astype(np.float32)
    cdec = np.broadcast_to(np.exp(log_g[hk] * T)[:, None], (RET_W, RET_W))
    pmat = (hv[:, None] == hv[None, :]).astype(np.float32) / RET_DK
    f = lambda a: jnp.asarray(np.asarray(a, dtype=np.float32))
    return (f(dmat), f(qd), f(kd), jnp.asarray(hq_mask, dtype=BF16), f(hv_mask), f(cdec), f(bd),
            jnp.asarray(pmat, dtype=BF16))


def _retention(rq, rk, rv, rg, B, S):
    consts = _ret_consts()
    nb = S // T_RET
    tok = lambda b, j: (b * nb + j, 0)
    full = lambda a: pl.BlockSpec(a.shape, lambda b, j: (0,) * a.ndim)
    return pl.pallas_call(
        _ret_kernel,
        grid=(B, nb),
        in_specs=[pl.BlockSpec((T_RET, RET_W), tok)] * 4 + [full(a) for a in consts],
        out_specs=pl.BlockSpec((T_RET, RET_W), tok),
        out_shape=jax.ShapeDtypeStruct(rq.shape, BF16),
        scratch_shapes=[pltpu.VMEM((RET_W, RET_W), F32)],
        compiler_params=_params(("parallel", "arbitrary")),
        name="retention",
    )(rq, rk, rv, rg, *consts)


def _memkv_kernel(mem_ref, g_ref, w_ref, k_ref, v_ref):
    mn = _rms(mem_ref[0], g_ref[...]).astype(BF16)
    kv = _dot(mn, w_ref[...])
    k_ref[0] = kv[:, :MEM_W].astype(BF16)
    v_ref[0] = kv[:, MEM_W:].astype(BF16)


def _mem_kv(mem, g, w):
    B = mem.shape[0]
    return pl.pallas_call(
        _memkv_kernel,
        grid=(B,),
        in_specs=[pl.BlockSpec((1, MEM_LEN, D_MODEL), lambda b: (b, 0, 0)),
                  pl.BlockSpec((1, D_MODEL), lambda b: (0, 0)),
                  pl.BlockSpec(w.shape, lambda b: (0, 0))],
        out_specs=[pl.BlockSpec((1, MEM_LEN, MEM_W), lambda b: (b, 0, 0))] * 2,
        out_shape=[jax.ShapeDtypeStruct((B, MEM_LEN, MEM_W), BF16)] * 2,
        compiler_params=_params(("parallel",)),
        name="mem_kv",
    )(mem, g, w)


def _prep_kernel(idx_ref, qn_ref, kvn_ref, wql_ref, wiqh_ref, wiql_ref,
                 qlat_ref, qidx_ref, wt_ref, ckvn_ref, ckt_ref, kidx_ref):
    z = idx_ref[...]
    cqn = _rms(z[:, :DSA_RQ], qn_ref[...])
    cb, cl = _split(cqn)
    ql = _dot(cb, wql_ref[...])
    y = _dot(cb, wiqh_ref[...]) + _dot(cl, wiqh_ref[...]) + _dot(cb, wiql_ref[...])
    yh = y.astype(BF16).astype(F32)
    lane = lax.broadcasted_iota(jnp.int32, y.shape, 1)
    qsel = jnp.where((lane % LANES) < 2 * IDX_DIM, yh, y - yh).astype(BF16)
    for h in range(DSA_HEADS):
        qlat_ref[0, h] = ql[:, h * LANES:(h + 1) * LANES].astype(BF16)
        qidx_ref[0, h] = qsel[:, h * LANES:(h + 1) * LANES]
    ckn = _rms(z[:, DSA_RQ:DSA_RQ + DSA_RKV], kvn_ref[...])
    ckvn_ref[0] = ckn.astype(BF16)
    ckt_ref[0] = ckn.T.astype(BF16)
    ik = z[:, 384:512]
    ikh = ik.astype(BF16).astype(F32)
    lane1 = lax.broadcasted_iota(jnp.int32, ik.shape, 1)
    kidx_ref[0] = jnp.where((lane1 // IDX_DIM) % 2 == 0, ikh, ik - ikh).astype(BF16)
    wfull = z[:, 512:640] * ((IDX_HEADS ** -0.5) * (IDX_DIM ** -0.5))
    wt_ref[...] = wfull.T[:IDX_HEADS, :]


def _dsa_prep(idx, qn, kvn, wql, wiqh, wiql):
    N = idx.shape[0]
    nt = N // TQ
    const = lambda i: (0, 0)
    tile3 = lambda i: (i, 0, 0)
    stacked = pl.BlockSpec((1, DSA_HEADS, TQ, LANES), lambda i: (i, 0, 0, 0))
    return pl.pallas_call(
        _prep_kernel,
        grid=(nt,),
        in_specs=[pl.BlockSpec((TQ, IDX_COLS), lambda i: (i, 0)),
                  pl.BlockSpec((1, DSA_RQ), const), pl.BlockSpec((1, DSA_RKV), const),
                  pl.BlockSpec(wql.shape, const), pl.BlockSpec(wiqh.shape, const),
                  pl.BlockSpec(wiql.shape, const)],
        out_specs=[stacked, stacked, pl.BlockSpec((IDX_HEADS, TQ), lambda i: (0, i)),
                   pl.BlockSpec((1, TQ, LANES), tile3), pl.BlockSpec((1, LANES, TQ), tile3),
                   pl.BlockSpec((1, TQ, LANES), tile3)],
        out_shape=[jax.ShapeDtypeStruct((nt, DSA_HEADS, TQ, LANES), BF16),
                   jax.ShapeDtypeStruct((nt, DSA_HEADS, TQ, LANES), BF16),
                   jax.ShapeDtypeStruct((IDX_HEADS, N), F32),
                   jax.ShapeDtypeStruct((nt, TQ, LANES), BF16),
                   jax.ShapeDtypeStruct((nt, LANES, TQ), BF16),
                   jax.ShapeDtypeStruct((nt, TQ, LANES), BF16)],
        compiler_params=_params(("parallel",)),
        name="dsa_prep",
    )(idx, qn, kvn, wql, wiqh, wiql)


def _key_to_f32(key):
    bits = jnp.where(key >= 0, key, key ^ jnp.int32(0x7FFFFFFF))
    return lax.bitcast_convert_type(bits, F32)


def _f32_to_key(x):
    bits = lax.bitcast_convert_type(x, jnp.int32)
    return jnp.where(bits >= 0, bits, bits ^ jnp.int32(0x7FFFFFFF))


def _sublane_all(a, op):
    for shift in (4, 2, 1):
        a = op(a, pltpu.roll(a, shift, 0))
    return a


def _dsa_kernel(qlat_ref, qidx_ref, wt_ref, ckvn_ref, ckt_ref, kidx_ref, bnear_ref, zq_ref, tri_ref, o_ref,
                sc_ref, m_ref, l_ref, acc_ref, *, k_sel):
    i = pl.program_id(1)
    wt = wt_ref[...]

    def score_tile(j):
        y_all = _dot_nt(kidx_ref[j], qidx_ref[0].reshape(IDX_HEADS * TQ, LANES))
        tot = jnp.zeros((TK, TQ), F32)
        for h in range(IDX_HEADS):
            tot = tot + jnp.maximum(y_all[:, h * TQ:(h + 1) * TQ], 0.0) * wt[h:h + 1, :]
        return tot

    def colsum8(a):
        return jnp.sum(a.reshape(TK // 8, 8, TQ), axis=0)

    def ones_where(cond_):
        return jnp.where(cond_, 1.0, 0.0)

    def stats(s_counted, s_summed):
        return (colsum8(s_summed), colsum8(s_summed * s_summed),
                colsum8(ones_where(s_counted >= 0.0)), colsum8(ones_where(s_counted > 0.0)))

    def fill(j, carry):
        s = score_tile(j)
        sc_ref[j] = s
        return tuple(a + b for a, b in zip(carry, stats(s, s)))

    zero8 = jnp.zeros((8, TQ), F32)
    carry = lax.fori_loop(0, i, fill, (zero8, zero8, zero8, zero8))
    key_r = lax.broadcasted_iota(jnp.int32, (TK, TQ), 0)
    qry_c = lax.broadcasted_iota(jnp.int32, (TK, TQ), 1)
    admissible = key_r <= (qry_c // CHUNK) * CHUNK + (CHUNK - 1)
    s_diag = score_tile(i)
    s_masked = jnp.where(admissible, s_diag, -jnp.inf)
    sc_ref[i] = s_masked
    carry = tuple(a + b for a, b in zip(carry, stats(s_masked, jnp.where(admissible, s_diag, 0.0))))
    s1, s2, c_ge0, c_gt0 = (_sublane_all(a, jnp.add)[0:1, :] for a in carry)

    def count_ge(thr):
        def body(j, c):
            return c + colsum8(ones_where(sc_ref[j] >= thr))
        c8 = lax.fori_loop(0, i + 1, body, zero8)
        return jnp.sum(c8, axis=0, keepdims=True)

    qc = lax.broadcasted_iota(jnp.int32, (1, TQ), 1)
    n_adm = (i * TQ + (qc // CHUNK + 1) * CHUNK).astype(F32)
    kf = jnp.float32(k_sel)
    key_min = jnp.int32(np.int32(np.array(-F32_MAX, np.float32).view(np.int32)) ^ np.int32(0x7FFFFFFF))
    key_max = jnp.int32(np.array(np.inf, np.float32).view(np.int32))
    inf = jnp.float32(np.inf)
    zq = zq_ref[0]
    mu = s1 / n_adm
    sd = jnp.sqrt(jnp.maximum(s2 / n_adm - mu * mu, 1e-30))
    dens = zq[1:2, :] / sd
    pos = c_gt0 >= kf
    tie0 = jnp.logical_and(jnp.logical_not(pos), c_ge0 >= kf)
    small = n_adm <= kf
    lo0 = jnp.where(pos | tie0, 0, key_min)
    hi0 = jnp.where(pos, key_max, jnp.where(tie0, 1, 0))
    lo0 = jnp.where(small, key_min, lo0)
    hi0 = jnp.where(small, key_min + 1, hi0)
    lf0 = jnp.where(pos | tie0, 0.0, -inf)
    hf0 = jnp.where(pos, inf, 0.0)
    cl0 = jnp.where(pos | tie0, c_ge0, n_adm)
    ch0 = jnp.where(pos, 0.0, jnp.where(tie0, c_gt0, c_ge0))
    t0 = mu + zq[0:1, :] * sd
    one = jnp.ones((1, TQ), F32)

    def mid_key(lo, hi):
        return (lo & hi) + ((lo ^ hi) >> 1)

    def unfinished(lo, hi, cl):
        return jnp.logical_not((cl == kf) | (mid_key(lo, hi) == lo))

    def cond(st):
        it, lo, hi, cl = st[0], st[1], st[2], st[5]
        return jnp.logical_and(it < SEARCH_CAP, jnp.sum(ones_where(unfinished(lo, hi, cl))) > 0.0)

    def step(st):
        it, lo, hi, lf, hf, cl, ch, t, last, wl, wh = st
        act = unfinished(lo, hi, cl)
        guided = (it < SEARCH_SWITCH).astype(jnp.int32)
        mid = mid_key(lo, hi)
        tk = mid + (_f32_to_key(t) - mid) * guided
        tk = jnp.minimum(jnp.maximum(tk, lo + 1), hi - 1)
        tf = _key_to_f32(tk)
        c = count_ge(tf)
        ge = c >= kf
        up_lo = act & ge
        up_hi = act & jnp.logical_not(ge)
        lo = jnp.where(up_lo, tk, lo)
        lf = jnp.where(up_lo, tf, lf)
        cl = jnp.where(up_lo, c, cl)
        hi = jnp.where(up_hi, tk, hi)
        hf = jnp.where(up_hi, tf, hf)
        ch = jnp.where(up_hi, c, ch)
        side = jnp.where(ge, 1.0, -1.0)
        same = side == last
        wh = jnp.where(ge, jnp.where(same, wh * 0.5, one), one)
        wl = jnp.where(ge, one, jnp.where(same, wl * 0.5, one))
        a = (cl - kf + 0.5) * wl
        b = (kf - 0.5 - ch) * wh
        t_bracket = lf + (hf - lf) * (a / (a + b))
        t_model = tf + 1.5 * (c - kf + jnp.where(ge, 0.5, -0.5)) / dens
        bracketed = (lf > -inf) & (hf < inf)
        t = jnp.where(bracketed, t_bracket, t_model)
        return it + 1, lo, hi, lf, hf, cl, ch, t, side, wl, wh

    st = lax.while_loop(cond, step, (jnp.int32(0), lo0, hi0, lf0, hf0, cl0, ch0, t0,
                                     jnp.zeros((1, TQ), F32), one, one))
    lo, cl, ch = st[1], st[5], st[6]
    thr = _key_to_f32(lo)
    tied = (cl != kf) & jnp.logical_not(small)
    need = jnp.where(tied, kf - ch, jnp.float32(1e9))

    def mask_plain(j, carry_):
        sc_ref[j] = jnp.where(sc_ref[j] >= thr, 0.0, NEG)
        return carry_

    def mask_ranked(j, before):
        s = sc_ref[j]
        eq = ones_where(s == thr)
        rank = _dot(tri_ref[...], eq.astype(BF16)) + before
        keep = (s > thr) | ((s == thr) & (rank <= need))
        sc_ref[j] = jnp.where(keep, 0.0, NEG)
        return before + jnp.sum(eq, axis=0, keepdims=True)

    any_tied = jnp.sum(ones_where(tied)) > 0.0

    @pl.when(any_tied)
    def _():
        lax.fori_loop(0, i + 1, mask_ranked, jnp.zeros((1, TQ), F32))

    @pl.when(jnp.logical_not(any_tied))
    def _():
        lax.fori_loop(0, i + 1, mask_plain, 0)

    m_ref[...] = jnp.full(m_ref.shape, NEG, F32)
    l_ref[...] = jnp.zeros(l_ref.shape, F32)
    acc_ref[...] = jnp.zeros(acc_ref.shape, F32)

    def attend(tiles):
        nk = len(tiles) * TK
        ck = jnp.concatenate([ckvn_ref[j] for j, _ in tiles], axis=0)
        ckt = jnp.concatenate([ckt_ref[j] for j, _ in tiles], axis=1)
        mask_bias = jnp.concatenate([sc_ref[j] for j, _ in tiles], axis=0)
        lg_all = _dot_nt(ck, qlat_ref[0].reshape(DSA_HEADS * TQ, LANES))
        ps, alphas = [], []
        for h in range(DSA_HEADS):
            bias = mask_bias
            if any(slot is not None for _, slot in tiles):
                bias = bias + jnp.concatenate(
                    [jnp.zeros((TK, TQ), F32) if slot is None else bnear_ref[h, slot] for _, slot in tiles], axis=0)
            lg3 = (lg_all[:, h * TQ:(h + 1) * TQ] + bias).reshape(nk // 8, 8, TQ)
            m_old = m_ref[h]
            m_new = jnp.maximum(m_old, _sublane_all(jnp.max(lg3, axis=0), jnp.maximum))
            alpha = jnp.exp2(m_old - m_new)
            p3 = jnp.exp2(lg3 - m_new[None])
            l_ref[h] = alpha * l_ref[h] + jnp.sum(p3, axis=0)
            m_ref[h] = m_new
            pv = _dot(ckt, p3.reshape(nk, TQ).astype(BF16))
            acc3 = acc_ref[h].reshape(DSA_RKV // 8, 8, TQ) * alpha[None]
            acc_ref[h] = acc3.reshape(DSA_RKV, TQ) + pv

    n_far = jnp.maximum(i - 1, 0)

    def far_pair(jj, carry):
        attend([(2 * jj, None), (2 * jj + 1, None)])
        return carry

    lax.fori_loop(0, n_far // 2, far_pair, 0)

    @pl.when(n_far % 2 == 1)
    def _():
        attend([(n_far - 1, None)])

    @pl.when(i >= 1)
    def _():
        attend([(i - 1, 0), (i, 1)])

    @pl.when(i == 0)
    def _():
        attend([(i, 1)])


    for h in range(DSA_HEADS):
        l_tot = _sublane_all(l_ref[h], jnp.add)
        o = (acc_ref[h].reshape(DSA_RKV // 8, 8, TQ) / l_tot[None]).reshape(DSA_RKV, TQ)
        o_ref[:, h * LANES:(h + 1) * LANES] = o.T.astype(BF16)


def _t5_bucket(rel):
    nb = REL_BUCKETS // 2
    max_exact = nb // 2
    base = jnp.where(rel > 0, nb, 0)
    n = jnp.abs(rel)
    nf = jnp.maximum(n, 1).astype(jnp.float32)
    large = max_exact + (jnp.log(nf / max_exact) / math.log(REL_MAX_DIST / max_exact)
                         * (nb - max_exact)).astype(jnp.int32)
    large = jnp.minimum(large, nb - 1)
    return base + jnp.where(n < max_exact, n, large)


def _bias_tables(rel_bias):
    s = jnp.arange(TK, dtype=jnp.int32)[:, None]
    t = jnp.arange(TQ, dtype=jnp.int32)[None, :]
    rel = jnp.stack([s - TK - t, s - t])
    table = rel_bias.astype(F32)
    near = jnp.transpose(table[_t5_bucket(rel)], (3, 0, 1, 2))
    far = table[_t5_bucket(jnp.full((1,), -REL_MAX_DIST, jnp.int32))[0]]
    return (near - far[:, None, None, None]) * LOG2E


def _search_tables(nq, k_sel):
    nd = statistics.NormalDist()
    tab = np.zeros((nq, 2, TQ), np.float32)
    for i in range(nq):
        for c in range(TQ // CHUNK):
            n = i * TQ + (c + 1) * CHUNK
            z = nd.inv_cdf(1.0 - (k_sel - 0.5) / n) if n > k_sel else 0.0
            tab[i, 0, c * CHUNK:(c + 1) * CHUNK] = z
            tab[i, 1, c * CHUNK:(c + 1) * CHUNK] = n * nd.pdf(z)
    return jnp.asarray(tab)


def _dsa_attention(qlat, qidx, wt, ckvn, ckt, kidx, bnear, B, S):
    nq = S // TQ
    N = B * S
    k_sel = min(TOPK_MAX, S // 4)
    zq = _search_tables(nq, k_sel)
    tri = jnp.asarray(np.tril(np.ones((TK, TK), np.float32)), dtype=BF16)
    stacked = pl.BlockSpec((1, DSA_HEADS, TQ, LANES), lambda b, i: (b * nq + i, 0, 0, 0))
    keys = pl.BlockSpec((nq, TK, LANES), lambda b, i: (b, 0, 0))
    keys_t = pl.BlockSpec((nq, LANES, TK), lambda b, i: (b, 0, 0))
    return pl.pallas_call(
        functools.partial(_dsa_kernel, k_sel=k_sel),
        grid=(B, nq),
        in_specs=[stacked, stacked, pl.BlockSpec((IDX_HEADS, TQ), lambda b, i: (0, b * nq + i)),
                  keys, keys_t, keys,
                  pl.BlockSpec(bnear.shape, lambda b, i: (0, 0, 0, 0)),
                  pl.BlockSpec((1, 2, TQ), lambda b, i: (i, 0, 0)),
                  pl.BlockSpec((TK, TK), lambda b, i: (0, 0))],
        out_specs=pl.BlockSpec((TQ, DSA_HEADS * DSA_RKV), lambda b, i: (b * nq + i, 0)),
        out_shape=jax.ShapeDtypeStruct((N, DSA_HEADS * DSA_RKV), BF16),
        scratch_shapes=[pltpu.VMEM((nq, TK, TQ), F32),
                        pltpu.VMEM((DSA_HEADS, 8, TQ), F32),
                        pltpu.VMEM((DSA_HEADS, 8, TQ), F32),
                        pltpu.VMEM((DSA_HEADS, DSA_RKV, TQ), F32)],
        compiler_params=_params(("parallel", "arbitrary")),
        name="dsa_attention",
    )(qlat, qidx, wt, ckvn, ckt, kidx, bnear, zq, tri)


def _merge_kernel(x_ref, ret_ref, olat_ref, mq_ref, gates_ref, km_ref, vm_ref,
                  wret_ref, wdsa_ref, wmem_ref, wout_ref, o_ref):
    mq = mq_ref[...]
    km = km_ref[0]
    vm = vm_ref[0]
    pvs = []
    for h in range(MEM_HEADS):
        sl = slice(h * MEM_DH, (h + 1) * MEM_DH)
        lg = _dot_nt(mq[:, sl], km[:, sl]) * (MEM_DH ** -0.5)
        p = jnp.exp(lg - jnp.max(lg, axis=1, keepdims=True))
        p = p / jnp.sum(p, axis=1, keepdims=True)
        pvs.append(_dot(p.astype(BF16), vm[:, sl]).astype(BF16))
    mem_b = _dot(jnp.concatenate(pvs, axis=1), wmem_ref[...])
    ret_b = _dot(ret_ref[...], wret_ref[...])
    dsa_b = _dot(olat_ref[...], wdsa_ref[...])
    g = jax.nn.sigmoid(gates_ref[...].astype(F32))
    merged = (g[:, :D_MODEL] * ret_b + g[:, D_MODEL:2 * D_MODEL] * dsa_b + g[:, 2 * D_MODEL:] * mem_b)
    o_ref[...] = x_ref[...] + _dot(merged.astype(BF16), wout_ref[...])


def _merge(x2d, ret, olat, mq, gates, km, vm, wret, wdsa, wmem, wout, S):
    N = x2d.shape[0]
    per_b = S // TM_MERGE
    row = lambda i: (i, 0)
    const = lambda i: (0, 0)
    memb = lambda i: (i // per_b, 0, 0)
    return pl.pallas_call(
        _merge_kernel,
        grid=(N // TM_MERGE,),
        in_specs=[pl.BlockSpec((TM_MERGE, D_MODEL), row), pl.BlockSpec((TM_MERGE, RET_W), row),
                  pl.BlockSpec((TM_MERGE, DSA_HEADS * DSA_RKV), row), pl.BlockSpec((TM_MERGE, MEM_W), row),
                  pl.BlockSpec((TM_MERGE, 3 * D_MODEL), row),
                  pl.BlockSpec((1, MEM_LEN, MEM_W), memb), pl.BlockSpec((1, MEM_LEN, MEM_W), memb),
                  pl.BlockSpec(wret.shape, const), pl.BlockSpec(wdsa.shape, const),
                  pl.BlockSpec(wmem.shape, const), pl.BlockSpec(wout.shape, const)],
        out_specs=pl.BlockSpec((TM_MERGE, D_MODEL), row),
        out_shape=jax.ShapeDtypeStruct(x2d.shape, F32),
        compiler_params=_params(("parallel",)),
        name="merge_out",
    )(x2d, ret, olat, mq, gates, km, vm, wret, wdsa, wmem, wout)


def _mlp_kernel(x_ref, g_ref, w1_ref, w2_ref, gf_ref, o_ref, *, final):
    x = x_ref[...]
    h = _rms(x, g_ref[...]).astype(BF16)
    acc = jnp.zeros(x.shape, F32)
    for c in range(D_FF // D_MODEL):
        sl = slice(c * D_MODEL, (c + 1) * D_MODEL)
        a = jnp.maximum(_dot(h, w1_ref[:, sl]), 0.0)
        acc = acc + _dot((a * a).astype(BF16), w2_ref[sl, :])
    y = x + acc
    if final:
        y = _rms(y, gf_ref[...])
    o_ref[...] = y


def _mlp(x2d, g, w1, w2, gf, final):
    N = x2d.shape[0]
    row = lambda i: (i, 0)
    const = lambda i: (0, 0)
    return pl.pallas_call(
        functools.partial(_mlp_kernel, final=final),
        grid=(N // TM_MLP,),
        in_specs=[pl.BlockSpec((TM_MLP, D_MODEL), row), pl.BlockSpec((1, D_MODEL), const),
                  pl.BlockSpec(w1.shape, const), pl.BlockSpec(w2.shape, const),
                  pl.BlockSpec((1, D_MODEL), const)],
        out_specs=pl.BlockSpec((TM_MLP, D_MODEL), row),
        out_shape=jax.ShapeDtypeStruct(x2d.shape, F32),
        compiler_params=_params(("parallel",)),
        name="mlp",
    )(x2d, g, w1, w2, gf)


def _rope_tables(S):
    half = RET_DK // 2
    pos = jnp.arange(S, dtype=F32)
    freqs = ROPE_BASE ** (-jnp.arange(half, dtype=F32) / half)
    ang = pos[:, None] * freqs[None, :]
    return jnp.tile(jnp.cos(ang), (1, RET_HEADS)), jnp.tile(jnp.sin(ang), (1, RET_HEADS))


def _in_weights(wi, w_iq_l):
    half = RET_DK // 2
    def gather_halves(w):
        return jnp.transpose(w.reshape(D_MODEL, RET_HEADS, 2, half), (0, 2, 1, 3)).reshape(D_MODEL, RET_W)

    offs = np.concatenate([[0], np.cumsum(IN_SIZES)])
    seg = lambda k: wi[:, int(offs[k]):int(offs[k + 1])]
    wa = jnp.concatenate([gather_halves(seg(0)), gather_halves(seg(1)), seg(2), seg(3), seg(8), seg(9)],
                         axis=1).astype(BF16)
    pad = jnp.zeros((D_MODEL, LANES - IDX_HEADS), F32)
    widx = jnp.concatenate([seg(4), seg(5), jnp.tile(seg(6), (1, 4)), seg(7), pad], axis=1)
    wih, wil = _split(widx)
    wiq = jnp.tile(w_iq_l[:, :, None, :], (1, 1, 4, 1)).reshape(DSA_RQ, IDX_HEADS * LANES)
    wiqh, wiql = _split(wiq)
    return wa, wih, wil, wiqh, wiql


def kernel(x, mem, norm1, w_in, q_norm, kv_norm, w_uq, w_iq, w_uk, w_uv, mem_norm, w_mem_kv,
           w_ret_o, w_dsa_o, w_mem_o, w_out, norm2, w_ff1, w_ff2, rel_bias, final_norm):
    B, S, D = x.shape
    depth = w_in.shape[0]
    assert D == D_MODEL and S % TM_MLP == 0 and S % TQ == 0 and TQ == TK
    cos_t, sin_t = _rope_tables(S)
    bnear = _bias_tables(rel_bias)
    wql_all, wdsa_all = _fold_weights(w_uq, w_uk, w_uv, w_dsa_o)
    x2d = x.reshape(B * S, D)
    row = lambda v: v.reshape(1, -1)
    for l in range(depth):
        wa, wih, wil, wiqh, wiql = _in_weights(w_in[l], w_iq[l])
        rq, rk, rv, rg, mq, gates, idx = _in_proj(x2d, row(norm1[l]), cos_t, sin_t, wa, wih, wil, S)
        ret = _retention(rq, rk, rv, rg, B, S)
        qlat, qidx, wt, ckvn, ckt, kidx = _dsa_prep(idx, row(q_norm[l]), row(kv_norm[l]), wql_all[l], wiqh, wiql)
        olat = _dsa_attention(qlat, qidx, wt, ckvn, ckt, kidx, bnear, B, S)
        km, vm = _mem_kv(mem, row(mem_norm[l]), w_mem_kv[l].astype(BF16))
        x2d = _merge(x2d, ret, olat, mq, gates, km, vm, w_ret_o[l].astype(BF16), wdsa_all[l],
                     w_mem_o[l].astype(BF16), w_out[l].astype(BF16), S)
        x2d = _mlp(x2d, row(norm2[l]), w_ff1[l].astype(BF16), w_ff2[l].astype(BF16), row(final_norm),
                   final=(l == depth - 1))
    return x2d.reshape(B, S, D)
```

```python
import functools
import math
import statistics

import numpy as np
import jax
import jax.numpy as jnp
from jax import lax
from jax.experimental import pallas as pl
from jax.experimental.pallas import tpu as pltpu

F32 = jnp.float32
BF16 = jnp.bfloat16

D_MODEL = 1024
CHUNK = 64
EPS = 1e-6
GN_EPS = 1e-5
RET_HEADS = 8
RET_DK = 64
RET_W = 512
ROPE_BASE = 10000.0
DSA_HEADS = 8
DSA_DH = 64
DSA_RQ = 256
DSA_RKV = 128
IDX_HEADS = 8
IDX_DIM = 32
TOPK_MAX = 256
MEM_LEN = 256
MEM_HEADS = 4
MEM_DH = 128
MEM_W = 512
REL_BUCKETS = 32
REL_MAX_DIST = 128
D_FF = 4096
IN_SIZES = (512, 512, 512, 512, DSA_RQ, DSA_RKV, IDX_DIM, IDX_HEADS, MEM_W, 3 * D_MODEL)

LANES = 128
IDX_COLS = 640
NEG = -1e30
LOG2E = math.log2(math.e)
F32_MAX = float(np.finfo(np.float32).max)
VMEM_LIMIT = 56 * 1024 * 1024

TM_IN = 256
T_RET = 256
TQ = 256
TK = 256
SEARCH_UNCHECKED = 8
SEARCH_SWITCH = 12
SEARCH_CAP = SEARCH_SWITCH + 34
TM_MERGE = 512
TM_MLP = 512


def _dot(a, b):
    return jnp.dot(a, b, preferred_element_type=F32)


def _dot_nt(a, b):
    return lax.dot_general(a, b, (((1,), (1,)), ((), ())), preferred_element_type=F32)


def _split(a):
    hi = a.astype(BF16)
    lo = (a - hi.astype(F32)).astype(BF16)
    return hi, lo


def _rms(x, g):
    return x * lax.rsqrt(jnp.mean(x * x, axis=-1, keepdims=True) + EPS) * g


def _params(sem):
    return pltpu.CompilerParams(dimension_semantics=sem, vmem_limit_bytes=VMEM_LIMIT)


def _fold_kernel(uq_ref, uk_ref, uv_ref, wo_ref, wql_ref, wdsa_ref):
    uq_h, uq_l = _split(uq_ref[0, 0])
    uk_h, uk_l = _split(uk_ref[0, 0])
    ql = _dot_nt(uq_h, uk_h) + _dot_nt(uq_h, uk_l) + _dot_nt(uq_l, uk_h)
    wql_ref[0] = (ql * (DSA_DH ** -0.5 * LOG2E)).astype(BF16)
    uv_h, uv_l = _split(uv_ref[0, 0])
    wo_h, wo_l = _split(wo_ref[0, 0])
    wdsa_ref[0] = (_dot(uv_h, wo_h) + _dot(uv_h, wo_l) + _dot(uv_l, wo_h)).astype(BF16)


def _fold_weights(w_uq, w_uk, w_uv, w_dsa_o):
    L = w_uq.shape[0]
    uq = jnp.transpose(w_uq, (0, 2, 1, 3))
    uk = jnp.transpose(w_uk, (0, 2, 1, 3))
    uv = jnp.transpose(w_uv, (0, 2, 1, 3))
    wo = w_dsa_o.reshape(L, DSA_HEADS, DSA_DH, D_MODEL)
    return pl.pallas_call(
        _fold_kernel,
        grid=(L, DSA_HEADS),
        in_specs=[
            pl.BlockSpec((1, 1, DSA_RQ, DSA_DH), lambda l, h: (l, h, 0, 0)),
            pl.BlockSpec((1, 1, DSA_RKV, DSA_DH), lambda l, h: (l, h, 0, 0)),
            pl.BlockSpec((1, 1, DSA_RKV, DSA_DH), lambda l, h: (l, h, 0, 0)),
            pl.BlockSpec((1, 1, DSA_DH, D_MODEL), lambda l, h: (l, h, 0, 0)),
        ],
        out_specs=[
            pl.BlockSpec((1, DSA_RQ, DSA_RKV), lambda l, h: (l, 0, h)),
            pl.BlockSpec((1, DSA_RKV, D_MODEL), lambda l, h: (l, h, 0)),
        ],
        out_shape=[
            jax.ShapeDtypeStruct((L, DSA_RQ, DSA_HEADS * DSA_RKV), BF16),
            jax.ShapeDtypeStruct((L, DSA_HEADS * DSA_RKV, D_MODEL), BF16),
        ],
        compiler_params=_params(("parallel", "parallel")),
        name="fold_weights",
    )(uq, uk, uv, wo)


def _in_kernel(x_ref, g_ref, cos_ref, sin_ref, wa_ref, wih_ref, wil_ref,
               rq_ref, rk_ref, rv_ref, rg_ref, mq_ref, gates_ref, idx_ref):
    h = _rms(x_ref[...], g_ref[...])
    hb, hl = _split(h)
    c = cos_ref[...]
    s = sin_ref[...]
    half = RET_W // 2

    def rope_store(ref, z, scale):
        x1 = z[:, :half]
        x2 = z[:, half:]
        ref[:, :half] = ((x1 * c - x2 * s) * scale).astype(BF16)
        ref[:, half:] = ((x2 * c + x1 * s) * scale).astype(BF16)

    rope_store(rq_ref, _dot(hb, wa_ref[:, 0:512]), 1.0)
    rope_store(rk_ref, _dot(hb, wa_ref[:, 512:1024]), RET_DK ** -0.5)
    rv_ref[...] = _dot(hb, wa_ref[:, 1024:1536]).astype(BF16)
    rg_ref[...] = _dot(hb, wa_ref[:, 1536:2048]).astype(BF16)
    mq_ref[...] = _dot(hb, wa_ref[:, 2048:2560]).astype(BF16)
    for j in range(6):
        gates_ref[:, j * 512:(j + 1) * 512] = _dot(hb, wa_ref[:, 2560 + j * 512:3072 + j * 512]).astype(BF16)
    idx_ref[...] = _dot(hb, wih_ref[...]) + _dot(hl, wih_ref[...]) + _dot(hb, wil_ref[...])


def _in_proj(x2d, g, cos_t, sin_t, wa, wih, wil, S):
    N = x2d.shape[0]
    n_pos = S // TM_IN
    row = lambda i: (i, 0)
    const = lambda i: (0, 0)
    pos = lambda i: (i % n_pos, 0)
    bf = lambda w: jax.ShapeDtypeStruct((N, w), BF16)
    return pl.pallas_call(
        _in_kernel,
        grid=(N // TM_IN,),
        in_specs=[
            pl.BlockSpec((TM_IN, D_MODEL), row),
            pl.BlockSpec((1, D_MODEL), const),
            pl.BlockSpec((TM_IN, RET_W // 2), pos),
            pl.BlockSpec((TM_IN, RET_W // 2), pos),
            pl.BlockSpec(wa.shape, const),
            pl.BlockSpec(wih.shape, const),
            pl.BlockSpec(wil.shape, const),
        ],
        out_specs=[
            pl.BlockSpec((TM_IN, 512), row), pl.BlockSpec((TM_IN, 512), row),
            pl.BlockSpec((TM_IN, 512), row), pl.BlockSpec((TM_IN, 512), row),
            pl.BlockSpec((TM_IN, 512), row), pl.BlockSpec((TM_IN, 3 * D_MODEL), row),
            pl.BlockSpec((TM_IN, IDX_COLS), row),
        ],
        out_shape=[bf(512), bf(512), bf(512), bf(512), bf(512), bf(3 * D_MODEL),
                   jax.ShapeDtypeStruct((N, IDX_COLS), F32)],
        compiler_params=_params(("parallel",)),
        name="in_proj",
    )(x2d, g, cos_t, sin_t, wa, wih, wil)


def _ret_kernel(rq_ref, rk_ref, rv_ref, rg_ref, dmat_ref, qd_ref, kd_ref, hq_ref, hv_ref,
                cdec_ref, bd_ref, p_ref, o_ref, s_ref):
    @pl.when(pl.program_id(1) == 0)
    def _():
        s_ref[...] = jnp.zeros_like(s_ref)

    q = rq_ref[...]
    k = rk_ref[...]
    v = rv_ref[...]
    def head_lane_tiles(a, h):
        c = h // (LANES // (RET_DK // 2))
        return jnp.concatenate([a[:, c * LANES:(c + 1) * LANES], a[:, (2 + c) * LANES:(3 + c) * LANES]], axis=1)

    o_cols = []
    for pair in range(RET_HEADS // 2):
        cols = slice(pair * LANES, (pair + 1) * LANES)
        o_pair = jnp.zeros((q.shape[0], LANES), F32)
        for h in (2 * pair, 2 * pair + 1):
            sc = _dot_nt(head_lane_tiles(q * hq_ref[h], h), head_lane_tiles(k, h)) * dmat_ref[h]
            o_pair = o_pair + _dot(sc.astype(BF16), v[:, cols]) * hv_ref[h][:, cols]
        o_cols.append(o_pair)
    o = jnp.concatenate(o_cols, axis=1)
    state = s_ref[...]
    qf = q.astype(F32) * qd_ref[...]
    o = o + _dot(qf.astype(BF16), state.astype(BF16))
    kf = k.astype(F32) * kd_ref[...]
    kv = _dot(kf.T.astype(BF16), v)
    s_ref[...] = state * cdec_ref[...] + kv * bd_ref[...]

    p = p_ref[...]

    oh, ol = _split(o)
    d = o - (_dot(oh, p) + _dot(ol, p))
    y = d * lax.rsqrt(_dot((d * d).astype(BF16), p) + GN_EPS)
    g = rg_ref[...].astype(F32)
    o_ref[...] = (g * jax.nn.sigmoid(g) * y).astype(BF16)


def _ret_consts():
    T = T_RET
    hh = np.arange(RET_HEADS, dtype=np.float64)
    log_g = np.log1p(-np.exp2(-5.0 - hh))
    t = np.arange(T)
    ct = t // CHUNK
    diff = (t[:, None] - t[None, :]).astype(np.float64)
    same = ct[:, None] == ct[None, :]
    past = ct[None, :] < ct[:, None]
    expo = np.where(same, np.abs(diff), diff)
    dmat = np.where((same | past)[None], np.exp(log_g[:, None, None] * expo[None]), 0.0)
    lane = np.arange(RET_W)
    hk = (lane % (RET_W // 2)) // (RET_DK // 2)
    hv = lane // RET_DK
    qd = np.exp(log_g[hk][None, :] * (t[:, None] + 1.0))
    kd = np.exp(log_g[hk][None, :] * (T - 1.0 - t[:, None]))
    hq_mask = (hk[None, :] == np.arange(RET_HEADS)[:, None]).astype(np.float32)[:, None, :]
    hv_mask = (hv[None, :] == np.arange(RET_HEADS)[:, None]).astype(np.float32)[:, None, :]
    bd = (hk[:, None] == hv[None, :]).astype(np.float32)
    cdec = np.broadcast_to(np.exp(log_g[hk] * T)[:, None], (RET_W, RET_W))
    pmat = (hv[:, None] == hv[None, :]).astype(np.float32) / RET_DK
    f = lambda a: jnp.asarray(np.asarray(a, dtype=np.float32))
    return (f(dmat), f(qd), f(kd), jnp.asarray(hq_mask, dtype=BF16), f(hv_mask), f(cdec), f(bd),
            jnp.asarray(pmat, dtype=BF16))


def _retention(rq, rk, rv, rg, B, S):
    consts = _ret_consts()
    nb = S // T_RET
    tok = lambda b, j: (b * nb + j, 0)
    full = lambda a: pl.BlockSpec(a.shape, lambda b, j: (0,) * a.ndim)
    return pl.pallas_call(
        _ret_kernel,
        grid=(B, nb),
        in_specs=[pl.BlockSpec((T_RET, RET_W), tok)] * 4 + [full(a) for a in consts],
        out_specs=pl.BlockSpec((T_RET, RET_W), tok),
        out_shape=jax.ShapeDtypeStruct(rq.shape, BF16),
        scratch_shapes=[pltpu.VMEM((RET_W, RET_W), F32)],
        compiler_params=_params(("parallel", "arbitrary")),
        name="retention",
    )(rq, rk, rv, rg, *consts)


def _memkv_kernel(mem_ref, g_ref, w_ref, k_ref, v_ref):
    mn = _rms(mem_ref[0], g_ref[...]).astype(BF16)
    kv = _dot(mn, w_ref[...])
    k_ref[0] = kv[:, :MEM_W].astype(BF16)
    v_ref[0] = kv[:, MEM_W:].astype(BF16)


def _mem_kv(mem, g, w):
    B = mem.shape[0]
    return pl.pallas_call(
        _memkv_kernel,
        grid=(B,),
        in_specs=[pl.BlockSpec((1, MEM_LEN, D_MODEL), lambda b: (b, 0, 0)),
                  pl.BlockSpec((1, D_MODEL), lambda b: (0, 0)),
                  pl.BlockSpec(w.shape, lambda b: (0, 0))],
        out_specs=[pl.BlockSpec((1, MEM_LEN, MEM_W), lambda b: (b, 0, 0))] * 2,
        out_shape=[jax.ShapeDtypeStruct((B, MEM_LEN, MEM_W), BF16)] * 2,
        compiler_params=_params(("parallel",)),
        name="mem_kv",
    )(mem, g, w)


def _prep_kernel(idx_ref, qn_ref, kvn_ref, wql_ref, wiqh_ref, wiql_ref,
                 qlat_ref, qidx_ref, wt_ref, ckvn_ref, ckt_ref, kidx_ref):
    z = idx_ref[...]
    cqn = _rms(z[:, :DSA_RQ], qn_ref[...])
    cb, cl = _split(cqn)
    ql = _dot(cb, wql_ref[...])
    y = _dot(cb, wiqh_ref[...]) + _dot(cl, wiqh_ref[...]) + _dot(cb, wiql_ref[...])
    yh = y.astype(BF16).astype(F32)
    lane = lax.broadcasted_iota(jnp.int32, y.shape, 1)
    qsel = jnp.where((lane % LANES) < 2 * IDX_DIM, yh, y - yh).astype(BF16)
    for h in range(DSA_HEADS):
        qlat_ref[0, h] = ql[:, h * LANES:(h + 1) * LANES].astype(BF16)
        qidx_ref[0, h] = qsel[:, h * LANES:(h + 1) * LANES]
    ckn = _rms(z[:, DSA_RQ:DSA_RQ + DSA_RKV], kvn_ref[...])
    ckvn_ref[0] = ckn.astype(BF16)
    ckt_ref[0] = ckn.T.astype(BF16)
    ik = z[:, 384:512]
    ikh = ik.astype(BF16).astype(F32)
    lane1 = lax.broadcasted_iota(jnp.int32, ik.shape, 1)
    kidx_ref[0] = jnp.where((lane1 // IDX_DIM) % 2 == 0, ikh, ik - ikh).astype(BF16)
    wfull = z[:, 512:640] * ((IDX_HEADS ** -0.5) * (IDX_DIM ** -0.5))
    wt_ref[...] = wfull.T[:IDX_HEADS, :]


def _dsa_prep(idx, qn, kvn, wql, wiqh, wiql):
    N = idx.shape[0]
    nt = N // TQ
    const = lambda i: (0, 0)
    tile3 = lambda i: (i, 0, 0)
    stacked = pl.BlockSpec((1, DSA_HEADS, TQ, LANES), lambda i: (i, 0, 0, 0))
    return pl.pallas_call(
        _prep_kernel,
        grid=(nt,),
        in_specs=[pl.BlockSpec((TQ, IDX_COLS), lambda i: (i, 0)),
                  pl.BlockSpec((1, DSA_RQ), const), pl.BlockSpec((1, DSA_RKV), const),
                  pl.BlockSpec(wql.shape, const), pl.BlockSpec(wiqh.shape, const),
                  pl.BlockSpec(wiql.shape, const)],
        out_specs=[stacked, stacked, pl.BlockSpec((IDX_HEADS, TQ), lambda i: (0, i)),
                   pl.BlockSpec((1, TQ, LANES), tile3), pl.BlockSpec((1, LANES, TQ), tile3),
                   pl.BlockSpec((1, TQ, LANES), tile3)],
        out_shape=[jax.ShapeDtypeStruct((nt, DSA_HEADS, TQ, LANES), BF16),
                   jax.ShapeDtypeStruct((nt, DSA_HEADS, TQ, LANES), BF16),
                   jax.ShapeDtypeStruct((IDX_HEADS, N), F32),
                   jax.ShapeDtypeStruct((nt, TQ, LANES), BF16),
                   jax.ShapeDtypeStruct((nt, LANES, TQ), BF16),
                   jax.ShapeDtypeStruct((nt, TQ, LANES), BF16)],
        compiler_params=_params(("parallel",)),
        name="dsa_prep",
    )(idx, qn, kvn, wql, wiqh, wiql)


def _key_to_f32(key):
    bits = jnp.where(key >= 0, key, key ^ jnp.int32(0x7FFFFFFF))
    return lax.bitcast_convert_type(bits, F32)


def _f32_to_key(x):
    bits = lax.bitcast_convert_type(x, jnp.int32)
    return jnp.where(bits >= 0, bits, bits ^ jnp.int32(0x7FFFFFFF))


def _sublane_all(a, op):
    for shift in (4, 2, 1):
        a = op(a, pltpu.roll(a, shift, 0))
    return a


def _dsa_kernel(qlat_ref, qidx_ref, wt_ref, ckvn_ref, ckt_ref, kidx_ref, bnear_ref, zq_ref, tri_ref, o_ref,
                sc_ref, m_ref, l_ref, acc_ref, *, k_sel):
    i = pl.program_id(1)
    wt = wt_ref[...]

    def score_tiles(js):
        keys = jnp.concatenate([kidx_ref[j] for j in js], axis=0)
        y_all = _dot_nt(keys, qidx_ref[0].reshape(IDX_HEADS * TQ, LANES))
        tot = jnp.zeros((len(js) * TK, TQ), F32)
        for h in range(IDX_HEADS):
            tot = tot + jnp.maximum(y_all[:, h * TQ:(h + 1) * TQ], 0.0) * wt[h:h + 1, :]
        return tot

    def colsum8(a):
        return jnp.sum(a.reshape(a.shape[0] // 8, 8, TQ), axis=0)

    def ones_where(cond_):
        return jnp.where(cond_, 1.0, 0.0)

    def stats(s_counted, s_summed):
        return (colsum8(s_summed), colsum8(s_summed * s_summed),
                colsum8(ones_where(s_counted >= 0.0)), colsum8(ones_where(s_counted > 0.0)))

    def add_stats(carry, s):
        return tuple(a + b for a, b in zip(carry, stats(s, s)))

    def fill_pair(jj, carry):
        s = score_tiles([2 * jj, 2 * jj + 1])
        sc_ref[2 * jj] = s[:TK]
        sc_ref[2 * jj + 1] = s[TK:]
        return add_stats(carry, s)

    def fill_one(carry):
        s = score_tiles([i - 1])
        sc_ref[i - 1] = s
        return add_stats(carry, s)

    zero8 = jnp.zeros((8, TQ), F32)
    carry = lax.fori_loop(0, i // 2, fill_pair, (zero8, zero8, zero8, zero8))
    carry = lax.cond(i % 2 == 1, fill_one, lambda c: c, carry)
    key_r = lax.broadcasted_iota(jnp.int32, (TK, TQ), 0)
    qry_c = lax.broadcasted_iota(jnp.int32, (TK, TQ), 1)
    admissible = key_r <= (qry_c // CHUNK) * CHUNK + (CHUNK - 1)
    s_diag = score_tiles([i])
    s_masked = jnp.where(admissible, s_diag, -jnp.inf)
    sc_ref[i] = s_masked
    carry = tuple(a + b for a, b in zip(carry, stats(s_masked, jnp.where(admissible, s_diag, 0.0))))
    s1, s2, c_ge0, c_gt0 = (_sublane_all(a, jnp.add)[0:1, :] for a in carry)

    def count_ge(thr):
        def pair(jj, c):
            ca, cb = c
            return (ca + colsum8(ones_where(sc_ref[2 * jj] >= thr)),
                    cb + colsum8(ones_where(sc_ref[2 * jj + 1] >= thr)))
        ca, cb = lax.fori_loop(0, (i + 1) // 2, pair, (zero8, zero8))
        odd = lax.cond((i + 1) % 2 == 1, lambda: colsum8(ones_where(sc_ref[i] >= thr)), lambda: zero8)
        return _sublane_all(ca + cb + odd, jnp.add)[0:1, :]

    qc = lax.broadcasted_iota(jnp.int32, (1, TQ), 1)
    n_adm = (i * TQ + (qc // CHUNK + 1) * CHUNK).astype(F32)
    kf = jnp.float32(k_sel)
    key_min = jnp.int32(np.int32(np.array(-F32_MAX, np.float32).view(np.int32)) ^ np.int32(0x7FFFFFFF))
    key_max = jnp.int32(np.array(np.inf, np.float32).view(np.int32))
    inf = jnp.float32(np.inf)
    zq = zq_ref[0]
    mu = s1 / n_adm
    sd = jnp.sqrt(jnp.maximum(s2 / n_adm - mu * mu, 1e-30))
    dens = zq[1:2, :] / sd
    pos = c_gt0 >= kf
    tie0 = jnp.logical_and(jnp.logical_not(pos), c_ge0 >= kf)
    small = n_adm <= kf
    lo0 = jnp.where(pos | tie0, 0, key_min)
    hi0 = jnp.where(pos, key_max, jnp.where(tie0, 1, 0))
    lo0 = jnp.where(small, key_min, lo0)
    hi0 = jnp.where(small, key_min + 1, hi0)
    lf0 = jnp.where(pos | tie0, 0.0, -inf)
    hf0 = jnp.where(pos, inf, 0.0)
    cl0 = jnp.where(pos | tie0, c_ge0, n_adm)
    ch0 = jnp.where(pos, 0.0, jnp.where(tie0, c_gt0, c_ge0))
    t0 = mu + zq[0:1, :] * sd
    one = jnp.ones((1, TQ), F32)

    def mid_key(lo, hi):
        return (lo & hi) + ((lo ^ hi) >> 1)

    def unfinished(lo, hi, cl):
        return jnp.logical_not((cl == kf) | (mid_key(lo, hi) == lo))

    def cond(st):
        it, lo, hi, cl = st[0], st[1], st[2], st[5]
        return jnp.logical_and(it < SEARCH_CAP, jnp.sum(ones_where(unfinished(lo, hi, cl))) > 0.0)

    def step(st):
        it, lo, hi, lf, hf, cl, ch, t, last, wl, wh = st
        act = unfinished(lo, hi, cl)
        guided = (it < SEARCH_SWITCH).astype(jnp.int32)
        mid = mid_key(lo, hi)
        tk = mid + (_f32_to_key(t) - mid) * guided
        tk = jnp.minimum(jnp.maximum(tk, lo + 1), hi - 1)
        tf = _key_to_f32(tk)
        c = count_ge(tf)
        ge = c >= kf
        up_lo = act & ge
        up_hi = act & jnp.logical_not(ge)
        lo = jnp.where(up_lo, tk, lo)
        lf = jnp.where(up_lo, tf, lf)
        cl = jnp.where(up_lo, c, cl)
        hi = jnp.where(up_hi, tk, hi)
        hf = jnp.where(up_hi, tf, hf)
        ch = jnp.where(up_hi, c, ch)
        side = jnp.where(ge, 1.0, -1.0)
        same = side == last
        wh = jnp.where(ge, jnp.where(same, wh * 0.5, one), one)
        wl = jnp.where(ge, one, jnp.where(same, wl * 0.5, one))
        a = (cl - kf + 0.5) * wl
        b = (kf - 0.5 - ch) * wh
        t_bracket = lf + (hf - lf) * (a / (a + b))
        t_model = tf + 1.5 * (c - kf + jnp.where(ge, 0.5, -0.5)) / dens
        bracketed = (lf > -inf) & (hf < inf)
        t = jnp.where(bracketed, t_bracket, t_model)
        return it + 1, lo, hi, lf, hf, cl, ch, t, side, wl, wh

    st = (jnp.int32(0), lo0, hi0, lf0, hf0, cl0, ch0, t0, jnp.zeros((1, TQ), F32), one, one)
    st = lax.fori_loop(0, SEARCH_UNCHECKED, lambda _, s: step(s), st)
    st = lax.while_loop(cond, lambda s: step(step(s)), st)
    lo, cl, ch = st[1], st[5], st[6]
    thr = _key_to_f32(lo)
    tied = (cl != kf) & jnp.logical_not(small)
    need = jnp.where(tied, kf - ch, jnp.float32(1e9))

    def mask_plain(j, carry_):
        sc_ref[j] = jnp.where(sc_ref[j] >= thr, 0.0, NEG)
        return carry_

    def mask_ranked(j, before):
        s = sc_ref[j]
        eq = ones_where(s == thr)
        rank = _dot(tri_ref[...], eq.astype(BF16)) + before
        keep = (s > thr) | ((s == thr) & (rank <= need))
        sc_ref[j] = jnp.where(keep, 0.0, NEG)
        return before + jnp.sum(eq, axis=0, keepdims=True)

    any_tied = jnp.sum(ones_where(tied)) > 0.0

    @pl.when(any_tied)
    def _():
        lax.fori_loop(0, i + 1, mask_ranked, jnp.zeros((1, TQ), F32))

    @pl.when(jnp.logical_not(any_tied))
    def _():
        lax.fori_loop(0, i + 1, mask_plain, 0)

    m_ref[...] = jnp.full(m_ref.shape, NEG, F32)
    l_ref[...] = jnp.zeros(l_ref.shape, F32)
    acc_ref[...] = jnp.zeros(acc_ref.shape, F32)

    def attend(tiles):
        nk = len(tiles) * TK
        ck = jnp.concatenate([ckvn_ref[j] for j, _ in tiles], axis=0)
        ckt = jnp.concatenate([ckt_ref[j] for j, _ in tiles], axis=1)
        mask_bias = jnp.concatenate([sc_ref[j] for j, _ in tiles], axis=0)
        lg_all = _dot_nt(ck, qlat_ref[0].reshape(DSA_HEADS * TQ, LANES))
        ps, alphas = [], []
        for h in range(DSA_HEADS):
            bias = mask_bias
            if any(slot is not None for _, slot in tiles):
                bias = bias + jnp.concatenate(
                    [jnp.zeros((TK, TQ), F32) if slot is None else bnear_ref[h, slot] for _, slot in tiles], axis=0)
            lg3 = (lg_all[:, h * TQ:(h + 1) * TQ] + bias).reshape(nk // 8, 8, TQ)
            m_old = m_ref[h]
            m_new = jnp.maximum(m_old, _sublane_all(jnp.max(lg3, axis=0), jnp.maximum))
            alpha = jnp.exp2(m_old - m_new)
            p3 = jnp.exp2(lg3 - m_new[None])
            l_ref[h] = alpha * l_ref[h] + jnp.sum(p3, axis=0)
            m_ref[h] = m_new
            pv = _dot(ckt, p3.reshape(nk, TQ).astype(BF16))
            acc3 = acc_ref[h].reshape(DSA_RKV // 8, 8, TQ) * alpha[None]
            acc_ref[h] = acc3.reshape(DSA_RKV, TQ) + pv

    n_far = jnp.maximum(i - 1, 0)

    def far_pair(jj, carry):
        attend([(2 * jj, None), (2 * jj + 1, None)])
        return carry

    lax.fori_loop(0, n_far // 2, far_pair, 0)

    @pl.when(n_far % 2 == 1)
    def _():
        attend([(n_far - 1, None)])

    @pl.when(i >= 1)
    def _():
        attend([(i - 1, 0), (i, 1)])

    @pl.when(i == 0)
    def _():
        attend([(i, 1)])


    for h in range(DSA_HEADS):
        l_tot = _sublane_all(l_ref[h], jnp.add)
        o = (acc_ref[h].reshape(DSA_RKV // 8, 8, TQ) / l_tot[None]).reshape(DSA_RKV, TQ)
        o_ref[:, h * LANES:(h + 1) * LANES] = o.T.astype(BF16)


def _t5_bucket(rel):
    nb = REL_BUCKETS // 2
    max_exact = nb // 2
    base = jnp.where(rel > 0, nb, 0)
    n = jnp.abs(rel)
    nf = jnp.maximum(n, 1).astype(jnp.float32)
    large = max_exact + (jnp.log(nf / max_exact) / math.log(REL_MAX_DIST / max_exact)
                         * (nb - max_exact)).astype(jnp.int32)
    large = jnp.minimum(large, nb - 1)
    return base + jnp.where(n < max_exact, n, large)


def _bias_tables(rel_bias):
    s = jnp.arange(TK, dtype=jnp.int32)[:, None]
    t = jnp.arange(TQ, dtype=jnp.int32)[None, :]
    rel = jnp.stack([s - TK - t, s - t])
    table = rel_bias.astype(F32)
    bucket = _t5_bucket(rel)
    far_bucket = _t5_bucket(jnp.full((), -REL_MAX_DIST, jnp.int32))
    near = jnp.zeros((DSA_HEADS,) + rel.shape, F32)
    far = jnp.zeros((DSA_HEADS,), F32)
    for b in range(REL_BUCKETS):
        near = near + jnp.where(bucket[None] == b, table[b][:, None, None, None], 0.0)
        far = far + jnp.where(far_bucket == b, table[b], 0.0)
    return (near - far[:, None, None, None]) * LOG2E


def _search_tables(nq, k_sel):
    nd = statistics.NormalDist()
    tab = np.zeros((nq, 2, TQ), np.float32)
    for i in range(nq):
        for c in range(TQ // CHUNK):
            n = i * TQ + (c + 1) * CHUNK
            z = nd.inv_cdf(1.0 - (k_sel - 0.5) / n) if n > k_sel else 0.0
            tab[i, 0, c * CHUNK:(c + 1) * CHUNK] = z
            tab[i, 1, c * CHUNK:(c + 1) * CHUNK] = n * nd.pdf(z)
    return jnp.asarray(tab)


def _dsa_attention(qlat, qidx, wt, ckvn, ckt, kidx, bnear, B, S):
    nq = S // TQ
    N = B * S
    k_sel = min(TOPK_MAX, S // 4)
    zq = _search_tables(nq, k_sel)
    tri = jnp.asarray(np.tril(np.ones((TK, TK), np.float32)), dtype=BF16)
    stacked = pl.BlockSpec((1, DSA_HEADS, TQ, LANES), lambda b, i: (b * nq + i, 0, 0, 0))
    keys = pl.BlockSpec((nq, TK, LANES), lambda b, i: (b, 0, 0))
    keys_t = pl.BlockSpec((nq, LANES, TK), lambda b, i: (b, 0, 0))
    return pl.pallas_call(
        functools.partial(_dsa_kernel, k_sel=k_sel),
        grid=(B, nq),
        in_specs=[stacked, stacked, pl.BlockSpec((IDX_HEADS, TQ), lambda b, i: (0, b * nq + i)),
                  keys, keys_t, keys,
                  pl.BlockSpec(bnear.shape, lambda b, i: (0, 0, 0, 0)),
                  pl.BlockSpec((1, 2, TQ), lambda b, i: (i, 0, 0)),
                  pl.BlockSpec((TK, TK), lambda b, i: (0, 0))],
        out_specs=pl.BlockSpec((TQ, DSA_HEADS * DSA_RKV), lambda b, i: (b * nq + i, 0)),
        out_shape=jax.ShapeDtypeStruct((N, DSA_HEADS * DSA_RKV), BF16),
        scratch_shapes=[pltpu.VMEM((nq, TK, TQ), F32),
                        pltpu.VMEM((DSA_HEADS, 8, TQ), F32),
                        pltpu.VMEM((DSA_HEADS, 8, TQ), F32),
                        pltpu.VMEM((DSA_HEADS, DSA_RKV, TQ), F32)],
        compiler_params=_params(("parallel", "arbitrary")),
        name="dsa_attention",
    )(qlat, qidx, wt, ckvn, ckt, kidx, bnear, zq, tri)


def _merge_kernel(x_ref, ret_ref, olat_ref, mq_ref, gates_ref, km_ref, vm_ref,
                  wret_ref, wdsa_ref, wmem_ref, wout_ref, o_ref):
    mq = mq_ref[...]
    km = km_ref[0]
    vm = vm_ref[0]
    pvs = []
    for h in range(MEM_HEADS):
        sl = slice(h * MEM_DH, (h + 1) * MEM_DH)
        lg = _dot_nt(mq[:, sl], km[:, sl]) * (MEM_DH ** -0.5)
        p = jnp.exp(lg - jnp.max(lg, axis=1, keepdims=True))
        p = p / jnp.sum(p, axis=1, keepdims=True)
        pvs.append(_dot(p.astype(BF16), vm[:, sl]).astype(BF16))
    mem_b = _dot(jnp.concatenate(pvs, axis=1), wmem_ref[...])
    ret_b = _dot(ret_ref[...], wret_ref[...])
    dsa_b = _dot(olat_ref[...], wdsa_ref[...])
    g = jax.nn.sigmoid(gates_ref[...].astype(F32))
    merged = (g[:, :D_MODEL] * ret_b + g[:, D_MODEL:2 * D_MODEL] * dsa_b + g[:, 2 * D_MODEL:] * mem_b)
    o_ref[...] = x_ref[...] + _dot(merged.astype(BF16), wout_ref[...])


def _merge(x2d, ret, olat, mq, gates, km, vm, wret, wdsa, wmem, wout, S):
    N = x2d.shape[0]
    per_b = S // TM_MERGE
    row = lambda i: (i, 0)
    const = lambda i: (0, 0)
    memb = lambda i: (i // per_b, 0, 0)
    return pl.pallas_call(
        _merge_kernel,
        grid=(N // TM_MERGE,),
        in_specs=[pl.BlockSpec((TM_MERGE, D_MODEL), row), pl.BlockSpec((TM_MERGE, RET_W), row),
                  pl.BlockSpec((TM_MERGE, DSA_HEADS * DSA_RKV), row), pl.BlockSpec((TM_MERGE, MEM_W), row),
                  pl.BlockSpec((TM_MERGE, 3 * D_MODEL), row),
                  pl.BlockSpec((1, MEM_LEN, MEM_W), memb), pl.BlockSpec((1, MEM_LEN, MEM_W), memb),
                  pl.BlockSpec(wret.shape, const), pl.BlockSpec(wdsa.shape, const),
                  pl.BlockSpec(wmem.shape, const), pl.BlockSpec(wout.shape, const)],
        out_specs=pl.BlockSpec((TM_MERGE, D_MODEL), row),
        out_shape=jax.ShapeDtypeStruct(x2d.shape, F32),
        compiler_params=_params(("parallel",)),
        name="merge_out",
    )(x2d, ret, olat, mq, gates, km, vm, wret, wdsa, wmem, wout)


def _mlp_kernel(x_ref, g_ref, w1_ref, w2_ref, gf_ref, o_ref, *, final):
    x = x_ref[...]
    h = _rms(x, g_ref[...]).astype(BF16)
    acc = jnp.zeros(x.shape, F32)
    for c in range(D_FF // D_MODEL):
        sl = slice(c * D_MODEL, (c + 1) * D_MODEL)
        a = jnp.maximum(_dot(h, w1_ref[:, sl]), 0.0)
        acc = acc + _dot((a * a).astype(BF16), w2_ref[sl, :])
    y = x + acc
    if final:
        y = _rms(y, gf_ref[...])
    o_ref[...] = y


def _mlp(x2d, g, w1, w2, gf, final):
    N = x2d.shape[0]
    row = lambda i: (i, 0)
    const = lambda i: (0, 0)
    return pl.pallas_call(
        functools.partial(_mlp_kernel, final=final),
        grid=(N // TM_MLP,),
        in_specs=[pl.BlockSpec((TM_MLP, D_MODEL), row), pl.BlockSpec((1, D_MODEL), const),
                  pl.BlockSpec(w1.shape, const), pl.BlockSpec(w2.shape, const),
                  pl.BlockSpec((1, D_MODEL), const)],
        out_specs=pl.BlockSpec((TM_MLP, D_MODEL), row),
        out_shape=jax.ShapeDtypeStruct(x2d.shape, F32),
        compiler_params=_params(("parallel",)),
        name="mlp",
    )(x2d, g, w1, w2, gf)


def _rope_tables(S):
    half = RET_DK // 2
    pos = jnp.arange(S, dtype=F32)
    freqs = ROPE_BASE ** (-jnp.arange(half, dtype=F32) / half)
    ang = pos[:, None] * freqs[None, :]
    return jnp.tile(jnp.cos(ang), (1, RET_HEADS)), jnp.tile(jnp.sin(ang), (1, RET_HEADS))


def _in_weights(wi, w_iq_l):
    half = RET_DK // 2
    def gather_halves(w):
        return jnp.transpose(w.reshape(D_MODEL, RET_HEADS, 2, half), (0, 2, 1, 3)).reshape(D_MODEL, RET_W)

    offs = np.concatenate([[0], np.cumsum(IN_SIZES)])
    seg = lambda k: wi[:, int(offs[k]):int(offs[k + 1])]
    wa = jnp.concatenate([gather_halves(seg(0)), gather_halves(seg(1)), seg(2), seg(3), seg(8), seg(9)],
                         axis=1).astype(BF16)
    pad = jnp.zeros((D_MODEL, LANES - IDX_HEADS), F32)
    widx = jnp.concatenate([seg(4), seg(5), jnp.tile(seg(6), (1, 4)), seg(7), pad], axis=1)
    wih, wil = _split(widx)
    wiq = jnp.tile(w_iq_l[:, :, None, :], (1, 1, 4, 1)).reshape(DSA_RQ, IDX_HEADS * LANES)
    wiqh, wiql = _split(wiq)
    return wa, wih, wil, wiqh, wiql


def kernel(x, mem, norm1, w_in, q_norm, kv_norm, w_uq, w_iq, w_uk, w_uv, mem_norm, w_mem_kv,
           w_ret_o, w_dsa_o, w_mem_o, w_out, norm2, w_ff1, w_ff2, rel_bias, final_norm):
    B, S, D = x.shape
    depth = w_in.shape[0]
    assert D == D_MODEL and S % TM_MLP == 0 and S % TQ == 0 and TQ == TK
    cos_t, sin_t = _rope_tables(S)
    bnear = _bias_tables(rel_bias)
    wql_all, wdsa_all = _fold_weights(w_uq, w_uk, w_uv, w_dsa_o)
    x2d = x.reshape(B * S, D)
    row = lambda v: v.reshape(1, -1)
    for l in range(depth):
        wa, wih, wil, wiqh, wiql = _in_weights(w_in[l], w_iq[l])
        rq, rk, rv, rg, mq, gates, idx = _in_proj(x2d, row(norm1[l]), cos_t, sin_t, wa, wih, wil, S)
        ret = _retention(rq, rk, rv, rg, B, S)
        qlat, qidx, wt, ckvn, ckt, kidx = _dsa_prep(idx, row(q_norm[l]), row(kv_norm[l]), wql_all[l], wiqh, wiql)
        olat = _dsa_attention(qlat, qidx, wt, ckvn, ckt, kidx, bnear, B, S)
        km, vm = _mem_kv(mem, row(mem_norm[l]), w_mem_kv[l].astype(BF16))
        x2d = _merge(x2d, ret, olat, mq, gates, km, vm, w_ret_o[l].astype(BF16), wdsa_all[l],
                     w_mem_o[l].astype(BF16), w_out[l].astype(BF16), S)
        x2d = _mlp(x2d, row(norm2[l]), w_ff1[l].astype(BF16), w_ff2[l].astype(BF16), row(final_norm),
                   final=(l == depth - 1))
    return x2d.reshape(B, S, D)
```

```python
import functools
import math
import statistics

import numpy as np
import jax
import jax.numpy as jnp
from jax import lax
from jax.experimental import pallas as pl
from jax.experimental.pallas import tpu as pltpu

F32 = jnp.float32
BF16 = jnp.bfloat16

D_MODEL = 1024
CHUNK = 64
EPS = 1e-6
GN_EPS = 1e-5
RET_HEADS = 8
RET_DK = 64
RET_W = 512
ROPE_BASE = 10000.0
DSA_HEADS = 8
DSA_DH = 64
DSA_RQ = 256
DSA_RKV = 128
IDX_HEADS = 8
IDX_DIM = 32
TOPK_MAX = 256
MEM_LEN = 256
MEM_HEADS = 4
MEM_DH = 128
MEM_W = 512
REL_BUCKETS = 32
REL_MAX_DIST = 128
D_FF = 4096
IN_SIZES = (512, 512, 512, 512, DSA_RQ, DSA_RKV, IDX_DIM, IDX_HEADS, MEM_W, 3 * D_MODEL)

LANES = 128
IDX_COLS = 640
NEG = -1e30
LOG2E = math.log2(math.e)
F32_MAX = float(np.finfo(np.float32).max)
VMEM_LIMIT = 56 * 1024 * 1024

TM_IN = 256
T_RET = 256
TQ = 256
TK = 256
LAG_LIMIT = 100.0
LAG_DOWN = 5.0
SEARCH_UNCHECKED = 8
SEARCH_SWITCH = 12
SEARCH_CAP = SEARCH_SWITCH + 34
TM_MERGE = 512
TM_MLP = 512


def _dot(a, b):
    return jnp.dot(a, b, preferred_element_type=F32)


def _dot_nt(a, b):
    return lax.dot_general(a, b, (((1,), (1,)), ((), ())), preferred_element_type=F32)


def _split(a):
    hi = a.astype(BF16)
    lo = (a - hi.astype(F32)).astype(BF16)
    return hi, lo


def _rms(x, g):
    return x * lax.rsqrt(jnp.mean(x * x, axis=-1, keepdims=True) + EPS) * g


def _params(sem):
    return pltpu.CompilerParams(dimension_semantics=sem, vmem_limit_bytes=VMEM_LIMIT)


def _fold_kernel(uq_ref, uk_ref, uv_ref, wo_ref, wql_ref, wdsa_ref):
    uq_h, uq_l = _split(uq_ref[0, 0])
    uk_h, uk_l = _split(uk_ref[0, 0])
    ql = _dot_nt(uq_h, uk_h) + _dot_nt(uq_h, uk_l) + _dot_nt(uq_l, uk_h)
    wql_ref[0] = (ql * (DSA_DH ** -0.5 * LOG2E)).astype(BF16)
    uv_h, uv_l = _split(uv_ref[0, 0])
    wo_h, wo_l = _split(wo_ref[0, 0])
    wdsa_ref[0] = (_dot(uv_h, wo_h) + _dot(uv_h, wo_l) + _dot(uv_l, wo_h)).astype(BF16)


def _fold_weights(w_uq, w_uk, w_uv, w_dsa_o):
    L = w_uq.shape[0]
    uq = jnp.transpose(w_uq, (0, 2, 1, 3))
    uk = jnp.transpose(w_uk, (0, 2, 1, 3))
    uv = jnp.transpose(w_uv, (0, 2, 1, 3))
    wo = w_dsa_o.reshape(L, DSA_HEADS, DSA_DH, D_MODEL)
    return pl.pallas_call(
        _fold_kernel,
        grid=(L, DSA_HEADS),
        in_specs=[
            pl.BlockSpec((1, 1, DSA_RQ, DSA_DH), lambda l, h: (l, h, 0, 0)),
            pl.BlockSpec((1, 1, DSA_RKV, DSA_DH), lambda l, h: (l, h, 0, 0)),
            pl.BlockSpec((1, 1, DSA_RKV, DSA_DH), lambda l, h: (l, h, 0, 0)),
            pl.BlockSpec((1, 1, DSA_DH, D_MODEL), lambda l, h: (l, h, 0, 0)),
        ],
        out_specs=[
            pl.BlockSpec((1, DSA_RQ, DSA_RKV), lambda l, h: (l, 0, h)),
            pl.BlockSpec((1, DSA_RKV, D_MODEL), lambda l, h: (l, h, 0)),
        ],
        out_shape=[
            jax.ShapeDtypeStruct((L, DSA_RQ, DSA_HEADS * DSA_RKV), BF16),
            jax.ShapeDtypeStruct((L, DSA_HEADS * DSA_RKV, D_MODEL), BF16),
        ],
        compiler_params=_params(("parallel", "parallel")),
        name="fold_weights",
    )(uq, uk, uv, wo)


def _in_kernel(x_ref, g_ref, cos_ref, sin_ref, wa_ref, wih_ref, wil_ref,
               rq_ref, rk_ref, rv_ref, rg_ref, mq_ref, gates_ref, idx_ref):
    h = _rms(x_ref[...], g_ref[...])
    hb, hl = _split(h)
    c = cos_ref[...]
    s = sin_ref[...]
    half = RET_W // 2

    def rope_store(ref, z, scale):
        x1 = z[:, :half]
        x2 = z[:, half:]
        ref[:, :half] = ((x1 * c - x2 * s) * scale).astype(BF16)
        ref[:, half:] = ((x2 * c + x1 * s) * scale).astype(BF16)

    rope_store(rq_ref, _dot(hb, wa_ref[:, 0:512]), 1.0)
    rope_store(rk_ref, _dot(hb, wa_ref[:, 512:1024]), RET_DK ** -0.5)
    rv_ref[...] = _dot(hb, wa_ref[:, 1024:1536]).astype(BF16)
    rg_ref[...] = _dot(hb, wa_ref[:, 1536:2048]).astype(BF16)
    mq_ref[...] = _dot(hb, wa_ref[:, 2048:2560]).astype(BF16)
    for j in range(6):
        gates_ref[:, j * 512:(j + 1) * 512] = _dot(hb, wa_ref[:, 2560 + j * 512:3072 + j * 512]).astype(BF16)
    idx_ref[...] = _dot(hb, wih_ref[...]) + _dot(hl, wih_ref[...]) + _dot(hb, wil_ref[...])


def _in_proj(x2d, g, cos_t, sin_t, wa, wih, wil, S):
    N = x2d.shape[0]
    n_pos = S // TM_IN
    row = lambda i: (i, 0)
    const = lambda i: (0, 0)
    pos = lambda i: (i % n_pos, 0)
    bf = lambda w: jax.ShapeDtypeStruct((N, w), BF16)
    return pl.pallas_call(
        _in_kernel,
        grid=(N // TM_IN,),
        in_specs=[
            pl.BlockSpec((TM_IN, D_MODEL), row),
            pl.BlockSpec((1, D_MODEL), const),
            pl.BlockSpec((TM_IN, RET_W // 2), pos),
            pl.BlockSpec((TM_IN, RET_W // 2), pos),
            pl.BlockSpec(wa.shape, const),
            pl.BlockSpec(wih.shape, const),
            pl.BlockSpec(wil.shape, const),
        ],
        out_specs=[
            pl.BlockSpec((TM_IN, 512), row), pl.BlockSpec((TM_IN, 512), row),
            pl.BlockSpec((TM_IN, 512), row), pl.BlockSpec((TM_IN, 512), row),
            pl.BlockSpec((TM_IN, 512), row), pl.BlockSpec((TM_IN, 3 * D_MODEL), row),
            pl.BlockSpec((TM_IN, IDX_COLS), row),
        ],
        out_shape=[bf(512), bf(512), bf(512), bf(512), bf(512), bf(3 * D_MODEL),
                   jax.ShapeDtypeStruct((N, IDX_COLS), F32)],
        compiler_params=_params(("parallel",)),
        name="in_proj",
    )(x2d, g, cos_t, sin_t, wa, wih, wil)


def _ret_kernel(rq_ref, rk_ref, rv_ref, rg_ref, dmat_ref, qd_ref, kd_ref, hq_ref, hv_ref,
                cdec_ref, bd_ref, p_ref, o_ref, s_ref):
    @pl.when(pl.program_id(1) == 0)
    def _():
        s_ref[...] = jnp.zeros_like(s_ref)

    q = rq_ref[...]
    k = rk_ref[...]
    v = rv_ref[...]
    def head_lane_tiles(a, h):
        c = h // (LANES // (RET_DK // 2))
        return jnp.concatenate([a[:, c * LANES:(c + 1) * LANES], a[:, (2 + c) * LANES:(3 + c) * LANES]], axis=1)

    o_cols = []
    for pair in range(RET_HEADS // 2):
        cols = slice(pair * LANES, (pair + 1) * LANES)
        o_pair = jnp.zeros((q.shape[0], LANES), F32)
        for h in (2 * pair, 2 * pair + 1):
            sc = _dot_nt(head_lane_tiles(q * hq_ref[h], h), head_lane_tiles(k, h)) * dmat_ref[h]
            o_pair = o_pair + _dot(sc.astype(BF16), v[:, cols]) * hv_ref[h][:, cols]
        o_cols.append(o_pair)
    o = jnp.concatenate(o_cols, axis=1)
    state = s_ref[...]
    qf = q.astype(F32) * qd_ref[...]
    o = o + _dot(qf.astype(BF16), state.astype(BF16))
    kf = k.astype(F32) * kd_ref[...]
    kv = _dot(kf.T.astype(BF16), v)
    s_ref[...] = state * cdec_ref[...] + kv * bd_ref[...]

    p = p_ref[...]

    oh, ol = _split(o)
    d = o - (_dot(oh, p) + _dot(ol, p))
    y = d * lax.rsqrt(_dot((d * d).astype(BF16), p) + GN_EPS)
    g = rg_ref[...].astype(F32)
    o_ref[...] = (g * jax.nn.sigmoid(g) * y).astype(BF16)


def _ret_consts():
    T = T_RET
    hh = np.arange(RET_HEADS, dtype=np.float64)
    log_g = np.log1p(-np.exp2(-5.0 - hh))
    t = np.arange(T)
    ct = t // CHUNK
    diff = (t[:, None] - t[None, :]).astype(np.float64)
    same = ct[:, None] == ct[None, :]
    past = ct[None, :] < ct[:, None]
    expo = np.where(same, np.abs(diff), diff)
    dmat = np.where((same | past)[None], np.exp(log_g[:, None, None] * expo[None]), 0.0)
    lane = np.arange(RET_W)
    hk = (lane % (RET_W // 2)) // (RET_DK // 2)
    hv = lane // RET_DK
    qd = np.exp(log_g[hk][None, :] * (t[:, None] + 1.0))
    kd = np.exp(log_g[hk][None, :] * (T - 1.0 - t[:, None]))
    hq_mask = (hk[None, :] == np.arange(RET_HEADS)[:, None]).astype(np.float32)[:, None, :]
    hv_mask = (hv[None, :] == np.arange(RET_HEADS)[:, None]).astype(np.float32)[:, None, :]
    bd = (hk[:, None] == hv[None, :]).astype(np.float32)
    cdec = np.broadcast_to(np.exp(log_g[hk] * T)[:, None], (RET_W, RET_W))
    pmat = (hv[:, None] == hv[None, :]).astype(np.float32) / RET_DK
    f = lambda a: jnp.asarray(np.asarray(a, dtype=np.float32))
    return (f(dmat), f(qd), f(kd), jnp.asarray(hq_mask, dtype=BF16), f(hv_mask), f(cdec), f(bd),
            jnp.asarray(pmat, dtype=BF16))


def _retention(rq, rk, rv, rg, B, S):
    consts = _ret_consts()
    nb = S // T_RET
    tok = lambda b, j: (b * nb + j, 0)
    full = lambda a: pl.BlockSpec(a.shape, lambda b, j: (0,) * a.ndim)
    return pl.pallas_call(
        _ret_kernel,
        grid=(B, nb),
        in_specs=[pl.BlockSpec((T_RET, RET_W), tok)] * 4 + [full(a) for a in consts],
        out_specs=pl.BlockSpec((T_RET, RET_W), tok),
        out_shape=jax.ShapeDtypeStruct(rq.shape, BF16),
        scratch_shapes=[pltpu.VMEM((RET_W, RET_W), F32)],
        compiler_params=_params(("parallel", "arbitrary")),
        name="retention",
    )(rq, rk, rv, rg, *consts)


def _memkv_kernel(mem_ref, g_ref, w_ref, k_ref, v_ref):
    mn = _rms(mem_ref[0], g_ref[...]).astype(BF16)
    kv = _dot(mn, w_ref[...])
    k_ref[0] = kv[:, :MEM_W].astype(BF16)
    v_ref[0] = kv[:, MEM_W:].astype(BF16)


def _mem_kv(mem, g, w):
    B = mem.shape[0]
    return pl.pallas_call(
        _memkv_kernel,
        grid=(B,),
        in_specs=[pl.BlockSpec((1, MEM_LEN, D_MODEL), lambda b: (b, 0, 0)),
                  pl.BlockSpec((1, D_MODEL), lambda b: (0, 0)),
                  pl.BlockSpec(w.shape, lambda b: (0, 0))],
        out_specs=[pl.BlockSpec((1, MEM_LEN, MEM_W), lambda b: (b, 0, 0))] * 2,
        out_shape=[jax.ShapeDtypeStruct((B, MEM_LEN, MEM_W), BF16)] * 2,
        compiler_params=_params(("parallel",)),
        name="mem_kv",
    )(mem, g, w)


def _prep_kernel(idx_ref, qn_ref, kvn_ref, wql_ref, wiqh_ref, wiql_ref,
                 qlat_ref, qidx_ref, wt_ref, ckvn_ref, ckt_ref, kidx_ref):
    z = idx_ref[...]
    cqn = _rms(z[:, :DSA_RQ], qn_ref[...])
    cb, cl = _split(cqn)
    ql = _dot(cb, wql_ref[...])
    y = _dot(cb, wiqh_ref[...]) + _dot(cl, wiqh_ref[...]) + _dot(cb, wiql_ref[...])
    yh = y.astype(BF16).astype(F32)
    lane = lax.broadcasted_iota(jnp.int32, y.shape, 1)
    qsel = jnp.where((lane % LANES) < 2 * IDX_DIM, yh, y - yh).astype(BF16)
    for h in range(DSA_HEADS):
        qlat_ref[0, h] = ql[:, h * LANES:(h + 1) * LANES].astype(BF16)
        qidx_ref[0, h] = qsel[:, h * LANES:(h + 1) * LANES]
    ckn = _rms(z[:, DSA_RQ:DSA_RQ + DSA_RKV], kvn_ref[...])
    ckvn_ref[0] = ckn.astype(BF16)
    ckt_ref[0] = ckn.T.astype(BF16)
    ik = z[:, 384:512]
    ikh = ik.astype(BF16).astype(F32)
    lane1 = lax.broadcasted_iota(jnp.int32, ik.shape, 1)
    kidx_ref[0] = jnp.where((lane1 // IDX_DIM) % 2 == 0, ikh, ik - ikh).astype(BF16)
    wfull = z[:, 512:640] * ((IDX_HEADS ** -0.5) * (IDX_DIM ** -0.5))
    wt_ref[...] = wfull.T[:IDX_HEADS, :]


def _dsa_prep(idx, qn, kvn, wql, wiqh, wiql):
    N = idx.shape[0]
    nt = N // TQ
    const = lambda i: (0, 0)
    tile3 = lambda i: (i, 0, 0)
    stacked = pl.BlockSpec((1, DSA_HEADS, TQ, LANES), lambda i: (i, 0, 0, 0))
    return pl.pallas_call(
        _prep_kernel,
        grid=(nt,),
        in_specs=[pl.BlockSpec((TQ, IDX_COLS), lambda i: (i, 0)),
                  pl.BlockSpec((1, DSA_RQ), const), pl.BlockSpec((1, DSA_RKV), const),
                  pl.BlockSpec(wql.shape, const), pl.BlockSpec(wiqh.shape, const),
                  pl.BlockSpec(wiql.shape, const)],
        out_specs=[stacked, stacked, pl.BlockSpec((IDX_HEADS, TQ), lambda i: (0, i)),
                   pl.BlockSpec((1, TQ, LANES), tile3), pl.BlockSpec((1, LANES, TQ), tile3),
                   pl.BlockSpec((1, TQ, LANES), tile3)],
        out_shape=[jax.ShapeDtypeStruct((nt, DSA_HEADS, TQ, LANES), BF16),
                   jax.ShapeDtypeStruct((nt, DSA_HEADS, TQ, LANES), BF16),
                   jax.ShapeDtypeStruct((IDX_HEADS, N), F32),
                   jax.ShapeDtypeStruct((nt, TQ, LANES), BF16),
                   jax.ShapeDtypeStruct((nt, LANES, TQ), BF16),
                   jax.ShapeDtypeStruct((nt, TQ, LANES), BF16)],
        compiler_params=_params(("parallel",)),
        name="dsa_prep",
    )(idx, qn, kvn, wql, wiqh, wiql)


def _key_to_f32(key):
    bits = jnp.where(key >= 0, key, key ^ jnp.int32(0x7FFFFFFF))
    return lax.bitcast_convert_type(bits, F32)


def _f32_to_key(x):
    bits = lax.bitcast_convert_type(x, jnp.int32)
    return jnp.where(bits >= 0, bits, bits ^ jnp.int32(0x7FFFFFFF))


def _sublane_all(a, op):
    for shift in (4, 2, 1):
        a = op(a, pltpu.roll(a, shift, 0))
    return a


def _dsa_kernel(qlat_ref, qidx_ref, wt_ref, ckvn_ref, ckt_ref, kidx_ref, bnear_ref, zq_ref, tri_ref, o_ref,
                sc_ref, m_ref, l_ref, x_ref, acc_ref, *, k_sel):
    i = pl.program_id(1)
    wt = wt_ref[...]

    def score_tiles(js):
        keys = jnp.concatenate([kidx_ref[j] for j in js], axis=0)
        y_all = _dot_nt(keys, qidx_ref[0].reshape(IDX_HEADS * TQ, LANES))
        tot = jnp.zeros((len(js) * TK, TQ), F32)
        for h in range(IDX_HEADS):
            tot = tot + jnp.maximum(y_all[:, h * TQ:(h + 1) * TQ], 0.0) * wt[h:h + 1, :]
        return tot

    def colsum8(a):
        return jnp.sum(a.reshape(a.shape[0] // 8, 8, TQ), axis=0)

    def ones_where(cond_):
        return jnp.where(cond_, 1.0, 0.0)

    def stats(s_counted, s_summed):
        return (colsum8(s_summed), colsum8(s_summed * s_summed),
                colsum8(ones_where(s_counted >= 0.0)), colsum8(ones_where(s_counted > 0.0)))

    def add_stats(carry, s):
        return tuple(a + b for a, b in zip(carry, stats(s, s)))

    def fill_pair(jj, carry):
        s = score_tiles([2 * jj, 2 * jj + 1])
        sc_ref[2 * jj] = s[:TK]
        sc_ref[2 * jj + 1] = s[TK:]
        return add_stats(carry, s)

    def fill_one(carry):
        s = score_tiles([i - 1])
        sc_ref[i - 1] = s
        return add_stats(carry, s)

    zero8 = jnp.zeros((8, TQ), F32)
    carry = lax.fori_loop(0, i // 2, fill_pair, (zero8, zero8, zero8, zero8))
    carry = lax.cond(i % 2 == 1, fill_one, lambda c: c, carry)
    key_r = lax.broadcasted_iota(jnp.int32, (TK, TQ), 0)
    qry_c = lax.broadcasted_iota(jnp.int32, (TK, TQ), 1)
    admissible = key_r <= (qry_c // CHUNK) * CHUNK + (CHUNK - 1)
    s_diag = score_tiles([i])
    s_masked = jnp.where(admissible, s_diag, -jnp.inf)
    sc_ref[i] = s_masked
    carry = tuple(a + b for a, b in zip(carry, stats(s_masked, jnp.where(admissible, s_diag, 0.0))))
    s1, s2, c_ge0, c_gt0 = (_sublane_all(a, jnp.add)[0:1, :] for a in carry)

    def count_ge(thr):
        def pair(jj, c):
            ca, cb = c
            return (ca + colsum8(ones_where(sc_ref[2 * jj] >= thr)),
                    cb + colsum8(ones_where(sc_ref[2 * jj + 1] >= thr)))
        ca, cb = lax.fori_loop(0, (i + 1) // 2, pair, (zero8, zero8))
        odd = lax.cond((i + 1) % 2 == 1, lambda: colsum8(ones_where(sc_ref[i] >= thr)), lambda: zero8)
        return _sublane_all(ca + cb + odd, jnp.add)[0:1, :]

    qc = lax.broadcasted_iota(jnp.int32, (1, TQ), 1)
    n_adm = (i * TQ + (qc // CHUNK + 1) * CHUNK).astype(F32)
    kf = jnp.float32(k_sel)
    key_min = jnp.int32(np.int32(np.array(-F32_MAX, np.float32).view(np.int32)) ^ np.int32(0x7FFFFFFF))
    key_max = jnp.int32(np.array(np.inf, np.float32).view(np.int32))
    inf = jnp.float32(np.inf)
    zq = zq_ref[0]
    mu = s1 / n_adm
    sd = jnp.sqrt(jnp.maximum(s2 / n_adm - mu * mu, 1e-30))
    dens = zq[1:2, :] / sd
    pos = c_gt0 >= kf
    tie0 = jnp.logical_and(jnp.logical_not(pos), c_ge0 >= kf)
    small = n_adm <= kf
    lo0 = jnp.where(pos | tie0, 0, key_min)
    hi0 = jnp.where(pos, key_max, jnp.where(tie0, 1, 0))
    lo0 = jnp.where(small, key_min, lo0)
    hi0 = jnp.where(small, key_min + 1, hi0)
    lf0 = jnp.where(pos | tie0, 0.0, -inf)
    hf0 = jnp.where(pos, inf, 0.0)
    cl0 = jnp.where(pos | tie0, c_ge0, n_adm)
    ch0 = jnp.where(pos, 0.0, jnp.where(tie0, c_gt0, c_ge0))
    t0 = mu + zq[0:1, :] * sd
    one = jnp.ones((1, TQ), F32)

    def mid_key(lo, hi):
        return (lo & hi) + ((lo ^ hi) >> 1)

    def unfinished(lo, hi, cl):
        return jnp.logical_not((cl == kf) | (mid_key(lo, hi) == lo))

    def cond(st):
        it, lo, hi, cl = st[0], st[1], st[2], st[5]
        return jnp.logical_and(it < SEARCH_CAP, jnp.sum(ones_where(unfinished(lo, hi, cl))) > 0.0)

    def step(st):
        it, lo, hi, lf, hf, cl, ch, t, last, wl, wh = st
        act = unfinished(lo, hi, cl)
        guided = (it < SEARCH_SWITCH).astype(jnp.int32)
        mid = mid_key(lo, hi)
        tk = mid + (_f32_to_key(t) - mid) * guided
        tk = jnp.minimum(jnp.maximum(tk, lo + 1), hi - 1)
        tf = _key_to_f32(tk)
        c = count_ge(tf)
        ge = c >= kf
        up_lo = act & ge
        up_hi = act & jnp.logical_not(ge)
        lo = jnp.where(up_lo, tk, lo)
        lf = jnp.where(up_lo, tf, lf)
        cl = jnp.where(up_lo, c, cl)
        hi = jnp.where(up_hi, tk, hi)
        hf = jnp.where(up_hi, tf, hf)
        ch = jnp.where(up_hi, c, ch)
        side = jnp.where(ge, 1.0, -1.0)
        same = side == last
        wh = jnp.where(ge, jnp.where(same, wh * 0.5, one), one)
        wl = jnp.where(ge, one, jnp.where(same, wl * 0.5, one))
        a = (cl - kf + 0.5) * wl
        b = (kf - 0.5 - ch) * wh
        t_bracket = lf + (hf - lf) * (a / (a + b))
        t_model = tf + 1.5 * (c - kf + jnp.where(ge, 0.5, -0.5)) / dens
        bracketed = (lf > -inf) & (hf < inf)
        t = jnp.where(bracketed, t_bracket, t_model)
        return it + 1, lo, hi, lf, hf, cl, ch, t, side, wl, wh

    st = (jnp.int32(0), lo0, hi0, lf0, hf0, cl0, ch0, t0, jnp.zeros((1, TQ), F32), one, one)
    st = lax.fori_loop(0, SEARCH_UNCHECKED, lambda _, s: step(s), st)
    st = lax.while_loop(cond, lambda s: step(step(s)), st)
    lo, cl, ch = st[1], st[5], st[6]
    thr = _key_to_f32(lo)
    tied = (cl != kf) & jnp.logical_not(small)
    need = jnp.where(tied, kf - ch, jnp.float32(1e9))

    def mask_plain(j, carry_):
        sc_ref[j] = jnp.where(sc_ref[j] >= thr, 0.0, NEG)
        return carry_

    def mask_ranked(j, before):
        s = sc_ref[j]
        eq = ones_where(s == thr)
        rank = _dot(tri_ref[...], eq.astype(BF16)) + before
        keep = (s > thr) | ((s == thr) & (rank <= need))
        sc_ref[j] = jnp.where(keep, 0.0, NEG)
        return before + jnp.sum(eq, axis=0, keepdims=True)

    any_tied = jnp.sum(ones_where(tied)) > 0.0

    @pl.when(any_tied)
    def _():
        lax.fori_loop(0, i + 1, mask_ranked, jnp.zeros((1, TQ), F32))

    @pl.when(jnp.logical_not(any_tied))
    def _():
        lax.fori_loop(0, i + 1, mask_plain, 0)

    def attend(tiles, lagged):
        nk = len(tiles) * TK
        ck = jnp.concatenate([ckvn_ref[j] for j, _ in tiles], axis=0)
        ckt = jnp.concatenate([ckt_ref[j] for j, _ in tiles], axis=1)
        mask_bias = jnp.concatenate([sc_ref[j] for j, _ in tiles], axis=0)
        lg_all = _dot_nt(ck, qlat_ref[0].reshape(DSA_HEADS * TQ, LANES))
        ps, alphas = [], []
        for h in range(DSA_HEADS):
            bias = mask_bias
            if any(slot is not None for _, slot in tiles):
                bias = bias + jnp.concatenate(
                    [jnp.zeros((TK, TQ), F32) if slot is None else bnear_ref[h, slot] for _, slot in tiles], axis=0)
            lg3 = (lg_all[:, h * TQ:(h + 1) * TQ] + bias).reshape(nk // 8, 8, TQ)
            m_old = m_ref[h]
            m_new = jnp.maximum(m_old, _sublane_all(jnp.max(lg3, axis=0), jnp.maximum))
            alpha = jnp.exp2(m_old - m_new)
            m_ref[h] = m_new
            if lagged:
                ref = jnp.where(m_old > 0.5 * NEG, m_old, 0.0)
                excess = jnp.maximum(m_new - ref, LAG_DOWN * (ref - m_new))
                x_ref[h] = jnp.maximum(x_ref[h], jnp.where(m_new > 0.5 * NEG, excess, 0.0))
                beta = jnp.exp2(jnp.minimum(ref - m_new, 126.0))
                p3 = jnp.exp2(lg3 - ref[None])
                l_ref[h] = alpha * l_ref[h] + beta * jnp.sum(p3, axis=0)
                pv = _dot(ckt, p3.reshape(nk, TQ).astype(BF16)).reshape(DSA_RKV // 8, 8, TQ)
                acc3 = acc_ref[h].reshape(DSA_RKV // 8, 8, TQ) * alpha[None] + pv * beta[None]
                acc_ref[h] = acc3.reshape(DSA_RKV, TQ)
            else:
                p3 = jnp.exp2(lg3 - m_new[None])
                l_ref[h] = alpha * l_ref[h] + jnp.sum(p3, axis=0)
                pv = _dot(ckt, p3.reshape(nk, TQ).astype(BF16))
                acc3 = acc_ref[h].reshape(DSA_RKV // 8, 8, TQ) * alpha[None]
                acc_ref[h] = acc3.reshape(DSA_RKV, TQ) + pv

    n_far = jnp.maximum(i - 1, 0)

    def attend_all(lagged):
        m_ref[...] = jnp.full(m_ref.shape, NEG, F32)
        l_ref[...] = jnp.zeros(l_ref.shape, F32)
        acc_ref[...] = jnp.zeros(acc_ref.shape, F32)

        @pl.when(i >= 1)
        def _():
            attend([(i - 1, 0), (i, 1)], False)

        @pl.when(i == 0)
        def _():
            attend([(i, 1)], False)

        def far_pair(jj, carry):
            attend([(2 * jj, None), (2 * jj + 1, None)], lagged)
            return carry

        lax.fori_loop(0, n_far // 2, far_pair, 0)

        @pl.when(n_far % 2 == 1)
        def _():
            attend([(n_far - 1, None)], lagged)

    x_ref[...] = jnp.zeros(x_ref.shape, F32)
    attend_all(True)

    @pl.when(jnp.max(x_ref[...]) > LAG_LIMIT)
    def _():
        attend_all(False)


    for h in range(DSA_HEADS):
        l_tot = _sublane_all(l_ref[h], jnp.add)
        o = (acc_ref[h].reshape(DSA_RKV // 8, 8, TQ) / l_tot[None]).reshape(DSA_RKV, TQ)
        o_ref[:, h * LANES:(h + 1) * LANES] = o.T.astype(BF16)


def _t5_bucket(rel):
    nb = REL_BUCKETS // 2
    max_exact = nb // 2
    base = jnp.where(rel > 0, nb, 0)
    n = jnp.abs(rel)
    nf = jnp.maximum(n, 1).astype(jnp.float32)
    large = max_exact + (jnp.log(nf / max_exact) / math.log(REL_MAX_DIST / max_exact)
                         * (nb - max_exact)).astype(jnp.int32)
    large = jnp.minimum(large, nb - 1)
    return base + jnp.where(n < max_exact, n, large)


def _bias_tables(rel_bias):
    s = jnp.arange(TK, dtype=jnp.int32)[:, None]
    t = jnp.arange(TQ, dtype=jnp.int32)[None, :]
    rel = jnp.stack([s - TK - t, s - t])
    table = rel_bias.astype(F32)
    bucket = _t5_bucket(rel)
    far_bucket = _t5_bucket(jnp.full((), -REL_MAX_DIST, jnp.int32))
    near = jnp.zeros((DSA_HEADS,) + rel.shape, F32)
    far = jnp.zeros((DSA_HEADS,), F32)
    for b in range(REL_BUCKETS):
        near = near + jnp.where(bucket[None] == b, table[b][:, None, None, None], 0.0)
        far = far + jnp.where(far_bucket == b, table[b], 0.0)
    return (near - far[:, None, None, None]) * LOG2E


def _search_tables(nq, k_sel):
    nd = statistics.NormalDist()
    tab = np.zeros((nq, 2, TQ), np.float32)
    for i in range(nq):
        for c in range(TQ // CHUNK):
            n = i * TQ + (c + 1) * CHUNK
            z = nd.inv_cdf(1.0 - (k_sel - 0.5) / n) if n > k_sel else 0.0
            tab[i, 0, c * CHUNK:(c + 1) * CHUNK] = z
            tab[i, 1, c * CHUNK:(c + 1) * CHUNK] = n * nd.pdf(z)
    return jnp.asarray(tab)


def _dsa_attention(qlat, qidx, wt, ckvn, ckt, kidx, bnear, B, S):
    nq = S // TQ
    N = B * S
    k_sel = min(TOPK_MAX, S // 4)
    zq = _search_tables(nq, k_sel)
    tri = jnp.asarray(np.tril(np.ones((TK, TK), np.float32)), dtype=BF16)
    stacked = pl.BlockSpec((1, DSA_HEADS, TQ, LANES), lambda b, i: (b * nq + i, 0, 0, 0))
    keys = pl.BlockSpec((nq, TK, LANES), lambda b, i: (b, 0, 0))
    keys_t = pl.BlockSpec((nq, LANES, TK), lambda b, i: (b, 0, 0))
    return pl.pallas_call(
        functools.partial(_dsa_kernel, k_sel=k_sel),
        grid=(B, nq),
        in_specs=[stacked, stacked, pl.BlockSpec((IDX_HEADS, TQ), lambda b, i: (0, b * nq + i)),
                  keys, keys_t, keys,
                  pl.BlockSpec(bnear.shape, lambda b, i: (0, 0, 0, 0)),
                  pl.BlockSpec((1, 2, TQ), lambda b, i: (i, 0, 0)),
                  pl.BlockSpec((TK, TK), lambda b, i: (0, 0))],
        out_specs=pl.BlockSpec((TQ, DSA_HEADS * DSA_RKV), lambda b, i: (b * nq + i, 0)),
        out_shape=jax.ShapeDtypeStruct((N, DSA_HEADS * DSA_RKV), BF16),
        scratch_shapes=[pltpu.VMEM((nq, TK, TQ), F32),
                        pltpu.VMEM((DSA_HEADS, 8, TQ), F32),
                        pltpu.VMEM((DSA_HEADS, 8, TQ), F32),
                        pltpu.VMEM((DSA_HEADS, 8, TQ), F32),
                        pltpu.VMEM((DSA_HEADS, DSA_RKV, TQ), F32)],
        compiler_params=_params(("parallel", "arbitrary")),
        name="dsa_attention",
    )(qlat, qidx, wt, ckvn, ckt, kidx, bnear, zq, tri)


def _merge_kernel(x_ref, ret_ref, olat_ref, mq_ref, gates_ref, km_ref, vm_ref,
                  wret_ref, wdsa_ref, wmem_ref, wout_ref, o_ref):
    mq = mq_ref[...]
    km = km_ref[0]
    vm = vm_ref[0]
    pvs = []
    for h in range(MEM_HEADS):
        sl = slice(h * MEM_DH, (h + 1) * MEM_DH)
        lg = _dot_nt(mq[:, sl], km[:, sl]) * (MEM_DH ** -0.5)
        p = jnp.exp(lg - jnp.max(lg, axis=1, keepdims=True))
        p = p / jnp.sum(p, axis=1, keepdims=True)
        pvs.append(_dot(p.astype(BF16), vm[:, sl]).astype(BF16))
    mem_b = _dot(jnp.concatenate(pvs, axis=1), wmem_ref[...])
    ret_b = _dot(ret_ref[...], wret_ref[...])
    dsa_b = _dot(olat_ref[...], wdsa_ref[...])
    g = jax.nn.sigmoid(gates_ref[...].astype(F32))
    merged = (g[:, :D_MODEL] * ret_b + g[:, D_MODEL:2 * D_MODEL] * dsa_b + g[:, 2 * D_MODEL:] * mem_b)
    o_ref[...] = x_ref[...] + _dot(merged.astype(BF16), wout_ref[...])


def _merge(x2d, ret, olat, mq, gates, km, vm, wret, wdsa, wmem, wout, S):
    N = x2d.shape[0]
    per_b = S // TM_MERGE
    row = lambda i: (i, 0)
    const = lambda i: (0, 0)
    memb = lambda i: (i // per_b, 0, 0)
    return pl.pallas_call(
        _merge_kernel,
        grid=(N // TM_MERGE,),
        in_specs=[pl.BlockSpec((TM_MERGE, D_MODEL), row), pl.BlockSpec((TM_MERGE, RET_W), row),
                  pl.BlockSpec((TM_MERGE, DSA_HEADS * DSA_RKV), row), pl.BlockSpec((TM_MERGE, MEM_W), row),
                  pl.BlockSpec((TM_MERGE, 3 * D_MODEL), row),
                  pl.BlockSpec((1, MEM_LEN, MEM_W), memb), pl.BlockSpec((1, MEM_LEN, MEM_W), memb),
                  pl.BlockSpec(wret.shape, const), pl.BlockSpec(wdsa.shape, const),
                  pl.BlockSpec(wmem.shape, const), pl.BlockSpec(wout.shape, const)],
        out_specs=pl.BlockSpec((TM_MERGE, D_MODEL), row),
        out_shape=jax.ShapeDtypeStruct(x2d.shape, F32),
        compiler_params=_params(("parallel",)),
        name="merge_out",
    )(x2d, ret, olat, mq, gates, km, vm, wret, wdsa, wmem, wout)


def _mlp_kernel(x_ref, g_ref, w1_ref, w2_ref, gf_ref, o_ref, *, final):
    x = x_ref[...]
    h = _rms(x, g_ref[...]).astype(BF16)
    acc = jnp.zeros(x.shape, F32)
    for c in range(D_FF // D_MODEL):
        sl = slice(c * D_MODEL, (c + 1) * D_MODEL)
        a = jnp.maximum(_dot(h, w1_ref[:, sl]), 0.0)
        acc = acc + _dot((a * a).astype(BF16), w2_ref[sl, :])
    y = x + acc
    if final:
        y = _rms(y, gf_ref[...])
    o_ref[...] = y


def _mlp(x2d, g, w1, w2, gf, final):
    N = x2d.shape[0]
    row = lambda i: (i, 0)
    const = lambda i: (0, 0)
    return pl.pallas_call(
        functools.partial(_mlp_kernel, final=final),
        grid=(N // TM_MLP,),
        in_specs=[pl.BlockSpec((TM_MLP, D_MODEL), row), pl.BlockSpec((1, D_MODEL), const),
                  pl.BlockSpec(w1.shape, const), pl.BlockSpec(w2.shape, const),
                  pl.BlockSpec((1, D_MODEL), const)],
        out_specs=pl.BlockSpec((TM_MLP, D_MODEL), row),
        out_shape=jax.ShapeDtypeStruct(x2d.shape, F32),
        compiler_params=_params(("parallel",)),
        name="mlp",
    )(x2d, g, w1, w2, gf)


def _rope_tables(S):
    half = RET_DK // 2
    pos = jnp.arange(S, dtype=F32)
    freqs = ROPE_BASE ** (-jnp.arange(half, dtype=F32) / half)
    ang = pos[:, None] * freqs[None, :]
    return jnp.tile(jnp.cos(ang), (1, RET_HEADS)), jnp.tile(jnp.sin(ang), (1, RET_HEADS))


def _in_weights(wi, w_iq_l):
    half = RET_DK // 2
    def gather_halves(w):
        return jnp.transpose(w.reshape(D_MODEL, RET_HEADS, 2, half), (0, 2, 1, 3)).reshape(D_MODEL, RET_W)

    offs = np.concatenate([[0], np.cumsum(IN_SIZES)])
    seg = lambda k: wi[:, int(offs[k]):int(offs[k + 1])]
    wa = jnp.concatenate([gather_halves(seg(0)), gather_halves(seg(1)), seg(2), seg(3), seg(8), seg(9)],
                         axis=1).astype(BF16)
    pad = jnp.zeros((D_MODEL, LANES - IDX_HEADS), F32)
    widx = jnp.concatenate([seg(4), seg(5), jnp.tile(seg(6), (1, 4)), seg(7), pad], axis=1)
    wih, wil = _split(widx)
    wiq = jnp.tile(w_iq_l[:, :, None, :], (1, 1, 4, 1)).reshape(DSA_RQ, IDX_HEADS * LANES)
    wiqh, wiql = _split(wiq)
    return wa, wih, wil, wiqh, wiql


def kernel(x, mem, norm1, w_in, q_norm, kv_norm, w_uq, w_iq, w_uk, w_uv, mem_norm, w_mem_kv,
           w_ret_o, w_dsa_o, w_mem_o, w_out, norm2, w_ff1, w_ff2, rel_bias, final_norm):
    B, S, D = x.shape
    depth = w_in.shape[0]
    assert D == D_MODEL and S % TM_MLP == 0 and S % TQ == 0 and TQ == TK
    cos_t, sin_t = _rope_tables(S)
    bnear = _bias_tables(rel_bias)
    wql_all, wdsa_all = _fold_weights(w_uq, w_uk, w_uv, w_dsa_o)
    x2d = x.reshape(B * S, D)
    row = lambda v: v.reshape(1, -1)
    for l in range(depth):
        wa, wih, wil, wiqh, wiql = _in_weights(w_in[l], w_iq[l])
        rq, rk, rv, rg, mq, gates, idx = _in_proj(x2d, row(norm1[l]), cos_t, sin_t, wa, wih, wil, S)
        ret = _retention(rq, rk, rv, rg, B, S)
        qlat, qidx, wt, ckvn, ckt, kidx = _dsa_prep(idx, row(q_norm[l]), row(kv_norm[l]), wql_all[l], wiqh, wiql)
        olat = _dsa_attention(qlat, qidx, wt, ckvn, ckt, kidx, bnear, B, S)
        km, vm = _mem_kv(mem, row(mem_norm[l]), w_mem_kv[l].astype(BF16))
        x2d = _merge(x2d, ret, olat, mq, gates, km, vm, w_ret_o[l].astype(BF16), wdsa_all[l],
                     w_mem_o[l].astype(BF16), w_out[l].astype(BF16), S)
        x2d = _mlp(x2d, row(norm2[l]), w_ff1[l].astype(BF16), w_ff2[l].astype(BF16), row(final_norm),
                   final=(l == depth - 1))
    return x2d.reshape(B, S, D)
```

```python
import functools
import math
import statistics

import numpy as np
import jax
import jax.numpy as jnp
from jax import lax
from jax.experimental import pallas as pl
from jax.experimental.pallas import tpu as pltpu

F32 = jnp.float32
BF16 = jnp.bfloat16

D_MODEL = 1024
CHUNK = 64
EPS = 1e-6
GN_EPS = 1e-5
RET_HEADS = 8
RET_DK = 64
RET_W = 512
ROPE_BASE = 10000.0
DSA_HEADS = 8
DSA_DH = 64
DSA_RQ = 256
DSA_RKV = 128
IDX_HEADS = 8
IDX_DIM = 32
TOPK_MAX = 256
MEM_LEN = 256
MEM_HEADS = 4
MEM_DH = 128
MEM_W = 512
REL_BUCKETS = 32
REL_MAX_DIST = 128
D_FF = 4096
IN_SIZES = (512, 512, 512, 512, DSA_RQ, DSA_RKV, IDX_DIM, IDX_HEADS, MEM_W, 3 * D_MODEL)

LANES = 128
IDX_COLS = 640
NEG = -1e30
LOG2E = math.log2(math.e)
F32_MAX = float(np.finfo(np.float32).max)
VMEM_LIMIT = 56 * 1024 * 1024

TM_IN = 256
T_RET = 256
TQ = 256
TK = 256
LAG_LIMIT = 100.0
LAG_DOWN = 5.0
SEARCH_UNCHECKED = 8
SEARCH_SWITCH = 12
SEARCH_CAP = SEARCH_SWITCH + 34
TM_MERGE = 512
TM_MLP = 512


def _dot(a, b):
    return jnp.dot(a, b, preferred_element_type=F32)


def _dot_nt(a, b):
    return lax.dot_general(a, b, (((1,), (1,)), ((), ())), preferred_element_type=F32)


def _split(a):
    hi = a.astype(BF16)
    lo = (a - hi.astype(F32)).astype(BF16)
    return hi, lo


def _rms(x, g):
    return x * lax.rsqrt(jnp.mean(x * x, axis=-1, keepdims=True) + EPS) * g


def _params(sem):
    return pltpu.CompilerParams(dimension_semantics=sem, vmem_limit_bytes=VMEM_LIMIT)


def _fold_kernel(uq_ref, uk_ref, uv_ref, wo_ref, wql_ref, wdsa_ref):
    uq_h, uq_l = _split(uq_ref[0, 0])
    uk_h, uk_l = _split(uk_ref[0, 0])
    ql = _dot_nt(uq_h, uk_h) + _dot_nt(uq_h, uk_l) + _dot_nt(uq_l, uk_h)
    wql_ref[0] = (ql * (DSA_DH ** -0.5 * LOG2E)).astype(BF16)
    uv_h, uv_l = _split(uv_ref[0, 0])
    wo_h, wo_l = _split(wo_ref[0, 0])
    wdsa_ref[0] = (_dot(uv_h, wo_h) + _dot(uv_h, wo_l) + _dot(uv_l, wo_h)).astype(BF16)


def _fold_weights(w_uq, w_uk, w_uv, w_dsa_o):
    L = w_uq.shape[0]
    uq = jnp.transpose(w_uq, (0, 2, 1, 3))
    uk = jnp.transpose(w_uk, (0, 2, 1, 3))
    uv = jnp.transpose(w_uv, (0, 2, 1, 3))
    wo = w_dsa_o.reshape(L, DSA_HEADS, DSA_DH, D_MODEL)
    return pl.pallas_call(
        _fold_kernel,
        grid=(L, DSA_HEADS),
        in_specs=[
            pl.BlockSpec((1, 1, DSA_RQ, DSA_DH), lambda l, h: (l, h, 0, 0)),
            pl.BlockSpec((1, 1, DSA_RKV, DSA_DH), lambda l, h: (l, h, 0, 0)),
            pl.BlockSpec((1, 1, DSA_RKV, DSA_DH), lambda l, h: (l, h, 0, 0)),
            pl.BlockSpec((1, 1, DSA_DH, D_MODEL), lambda l, h: (l, h, 0, 0)),
        ],
        out_specs=[
            pl.BlockSpec((1, DSA_RQ, DSA_RKV), lambda l, h: (l, 0, h)),
            pl.BlockSpec((1, DSA_RKV, D_MODEL), lambda l, h: (l, h, 0)),
        ],
        out_shape=[
            jax.ShapeDtypeStruct((L, DSA_RQ, DSA_HEADS * DSA_RKV), BF16),
            jax.ShapeDtypeStruct((L, DSA_HEADS * DSA_RKV, D_MODEL), BF16),
        ],
        compiler_params=_params(("parallel", "parallel")),
        name="fold_weights",
    )(uq, uk, uv, wo)


def _in_kernel(x_ref, g_ref, cos_ref, sin_ref, wa_ref, widx_ref,
               rq_ref, rk_ref, rv_ref, rg_ref, mq_ref, gates_ref, idx_ref):
    h = _rms(x_ref[...], g_ref[...])
    hb = h.astype(BF16)
    c = cos_ref[...]
    s = sin_ref[...]
    half = RET_W // 2

    def rope_store(ref, z, scale):
        x1 = z[:, :half]
        x2 = z[:, half:]
        ref[:, :half] = ((x1 * c - x2 * s) * scale).astype(BF16)
        ref[:, half:] = ((x2 * c + x1 * s) * scale).astype(BF16)

    rope_store(rq_ref, _dot(hb, wa_ref[:, 0:512]), 1.0)
    rope_store(rk_ref, _dot(hb, wa_ref[:, 512:1024]), RET_DK ** -0.5)
    rv_ref[...] = _dot(hb, wa_ref[:, 1024:1536]).astype(BF16)
    rg_ref[...] = _dot(hb, wa_ref[:, 1536:2048]).astype(BF16)
    mq_ref[...] = _dot(hb, wa_ref[:, 2048:2560]).astype(BF16)
    for j in range(6):
        gates_ref[:, j * 512:(j + 1) * 512] = _dot(hb, wa_ref[:, 2560 + j * 512:3072 + j * 512]).astype(BF16)
    idx_ref[...] = _dot(hb, widx_ref[...])


def _in_proj(x2d, g, cos_t, sin_t, wa, widx, S):
    N = x2d.shape[0]
    n_pos = S // TM_IN
    row = lambda i: (i, 0)
    const = lambda i: (0, 0)
    pos = lambda i: (i % n_pos, 0)
    bf = lambda w: jax.ShapeDtypeStruct((N, w), BF16)
    return pl.pallas_call(
        _in_kernel,
        grid=(N // TM_IN,),
        in_specs=[
            pl.BlockSpec((TM_IN, D_MODEL), row),
            pl.BlockSpec((1, D_MODEL), const),
            pl.BlockSpec((TM_IN, RET_W // 2), pos),
            pl.BlockSpec((TM_IN, RET_W // 2), pos),
            pl.BlockSpec(wa.shape, const),
            pl.BlockSpec(widx.shape, const),
        ],
        out_specs=[
            pl.BlockSpec((TM_IN, 512), row), pl.BlockSpec((TM_IN, 512), row),
            pl.BlockSpec((TM_IN, 512), row), pl.BlockSpec((TM_IN, 512), row),
            pl.BlockSpec((TM_IN, 512), row), pl.BlockSpec((TM_IN, 3 * D_MODEL), row),
            pl.BlockSpec((TM_IN, IDX_COLS), row),
        ],
        out_shape=[bf(512), bf(512), bf(512), bf(512), bf(512), bf(3 * D_MODEL),
                   jax.ShapeDtypeStruct((N, IDX_COLS), F32)],
        compiler_params=_params(("parallel",)),
        name="in_proj",
    )(x2d, g, cos_t, sin_t, wa, widx)


def _ret_kernel(rq_ref, rk_ref, rv_ref, rg_ref, dmat_ref, qd_ref, kd_ref, hq_ref, hv_ref,
                cdec_ref, bd_ref, p_ref, o_ref, s_ref):
    @pl.when(pl.program_id(1) == 0)
    def _():
        s_ref[...] = jnp.zeros_like(s_ref)

    q = rq_ref[...]
    k = rk_ref[...]
    v = rv_ref[...]
    def head_lane_tiles(a, h):
        c = h // (LANES // (RET_DK // 2))
        return jnp.concatenate([a[:, c * LANES:(c + 1) * LANES], a[:, (2 + c) * LANES:(3 + c) * LANES]], axis=1)

    o_cols = []
    for pair in range(RET_HEADS // 2):
        cols = slice(pair * LANES, (pair + 1) * LANES)
        o_pair = jnp.zeros((q.shape[0], LANES), F32)
        for h in (2 * pair, 2 * pair + 1):
            sc = _dot_nt(head_lane_tiles(q * hq_ref[h], h), head_lane_tiles(k, h)) * dmat_ref[h]
            o_pair = o_pair + _dot(sc.astype(BF16), v[:, cols]) * hv_ref[h][:, cols]
        o_cols.append(o_pair)
    o = jnp.concatenate(o_cols, axis=1)
    state = s_ref[...]
    qf = q.astype(F32) * qd_ref[...]
    o = o + _dot(qf.astype(BF16), state.astype(BF16))
    kf = k.astype(F32) * kd_ref[...]
    kv = _dot(kf.T.astype(BF16), v)
    s_ref[...] = state * cdec_ref[...] + kv * bd_ref[...]

    p = p_ref[...]

    oh, ol = _split(o)
    d = o - (_dot(oh, p) + _dot(ol, p))
    y = d * lax.rsqrt(_dot((d * d).astype(BF16), p) + GN_EPS)
    g = rg_ref[...].astype(F32)
    o_ref[...] = (g * jax.nn.sigmoid(g) * y).astype(BF16)


def _ret_consts():
    T = T_RET
    hh = np.arange(RET_HEADS, dtype=np.float64)
    log_g = np.log1p(-np.exp2(-5.0 - hh))
    t = np.arange(T)
    ct = t // CHUNK
    diff = (t[:, None] - t[None, :]).astype(np.float64)
    same = ct[:, None] == ct[None, :]
    past = ct[None, :] < ct[:, None]
    expo = np.where(same, np.abs(diff), diff)
    dmat = np.where((same | past)[None], np.exp(log_g[:, None, None] * expo[None]), 0.0)
    lane = np.arange(RET_W)
    hk = (lane % (RET_W // 2)) // (RET_DK // 2)
    hv = lane // RET_DK
    qd = np.exp(log_g[hk][None, :] * (t[:, None] + 1.0))
    kd = np.exp(log_g[hk][None, :] * (T - 1.0 - t[:, None]))
    hq_mask = (hk[None, :] == np.arange(RET_HEADS)[:, None]).astype(np.float32)[:, None, :]
    hv_mask = (hv[None, :] == np.arange(RET_HEADS)[:, None]).astype(np.float32)[:, None, :]
    bd = (hk[:, None] == hv[None, :]).astype(np.float32)
    cdec = np.broadcast_to(np.exp(log_g[hk] * T)[:, None], (RET_W, RET_W))
    pmat = (hv[:, None] == hv[None, :]).astype(np.float32) / RET_DK
    f = lambda a: jnp.asarray(np.asarray(a, dtype=np.float32))
    return (f(dmat), f(qd), f(kd), jnp.asarray(hq_mask, dtype=BF16), f(hv_mask), f(cdec), f(bd),
            jnp.asarray(pmat, dtype=BF16))


def _retention(rq, rk, rv, rg, B, S):
    consts = _ret_consts()
    nb = S // T_RET
    tok = lambda b, j: (b * nb + j, 0)
    full = lambda a: pl.BlockSpec(a.shape, lambda b, j: (0,) * a.ndim)
    return pl.pallas_call(
        _ret_kernel,
        grid=(B, nb),
        in_specs=[pl.BlockSpec((T_RET, RET_W), tok)] * 4 + [full(a) for a in consts],
        out_specs=pl.BlockSpec((T_RET, RET_W), tok),
        out_shape=jax.ShapeDtypeStruct(rq.shape, BF16),
        scratch_shapes=[pltpu.VMEM((RET_W, RET_W), F32)],
        compiler_params=_params(("parallel", "arbitrary")),
        name="retention",
    )(rq, rk, rv, rg, *consts)


def _memkv_kernel(mem_ref, g_ref, w_ref, k_ref, v_ref):
    mn = _rms(mem_ref[0], g_ref[...]).astype(BF16)
    kv = _dot(mn, w_ref[...])
    k_ref[0] = kv[:, :MEM_W].astype(BF16)
    v_ref[0] = kv[:, MEM_W:].astype(BF16)


def _mem_kv(mem, g, w):
    B = mem.shape[0]
    return pl.pallas_call(
        _memkv_kernel,
        grid=(B,),
        in_specs=[pl.BlockSpec((1, MEM_LEN, D_MODEL), lambda b: (b, 0, 0)),
                  pl.BlockSpec((1, D_MODEL), lambda b: (0, 0)),
                  pl.BlockSpec(w.shape, lambda b: (0, 0))],
        out_specs=[pl.BlockSpec((1, MEM_LEN, MEM_W), lambda b: (b, 0, 0))] * 2,
        out_shape=[jax.ShapeDtypeStruct((B, MEM_LEN, MEM_W), BF16)] * 2,
        compiler_params=_params(("parallel",)),
        name="mem_kv",
    )(mem, g, w)


def _prep_kernel(idx_ref, qn_ref, kvn_ref, wql_ref, wiq_ref,
                 qlat_ref, qidx_ref, wt_ref, ckvn_ref, ckt_ref, kidx_ref):
    z = idx_ref[...]
    cqn = _rms(z[:, :DSA_RQ], qn_ref[...])
    cb = cqn.astype(BF16)
    ql = _dot(cb, wql_ref[...])
    y = _dot(cb, wiq_ref[...])
    yh = y.astype(BF16).astype(F32)
    lane = lax.broadcasted_iota(jnp.int32, y.shape, 1)
    qsel = jnp.where((lane % LANES) < 2 * IDX_DIM, yh, y - yh).astype(BF16)
    for h in range(DSA_HEADS):
        qlat_ref[0, h] = ql[:, h * LANES:(h + 1) * LANES].astype(BF16)
        qidx_ref[0, h] = qsel[:, h * LANES:(h + 1) * LANES]
    ckn = _rms(z[:, DSA_RQ:DSA_RQ + DSA_RKV], kvn_ref[...])
    ckvn_ref[0] = ckn.astype(BF16)
    ckt_ref[0] = ckn.T.astype(BF16)
    ik = z[:, 384:512]
    ikh = ik.astype(BF16).astype(F32)
    lane1 = lax.broadcasted_iota(jnp.int32, ik.shape, 1)
    kidx_ref[0] = jnp.where((lane1 // IDX_DIM) % 2 == 0, ikh, ik - ikh).astype(BF16)
    wfull = z[:, 512:640] * ((IDX_HEADS ** -0.5) * (IDX_DIM ** -0.5))
    wt_ref[...] = wfull.T[:IDX_HEADS, :]


def _dsa_prep(idx, qn, kvn, wql, wiq):
    N = idx.shape[0]
    nt = N // TQ
    const = lambda i: (0, 0)
    tile3 = lambda i: (i, 0, 0)
    stacked = pl.BlockSpec((1, DSA_HEADS, TQ, LANES), lambda i: (i, 0, 0, 0))
    return pl.pallas_call(
        _prep_kernel,
        grid=(nt,),
        in_specs=[pl.BlockSpec((TQ, IDX_COLS), lambda i: (i, 0)),
                  pl.BlockSpec((1, DSA_RQ), const), pl.BlockSpec((1, DSA_RKV), const),
                  pl.BlockSpec(wql.shape, const), pl.BlockSpec(wiq.shape, const)],
        out_specs=[stacked, stacked, pl.BlockSpec((IDX_HEADS, TQ), lambda i: (0, i)),
                   pl.BlockSpec((1, TQ, LANES), tile3), pl.BlockSpec((1, LANES, TQ), tile3),
                   pl.BlockSpec((1, TQ, LANES), tile3)],
        out_shape=[jax.ShapeDtypeStruct((nt, DSA_HEADS, TQ, LANES), BF16),
                   jax.ShapeDtypeStruct((nt, DSA_HEADS, TQ, LANES), BF16),
                   jax.ShapeDtypeStruct((IDX_HEADS, N), F32),
                   jax.ShapeDtypeStruct((nt, TQ, LANES), BF16),
                   jax.ShapeDtypeStruct((nt, LANES, TQ), BF16),
                   jax.ShapeDtypeStruct((nt, TQ, LANES), BF16)],
        compiler_params=_params(("parallel",)),
        name="dsa_prep",
    )(idx, qn, kvn, wql, wiq)


def _key_to_f32(key):
    bits = jnp.where(key >= 0, key, key ^ jnp.int32(0x7FFFFFFF))
    return lax.bitcast_convert_type(bits, F32)


def _f32_to_key(x):
    bits = lax.bitcast_convert_type(x, jnp.int32)
    return jnp.where(bits >= 0, bits, bits ^ jnp.int32(0x7FFFFFFF))


def _sublane_all(a, op):
    for shift in (4, 2, 1):
        a = op(a, pltpu.roll(a, shift, 0))
    return a


def _dsa_kernel(qlat_ref, qidx_ref, wt_ref, ckvn_ref, ckt_ref, kidx_ref, bnear_ref, zq_ref, tri_ref, o_ref,
                sc_ref, m_ref, l_ref, x_ref, acc_ref, *, k_sel):
    i = pl.program_id(1)
    wt = wt_ref[...]

    def score_tiles(js):
        keys = jnp.concatenate([kidx_ref[j] for j in js], axis=0)
        y_all = _dot_nt(keys, qidx_ref[0].reshape(IDX_HEADS * TQ, LANES))
        tot = jnp.zeros((len(js) * TK, TQ), F32)
        for h in range(IDX_HEADS):
            tot = tot + jnp.maximum(y_all[:, h * TQ:(h + 1) * TQ], 0.0) * wt[h:h + 1, :]
        return tot

    def colsum8(a):
        return jnp.sum(a.reshape(a.shape[0] // 8, 8, TQ), axis=0)

    def ones_where(cond_):
        return jnp.where(cond_, 1.0, 0.0)

    def stats(s_counted, s_summed):
        return (colsum8(s_summed), colsum8(s_summed * s_summed),
                colsum8(ones_where(s_counted >= 0.0)), colsum8(ones_where(s_counted > 0.0)))

    def add_stats(carry, s):
        return tuple(a + b for a, b in zip(carry, stats(s, s)))

    def fill_pair(jj, carry):
        s = score_tiles([2 * jj, 2 * jj + 1])
        sc_ref[2 * jj] = s[:TK]
        sc_ref[2 * jj + 1] = s[TK:]
        return add_stats(carry, s)

    def fill_one(carry):
        s = score_tiles([i - 1])
        sc_ref[i - 1] = s
        return add_stats(carry, s)

    zero8 = jnp.zeros((8, TQ), F32)
    carry = lax.fori_loop(0, i // 2, fill_pair, (zero8, zero8, zero8, zero8))
    carry = lax.cond(i % 2 == 1, fill_one, lambda c: c, carry)
    key_r = lax.broadcasted_iota(jnp.int32, (TK, TQ), 0)
    qry_c = lax.broadcasted_iota(jnp.int32, (TK, TQ), 1)
    admissible = key_r <= (qry_c // CHUNK) * CHUNK + (CHUNK - 1)
    s_diag = score_tiles([i])
    s_masked = jnp.where(admissible, s_diag, -jnp.inf)
    sc_ref[i] = s_masked
    carry = tuple(a + b for a, b in zip(carry, stats(s_masked, jnp.where(admissible, s_diag, 0.0))))
    s1, s2, c_ge0, c_gt0 = (_sublane_all(a, jnp.add)[0:1, :] for a in carry)

    def count_ge(thr):
        def pair(jj, c):
            ca, cb = c
            return (ca + colsum8(ones_where(sc_ref[2 * jj] >= thr)),
                    cb + colsum8(ones_where(sc_ref[2 * jj + 1] >= thr)))
        ca, cb = lax.fori_loop(0, (i + 1) // 2, pair, (zero8, zero8))
        odd = lax.cond((i + 1) % 2 == 1, lambda: colsum8(ones_where(sc_ref[i] >= thr)), lambda: zero8)
        return _sublane_all(ca + cb + odd, jnp.add)[0:1, :]

    qc = lax.broadcasted_iota(jnp.int32, (1, TQ), 1)
    n_adm = (i * TQ + (qc // CHUNK + 1) * CHUNK).astype(F32)
    kf = jnp.float32(k_sel)
    key_min = jnp.int32(np.int32(np.array(-F32_MAX, np.float32).view(np.int32)) ^ np.int32(0x7FFFFFFF))
    key_max = jnp.int32(np.array(np.inf, np.float32).view(np.int32))
    inf = jnp.float32(np.inf)
    zq = zq_ref[0]
    mu = s1 / n_adm
    sd = jnp.sqrt(jnp.maximum(s2 / n_adm - mu * mu, 1e-30))
    dens = zq[1:2, :] / sd
    pos = c_gt0 >= kf
    tie0 = jnp.logical_and(jnp.logical_not(pos), c_ge0 >= kf)
    small = n_adm <= kf
    lo0 = jnp.where(pos | tie0, 0, key_min)
    hi0 = jnp.where(pos, key_max, jnp.where(tie0, 1, 0))
    lo0 = jnp.where(small, key_min, lo0)
    hi0 = jnp.where(small, key_min + 1, hi0)
    lf0 = jnp.where(pos | tie0, 0.0, -inf)
    hf0 = jnp.where(pos, inf, 0.0)
    cl0 = jnp.where(pos | tie0, c_ge0, n_adm)
    ch0 = jnp.where(pos, 0.0, jnp.where(tie0, c_gt0, c_ge0))
    t0 = mu + zq[0:1, :] * sd
    one = jnp.ones((1, TQ), F32)

    def mid_key(lo, hi):
        return (lo & hi) + ((lo ^ hi) >> 1)

    def unfinished(lo, hi, cl):
        return jnp.logical_not((cl == kf) | (mid_key(lo, hi) == lo))

    def cond(st):
        it, lo, hi, cl = st[0], st[1], st[2], st[5]
        return jnp.logical_and(it < SEARCH_CAP, jnp.sum(ones_where(unfinished(lo, hi, cl))) > 0.0)

    def step(st):
        it, lo, hi, lf, hf, cl, ch, t, last, wl, wh = st
        act = unfinished(lo, hi, cl)
        guided = (it < SEARCH_SWITCH).astype(jnp.int32)
        mid = mid_key(lo, hi)
        tk = mid + (_f32_to_key(t) - mid) * guided
        tk = jnp.minimum(jnp.maximum(tk, lo + 1), hi - 1)
        tf = _key_to_f32(tk)
        c = count_ge(tf)
        ge = c >= kf
        up_lo = act & ge
        up_hi = act & jnp.logical_not(ge)
        lo = jnp.where(up_lo, tk, lo)
        lf = jnp.where(up_lo, tf, lf)
        cl = jnp.where(up_lo, c, cl)
        hi = jnp.where(up_hi, tk, hi)
        hf = jnp.where(up_hi, tf, hf)
        ch = jnp.where(up_hi, c, ch)
        side = jnp.where(ge, 1.0, -1.0)
        same = side == last
        wh = jnp.where(ge, jnp.where(same, wh * 0.5, one), one)
        wl = jnp.where(ge, one, jnp.where(same, wl * 0.5, one))
        a = (cl - kf + 0.5) * wl
        b = (kf - 0.5 - ch) * wh
        t_bracket = lf + (hf - lf) * (a / (a + b))
        t_model = tf + 1.5 * (c - kf + jnp.where(ge, 0.5, -0.5)) / dens
        bracketed = (lf > -inf) & (hf < inf)
        t = jnp.where(bracketed, t_bracket, t_model)
        return it + 1, lo, hi, lf, hf, cl, ch, t, side, wl, wh

    st = (jnp.int32(0), lo0, hi0, lf0, hf0, cl0, ch0, t0, jnp.zeros((1, TQ), F32), one, one)
    st = lax.fori_loop(0, SEARCH_UNCHECKED, lambda _, s: step(s), st)
    st = lax.while_loop(cond, lambda s: step(step(s)), st)
    lo, cl, ch = st[1], st[5], st[6]
    thr = _key_to_f32(lo)
    tied = (cl != kf) & jnp.logical_not(small)
    need = jnp.where(tied, kf - ch, jnp.float32(1e9))

    def mask_plain(j, carry_):
        sc_ref[j] = jnp.where(sc_ref[j] >= thr, 0.0, NEG)
        return carry_

    def mask_ranked(j, before):
        s = sc_ref[j]
        eq = ones_where(s == thr)
        rank = _dot(tri_ref[...], eq.astype(BF16)) + before
        keep = (s > thr) | ((s == thr) & (rank <= need))
        sc_ref[j] = jnp.where(keep, 0.0, NEG)
        return before + jnp.sum(eq, axis=0, keepdims=True)

    any_tied = jnp.sum(ones_where(tied)) > 0.0

    def mask_ranked_pair(jj, before):
        return mask_ranked(2 * jj + 1, mask_ranked(2 * jj, before))

    @pl.when(any_tied)
    def _():
        before = lax.fori_loop(0, (i + 1) // 2, mask_ranked_pair, jnp.zeros((1, TQ), F32))

        @pl.when((i + 1) % 2 == 1)
        def _():
            mask_ranked(i, before)

    @pl.when(jnp.logical_not(any_tied))
    def _():
        lax.fori_loop(0, i + 1, mask_plain, 0)

    def attend(tiles, lagged):
        nk = len(tiles) * TK
        ck = jnp.concatenate([ckvn_ref[j] for j, _ in tiles], axis=0)
        ckt = jnp.concatenate([ckt_ref[j] for j, _ in tiles], axis=1)
        mask_bias = jnp.concatenate([sc_ref[j] for j, _ in tiles], axis=0)
        lg_all = _dot_nt(ck, qlat_ref[0].reshape(DSA_HEADS * TQ, LANES))
        ps, alphas = [], []
        for h in range(DSA_HEADS):
            bias = mask_bias
            if any(slot is not None for _, slot in tiles):
                bias = bias + jnp.concatenate(
                    [jnp.zeros((TK, TQ), F32) if slot is None else bnear_ref[h, slot] for _, slot in tiles], axis=0)
            lg3 = (lg_all[:, h * TQ:(h + 1) * TQ] + bias).reshape(nk // 8, 8, TQ)
            m_old = m_ref[h]
            m_new = jnp.maximum(m_old, _sublane_all(jnp.max(lg3, axis=0), jnp.maximum))
            alpha = jnp.exp2(m_old - m_new)
            m_ref[h] = m_new
            if lagged:
                ref = jnp.where(m_old > 0.5 * NEG, m_old, 0.0)
                excess = jnp.maximum(m_new - ref, LAG_DOWN * (ref - m_new))
                x_ref[h] = jnp.maximum(x_ref[h], jnp.where(m_new > 0.5 * NEG, excess, 0.0))
                beta = jnp.exp2(jnp.minimum(ref - m_new, 126.0))
                p3 = jnp.exp2(lg3 - ref[None])
                l_ref[h] = alpha * l_ref[h] + beta * jnp.sum(p3, axis=0)
                pv = _dot(ckt, p3.reshape(nk, TQ).astype(BF16)).reshape(DSA_RKV // 8, 8, TQ)
                acc3 = acc_ref[h].reshape(DSA_RKV // 8, 8, TQ) * alpha[None] + pv * beta[None]
                acc_ref[h] = acc3.reshape(DSA_RKV, TQ)
            else:
                p3 = jnp.exp2(lg3 - m_new[None])
                l_ref[h] = alpha * l_ref[h] + jnp.sum(p3, axis=0)
                pv = _dot(ckt, p3.reshape(nk, TQ).astype(BF16))
                acc3 = acc_ref[h].reshape(DSA_RKV // 8, 8, TQ) * alpha[None]
                acc_ref[h] = acc3.reshape(DSA_RKV, TQ) + pv

    n_far = jnp.maximum(i - 1, 0)

    def attend_all(lagged):
        m_ref[...] = jnp.full(m_ref.shape, NEG, F32)
        l_ref[...] = jnp.zeros(l_ref.shape, F32)
        acc_ref[...] = jnp.zeros(acc_ref.shape, F32)

        @pl.when(i >= 1)
        def _():
            attend([(i - 1, 0), (i, 1)], False)

        @pl.when(i == 0)
        def _():
            attend([(i, 1)], False)

        def far_pair(jj, carry):
            attend([(2 * jj, None), (2 * jj + 1, None)], lagged)
            return carry

        lax.fori_loop(0, n_far // 2, far_pair, 0)

        @pl.when(n_far % 2 == 1)
        def _():
            attend([(n_far - 1, None)], lagged)

    x_ref[...] = jnp.zeros(x_ref.shape, F32)
    attend_all(True)

    @pl.when(jnp.max(x_ref[...]) > LAG_LIMIT)
    def _():
        attend_all(False)


    for h in range(DSA_HEADS):
        l_tot = _sublane_all(l_ref[h], jnp.add)
        o = (acc_ref[h].reshape(DSA_RKV // 8, 8, TQ) / l_tot[None]).reshape(DSA_RKV, TQ)
        o_ref[:, h * LANES:(h + 1) * LANES] = o.T.astype(BF16)


def _t5_bucket(rel):
    nb = REL_BUCKETS // 2
    max_exact = nb // 2
    base = jnp.where(rel > 0, nb, 0)
    n = jnp.abs(rel)
    nf = jnp.maximum(n, 1).astype(jnp.float32)
    large = max_exact + (jnp.log(nf / max_exact) / math.log(REL_MAX_DIST / max_exact)
                         * (nb - max_exact)).astype(jnp.int32)
    large = jnp.minimum(large, nb - 1)
    return base + jnp.where(n < max_exact, n, large)


def _bias_tables(rel_bias):
    s = jnp.arange(TK, dtype=jnp.int32)[:, None]
    t = jnp.arange(TQ, dtype=jnp.int32)[None, :]
    rel = jnp.stack([s - TK - t, s - t])
    table = rel_bias.astype(F32)
    bucket = _t5_bucket(rel)
    far_bucket = _t5_bucket(jnp.full((), -REL_MAX_DIST, jnp.int32))
    near = jnp.zeros((DSA_HEADS,) + rel.shape, F32)
    far = jnp.zeros((DSA_HEADS,), F32)
    for b in range(REL_BUCKETS):
        near = near + jnp.where(bucket[None] == b, table[b][:, None, None, None], 0.0)
        far = far + jnp.where(far_bucket == b, table[b], 0.0)
    return (near - far[:, None, None, None]) * LOG2E


def _search_tables(nq, k_sel):
    nd = statistics.NormalDist()
    tab = np.zeros((nq, 2, TQ), np.float32)
    for i in range(nq):
        for c in range(TQ // CHUNK):
            n = i * TQ + (c + 1) * CHUNK
            z = nd.inv_cdf(1.0 - (k_sel - 0.5) / n) if n > k_sel else 0.0
            tab[i, 0, c * CHUNK:(c + 1) * CHUNK] = z
            tab[i, 1, c * CHUNK:(c + 1) * CHUNK] = n * nd.pdf(z)
    return jnp.asarray(tab)


def _dsa_attention(qlat, qidx, wt, ckvn, ckt, kidx, bnear, B, S):
    nq = S // TQ
    N = B * S
    k_sel = min(TOPK_MAX, S // 4)
    zq = _search_tables(nq, k_sel)
    tri = jnp.asarray(np.tril(np.ones((TK, TK), np.float32)), dtype=BF16)
    stacked = pl.BlockSpec((1, DSA_HEADS, TQ, LANES), lambda b, i: (b * nq + i, 0, 0, 0))
    keys = pl.BlockSpec((nq, TK, LANES), lambda b, i: (b, 0, 0))
    keys_t = pl.BlockSpec((nq, LANES, TK), lambda b, i: (b, 0, 0))
    return pl.pallas_call(
        functools.partial(_dsa_kernel, k_sel=k_sel),
        grid=(B, nq),
        in_specs=[stacked, stacked, pl.BlockSpec((IDX_HEADS, TQ), lambda b, i: (0, b * nq + i)),
                  keys, keys_t, keys,
                  pl.BlockSpec(bnear.shape, lambda b, i: (0, 0, 0, 0)),
                  pl.BlockSpec((1, 2, TQ), lambda b, i: (i, 0, 0)),
                  pl.BlockSpec((TK, TK), lambda b, i: (0, 0))],
        out_specs=pl.BlockSpec((TQ, DSA_HEADS * DSA_RKV), lambda b, i: (b * nq + i, 0)),
        out_shape=jax.ShapeDtypeStruct((N, DSA_HEADS * DSA_RKV), BF16),
        scratch_shapes=[pltpu.VMEM((nq, TK, TQ), F32),
                        pltpu.VMEM((DSA_HEADS, 8, TQ), F32),
                        pltpu.VMEM((DSA_HEADS, 8, TQ), F32),
                        pltpu.VMEM((DSA_HEADS, 8, TQ), F32),
                        pltpu.VMEM((DSA_HEADS, DSA_RKV, TQ), F32)],
        compiler_params=_params(("parallel", "arbitrary")),
        name="dsa_attention",
    )(qlat, qidx, wt, ckvn, ckt, kidx, bnear, zq, tri)


def _merge_kernel(x_ref, ret_ref, olat_ref, mq_ref, gates_ref, km_ref, vm_ref,
                  wret_ref, wdsa_ref, wmem_ref, wout_ref, o_ref):
    mq = mq_ref[...]
    km = km_ref[0]
    vm = vm_ref[0]
    pvs = []
    for h in range(MEM_HEADS):
        sl = slice(h * MEM_DH, (h + 1) * MEM_DH)
        lg = _dot_nt(mq[:, sl], km[:, sl]) * (MEM_DH ** -0.5)
        p = jnp.exp(lg - jnp.max(lg, axis=1, keepdims=True))
        p = p / jnp.sum(p, axis=1, keepdims=True)
        pvs.append(_dot(p.astype(BF16), vm[:, sl]).astype(BF16))
    mem_b = _dot(jnp.concatenate(pvs, axis=1), wmem_ref[...])
    ret_b = _dot(ret_ref[...], wret_ref[...])
    dsa_b = _dot(olat_ref[...], wdsa_ref[...])
    g = jax.nn.sigmoid(gates_ref[...].astype(F32))
    merged = (g[:, :D_MODEL] * ret_b + g[:, D_MODEL:2 * D_MODEL] * dsa_b + g[:, 2 * D_MODEL:] * mem_b)
    o_ref[...] = x_ref[...] + _dot(merged.astype(BF16), wout_ref[...])


def _merge(x2d, ret, olat, mq, gates, km, vm, wret, wdsa, wmem, wout, S):
    N = x2d.shape[0]
    per_b = S // TM_MERGE
    row = lambda i: (i, 0)
    const = lambda i: (0, 0)
    memb = lambda i: (i // per_b, 0, 0)
    return pl.pallas_call(
        _merge_kernel,
        grid=(N // TM_MERGE,),
        in_specs=[pl.BlockSpec((TM_MERGE, D_MODEL), row), pl.BlockSpec((TM_MERGE, RET_W), row),
                  pl.BlockSpec((TM_MERGE, DSA_HEADS * DSA_RKV), row), pl.BlockSpec((TM_MERGE, MEM_W), row),
                  pl.BlockSpec((TM_MERGE, 3 * D_MODEL), row),
                  pl.BlockSpec((1, MEM_LEN, MEM_W), memb), pl.BlockSpec((1, MEM_LEN, MEM_W), memb),
                  pl.BlockSpec(wret.shape, const), pl.BlockSpec(wdsa.shape, const),
                  pl.BlockSpec(wmem.shape, const), pl.BlockSpec(wout.shape, const)],
        out_specs=pl.BlockSpec((TM_MERGE, D_MODEL), row),
        out_shape=jax.ShapeDtypeStruct(x2d.shape, F32),
        compiler_params=_params(("parallel",)),
        name="merge_out",
    )(x2d, ret, olat, mq, gates, km, vm, wret, wdsa, wmem, wout)


def _mlp_kernel(x_ref, g_ref, w1_ref, w2_ref, gf_ref, o_ref, *, final):
    x = x_ref[...]
    h = _rms(x, g_ref[...]).astype(BF16)
    acc = jnp.zeros(x.shape, F32)
    for c in range(D_FF // D_MODEL):
        sl = slice(c * D_MODEL, (c + 1) * D_MODEL)
        a = jnp.maximum(_dot(h, w1_ref[:, sl]), 0.0)
        acc = acc + _dot((a * a).astype(BF16), w2_ref[sl, :])
    y = x + acc
    if final:
        y = _rms(y, gf_ref[...])
    o_ref[...] = y


def _mlp(x2d, g, w1, w2, gf, final):
    N = x2d.shape[0]
    row = lambda i: (i, 0)
    const = lambda i: (0, 0)
    return pl.pallas_call(
        functools.partial(_mlp_kernel, final=final),
        grid=(N // TM_MLP,),
        in_specs=[pl.BlockSpec((TM_MLP, D_MODEL), row), pl.BlockSpec((1, D_MODEL), const),
                  pl.BlockSpec(w1.shape, const), pl.BlockSpec(w2.shape, const),
                  pl.BlockSpec((1, D_MODEL), const)],
        out_specs=pl.BlockSpec((TM_MLP, D_MODEL), row),
        out_shape=jax.ShapeDtypeStruct(x2d.shape, F32),
        compiler_params=_params(("parallel",)),
        name="mlp",
    )(x2d, g, w1, w2, gf)


def _rope_tables(S):
    half = RET_DK // 2
    pos = jnp.arange(S, dtype=F32)
    freqs = ROPE_BASE ** (-jnp.arange(half, dtype=F32) / half)
    ang = pos[:, None] * freqs[None, :]
    return jnp.tile(jnp.cos(ang), (1, RET_HEADS)), jnp.tile(jnp.sin(ang), (1, RET_HEADS))


def _in_weights(wi, w_iq_l):
    half = RET_DK // 2
    def gather_halves(w):
        return jnp.transpose(w.reshape(D_MODEL, RET_HEADS, 2, half), (0, 2, 1, 3)).reshape(D_MODEL, RET_W)

    offs = np.concatenate([[0], np.cumsum(IN_SIZES)])
    seg = lambda k: wi[:, int(offs[k]):int(offs[k + 1])]
    wa = jnp.concatenate([gather_halves(seg(0)), gather_halves(seg(1)), seg(2), seg(3), seg(8), seg(9)],
                         axis=1).astype(BF16)
    pad = jnp.zeros((D_MODEL, LANES - IDX_HEADS), F32)
    widx = jnp.concatenate([seg(4), seg(5), jnp.tile(seg(6), (1, 4)), seg(7), pad], axis=1)
    wiq = jnp.tile(w_iq_l[:, :, None, :], (1, 1, 4, 1)).reshape(DSA_RQ, IDX_HEADS * LANES)
    return wa, widx.astype(BF16), wiq.astype(BF16)


def kernel(x, mem, norm1, w_in, q_norm, kv_norm, w_uq, w_iq, w_uk, w_uv, mem_norm, w_mem_kv,
           w_ret_o, w_dsa_o, w_mem_o, w_out, norm2, w_ff1, w_ff2, rel_bias, final_norm):
    B, S, D = x.shape
    depth = w_in.shape[0]
    assert D == D_MODEL and S % TM_MLP == 0 and S % TQ == 0 and TQ == TK
    cos_t, sin_t = _rope_tables(S)
    bnear = _bias_tables(rel_bias)
    wql_all, wdsa_all = _fold_weights(w_uq, w_uk, w_uv, w_dsa_o)
    x2d = x.reshape(B * S, D)
    row = lambda v: v.reshape(1, -1)
    for l in range(depth):
        wa, widx, wiq = _in_weights(w_in[l], w_iq[l])
        rq, rk, rv, rg, mq, gates, idx = _in_proj(x2d, row(norm1[l]), cos_t, sin_t, wa, widx, S)
        ret = _retention(rq, rk, rv, rg, B, S)
        qlat, qidx, wt, ckvn, ckt, kidx = _dsa_prep(idx, row(q_norm[l]), row(kv_norm[l]), wql_all[l], wiq)
        olat = _dsa_attention(qlat, qidx, wt, ckvn, ckt, kidx, bnear, B, S)
        km, vm = _mem_kv(mem, row(mem_norm[l]), w_mem_kv[l].astype(BF16))
        x2d = _merge(x2d, ret, olat, mq, gates, km, vm, w_ret_o[l].astype(BF16), wdsa_all[l],
                     w_mem_o[l].astype(BF16), w_out[l].astype(BF16), S)
        x2d = _mlp(x2d, row(norm2[l]), w_ff1[l].astype(BF16), w_ff2[l].astype(BF16), row(final_norm),
                   final=(l == depth - 1))
    return x2d.reshape(B, S, D)
```

```python
import functools
import math
import statistics

import numpy as np
import jax
import jax.numpy as jnp
from jax import lax
from jax.experimental import pallas as pl
from jax.experimental.pallas import tpu as pltpu

F32 = jnp.float32
BF16 = jnp.bfloat16

D_MODEL = 1024
CHUNK = 64
EPS = 1e-6
GN_EPS = 1e-5
RET_HEADS = 8
RET_DK = 64
RET_W = 512
ROPE_BASE = 10000.0
DSA_HEADS = 8
DSA_DH = 64
DSA_RQ = 256
DSA_RKV = 128
IDX_HEADS = 8
IDX_DIM = 32
TOPK_MAX = 256
MEM_LEN = 256
MEM_HEADS = 4
MEM_DH = 128
MEM_W = 512
REL_BUCKETS = 32
REL_MAX_DIST = 128
D_FF = 4096
IN_SIZES = (512, 512, 512, 512, DSA_RQ, DSA_RKV, IDX_DIM, IDX_HEADS, MEM_W, 3 * D_MODEL)

LANES = 128
IDX_COLS = 640
NEG = -1e30
LOG2E = math.log2(math.e)
F32_MAX = float(np.finfo(np.float32).max)
VMEM_LIMIT = 56 * 1024 * 1024

TM_IN = 256
T_RET = 256
TQ = 256
TK = 256
LAG_LIMIT = 100.0
LAG_DOWN = 5.0
SEARCH_UNCHECKED = 8
SEARCH_SWITCH = 16
SEARCH_FEW = 4.0
SEARCH_CAP = SEARCH_SWITCH + 34
TM_MERGE = 512
TM_MLP = 512


def _dot(a, b):
    return jnp.dot(a, b, preferred_element_type=F32)


def _dot_nt(a, b):
    return lax.dot_general(a, b, (((1,), (1,)), ((), ())), preferred_element_type=F32)


def _split(a):
    hi = a.astype(BF16)
    lo = (a - hi.astype(F32)).astype(BF16)
    return hi, lo


def _rms(x, g):
    return x * lax.rsqrt(jnp.mean(x * x, axis=-1, keepdims=True) + EPS) * g


def _params(sem):
    return pltpu.CompilerParams(dimension_semantics=sem, vmem_limit_bytes=VMEM_LIMIT)


def _fold_kernel(uq_ref, uk_ref, uv_ref, wo_ref, wql_ref, wdsa_ref):
    uq_h, uq_l = _split(uq_ref[0, 0])
    uk_h, uk_l = _split(uk_ref[0, 0])
    ql = _dot_nt(uq_h, uk_h) + _dot_nt(uq_h, uk_l) + _dot_nt(uq_l, uk_h)
    wql_ref[0] = (ql * (DSA_DH ** -0.5 * LOG2E)).astype(BF16)
    uv_h, uv_l = _split(uv_ref[0, 0])
    wo_h, wo_l = _split(wo_ref[0, 0])
    wdsa_ref[0] = (_dot(uv_h, wo_h) + _dot(uv_h, wo_l) + _dot(uv_l, wo_h)).astype(BF16)


def _fold_weights(w_uq, w_uk, w_uv, w_dsa_o):
    L = w_uq.shape[0]
    uq = jnp.transpose(w_uq, (0, 2, 1, 3))
    uk = jnp.transpose(w_uk, (0, 2, 1, 3))
    uv = jnp.transpose(w_uv, (0, 2, 1, 3))
    wo = w_dsa_o.reshape(L, DSA_HEADS, DSA_DH, D_MODEL)
    return pl.pallas_call(
        _fold_kernel,
        grid=(L, DSA_HEADS),
        in_specs=[
            pl.BlockSpec((1, 1, DSA_RQ, DSA_DH), lambda l, h: (l, h, 0, 0)),
            pl.BlockSpec((1, 1, DSA_RKV, DSA_DH), lambda l, h: (l, h, 0, 0)),
            pl.BlockSpec((1, 1, DSA_RKV, DSA_DH), lambda l, h: (l, h, 0, 0)),
            pl.BlockSpec((1, 1, DSA_DH, D_MODEL), lambda l, h: (l, h, 0, 0)),
        ],
        out_specs=[
            pl.BlockSpec((1, DSA_RQ, DSA_RKV), lambda l, h: (l, 0, h)),
            pl.BlockSpec((1, DSA_RKV, D_MODEL), lambda l, h: (l, h, 0)),
        ],
        out_shape=[
            jax.ShapeDtypeStruct((L, DSA_RQ, DSA_HEADS * DSA_RKV), BF16),
            jax.ShapeDtypeStruct((L, DSA_HEADS * DSA_RKV, D_MODEL), BF16),
        ],
        compiler_params=_params(("parallel", "parallel")),
        name="fold_weights",
    )(uq, uk, uv, wo)


def _in_kernel(x_ref, g_ref, cos_ref, sin_ref, wa_ref, widx_ref,
               rq_ref, rk_ref, rv_ref, rg_ref, mq_ref, gates_ref, idx_ref):
    h = _rms(x_ref[...], g_ref[...])
    hb = h.astype(BF16)
    c = cos_ref[...]
    s = sin_ref[...]
    half = RET_W // 2

    def rope_store(ref, z, scale):
        x1 = z[:, :half]
        x2 = z[:, half:]
        ref[:, :half] = ((x1 * c - x2 * s) * scale).astype(BF16)
        ref[:, half:] = ((x2 * c + x1 * s) * scale).astype(BF16)

    rope_store(rq_ref, _dot(hb, wa_ref[:, 0:512]), 1.0)
    rope_store(rk_ref, _dot(hb, wa_ref[:, 512:1024]), RET_DK ** -0.5)
    rv_ref[...] = _dot(hb, wa_ref[:, 1024:1536]).astype(BF16)
    rg_ref[...] = _dot(hb, wa_ref[:, 1536:2048]).astype(BF16)
    mq_ref[...] = _dot(hb, wa_ref[:, 2048:2560]).astype(BF16)
    for j in range(6):
        gates_ref[:, j * 512:(j + 1) * 512] = _dot(hb, wa_ref[:, 2560 + j * 512:3072 + j * 512]).astype(BF16)
    idx_ref[...] = _dot(hb, widx_ref[...])


def _in_proj(x2d, g, cos_t, sin_t, wa, widx, S):
    N = x2d.shape[0]
    n_pos = S // TM_IN
    row = lambda i: (i, 0)
    const = lambda i: (0, 0)
    pos = lambda i: (i % n_pos, 0)
    bf = lambda w: jax.ShapeDtypeStruct((N, w), BF16)
    return pl.pallas_call(
        _in_kernel,
        grid=(N // TM_IN,),
        in_specs=[
            pl.BlockSpec((TM_IN, D_MODEL), row),
            pl.BlockSpec((1, D_MODEL), const),
            pl.BlockSpec((TM_IN, RET_W // 2), pos),
            pl.BlockSpec((TM_IN, RET_W // 2), pos),
            pl.BlockSpec(wa.shape, const),
            pl.BlockSpec(widx.shape, const),
        ],
        out_specs=[
            pl.BlockSpec((TM_IN, 512), row), pl.BlockSpec((TM_IN, 512), row),
            pl.BlockSpec((TM_IN, 512), row), pl.BlockSpec((TM_IN, 512), row),
            pl.BlockSpec((TM_IN, 512), row), pl.BlockSpec((TM_IN, 3 * D_MODEL), row),
            pl.BlockSpec((TM_IN, IDX_COLS), row),
        ],
        out_shape=[bf(512), bf(512), bf(512), bf(512), bf(512), bf(3 * D_MODEL),
                   jax.ShapeDtypeStruct((N, IDX_COLS), F32)],
        compiler_params=_params(("parallel",)),
        name="in_proj",
    )(x2d, g, cos_t, sin_t, wa, widx)


def _ret_kernel(rq_ref, rk_ref, rv_ref, rg_ref, dmat_ref, qd_ref, kd_ref, hq_ref, hv_ref,
                cdec_ref, bd_ref, p_ref, o_ref, s_ref):
    @pl.when(pl.program_id(1) == 0)
    def _():
        s_ref[...] = jnp.zeros_like(s_ref)

    q = rq_ref[...]
    k = rk_ref[...]
    v = rv_ref[...]
    def head_lane_tiles(a, h):
        c = h // (LANES // (RET_DK // 2))
        return jnp.concatenate([a[:, c * LANES:(c + 1) * LANES], a[:, (2 + c) * LANES:(3 + c) * LANES]], axis=1)

    o_cols = []
    for pair in range(RET_HEADS // 2):
        cols = slice(pair * LANES, (pair + 1) * LANES)
        o_pair = jnp.zeros((q.shape[0], LANES), F32)
        for h in (2 * pair, 2 * pair + 1):
            sc = _dot_nt(head_lane_tiles(q * hq_ref[h], h), head_lane_tiles(k, h)) * dmat_ref[h]
            o_pair = o_pair + _dot(sc.astype(BF16), v[:, cols]) * hv_ref[h][:, cols]
        o_cols.append(o_pair)
    o = jnp.concatenate(o_cols, axis=1)
    state = s_ref[...]
    qf = q.astype(F32) * qd_ref[...]
    o = o + _dot(qf.astype(BF16), state.astype(BF16))
    kf = k.astype(F32) * kd_ref[...]
    kv = _dot(kf.T.astype(BF16), v)
    s_ref[...] = state * cdec_ref[...] + kv * bd_ref[...]

    p = p_ref[...]

    oh, ol = _split(o)
    d = o - (_dot(oh, p) + _dot(ol, p))
    y = d * lax.rsqrt(_dot((d * d).astype(BF16), p) + GN_EPS)
    g = rg_ref[...].astype(F32)
    o_ref[...] = (g * jax.nn.sigmoid(g) * y).astype(BF16)


def _ret_consts():
    T = T_RET
    hh = np.arange(RET_HEADS, dtype=np.float64)
    log_g = np.log1p(-np.exp2(-5.0 - hh))
    t = np.arange(T)
    ct = t // CHUNK
    diff = (t[:, None] - t[None, :]).astype(np.float64)
    same = ct[:, None] == ct[None, :]
    past = ct[None, :] < ct[:, None]
    expo = np.where(same, np.abs(diff), diff)
    dmat = np.where((same | past)[None], np.exp(log_g[:, None, None] * expo[None]), 0.0)
    lane = np.arange(RET_W)
    hk = (lane % (RET_W // 2)) // (RET_DK // 2)
    hv = lane // RET_DK
    qd = np.exp(log_g[hk][None, :] * (t[:, None] + 1.0))
    kd = np.exp(log_g[hk][None, :] * (T - 1.0 - t[:, None]))
    hq_mask = (hk[None, :] == np.arange(RET_HEADS)[:, None]).astype(np.float32)[:, None, :]
    hv_mask = (hv[None, :] == np.arange(RET_HEADS)[:, None]).astype(np.float32)[:, None, :]
    bd = (hk[:, None] == hv[None, :]).astype(np.float32)
    cdec = np.broadcast_to(np.exp(log_g[hk] * T)[:, None], (RET_W, RET_W))
    pmat = (hv[:, None] == hv[None, :]).astype(np.float32) / RET_DK
    f = lambda a: jnp.asarray(np.asarray(a, dtype=np.float32))
    return (f(dmat), f(qd), f(kd), jnp.asarray(hq_mask, dtype=BF16), f(hv_mask), f(cdec), f(bd),
            jnp.asarray(pmat, dtype=BF16))


def _retention(rq, rk, rv, rg, B, S):
    consts = _ret_consts()
    nb = S // T_RET
    tok = lambda b, j: (b * nb + j, 0)
    full = lambda a: pl.BlockSpec(a.shape, lambda b, j: (0,) * a.ndim)
    return pl.pallas_call(
        _ret_kernel,
        grid=(B, nb),
        in_specs=[pl.BlockSpec((T_RET, RET_W), tok)] * 4 + [full(a) for a in consts],
        out_specs=pl.BlockSpec((T_RET, RET_W), tok),
        out_shape=jax.ShapeDtypeStruct(rq.shape, BF16),
        scratch_shapes=[pltpu.VMEM((RET_W, RET_W), F32)],
        compiler_params=_params(("parallel", "arbitrary")),
        name="retention",
    )(rq, rk, rv, rg, *consts)


def _memkv_kernel(mem_ref, g_ref, w_ref, k_ref, v_ref):
    mn = _rms(mem_ref[0], g_ref[...]).astype(BF16)
    kv = _dot(mn, w_ref[...])
    k_ref[0] = kv[:, :MEM_W].astype(BF16)
    v_ref[0] = kv[:, MEM_W:].astype(BF16)


def _mem_kv(mem, g, w):
    B = mem.shape[0]
    return pl.pallas_call(
        _memkv_kernel,
        grid=(B,),
        in_specs=[pl.BlockSpec((1, MEM_LEN, D_MODEL), lambda b: (b, 0, 0)),
                  pl.BlockSpec((1, D_MODEL), lambda b: (0, 0)),
                  pl.BlockSpec(w.shape, lambda b: (0, 0))],
        out_specs=[pl.BlockSpec((1, MEM_LEN, MEM_W), lambda b: (b, 0, 0))] * 2,
        out_shape=[jax.ShapeDtypeStruct((B, MEM_LEN, MEM_W), BF16)] * 2,
        compiler_params=_params(("parallel",)),
        name="mem_kv",
    )(mem, g, w)


def _prep_kernel(idx_ref, qn_ref, kvn_ref, wql_ref, wiq_ref,
                 qlat_ref, qidx_ref, wt_ref, ckvn_ref, ckt_ref, kidx_ref):
    z = idx_ref[...]
    cqn = _rms(z[:, :DSA_RQ], qn_ref[...])
    cb = cqn.astype(BF16)
    ql = _dot(cb, wql_ref[...])
    y = _dot(cb, wiq_ref[...])
    yh = y.astype(BF16).astype(F32)
    lane = lax.broadcasted_iota(jnp.int32, y.shape, 1)
    qsel = jnp.where((lane % LANES) < 2 * IDX_DIM, yh, y - yh).astype(BF16)
    for h in range(DSA_HEADS):
        qlat_ref[0, h] = ql[:, h * LANES:(h + 1) * LANES].astype(BF16)
        qidx_ref[0, h] = qsel[:, h * LANES:(h + 1) * LANES]
    ckn = _rms(z[:, DSA_RQ:DSA_RQ + DSA_RKV], kvn_ref[...])
    ckvn_ref[0] = ckn.astype(BF16)
    ckt_ref[0] = ckn.T.astype(BF16)
    ik = z[:, 384:512]
    ikh = ik.astype(BF16).astype(F32)
    lane1 = lax.broadcasted_iota(jnp.int32, ik.shape, 1)
    kidx_ref[0] = jnp.where((lane1 // IDX_DIM) % 2 == 0, ikh, ik - ikh).astype(BF16)
    wfull = z[:, 512:640] * ((IDX_HEADS ** -0.5) * (IDX_DIM ** -0.5))
    wt_ref[...] = wfull.T[:IDX_HEADS, :]


def _dsa_prep(idx, qn, kvn, wql, wiq):
    N = idx.shape[0]
    nt = N // TQ
    const = lambda i: (0, 0)
    tile3 = lambda i: (i, 0, 0)
    stacked = pl.BlockSpec((1, DSA_HEADS, TQ, LANES), lambda i: (i, 0, 0, 0))
    return pl.pallas_call(
        _prep_kernel,
        grid=(nt,),
        in_specs=[pl.BlockSpec((TQ, IDX_COLS), lambda i: (i, 0)),
                  pl.BlockSpec((1, DSA_RQ), const), pl.BlockSpec((1, DSA_RKV), const),
                  pl.BlockSpec(wql.shape, const), pl.BlockSpec(wiq.shape, const)],
        out_specs=[stacked, stacked, pl.BlockSpec((IDX_HEADS, TQ), lambda i: (0, i)),
                   pl.BlockSpec((1, TQ, LANES), tile3), pl.BlockSpec((1, LANES, TQ), tile3),
                   pl.BlockSpec((1, TQ, LANES), tile3)],
        out_shape=[jax.ShapeDtypeStruct((nt, DSA_HEADS, TQ, LANES), BF16),
                   jax.ShapeDtypeStruct((nt, DSA_HEADS, TQ, LANES), BF16),
                   jax.ShapeDtypeStruct((IDX_HEADS, N), F32),
                   jax.ShapeDtypeStruct((nt, TQ, LANES), BF16),
                   jax.ShapeDtypeStruct((nt, LANES, TQ), BF16),
                   jax.ShapeDtypeStruct((nt, TQ, LANES), BF16)],
        compiler_params=_params(("parallel",)),
        name="dsa_prep",
    )(idx, qn, kvn, wql, wiq)


def _key_to_f32(key):
    bits = jnp.where(key >= 0, key, key ^ jnp.int32(0x7FFFFFFF))
    return lax.bitcast_convert_type(bits, F32)


def _f32_to_key(x):
    bits = lax.bitcast_convert_type(x, jnp.int32)
    return jnp.where(bits >= 0, bits, bits ^ jnp.int32(0x7FFFFFFF))


def _sublane_all(a, op):
    for shift in (4, 2, 1):
        a = op(a, pltpu.roll(a, shift, 0))
    return a


def _dsa_kernel(qlat_ref, qidx_ref, wt_ref, ckvn_ref, ckt_ref, kidx_ref, bnear_ref, zq_ref, tri_ref, o_ref,
                sc_ref, m_ref, l_ref, x_ref, acc_ref, *, k_sel):
    i = pl.program_id(1)
    wt = wt_ref[...]

    def score_tiles(js):
        keys = jnp.concatenate([kidx_ref[j] for j in js], axis=0)
        y_all = _dot_nt(keys, qidx_ref[0].reshape(IDX_HEADS * TQ, LANES))
        tot = jnp.zeros((len(js) * TK, TQ), F32)
        for h in range(IDX_HEADS):
            tot = tot + jnp.maximum(y_all[:, h * TQ:(h + 1) * TQ], 0.0) * wt[h:h + 1, :]
        return tot

    def colsum8(a):
        return jnp.sum(a.reshape(a.shape[0] // 8, 8, TQ), axis=0)

    def ones_where(cond_):
        return jnp.where(cond_, 1.0, 0.0)

    def stats(s_counted, s_summed):
        return (colsum8(s_summed), colsum8(s_summed * s_summed),
                colsum8(ones_where(s_counted >= 0.0)), colsum8(ones_where(s_counted > 0.0)))

    def add_stats(carry, s):
        return tuple(a + b for a, b in zip(carry, stats(s, s)))

    def fill_pair(jj, carry):
        s = score_tiles([2 * jj, 2 * jj + 1])
        sc_ref[2 * jj] = s[:TK]
        sc_ref[2 * jj + 1] = s[TK:]
        return add_stats(carry, s)

    def fill_one(carry):
        s = score_tiles([i - 1])
        sc_ref[i - 1] = s
        return add_stats(carry, s)

    zero8 = jnp.zeros((8, TQ), F32)
    carry = lax.fori_loop(0, i // 2, fill_pair, (zero8, zero8, zero8, zero8))
    carry = lax.cond(i % 2 == 1, fill_one, lambda c: c, carry)
    key_r = lax.broadcasted_iota(jnp.int32, (TK, TQ), 0)
    qry_c = lax.broadcasted_iota(jnp.int32, (TK, TQ), 1)
    admissible = key_r <= (qry_c // CHUNK) * CHUNK + (CHUNK - 1)
    s_diag = score_tiles([i])
    s_masked = jnp.where(admissible, s_diag, -jnp.inf)
    sc_ref[i] = s_masked
    carry = tuple(a + b for a, b in zip(carry, stats(s_masked, jnp.where(admissible, s_diag, 0.0))))
    s1, s2, c_ge0, c_gt0 = (_sublane_all(a, jnp.add)[0:1, :] for a in carry)

    def count_ge(thr):
        def pair(jj, c):
            ca, cb = c
            return (ca + colsum8(ones_where(sc_ref[2 * jj] >= thr)),
                    cb + colsum8(ones_where(sc_ref[2 * jj + 1] >= thr)))
        ca, cb = lax.fori_loop(0, (i + 1) // 2, pair, (zero8, zero8))
        odd = lax.cond((i + 1) % 2 == 1, lambda: colsum8(ones_where(sc_ref[i] >= thr)), lambda: zero8)
        return _sublane_all(ca + cb + odd, jnp.add)[0:1, :]

    qc = lax.broadcasted_iota(jnp.int32, (1, TQ), 1)
    n_adm = (i * TQ + (qc // CHUNK + 1) * CHUNK).astype(F32)
    kf = jnp.float32(k_sel)
    key_min = jnp.int32(np.int32(np.array(-F32_MAX, np.float32).view(np.int32)) ^ np.int32(0x7FFFFFFF))
    key_max = jnp.int32(np.array(np.inf, np.float32).view(np.int32))
    inf = jnp.float32(np.inf)
    zq = zq_ref[0]
    mu = s1 / n_adm
    sd = jnp.sqrt(jnp.maximum(s2 / n_adm - mu * mu, 1e-30))
    dens = zq[1:2, :] / sd
    pos = c_gt0 >= kf
    tie0 = jnp.logical_and(jnp.logical_not(pos), c_ge0 >= kf)
    small = n_adm <= kf
    lo0 = jnp.where(pos | tie0, 0, key_min)
    hi0 = jnp.where(pos, key_max, jnp.where(tie0, 1, 0))
    lo0 = jnp.where(small, key_min, lo0)
    hi0 = jnp.where(small, key_min + 1, hi0)
    lf0 = jnp.where(pos | tie0, 0.0, -inf)
    hf0 = jnp.where(pos, inf, 0.0)
    cl0 = jnp.where(pos | tie0, c_ge0, n_adm)
    ch0 = jnp.where(pos, 0.0, jnp.where(tie0, c_gt0, c_ge0))
    t0 = mu + zq[0:1, :] * sd
    one = jnp.ones((1, TQ), F32)

    def mid_key(lo, hi):
        return (lo & hi) + ((lo ^ hi) >> 1)

    def unfinished(lo, hi, cl):
        return jnp.logical_not((cl == kf) | (mid_key(lo, hi) == lo))

    def cond(st):
        it, lo, hi, cl = st[0], st[1], st[2], st[5]
        return jnp.logical_and(it < SEARCH_CAP, jnp.sum(ones_where(unfinished(lo, hi, cl))) > 0.0)

    def step(st):
        it, lo, hi, lf, hf, cl, ch, t, last, wl, wh = st
        act = unfinished(lo, hi, cl)
        guided = (it < SEARCH_SWITCH).astype(jnp.int32)
        mid = mid_key(lo, hi)
        tk = mid + (_f32_to_key(t) - mid) * guided
        tk = jnp.minimum(jnp.maximum(tk, lo + 1), hi - 1)
        tf = _key_to_f32(tk)
        c = count_ge(tf)
        ge = c >= kf
        up_lo = act & ge
        up_hi = act & jnp.logical_not(ge)
        lo = jnp.where(up_lo, tk, lo)
        lf = jnp.where(up_lo, tf, lf)
        cl = jnp.where(up_lo, c, cl)
        hi = jnp.where(up_hi, tk, hi)
        hf = jnp.where(up_hi, tf, hf)
        ch = jnp.where(up_hi, c, ch)
        side = jnp.where(ge, 1.0, -1.0)
        same = side == last
        wh = jnp.where(ge, jnp.where(same, wh * 0.5, one), one)
        wl = jnp.where(ge, one, jnp.where(same, wl * 0.5, one))
        a = (cl - kf + 0.5) * wl
        b = (kf - 0.5 - ch) * wh
        frac = jnp.where(cl - ch <= SEARCH_FEW, 0.5, a / (a + b))
        t_bracket = lf + (hf - lf) * frac
        t_model = tf + 1.5 * (c - kf + jnp.where(ge, 0.5, -0.5)) / dens
        bracketed = (lf > -inf) & (hf < inf)
        t = jnp.where(bracketed, t_bracket, t_model)
        return it + 1, lo, hi, lf, hf, cl, ch, t, side, wl, wh

    st = (jnp.int32(0), lo0, hi0, lf0, hf0, cl0, ch0, t0, jnp.zeros((1, TQ), F32), one, one)
    st = lax.fori_loop(0, SEARCH_UNCHECKED, lambda _, s: step(s), st)
    st = lax.while_loop(cond, lambda s: step(step(s)), st)
    lo, cl, ch = st[1], st[5], st[6]
    thr = _key_to_f32(lo)
    tied = (cl != kf) & jnp.logical_not(small)
    need = jnp.where(tied, kf - ch, jnp.float32(1e9))

    def mask_plain(j, carry_):
        sc_ref[j] = jnp.where(sc_ref[j] >= thr, 0.0, NEG)
        return carry_

    def mask_ranked(j, before):
        s = sc_ref[j]
        eq = ones_where(s == thr)
        rank = _dot(tri_ref[...], eq.astype(BF16)) + before
        keep = (s > thr) | ((s == thr) & (rank <= need))
        sc_ref[j] = jnp.where(keep, 0.0, NEG)
        return before + jnp.sum(eq, axis=0, keepdims=True)

    any_tied = jnp.sum(ones_where(tied)) > 0.0

    def mask_ranked_pair(jj, before):
        return mask_ranked(2 * jj + 1, mask_ranked(2 * jj, before))

    @pl.when(any_tied)
    def _():
        before = lax.fori_loop(0, (i + 1) // 2, mask_ranked_pair, jnp.zeros((1, TQ), F32))

        @pl.when((i + 1) % 2 == 1)
        def _():
            mask_ranked(i, before)

    @pl.when(jnp.logical_not(any_tied))
    def _():
        lax.fori_loop(0, i + 1, mask_plain, 0)

    def attend(tiles, lagged):
        nk = len(tiles) * TK
        ck = jnp.concatenate([ckvn_ref[j] for j, _ in tiles], axis=0)
        ckt = jnp.concatenate([ckt_ref[j] for j, _ in tiles], axis=1)
        mask_bias = jnp.concatenate([sc_ref[j] for j, _ in tiles], axis=0)
        lg_all = _dot_nt(ck, qlat_ref[0].reshape(DSA_HEADS * TQ, LANES))
        ps, alphas = [], []
        for h in range(DSA_HEADS):
            bias = mask_bias
            if any(slot is not None for _, slot in tiles):
                bias = bias + jnp.concatenate(
                    [jnp.zeros((TK, TQ), F32) if slot is None else bnear_ref[h, slot] for _, slot in tiles], axis=0)
            lg3 = (lg_all[:, h * TQ:(h + 1) * TQ] + bias).reshape(nk // 8, 8, TQ)
            m_old = m_ref[h]
            m_new = jnp.maximum(m_old, _sublane_all(jnp.max(lg3, axis=0), jnp.maximum))
            alpha = jnp.exp2(m_old - m_new)
            m_ref[h] = m_new
            if lagged:
                ref = jnp.where(m_old > 0.5 * NEG, m_old, 0.0)
                excess = jnp.maximum(m_new - ref, LAG_DOWN * (ref - m_new))
                x_ref[h] = jnp.maximum(x_ref[h], jnp.where(m_new > 0.5 * NEG, excess, 0.0))
                beta = jnp.exp2(jnp.minimum(ref - m_new, 126.0))
                p3 = jnp.exp2(lg3 - ref[None])
                l_ref[h] = alpha * l_ref[h] + beta * jnp.sum(p3, axis=0)
                pv = _dot(ckt, p3.reshape(nk, TQ).astype(BF16)).reshape(DSA_RKV // 8, 8, TQ)
                acc3 = acc_ref[h].reshape(DSA_RKV // 8, 8, TQ) * alpha[None] + pv * beta[None]
                acc_ref[h] = acc3.reshape(DSA_RKV, TQ)
            else:
                p3 = jnp.exp2(lg3 - m_new[None])
                l_ref[h] = alpha * l_ref[h] + jnp.sum(p3, axis=0)
                pv = _dot(ckt, p3.reshape(nk, TQ).astype(BF16))
                acc3 = acc_ref[h].reshape(DSA_RKV // 8, 8, TQ) * alpha[None]
                acc_ref[h] = acc3.reshape(DSA_RKV, TQ) + pv

    n_far = jnp.maximum(i - 1, 0)

    def attend_all(lagged):
        m_ref[...] = jnp.full(m_ref.shape, NEG, F32)
        l_ref[...] = jnp.zeros(l_ref.shape, F32)
        acc_ref[...] = jnp.zeros(acc_ref.shape, F32)

        @pl.when(i >= 1)
        def _():
            attend([(i - 1, 0), (i, 1)], lagged)

        @pl.when(i == 0)
        def _():
            attend([(i, 1)], lagged)

        def far_pair(jj, carry):
            attend([(2 * jj, None), (2 * jj + 1, None)], lagged)
            return carry

        lax.fori_loop(0, n_far // 2, far_pair, 0)

        @pl.when(n_far % 2 == 1)
        def _():
            attend([(n_far - 1, None)], lagged)

    x_ref[...] = jnp.zeros(x_ref.shape, F32)
    attend_all(True)

    @pl.when(jnp.max(x_ref[...]) > LAG_LIMIT)
    def _():
        attend_all(False)


    for h in range(DSA_HEADS):
        l_tot = _sublane_all(l_ref[h], jnp.add)
        o = (acc_ref[h].reshape(DSA_RKV // 8, 8, TQ) / l_tot[None]).reshape(DSA_RKV, TQ)
        o_ref[:, h * LANES:(h + 1) * LANES] = o.T.astype(BF16)


def _t5_bucket(rel):
    nb = REL_BUCKETS // 2
    max_exact = nb // 2
    base = jnp.where(rel > 0, nb, 0)
    n = jnp.abs(rel)
    nf = jnp.maximum(n, 1).astype(jnp.float32)
    large = max_exact + (jnp.log(nf / max_exact) / math.log(REL_MAX_DIST / max_exact)
                         * (nb - max_exact)).astype(jnp.int32)
    large = jnp.minimum(large, nb - 1)
    return base + jnp.where(n < max_exact, n, large)


def _bias_tables(rel_bias):
    s = jnp.arange(TK, dtype=jnp.int32)[:, None]
    t = jnp.arange(TQ, dtype=jnp.int32)[None, :]
    rel = jnp.stack([s - TK - t, s - t])
    table = rel_bias.astype(F32)
    bucket = _t5_bucket(rel)
    far_bucket = _t5_bucket(jnp.full((), -REL_MAX_DIST, jnp.int32))
    near = jnp.zeros((DSA_HEADS,) + rel.shape, F32)
    far = jnp.zeros((DSA_HEADS,), F32)
    for b in range(REL_BUCKETS):
        near = near + jnp.where(bucket[None] == b, table[b][:, None, None, None], 0.0)
        far = far + jnp.where(far_bucket == b, table[b], 0.0)
    return (near - far[:, None, None, None]) * LOG2E


def _search_tables(nq, k_sel):
    nd = statistics.NormalDist()
    tab = np.zeros((nq, 2, TQ), np.float32)
    for i in range(nq):
        for c in range(TQ // CHUNK):
            n = i * TQ + (c + 1) * CHUNK
            z = nd.inv_cdf(1.0 - (k_sel - 0.5) / n) if n > k_sel else 0.0
            tab[i, 0, c * CHUNK:(c + 1) * CHUNK] = z
            tab[i, 1, c * CHUNK:(c + 1) * CHUNK] = n * nd.pdf(z)
    return jnp.asarray(tab)


def _dsa_attention(qlat, qidx, wt, ckvn, ckt, kidx, bnear, B, S):
    nq = S // TQ
    N = B * S
    k_sel = min(TOPK_MAX, S // 4)
    zq = _search_tables(nq, k_sel)
    tri = jnp.asarray(np.tril(np.ones((TK, TK), np.float32)), dtype=BF16)
    stacked = pl.BlockSpec((1, DSA_HEADS, TQ, LANES), lambda b, i: (b * nq + i, 0, 0, 0))
    keys = pl.BlockSpec((nq, TK, LANES), lambda b, i: (b, 0, 0))
    keys_t = pl.BlockSpec((nq, LANES, TK), lambda b, i: (b, 0, 0))
    return pl.pallas_call(
        functools.partial(_dsa_kernel, k_sel=k_sel),
        grid=(B, nq),
        in_specs=[stacked, stacked, pl.BlockSpec((IDX_HEADS, TQ), lambda b, i: (0, b * nq + i)),
                  keys, keys_t, keys,
                  pl.BlockSpec(bnear.shape, lambda b, i: (0, 0, 0, 0)),
                  pl.BlockSpec((1, 2, TQ), lambda b, i: (i, 0, 0)),
                  pl.BlockSpec((TK, TK), lambda b, i: (0, 0))],
        out_specs=pl.BlockSpec((TQ, DSA_HEADS * DSA_RKV), lambda b, i: (b * nq + i, 0)),
        out_shape=jax.ShapeDtypeStruct((N, DSA_HEADS * DSA_RKV), BF16),
        scratch_shapes=[pltpu.VMEM((nq, TK, TQ), F32),
                        pltpu.VMEM((DSA_HEADS, 8, TQ), F32),
                        pltpu.VMEM((DSA_HEADS, 8, TQ), F32),
                        pltpu.VMEM((DSA_HEADS, 8, TQ), F32),
                        pltpu.VMEM((DSA_HEADS, DSA_RKV, TQ), F32)],
        compiler_params=_params(("parallel", "arbitrary")),
        name="dsa_attention",
    )(qlat, qidx, wt, ckvn, ckt, kidx, bnear, zq, tri)


def _merge_kernel(x_ref, ret_ref, olat_ref, mq_ref, gates_ref, km_ref, vm_ref,
                  wret_ref, wdsa_ref, wmem_ref, wout_ref, o_ref):
    mq = mq_ref[...]
    km = km_ref[0]
    vm = vm_ref[0]
    pvs = []
    for h in range(MEM_HEADS):
        sl = slice(h * MEM_DH, (h + 1) * MEM_DH)
        lg = _dot_nt(mq[:, sl], km[:, sl]) * (MEM_DH ** -0.5)
        p = jnp.exp(lg - jnp.max(lg, axis=1, keepdims=True))
        p = p / jnp.sum(p, axis=1, keepdims=True)
        pvs.append(_dot(p.astype(BF16), vm[:, sl]).astype(BF16))
    mem_b = _dot(jnp.concatenate(pvs, axis=1), wmem_ref[...])
    ret_b = _dot(ret_ref[...], wret_ref[...])
    dsa_b = _dot(olat_ref[...], wdsa_ref[...])
    g = jax.nn.sigmoid(gates_ref[...].astype(F32))
    merged = (g[:, :D_MODEL] * ret_b + g[:, D_MODEL:2 * D_MODEL] * dsa_b + g[:, 2 * D_MODEL:] * mem_b)
    o_ref[...] = x_ref[...] + _dot(merged.astype(BF16), wout_ref[...])


def _merge(x2d, ret, olat, mq, gates, km, vm, wret, wdsa, wmem, wout, S):
    N = x2d.shape[0]
    per_b = S // TM_MERGE
    row = lambda i: (i, 0)
    const = lambda i: (0, 0)
    memb = lambda i: (i // per_b, 0, 0)
    return pl.pallas_call(
        _merge_kernel,
        grid=(N // TM_MERGE,),
        in_specs=[pl.BlockSpec((TM_MERGE, D_MODEL), row), pl.BlockSpec((TM_MERGE, RET_W), row),
                  pl.BlockSpec((TM_MERGE, DSA_HEADS * DSA_RKV), row), pl.BlockSpec((TM_MERGE, MEM_W), row),
                  pl.BlockSpec((TM_MERGE, 3 * D_MODEL), row),
                  pl.BlockSpec((1, MEM_LEN, MEM_W), memb), pl.BlockSpec((1, MEM_LEN, MEM_W), memb),
                  pl.BlockSpec(wret.shape, const), pl.BlockSpec(wdsa.shape, const),
                  pl.BlockSpec(wmem.shape, const), pl.BlockSpec(wout.shape, const)],
        out_specs=pl.BlockSpec((TM_MERGE, D_MODEL), row),
        out_shape=jax.ShapeDtypeStruct(x2d.shape, F32),
        compiler_params=_params(("parallel",)),
        name="merge_out",
    )(x2d, ret, olat, mq, gates, km, vm, wret, wdsa, wmem, wout)


def _mlp_kernel(x_ref, g_ref, w1_ref, w2_ref, gf_ref, o_ref, *, final):
    x = x_ref[...]
    h = _rms(x, g_ref[...]).astype(BF16)
    acc = jnp.zeros(x.shape, F32)
    for c in range(D_FF // D_MODEL):
        sl = slice(c * D_MODEL, (c + 1) * D_MODEL)
        a = jnp.maximum(_dot(h, w1_ref[:, sl]), 0.0)
        acc = acc + _dot((a * a).astype(BF16), w2_ref[sl, :])
    y = x + acc
    if final:
        y = _rms(y, gf_ref[...])
    o_ref[...] = y


def _mlp(x2d, g, w1, w2, gf, final):
    N = x2d.shape[0]
    row = lambda i: (i, 0)
    const = lambda i: (0, 0)
    return pl.pallas_call(
        functools.partial(_mlp_kernel, final=final),
        grid=(N // TM_MLP,),
        in_specs=[pl.BlockSpec((TM_MLP, D_MODEL), row), pl.BlockSpec((1, D_MODEL), const),
                  pl.BlockSpec(w1.shape, const), pl.BlockSpec(w2.shape, const),
                  pl.BlockSpec((1, D_MODEL), const)],
        out_specs=pl.BlockSpec((TM_MLP, D_MODEL), row),
        out_shape=jax.ShapeDtypeStruct(x2d.shape, F32),
        compiler_params=_params(("parallel",)),
        name="mlp",
    )(x2d, g, w1, w2, gf)


def _rope_tables(S):
    half = RET_DK // 2
    pos = jnp.arange(S, dtype=F32)
    freqs = ROPE_BASE ** (-jnp.arange(half, dtype=F32) / half)
    ang = pos[:, None] * freqs[None, :]
    return jnp.tile(jnp.cos(ang), (1, RET_HEADS)), jnp.tile(jnp.sin(ang), (1, RET_HEADS))


def _in_weights(wi, w_iq_l):
    half = RET_DK // 2
    def gather_halves(w):
        return jnp.transpose(w.reshape(D_MODEL, RET_HEADS, 2, half), (0, 2, 1, 3)).reshape(D_MODEL, RET_W)

    offs = np.concatenate([[0], np.cumsum(IN_SIZES)])
    seg = lambda k: wi[:, int(offs[k]):int(offs[k + 1])]
    wa = jnp.concatenate([gather_halves(seg(0)), gather_halves(seg(1)), seg(2), seg(3), seg(8), seg(9)],
                         axis=1).astype(BF16)
    pad = jnp.zeros((D_MODEL, LANES - IDX_HEADS), F32)
    widx = jnp.concatenate([seg(4), seg(5), jnp.tile(seg(6), (1, 4)), seg(7), pad], axis=1)
    wiq = jnp.tile(w_iq_l[:, :, None, :], (1, 1, 4, 1)).reshape(DSA_RQ, IDX_HEADS * LANES)
    return wa, widx.astype(BF16), wiq.astype(BF16)


def kernel(x, mem, norm1, w_in, q_norm, kv_norm, w_uq, w_iq, w_uk, w_uv, mem_norm, w_mem_kv,
           w_ret_o, w_dsa_o, w_mem_o, w_out, norm2, w_ff1, w_ff2, rel_bias, final_norm):
    B, S, D = x.shape
    depth = w_in.shape[0]
    assert D == D_MODEL and S % TM_MLP == 0 and S % TQ == 0 and TQ == TK
    cos_t, sin_t = _rope_tables(S)
    bnear = _bias_tables(rel_bias)
    wql_all, wdsa_all = _fold_weights(w_uq, w_uk, w_uv, w_dsa_o)
    x2d = x.reshape(B * S, D)
    row = lambda v: v.reshape(1, -1)
    for l in range(depth):
        wa, widx, wiq = _in_weights(w_in[l], w_iq[l])
        rq, rk, rv, rg, mq, gates, idx = _in_proj(x2d, row(norm1[l]), cos_t, sin_t, wa, widx, S)
        ret = _retention(rq, rk, rv, rg, B, S)
        qlat, qidx, wt, ckvn, ckt, kidx = _dsa_prep(idx, row(q_norm[l]), row(kv_norm[l]), wql_all[l], wiq)
        olat = _dsa_attention(qlat, qidx, wt, ckvn, ckt, kidx, bnear, B, S)
        km, vm = _mem_kv(mem, row(mem_norm[l]), w_mem_kv[l].astype(BF16))
        x2d = _merge(x2d, ret, olat, mq, gates, km, vm, w_ret_o[l].astype(BF16), wdsa_all[l],
                     w_mem_o[l].astype(BF16), w_out[l].astype(BF16), S)
        x2d = _mlp(x2d, row(norm2[l]), w_ff1[l].astype(BF16), w_ff2[l].astype(BF16), row(final_norm),
                   final=(l == depth - 1))
    return x2d.reshape(B, S, D)
```

```python
import functools
import math
import statistics

import numpy as np
import jax
import jax.numpy as jnp
from jax import lax
from jax.experimental import pallas as pl
from jax.experimental.pallas import tpu as pltpu

F32 = jnp.float32
BF16 = jnp.bfloat16

D_MODEL = 1024
CHUNK = 64
EPS = 1e-6
GN_EPS = 1e-5
RET_HEADS = 8
RET_DK = 64
RET_W = 512
ROPE_BASE = 10000.0
DSA_HEADS = 8
DSA_DH = 64
DSA_RQ = 256
DSA_RKV = 128
IDX_HEADS = 8
IDX_DIM = 32
TOPK_MAX = 256
MEM_LEN = 256
MEM_HEADS = 4
MEM_DH = 128
MEM_W = 512
REL_BUCKETS = 32
REL_MAX_DIST = 128
D_FF = 4096
IN_SIZES = (512, 512, 512, 512, DSA_RQ, DSA_RKV, IDX_DIM, IDX_HEADS, MEM_W, 3 * D_MODEL)

LANES = 128
IDX_COLS = 640
NEG = -1e30
LOG2E = math.log2(math.e)
F32_MAX = float(np.finfo(np.float32).max)
VMEM_LIMIT = 56 * 1024 * 1024

TM_IN = 256
T_RET = 256
TQ = 256
TK = 256
SUM_MAX = 1e30
SUM_MIN = 2.0 ** -80
SEARCH_UNCHECKED = 8
SEARCH_SWITCH = 16
SEARCH_FEW = 4.0
SEARCH_CAP = SEARCH_SWITCH + 34
TM_MERGE = 512
TM_MLP = 512


def _dot(a, b):
    return jnp.dot(a, b, preferred_element_type=F32)


def _dot_nt(a, b):
    return lax.dot_general(a, b, (((1,), (1,)), ((), ())), preferred_element_type=F32)


def _split(a):
    hi = a.astype(BF16)
    lo = (a - hi.astype(F32)).astype(BF16)
    return hi, lo


def _rms(x, g):
    return x * lax.rsqrt(jnp.mean(x * x, axis=-1, keepdims=True) + EPS) * g


def _params(sem):
    return pltpu.CompilerParams(dimension_semantics=sem, vmem_limit_bytes=VMEM_LIMIT)


def _fold_kernel(uq_ref, uk_ref, uv_ref, wo_ref, wql_ref, wdsa_ref):
    uq_h, uq_l = _split(uq_ref[0, 0])
    uk_h, uk_l = _split(uk_ref[0, 0])
    ql = _dot_nt(uq_h, uk_h) + _dot_nt(uq_h, uk_l) + _dot_nt(uq_l, uk_h)
    wql_ref[0] = (ql * (DSA_DH ** -0.5 * LOG2E)).astype(BF16)
    uv_h, uv_l = _split(uv_ref[0, 0])
    wo_h, wo_l = _split(wo_ref[0, 0])
    wdsa_ref[0] = (_dot(uv_h, wo_h) + _dot(uv_h, wo_l) + _dot(uv_l, wo_h)).astype(BF16)


def _fold_weights(w_uq, w_uk, w_uv, w_dsa_o):
    L = w_uq.shape[0]
    uq = jnp.transpose(w_uq, (0, 2, 1, 3))
    uk = jnp.transpose(w_uk, (0, 2, 1, 3))
    uv = jnp.transpose(w_uv, (0, 2, 1, 3))
    wo = w_dsa_o.reshape(L, DSA_HEADS, DSA_DH, D_MODEL)
    return pl.pallas_call(
        _fold_kernel,
        grid=(L, DSA_HEADS),
        in_specs=[
            pl.BlockSpec((1, 1, DSA_RQ, DSA_DH), lambda l, h: (l, h, 0, 0)),
            pl.BlockSpec((1, 1, DSA_RKV, DSA_DH), lambda l, h: (l, h, 0, 0)),
            pl.BlockSpec((1, 1, DSA_RKV, DSA_DH), lambda l, h: (l, h, 0, 0)),
            pl.BlockSpec((1, 1, DSA_DH, D_MODEL), lambda l, h: (l, h, 0, 0)),
        ],
        out_specs=[
            pl.BlockSpec((1, DSA_RQ, DSA_RKV), lambda l, h: (l, 0, h)),
            pl.BlockSpec((1, DSA_RKV, D_MODEL), lambda l, h: (l, h, 0)),
        ],
        out_shape=[
            jax.ShapeDtypeStruct((L, DSA_RQ, DSA_HEADS * DSA_RKV), BF16),
            jax.ShapeDtypeStruct((L, DSA_HEADS * DSA_RKV, D_MODEL), BF16),
        ],
        compiler_params=_params(("parallel", "parallel")),
        name="fold_weights",
    )(uq, uk, uv, wo)


def _in_kernel(x_ref, g_ref, cos_ref, sin_ref, wa_ref, widx_ref,
               rq_ref, rk_ref, rv_ref, rg_ref, mq_ref, gates_ref, idx_ref):
    h = _rms(x_ref[...], g_ref[...])
    hb = h.astype(BF16)
    c = cos_ref[...]
    s = sin_ref[...]
    half = RET_W // 2

    def rope_store(ref, z, scale):
        x1 = z[:, :half]
        x2 = z[:, half:]
        ref[:, :half] = ((x1 * c - x2 * s) * scale).astype(BF16)
        ref[:, half:] = ((x2 * c + x1 * s) * scale).astype(BF16)

    rope_store(rq_ref, _dot(hb, wa_ref[:, 0:512]), 1.0)
    rope_store(rk_ref, _dot(hb, wa_ref[:, 512:1024]), RET_DK ** -0.5)
    rv_ref[...] = _dot(hb, wa_ref[:, 1024:1536]).astype(BF16)
    rg_ref[...] = _dot(hb, wa_ref[:, 1536:2048]).astype(BF16)
    mq_ref[...] = _dot(hb, wa_ref[:, 2048:2560]).astype(BF16)
    for j in range(6):
        gates_ref[:, j * 512:(j + 1) * 512] = _dot(hb, wa_ref[:, 2560 + j * 512:3072 + j * 512]).astype(BF16)
    idx_ref[...] = _dot(hb, widx_ref[...])


def _in_proj(x2d, g, cos_t, sin_t, wa, widx, S):
    N = x2d.shape[0]
    n_pos = S // TM_IN
    row = lambda i: (i, 0)
    const = lambda i: (0, 0)
    pos = lambda i: (i % n_pos, 0)
    bf = lambda w: jax.ShapeDtypeStruct((N, w), BF16)
    return pl.pallas_call(
        _in_kernel,
        grid=(N // TM_IN,),
        in_specs=[
            pl.BlockSpec((TM_IN, D_MODEL), row),
            pl.BlockSpec((1, D_MODEL), const),
            pl.BlockSpec((TM_IN, RET_W // 2), pos),
            pl.BlockSpec((TM_IN, RET_W // 2), pos),
            pl.BlockSpec(wa.shape, const),
            pl.BlockSpec(widx.shape, const),
        ],
        out_specs=[
            pl.BlockSpec((TM_IN, 512), row), pl.BlockSpec((TM_IN, 512), row),
            pl.BlockSpec((TM_IN, 512), row), pl.BlockSpec((TM_IN, 512), row),
            pl.BlockSpec((TM_IN, 512), row), pl.BlockSpec((TM_IN, 3 * D_MODEL), row),
            pl.BlockSpec((TM_IN, IDX_COLS), row),
        ],
        out_shape=[bf(512), bf(512), bf(512), bf(512), bf(512), bf(3 * D_MODEL),
                   jax.ShapeDtypeStruct((N, IDX_COLS), F32)],
        compiler_params=_params(("parallel",)),
        name="in_proj",
    )(x2d, g, cos_t, sin_t, wa, widx)


def _ret_kernel(rq_ref, rk_ref, rv_ref, rg_ref, dmat_ref, qd_ref, kd_ref, hq_ref, hv_ref,
                cdec_ref, bd_ref, p_ref, o_ref, s_ref):
    @pl.when(pl.program_id(1) == 0)
    def _():
        s_ref[...] = jnp.zeros_like(s_ref)

    q = rq_ref[...]
    k = rk_ref[...]
    v = rv_ref[...]
    def head_lane_tiles(a, h):
        c = h // (LANES // (RET_DK // 2))
        return jnp.concatenate([a[:, c * LANES:(c + 1) * LANES], a[:, (2 + c) * LANES:(3 + c) * LANES]], axis=1)

    o_cols = []
    for pair in range(RET_HEADS // 2):
        cols = slice(pair * LANES, (pair + 1) * LANES)
        o_pair = jnp.zeros((q.shape[0], LANES), F32)
        for h in (2 * pair, 2 * pair + 1):
            sc = _dot_nt(head_lane_tiles(q * hq_ref[h], h), head_lane_tiles(k, h)) * dmat_ref[h]
            o_pair = o_pair + _dot(sc.astype(BF16), v[:, cols]) * hv_ref[h][:, cols]
        o_cols.append(o_pair)
    o = jnp.concatenate(o_cols, axis=1)
    state = s_ref[...]
    qf = q.astype(F32) * qd_ref[...]
    o = o + _dot(qf.astype(BF16), state.astype(BF16))
    kf = k.astype(F32) * kd_ref[...]
    kv = _dot(kf.T.astype(BF16), v)
    s_ref[...] = state * cdec_ref[...] + kv * bd_ref[...]

    p = p_ref[...]

    oh, ol = _split(o)
    d = o - (_dot(oh, p) + _dot(ol, p))
    y = d * lax.rsqrt(_dot((d * d).astype(BF16), p) + GN_EPS)
    g = rg_ref[...].astype(F32)
    o_ref[...] = (g * jax.nn.sigmoid(g) * y).astype(BF16)


def _ret_consts():
    T = T_RET
    hh = np.arange(RET_HEADS, dtype=np.float64)
    log_g = np.log1p(-np.exp2(-5.0 - hh))
    t = np.arange(T)
    ct = t // CHUNK
    diff = (t[:, None] - t[None, :]).astype(np.float64)
    same = ct[:, None] == ct[None, :]
    past = ct[None, :] < ct[:, None]
    expo = np.where(same, np.abs(diff), diff)
    dmat = np.where((same | past)[None], np.exp(log_g[:, None, None] * expo[None]), 0.0)
    lane = np.arange(RET_W)
    hk = (lane % (RET_W // 2)) // (RET_DK // 2)
    hv = lane // RET_DK
    qd = np.exp(log_g[hk][None, :] * (t[:, None] + 1.0))
    kd = np.exp(log_g[hk][None, :] * (T - 1.0 - t[:, None]))
    hq_mask = (hk[None, :] == np.arange(RET_HEADS)[:, None]).astype(np.float32)[:, None, :]
    hv_mask = (hv[None, :] == np.arange(RET_HEADS)[:, None]).astype(np.float32)[:, None, :]
    bd = (hk[:, None] == hv[None, :]).astype(np.float32)
    cdec = np.broadcast_to(np.exp(log_g[hk] * T)[:, None], (RET_W, RET_W))
    pmat = (hv[:, None] == hv[None, :]).astype(np.float32) / RET_DK
    f = lambda a: jnp.asarray(np.asarray(a, dtype=np.float32))
    return (f(dmat), f(qd), f(kd), jnp.asarray(hq_mask, dtype=BF16), f(hv_mask), f(cdec), f(bd),
            jnp.asarray(pmat, dtype=BF16))


def _retention(rq, rk, rv, rg, B, S):
    consts = _ret_consts()
    nb = S // T_RET
    tok = lambda b, j: (b * nb + j, 0)
    full = lambda a: pl.BlockSpec(a.shape, lambda b, j: (0,) * a.ndim)
    return pl.pallas_call(
        _ret_kernel,
        grid=(B, nb),
        in_specs=[pl.BlockSpec((T_RET, RET_W), tok)] * 4 + [full(a) for a in consts],
        out_specs=pl.BlockSpec((T_RET, RET_W), tok),
        out_shape=jax.ShapeDtypeStruct(rq.shape, BF16),
        scratch_shapes=[pltpu.VMEM((RET_W, RET_W), F32)],
        compiler_params=_params(("parallel", "arbitrary")),
        name="retention",
    )(rq, rk, rv, rg, *consts)


def _memkv_kernel(mem_ref, g_ref, w_ref, k_ref, v_ref):
    mn = _rms(mem_ref[0], g_ref[...]).astype(BF16)
    kv = _dot(mn, w_ref[...])
    k_ref[0] = kv[:, :MEM_W].astype(BF16)
    v_ref[0] = kv[:, MEM_W:].astype(BF16)


def _mem_kv(mem, g, w):
    B = mem.shape[0]
    return pl.pallas_call(
        _memkv_kernel,
        grid=(B,),
        in_specs=[pl.BlockSpec((1, MEM_LEN, D_MODEL), lambda b: (b, 0, 0)),
                  pl.BlockSpec((1, D_MODEL), lambda b: (0, 0)),
                  pl.BlockSpec(w.shape, lambda b: (0, 0))],
        out_specs=[pl.BlockSpec((1, MEM_LEN, MEM_W), lambda b: (b, 0, 0))] * 2,
        out_shape=[jax.ShapeDtypeStruct((B, MEM_LEN, MEM_W), BF16)] * 2,
        compiler_params=_params(("parallel",)),
        name="mem_kv",
    )(mem, g, w)


def _prep_kernel(idx_ref, qn_ref, kvn_ref, wql_ref, wiq_ref,
                 qlat_ref, qidx_ref, wt_ref, ckvn_ref, ckt_ref, kidx_ref):
    z = idx_ref[...]
    cqn = _rms(z[:, :DSA_RQ], qn_ref[...])
    cb = cqn.astype(BF16)
    ql = _dot(cb, wql_ref[...])
    y = _dot(cb, wiq_ref[...])
    yh = y.astype(BF16).astype(F32)
    lane = lax.broadcasted_iota(jnp.int32, y.shape, 1)
    qsel = jnp.where((lane % LANES) < 2 * IDX_DIM, yh, y - yh).astype(BF16)
    for h in range(DSA_HEADS):
        qlat_ref[0, h] = ql[:, h * LANES:(h + 1) * LANES].astype(BF16)
        qidx_ref[0, h] = qsel[:, h * LANES:(h + 1) * LANES]
    ckn = _rms(z[:, DSA_RQ:DSA_RQ + DSA_RKV], kvn_ref[...])
    ckvn_ref[0] = ckn.astype(BF16)
    ckt_ref[0] = ckn.T.astype(BF16)
    ik = z[:, 384:512]
    ikh = ik.astype(BF16).astype(F32)
    lane1 = lax.broadcasted_iota(jnp.int32, ik.shape, 1)
    kidx_ref[0] = jnp.where((lane1 // IDX_DIM) % 2 == 0, ikh, ik - ikh).astype(BF16)
    wfull = z[:, 512:640] * ((IDX_HEADS ** -0.5) * (IDX_DIM ** -0.5))
    wt_ref[...] = wfull.T[:IDX_HEADS, :]


def _dsa_prep(idx, qn, kvn, wql, wiq):
    N = idx.shape[0]
    nt = N // TQ
    const = lambda i: (0, 0)
    tile3 = lambda i: (i, 0, 0)
    stacked = pl.BlockSpec((1, DSA_HEADS, TQ, LANES), lambda i: (i, 0, 0, 0))
    return pl.pallas_call(
        _prep_kernel,
        grid=(nt,),
        in_specs=[pl.BlockSpec((TQ, IDX_COLS), lambda i: (i, 0)),
                  pl.BlockSpec((1, DSA_RQ), const), pl.BlockSpec((1, DSA_RKV), const),
                  pl.BlockSpec(wql.shape, const), pl.BlockSpec(wiq.shape, const)],
        out_specs=[stacked, stacked, pl.BlockSpec((IDX_HEADS, TQ), lambda i: (0, i)),
                   pl.BlockSpec((1, TQ, LANES), tile3), pl.BlockSpec((1, LANES, TQ), tile3),
                   pl.BlockSpec((1, TQ, LANES), tile3)],
        out_shape=[jax.ShapeDtypeStruct((nt, DSA_HEADS, TQ, LANES), BF16),
                   jax.ShapeDtypeStruct((nt, DSA_HEADS, TQ, LANES), BF16),
                   jax.ShapeDtypeStruct((IDX_HEADS, N), F32),
                   jax.ShapeDtypeStruct((nt, TQ, LANES), BF16),
                   jax.ShapeDtypeStruct((nt, LANES, TQ), BF16),
                   jax.ShapeDtypeStruct((nt, TQ, LANES), BF16)],
        compiler_params=_params(("parallel",)),
        name="dsa_prep",
    )(idx, qn, kvn, wql, wiq)


def _key_to_f32(key):
    bits = jnp.where(key >= 0, key, key ^ jnp.int32(0x7FFFFFFF))
    return lax.bitcast_convert_type(bits, F32)


def _f32_to_key(x):
    bits = lax.bitcast_convert_type(x, jnp.int32)
    return jnp.where(bits >= 0, bits, bits ^ jnp.int32(0x7FFFFFFF))


def _sublane_all(a, op):
    for shift in (4, 2, 1):
        a = op(a, pltpu.roll(a, shift, 0))
    return a


def _dsa_kernel(qlat_ref, qidx_ref, wt_ref, ckvn_ref, ckt_ref, kidx_ref, bnear_ref, zq_ref, tri_ref, o_ref,
                sc_ref, m_ref, l_ref, acc_ref, *, k_sel):
    i = pl.program_id(1)
    wt = wt_ref[...]

    def score_tiles(js):
        keys = jnp.concatenate([kidx_ref[j] for j in js], axis=0)
        y_all = _dot_nt(keys, qidx_ref[0].reshape(IDX_HEADS * TQ, LANES))
        tot = jnp.zeros((len(js) * TK, TQ), F32)
        for h in range(IDX_HEADS):
            tot = tot + jnp.maximum(y_all[:, h * TQ:(h + 1) * TQ], 0.0) * wt[h:h + 1, :]
        return tot

    def colsum8(a):
        return jnp.sum(a.reshape(a.shape[0] // 8, 8, TQ), axis=0)

    def ones_where(cond_):
        return jnp.where(cond_, 1.0, 0.0)

    def stats(s_counted, s_summed):
        return (colsum8(s_summed), colsum8(s_summed * s_summed),
                colsum8(ones_where(s_counted >= 0.0)), colsum8(ones_where(s_counted > 0.0)))

    def add_stats(carry, s):
        return tuple(a + b for a, b in zip(carry, stats(s, s)))

    def fill_pair(jj, carry):
        s = score_tiles([2 * jj, 2 * jj + 1])
        sc_ref[2 * jj] = s[:TK]
        sc_ref[2 * jj + 1] = s[TK:]
        return add_stats(carry, s)

    def fill_one(carry):
        s = score_tiles([i - 1])
        sc_ref[i - 1] = s
        return add_stats(carry, s)

    zero8 = jnp.zeros((8, TQ), F32)
    carry = lax.fori_loop(0, i // 2, fill_pair, (zero8, zero8, zero8, zero8))
    carry = lax.cond(i % 2 == 1, fill_one, lambda c: c, carry)
    key_r = lax.broadcasted_iota(jnp.int32, (TK, TQ), 0)
    qry_c = lax.broadcasted_iota(jnp.int32, (TK, TQ), 1)
    admissible = key_r <= (qry_c // CHUNK) * CHUNK + (CHUNK - 1)
    s_diag = score_tiles([i])
    s_masked = jnp.where(admissible, s_diag, -jnp.inf)
    sc_ref[i] = s_masked
    carry = tuple(a + b for a, b in zip(carry, stats(s_masked, jnp.where(admissible, s_diag, 0.0))))
    s1, s2, c_ge0, c_gt0 = (_sublane_all(a, jnp.add)[0:1, :] for a in carry)

    def count_ge(thr):
        def pair(jj, c):
            ca, cb = c
            return (ca + colsum8(ones_where(sc_ref[2 * jj] >= thr)),
                    cb + colsum8(ones_where(sc_ref[2 * jj + 1] >= thr)))
        ca, cb = lax.fori_loop(0, (i + 1) // 2, pair, (zero8, zero8))
        odd = lax.cond((i + 1) % 2 == 1, lambda: colsum8(ones_where(sc_ref[i] >= thr)), lambda: zero8)
        return _sublane_all(ca + cb + odd, jnp.add)[0:1, :]

    qc = lax.broadcasted_iota(jnp.int32, (1, TQ), 1)
    n_adm = (i * TQ + (qc // CHUNK + 1) * CHUNK).astype(F32)
    kf = jnp.float32(k_sel)
    key_min = jnp.int32(np.int32(np.array(-F32_MAX, np.float32).view(np.int32)) ^ np.int32(0x7FFFFFFF))
    key_max = jnp.int32(np.array(np.inf, np.float32).view(np.int32))
    inf = jnp.float32(np.inf)
    zq = zq_ref[0]
    mu = s1 / n_adm
    sd = jnp.sqrt(jnp.maximum(s2 / n_adm - mu * mu, 1e-30))
    dens = zq[1:2, :] / sd
    pos = c_gt0 >= kf
    tie0 = jnp.logical_and(jnp.logical_not(pos), c_ge0 >= kf)
    small = n_adm <= kf
    lo0 = jnp.where(pos | tie0, 0, key_min)
    hi0 = jnp.where(pos, key_max, jnp.where(tie0, 1, 0))
    lo0 = jnp.where(small, key_min, lo0)
    hi0 = jnp.where(small, key_min + 1, hi0)
    lf0 = jnp.where(pos | tie0, 0.0, -inf)
    hf0 = jnp.where(pos, inf, 0.0)
    cl0 = jnp.where(pos | tie0, c_ge0, n_adm)
    ch0 = jnp.where(pos, 0.0, jnp.where(tie0, c_gt0, c_ge0))
    t0 = mu + zq[0:1, :] * sd
    one = jnp.ones((1, TQ), F32)

    def mid_key(lo, hi):
        return (lo & hi) + ((lo ^ hi) >> 1)

    def unfinished(lo, hi, cl):
        return jnp.logical_not((cl == kf) | (mid_key(lo, hi) == lo))

    def cond(st):
        it, lo, hi, cl = st[0], st[1], st[2], st[5]
        return jnp.logical_and(it < SEARCH_CAP, jnp.sum(ones_where(unfinished(lo, hi, cl))) > 0.0)

    def step(st):
        it, lo, hi, lf, hf, cl, ch, t, last, wl, wh = st
        act = unfinished(lo, hi, cl)
        guided = (it < SEARCH_SWITCH).astype(jnp.int32)
        mid = mid_key(lo, hi)
        tk = mid + (_f32_to_key(t) - mid) * guided
        tk = jnp.minimum(jnp.maximum(tk, lo + 1), hi - 1)
        tf = _key_to_f32(tk)
        c = count_ge(tf)
        ge = c >= kf
        up_lo = act & ge
        up_hi = act & jnp.logical_not(ge)
        lo = jnp.where(up_lo, tk, lo)
        lf = jnp.where(up_lo, tf, lf)
        cl = jnp.where(up_lo, c, cl)
        hi = jnp.where(up_hi, tk, hi)
        hf = jnp.where(up_hi, tf, hf)
        ch = jnp.where(up_hi, c, ch)
        side = jnp.where(ge, 1.0, -1.0)
        same = side == last
        wh = jnp.where(ge, jnp.where(same, wh * 0.5, one), one)
        wl = jnp.where(ge, one, jnp.where(same, wl * 0.5, one))
        a = (cl - kf + 0.5) * wl
        b = (kf - 0.5 - ch) * wh
        frac = jnp.where(cl - ch <= SEARCH_FEW, 0.5, a / (a + b))
        t_bracket = lf + (hf - lf) * frac
        t_model = tf + 1.5 * (c - kf + jnp.where(ge, 0.5, -0.5)) / dens
        bracketed = (lf > -inf) & (hf < inf)
        t = jnp.where(bracketed, t_bracket, t_model)
        return it + 1, lo, hi, lf, hf, cl, ch, t, side, wl, wh

    st = (jnp.int32(0), lo0, hi0, lf0, hf0, cl0, ch0, t0, jnp.zeros((1, TQ), F32), one, one)
    st = lax.fori_loop(0, SEARCH_UNCHECKED, lambda _, s: step(s), st)
    st = lax.while_loop(cond, lambda s: step(step(s)), st)
    lo, cl, ch = st[1], st[5], st[6]
    thr = _key_to_f32(lo)
    tied = (cl != kf) & jnp.logical_not(small)
    need = jnp.where(tied, kf - ch, jnp.float32(1e9))

    def mask_plain(j, carry_):
        sc_ref[j] = jnp.where(sc_ref[j] >= thr, 0.0, NEG)
        return carry_

    def mask_ranked(j, before):
        s = sc_ref[j]
        eq = ones_where(s == thr)
        rank = _dot(tri_ref[...], eq.astype(BF16)) + before
        keep = (s > thr) | ((s == thr) & (rank <= need))
        sc_ref[j] = jnp.where(keep, 0.0, NEG)
        return before + jnp.sum(eq, axis=0, keepdims=True)

    any_tied = jnp.sum(ones_where(tied)) > 0.0

    def mask_ranked_pair(jj, before):
        return mask_ranked(2 * jj + 1, mask_ranked(2 * jj, before))

    @pl.when(any_tied)
    def _():
        before = lax.fori_loop(0, (i + 1) // 2, mask_ranked_pair, jnp.zeros((1, TQ), F32))

        @pl.when((i + 1) % 2 == 1)
        def _():
            mask_ranked(i, before)

    @pl.when(jnp.logical_not(any_tied))
    def _():
        lax.fori_loop(0, i + 1, mask_plain, 0)

    def attend(tiles, shifted):
        nk = len(tiles) * TK
        ck = jnp.concatenate([ckvn_ref[j] for j, _ in tiles], axis=0)
        ckt = jnp.concatenate([ckt_ref[j] for j, _ in tiles], axis=1)
        mask_bias = jnp.concatenate([sc_ref[j] for j, _ in tiles], axis=0)
        lg_all = _dot_nt(ck, qlat_ref[0].reshape(DSA_HEADS * TQ, LANES))
        for h in range(DSA_HEADS):
            bias = mask_bias
            if any(slot is not None for _, slot in tiles):
                bias = bias + jnp.concatenate(
                    [jnp.zeros((TK, TQ), F32) if slot is None else bnear_ref[h, slot] for _, slot in tiles], axis=0)
            lg3 = (lg_all[:, h * TQ:(h + 1) * TQ] + bias).reshape(nk // 8, 8, TQ)
            if shifted:
                m_old = m_ref[h]
                m_new = jnp.maximum(m_old, _sublane_all(jnp.max(lg3, axis=0), jnp.maximum))
                alpha = jnp.exp2(m_old - m_new)
                m_ref[h] = m_new
                p3 = jnp.exp2(lg3 - m_new[None])
                l_ref[h] = alpha * l_ref[h] + jnp.sum(p3, axis=0)
                pv = _dot(ckt, p3.reshape(nk, TQ).astype(BF16))
                acc3 = acc_ref[h].reshape(DSA_RKV // 8, 8, TQ) * alpha[None]
                acc_ref[h] = acc3.reshape(DSA_RKV, TQ) + pv
            else:
                p3 = jnp.exp2(lg3)
                l_ref[h] = l_ref[h] + jnp.sum(p3, axis=0)
                acc_ref[h] = acc_ref[h] + _dot(ckt, p3.reshape(nk, TQ).astype(BF16))

    n_far = jnp.maximum(i - 1, 0)

    def attend_all(shifted):
        m_ref[...] = jnp.full(m_ref.shape, NEG, F32)
        l_ref[...] = jnp.zeros(l_ref.shape, F32)
        acc_ref[...] = jnp.zeros(acc_ref.shape, F32)

        @pl.when(i >= 1)
        def _():
            attend([(i - 1, 0), (i, 1)], shifted)

        @pl.when(i == 0)
        def _():
            attend([(i, 1)], shifted)

        def far_pair(jj, carry):
            attend([(2 * jj, None), (2 * jj + 1, None)], shifted)
            return carry

        lax.fori_loop(0, n_far // 2, far_pair, 0)

        @pl.when(n_far % 2 == 1)
        def _():
            attend([(n_far - 1, None)], shifted)

    attend_all(False)

    bad = jnp.zeros((8, TQ), F32)
    for h in range(DSA_HEADS):
        l_tot = _sublane_all(l_ref[h], jnp.add)
        bad = bad + jnp.where((l_tot >= SUM_MIN) & (l_tot <= SUM_MAX), 0.0, 1.0)

    @pl.when(jnp.max(bad) > 0.0)
    def _():
        attend_all(True)


    for h in range(DSA_HEADS):
        l_tot = _sublane_all(l_ref[h], jnp.add)
        o = (acc_ref[h].reshape(DSA_RKV // 8, 8, TQ) / l_tot[None]).reshape(DSA_RKV, TQ)
        o_ref[:, h * LANES:(h + 1) * LANES] = o.T.astype(BF16)


def _t5_bucket(rel):
    nb = REL_BUCKETS // 2
    max_exact = nb // 2
    base = jnp.where(rel > 0, nb, 0)
    n = jnp.abs(rel)
    nf = jnp.maximum(n, 1).astype(jnp.float32)
    large = max_exact + (jnp.log(nf / max_exact) / math.log(REL_MAX_DIST / max_exact)
                         * (nb - max_exact)).astype(jnp.int32)
    large = jnp.minimum(large, nb - 1)
    return base + jnp.where(n < max_exact, n, large)


def _bias_tables(rel_bias):
    s = jnp.arange(TK, dtype=jnp.int32)[:, None]
    t = jnp.arange(TQ, dtype=jnp.int32)[None, :]
    rel = jnp.stack([s - TK - t, s - t])
    table = rel_bias.astype(F32)
    bucket = _t5_bucket(rel)
    far_bucket = _t5_bucket(jnp.full((), -REL_MAX_DIST, jnp.int32))
    near = jnp.zeros((DSA_HEADS,) + rel.shape, F32)
    far = jnp.zeros((DSA_HEADS,), F32)
    for b in range(REL_BUCKETS):
        near = near + jnp.where(bucket[None] == b, table[b][:, None, None, None], 0.0)
        far = far + jnp.where(far_bucket == b, table[b], 0.0)
    return (near - far[:, None, None, None]) * LOG2E


def _search_tables(nq, k_sel):
    nd = statistics.NormalDist()
    tab = np.zeros((nq, 2, TQ), np.float32)
    for i in range(nq):
        for c in range(TQ // CHUNK):
            n = i * TQ + (c + 1) * CHUNK
            z = nd.inv_cdf(1.0 - (k_sel - 0.5) / n) if n > k_sel else 0.0
            tab[i, 0, c * CHUNK:(c + 1) * CHUNK] = z
            tab[i, 1, c * CHUNK:(c + 1) * CHUNK] = n * nd.pdf(z)
    return jnp.asarray(tab)


def _dsa_attention(qlat, qidx, wt, ckvn, ckt, kidx, bnear, B, S):
    nq = S // TQ
    N = B * S
    k_sel = min(TOPK_MAX, S // 4)
    zq = _search_tables(nq, k_sel)
    tri = jnp.asarray(np.tril(np.ones((TK, TK), np.float32)), dtype=BF16)
    stacked = pl.BlockSpec((1, DSA_HEADS, TQ, LANES), lambda b, i: (b * nq + i, 0, 0, 0))
    keys = pl.BlockSpec((nq, TK, LANES), lambda b, i: (b, 0, 0))
    keys_t = pl.BlockSpec((nq, LANES, TK), lambda b, i: (b, 0, 0))
    return pl.pallas_call(
        functools.partial(_dsa_kernel, k_sel=k_sel),
        grid=(B, nq),
        in_specs=[stacked, stacked, pl.BlockSpec((IDX_HEADS, TQ), lambda b, i: (0, b * nq + i)),
                  keys, keys_t, keys,
                  pl.BlockSpec(bnear.shape, lambda b, i: (0, 0, 0, 0)),
                  pl.BlockSpec((1, 2, TQ), lambda b, i: (i, 0, 0)),
                  pl.BlockSpec((TK, TK), lambda b, i: (0, 0))],
        out_specs=pl.BlockSpec((TQ, DSA_HEADS * DSA_RKV), lambda b, i: (b * nq + i, 0)),
        out_shape=jax.ShapeDtypeStruct((N, DSA_HEADS * DSA_RKV), BF16),
        scratch_shapes=[pltpu.VMEM((nq, TK, TQ), F32),
                        pltpu.VMEM((DSA_HEADS, 8, TQ), F32),
                        pltpu.VMEM((DSA_HEADS, 8, TQ), F32),
                        pltpu.VMEM((DSA_HEADS, DSA_RKV, TQ), F32)],
        compiler_params=_params(("parallel", "arbitrary")),
        name="dsa_attention",
    )(qlat, qidx, wt, ckvn, ckt, kidx, bnear, zq, tri)


def _merge_kernel(x_ref, ret_ref, olat_ref, mq_ref, gates_ref, km_ref, vm_ref,
                  wret_ref, wdsa_ref, wmem_ref, wout_ref, o_ref):
    mq = mq_ref[...]
    km = km_ref[0]
    vm = vm_ref[0]
    pvs = []
    for h in range(MEM_HEADS):
        sl = slice(h * MEM_DH, (h + 1) * MEM_DH)
        lg = _dot_nt(mq[:, sl], km[:, sl]) * (MEM_DH ** -0.5)
        p = jnp.exp(lg - jnp.max(lg, axis=1, keepdims=True))
        p = p / jnp.sum(p, axis=1, keepdims=True)
        pvs.append(_dot(p.astype(BF16), vm[:, sl]).astype(BF16))
    mem_b = _dot(jnp.concatenate(pvs, axis=1), wmem_ref[...])
    ret_b = _dot(ret_ref[...], wret_ref[...])
    dsa_b = _dot(olat_ref[...], wdsa_ref[...])
    g = jax.nn.sigmoid(gates_ref[...].astype(F32))
    merged = (g[:, :D_MODEL] * ret_b + g[:, D_MODEL:2 * D_MODEL] * dsa_b + g[:, 2 * D_MODEL:] * mem_b)
    o_ref[...] = x_ref[...] + _dot(merged.astype(BF16), wout_ref[...])


def _merge(x2d, ret, olat, mq, gates, km, vm, wret, wdsa, wmem, wout, S):
    N = x2d.shape[0]
    per_b = S // TM_MERGE
    row = lambda i: (i, 0)
    const = lambda i: (0, 0)
    memb = lambda i: (i // per_b, 0, 0)
    return pl.pallas_call(
        _merge_kernel,
        grid=(N // TM_MERGE,),
        in_specs=[pl.BlockSpec((TM_MERGE, D_MODEL), row), pl.BlockSpec((TM_MERGE, RET_W), row),
                  pl.BlockSpec((TM_MERGE, DSA_HEADS * DSA_RKV), row), pl.BlockSpec((TM_MERGE, MEM_W), row),
                  pl.BlockSpec((TM_MERGE, 3 * D_MODEL), row),
                  pl.BlockSpec((1, MEM_LEN, MEM_W), memb), pl.BlockSpec((1, MEM_LEN, MEM_W), memb),
                  pl.BlockSpec(wret.shape, const), pl.BlockSpec(wdsa.shape, const),
                  pl.BlockSpec(wmem.shape, const), pl.BlockSpec(wout.shape, const)],
        out_specs=pl.BlockSpec((TM_MERGE, D_MODEL), row),
        out_shape=jax.ShapeDtypeStruct(x2d.shape, F32),
        compiler_params=_params(("parallel",)),
        name="merge_out",
    )(x2d, ret, olat, mq, gates, km, vm, wret, wdsa, wmem, wout)


def _mlp_kernel(x_ref, g_ref, w1_ref, w2_ref, gf_ref, o_ref, *, final):
    x = x_ref[...]
    h = _rms(x, g_ref[...]).astype(BF16)
    acc = jnp.zeros(x.shape, F32)
    for c in range(D_FF // D_MODEL):
        sl = slice(c * D_MODEL, (c + 1) * D_MODEL)
        a = jnp.maximum(_dot(h, w1_ref[:, sl]), 0.0)
        acc = acc + _dot((a * a).astype(BF16), w2_ref[sl, :])
    y = x + acc
    if final:
        y = _rms(y, gf_ref[...])
    o_ref[...] = y


def _mlp(x2d, g, w1, w2, gf, final):
    N = x2d.shape[0]
    row = lambda i: (i, 0)
    const = lambda i: (0, 0)
    return pl.pallas_call(
        functools.partial(_mlp_kernel, final=final),
        grid=(N // TM_MLP,),
        in_specs=[pl.BlockSpec((TM_MLP, D_MODEL), row), pl.BlockSpec((1, D_MODEL), const),
                  pl.BlockSpec(w1.shape, const), pl.BlockSpec(w2.shape, const),
                  pl.BlockSpec((1, D_MODEL), const)],
        out_specs=pl.BlockSpec((TM_MLP, D_MODEL), row),
        out_shape=jax.ShapeDtypeStruct(x2d.shape, F32),
        compiler_params=_params(("parallel",)),
        name="mlp",
    )(x2d, g, w1, w2, gf)


def _rope_tables(S):
    half = RET_DK // 2
    pos = jnp.arange(S, dtype=F32)
    freqs = ROPE_BASE ** (-jnp.arange(half, dtype=F32) / half)
    ang = pos[:, None] * freqs[None, :]
    return jnp.tile(jnp.cos(ang), (1, RET_HEADS)), jnp.tile(jnp.sin(ang), (1, RET_HEADS))


def _in_weights(wi, w_iq_l):
    half = RET_DK // 2
    def gather_halves(w):
        return jnp.transpose(w.reshape(D_MODEL, RET_HEADS, 2, half), (0, 2, 1, 3)).reshape(D_MODEL, RET_W)

    offs = np.concatenate([[0], np.cumsum(IN_SIZES)])
    seg = lambda k: wi[:, int(offs[k]):int(offs[k + 1])]
    wa = jnp.concatenate([gather_halves(seg(0)), gather_halves(seg(1)), seg(2), seg(3), seg(8), seg(9)],
                         axis=1).astype(BF16)
    pad = jnp.zeros((D_MODEL, LANES - IDX_HEADS), F32)
    widx = jnp.concatenate([seg(4), seg(5), jnp.tile(seg(6), (1, 4)), seg(7), pad], axis=1)
    wiq = jnp.tile(w_iq_l[:, :, None, :], (1, 1, 4, 1)).reshape(DSA_RQ, IDX_HEADS * LANES)
    return wa, widx.astype(BF16), wiq.astype(BF16)


def kernel(x, mem, norm1, w_in, q_norm, kv_norm, w_uq, w_iq, w_uk, w_uv, mem_norm, w_mem_kv,
           w_ret_o, w_dsa_o, w_mem_o, w_out, norm2, w_ff1, w_ff2, rel_bias, final_norm):
    B, S, D = x.shape
    depth = w_in.shape[0]
    assert D == D_MODEL and S % TM_MLP == 0 and S % TQ == 0 and TQ == TK
    cos_t, sin_t = _rope_tables(S)
    bnear = _bias_tables(rel_bias)
    wql_all, wdsa_all = _fold_weights(w_uq, w_uk, w_uv, w_dsa_o)
    x2d = x.reshape(B * S, D)
    row = lambda v: v.reshape(1, -1)
    for l in range(depth):
        wa, widx, wiq = _in_weights(w_in[l], w_iq[l])
        rq, rk, rv, rg, mq, gates, idx = _in_proj(x2d, row(norm1[l]), cos_t, sin_t, wa, widx, S)
        ret = _retention(rq, rk, rv, rg, B, S)
        qlat, qidx, wt, ckvn, ckt, kidx = _dsa_prep(idx, row(q_norm[l]), row(kv_norm[l]), wql_all[l], wiq)
        olat = _dsa_attention(qlat, qidx, wt, ckvn, ckt, kidx, bnear, B, S)
        km, vm = _mem_kv(mem, row(mem_norm[l]), w_mem_kv[l].astype(BF16))
        x2d = _merge(x2d, ret, olat, mq, gates, km, vm, w_ret_o[l].astype(BF16), wdsa_all[l],
                     w_mem_o[l].astype(BF16), w_out[l].astype(BF16), S)
        x2d = _mlp(x2d, row(norm2[l]), w_ff1[l].astype(BF16), w_ff2[l].astype(BF16), row(final_norm),
                   final=(l == depth - 1))
    return x2d.reshape(B, S, D)
```

```python
import functools
import math
import statistics

import numpy as np
import jax
import jax.numpy as jnp
from jax import lax
from jax.experimental import pallas as pl
from jax.experimental.pallas import tpu as pltpu

F32 = jnp.float32
BF16 = jnp.bfloat16

D_MODEL = 1024
CHUNK = 64
EPS = 1e-6
GN_EPS = 1e-5
RET_HEADS = 8
RET_DK = 64
RET_W = 512
ROPE_BASE = 10000.0
DSA_HEADS = 8
DSA_DH = 64
DSA_RQ = 256
DSA_RKV = 128
IDX_HEADS = 8
IDX_DIM = 32
TOPK_MAX = 256
MEM_LEN = 256
MEM_HEADS = 4
MEM_DH = 128
MEM_W = 512
REL_BUCKETS = 32
REL_MAX_DIST = 128
D_FF = 4096
IN_SIZES = (512, 512, 512, 512, DSA_RQ, DSA_RKV, IDX_DIM, IDX_HEADS, MEM_W, 3 * D_MODEL)

LANES = 128
IDX_COLS = 640
NEG = -1e30
LOG2E = math.log2(math.e)
F32_MAX = float(np.finfo(np.float32).max)
VMEM_LIMIT = 56 * 1024 * 1024

TM_IN = 256
T_RET = 256
TQ = 256
TK = 256
SUM_MAX = 1e30
SUM_MIN = 2.0 ** -80
SEARCH_UNCHECKED = 8
SEARCH_SWITCH = 16
SEARCH_FEW = 4.0
SEARCH_CAP = SEARCH_SWITCH + 34
TM_MERGE = 512
TM_MLP = 512


def _dot(a, b):
    return jnp.dot(a, b, preferred_element_type=F32)


def _dot_nt(a, b):
    return lax.dot_general(a, b, (((1,), (1,)), ((), ())), preferred_element_type=F32)


def _split(a):
    hi = a.astype(BF16)
    lo = (a - hi.astype(F32)).astype(BF16)
    return hi, lo


def _rms(x, g):
    return x * lax.rsqrt(jnp.mean(x * x, axis=-1, keepdims=True) + EPS) * g


def _params(sem):
    return pltpu.CompilerParams(dimension_semantics=sem, vmem_limit_bytes=VMEM_LIMIT)


def _fold_kernel(uq_ref, uk_ref, uv_ref, wo_ref, wql_ref, wdsa_ref):
    uq_h, uq_l = _split(uq_ref[0, 0])
    uk_h, uk_l = _split(uk_ref[0, 0])
    ql = _dot_nt(uq_h, uk_h) + _dot_nt(uq_h, uk_l) + _dot_nt(uq_l, uk_h)
    wql_ref[0] = (ql * (DSA_DH ** -0.5 * LOG2E)).astype(BF16)
    uv_h, uv_l = _split(uv_ref[0, 0])
    wo_h, wo_l = _split(wo_ref[0, 0])
    wdsa_ref[0] = (_dot(uv_h, wo_h) + _dot(uv_h, wo_l) + _dot(uv_l, wo_h)).astype(BF16)


def _fold_weights(w_uq, w_uk, w_uv, w_dsa_o):
    L = w_uq.shape[0]
    uq = jnp.transpose(w_uq, (0, 2, 1, 3))
    uk = jnp.transpose(w_uk, (0, 2, 1, 3))
    uv = jnp.transpose(w_uv, (0, 2, 1, 3))
    wo = w_dsa_o.reshape(L, DSA_HEADS, DSA_DH, D_MODEL)
    return pl.pallas_call(
        _fold_kernel,
        grid=(L, DSA_HEADS),
        in_specs=[
            pl.BlockSpec((1, 1, DSA_RQ, DSA_DH), lambda l, h: (l, h, 0, 0)),
            pl.BlockSpec((1, 1, DSA_RKV, DSA_DH), lambda l, h: (l, h, 0, 0)),
            pl.BlockSpec((1, 1, DSA_RKV, DSA_DH), lambda l, h: (l, h, 0, 0)),
            pl.BlockSpec((1, 1, DSA_DH, D_MODEL), lambda l, h: (l, h, 0, 0)),
        ],
        out_specs=[
            pl.BlockSpec((1, DSA_RQ, DSA_RKV), lambda l, h: (l, 0, h)),
            pl.BlockSpec((1, DSA_RKV, D_MODEL), lambda l, h: (l, h, 0)),
        ],
        out_shape=[
            jax.ShapeDtypeStruct((L, DSA_RQ, DSA_HEADS * DSA_RKV), BF16),
            jax.ShapeDtypeStruct((L, DSA_HEADS * DSA_RKV, D_MODEL), BF16),
        ],
        compiler_params=_params(("parallel", "parallel")),
        name="fold_weights",
    )(uq, uk, uv, wo)


def _in_kernel(x_ref, g_ref, cos_ref, sin_ref, wa_ref, widx_ref, qn_ref, kvn_ref, wql_ref, wiq_ref,
               rq_ref, rk_ref, rv_ref, rg_ref, mq_ref, gates_ref,
               qlat_ref, qidx_ref, wt_ref, ckvn_ref, ckt_ref, kidx_ref):
    h = _rms(x_ref[...], g_ref[...])
    hb = h.astype(BF16)
    c = cos_ref[...]
    s = sin_ref[...]
    half = RET_W // 2

    def rope_store(ref, z, scale):
        x1 = z[:, :half]
        x2 = z[:, half:]
        ref[:, :half] = ((x1 * c - x2 * s) * scale).astype(BF16)
        ref[:, half:] = ((x2 * c + x1 * s) * scale).astype(BF16)

    rope_store(rq_ref, _dot(hb, wa_ref[:, 0:512]), 1.0)
    rope_store(rk_ref, _dot(hb, wa_ref[:, 512:1024]), RET_DK ** -0.5)
    rv_ref[...] = _dot(hb, wa_ref[:, 1024:1536]).astype(BF16)
    rg_ref[...] = _dot(hb, wa_ref[:, 1536:2048]).astype(BF16)
    mq_ref[...] = _dot(hb, wa_ref[:, 2048:2560]).astype(BF16)
    for j in range(6):
        gates_ref[:, j * 512:(j + 1) * 512] = _dot(hb, wa_ref[:, 2560 + j * 512:3072 + j * 512]).astype(BF16)
    _dsa_prep_tile(_dot(hb, widx_ref[...]), qn_ref, kvn_ref, wql_ref, wiq_ref,
                   qlat_ref, qidx_ref, wt_ref, ckvn_ref, ckt_ref, kidx_ref)


def _in_proj(x2d, g, cos_t, sin_t, wa, widx, qn, kvn, wql, wiq, S):
    N = x2d.shape[0]
    nt = N // TM_IN
    n_pos = S // TM_IN
    row = lambda i: (i, 0)
    const = lambda i: (0, 0)
    pos = lambda i: (i % n_pos, 0)
    tile3 = lambda i: (i, 0, 0)
    stacked = pl.BlockSpec((1, DSA_HEADS, TQ, LANES), lambda i: (i, 0, 0, 0))
    bf = lambda w: jax.ShapeDtypeStruct((N, w), BF16)
    return pl.pallas_call(
        _in_kernel,
        grid=(N // TM_IN,),
        in_specs=[
            pl.BlockSpec((TM_IN, D_MODEL), row),
            pl.BlockSpec((1, D_MODEL), const),
            pl.BlockSpec((TM_IN, RET_W // 2), pos),
            pl.BlockSpec((TM_IN, RET_W // 2), pos),
            pl.BlockSpec(wa.shape, const),
            pl.BlockSpec(widx.shape, const),
            pl.BlockSpec((1, DSA_RQ), const), pl.BlockSpec((1, DSA_RKV), const),
            pl.BlockSpec(wql.shape, const), pl.BlockSpec(wiq.shape, const),
        ],
        out_specs=[
            pl.BlockSpec((TM_IN, 512), row), pl.BlockSpec((TM_IN, 512), row),
            pl.BlockSpec((TM_IN, 512), row), pl.BlockSpec((TM_IN, 512), row),
            pl.BlockSpec((TM_IN, 512), row), pl.BlockSpec((TM_IN, 3 * D_MODEL), row),
            stacked, stacked, pl.BlockSpec((IDX_HEADS, TQ), lambda i: (0, i)),
            pl.BlockSpec((1, TQ, LANES), tile3), pl.BlockSpec((1, LANES, TQ), tile3),
            pl.BlockSpec((1, TQ, LANES), tile3),
        ],
        out_shape=[bf(512), bf(512), bf(512), bf(512), bf(512), bf(3 * D_MODEL),
                   jax.ShapeDtypeStruct((nt, DSA_HEADS, TQ, LANES), BF16),
                   jax.ShapeDtypeStruct((nt, DSA_HEADS, TQ, LANES), BF16),
                   jax.ShapeDtypeStruct((IDX_HEADS, N), F32),
                   jax.ShapeDtypeStruct((nt, TQ, LANES), BF16),
                   jax.ShapeDtypeStruct((nt, LANES, TQ), BF16),
                   jax.ShapeDtypeStruct((nt, TQ, LANES), BF16)],
        compiler_params=_params(("parallel",)),
        name="in_proj",
    )(x2d, g, cos_t, sin_t, wa, widx, qn, kvn, wql, wiq)


def _ret_kernel(rq_ref, rk_ref, rv_ref, rg_ref, dmat_ref, qd_ref, kd_ref, hq_ref, hv_ref,
                cdec_ref, bd_ref, p_ref, o_ref, s_ref):
    @pl.when(pl.program_id(1) == 0)
    def _():
        s_ref[...] = jnp.zeros_like(s_ref)

    q = rq_ref[...]
    k = rk_ref[...]
    v = rv_ref[...]
    def head_lane_tiles(a, h):
        c = h // (LANES // (RET_DK // 2))
        return jnp.concatenate([a[:, c * LANES:(c + 1) * LANES], a[:, (2 + c) * LANES:(3 + c) * LANES]], axis=1)

    o_cols = []
    for pair in range(RET_HEADS // 2):
        cols = slice(pair * LANES, (pair + 1) * LANES)
        o_pair = jnp.zeros((q.shape[0], LANES), F32)
        for h in (2 * pair, 2 * pair + 1):
            sc = _dot_nt(head_lane_tiles(q * hq_ref[h], h), head_lane_tiles(k, h)) * dmat_ref[h]
            o_pair = o_pair + _dot(sc.astype(BF16), v[:, cols]) * hv_ref[h][:, cols]
        o_cols.append(o_pair)
    o = jnp.concatenate(o_cols, axis=1)
    state = s_ref[...]
    qf = q.astype(F32) * qd_ref[...]
    o = o + _dot(qf.astype(BF16), state.astype(BF16))
    kf = k.astype(F32) * kd_ref[...]
    kv = _dot(kf.T.astype(BF16), v)
    s_ref[...] = state * cdec_ref[...] + kv * bd_ref[...]

    p = p_ref[...]

    oh, ol = _split(o)
    d = o - (_dot(oh, p) + _dot(ol, p))
    y = d * lax.rsqrt(_dot((d * d).astype(BF16), p) + GN_EPS)
    g = rg_ref[...].astype(F32)
    o_ref[...] = (g * jax.nn.sigmoid(g) * y).astype(BF16)


def _ret_consts():
    T = T_RET
    hh = np.arange(RET_HEADS, dtype=np.float64)
    log_g = np.log1p(-np.exp2(-5.0 - hh))
    t = np.arange(T)
    ct = t // CHUNK
    diff = (t[:, None] - t[None, :]).astype(np.float64)
    same = ct[:, None] == ct[None, :]
    past = ct[None, :] < ct[:, None]
    expo = np.where(same, np.abs(diff), diff)
    dmat = np.where((same | past)[None], np.exp(log_g[:, None, None] * expo[None]), 0.0)
    lane = np.arange(RET_W)
    hk = (lane % (RET_W // 2)) // (RET_DK // 2)
    hv = lane // RET_DK
    qd = np.exp(log_g[hk][None, :] * (t[:, None] + 1.0))
    kd = np.exp(log_g[hk][None, :] * (T - 1.0 - t[:, None]))
    hq_mask = (hk[None, :] == np.arange(RET_HEADS)[:, None]).astype(np.float32)[:, None, :]
    hv_mask = (hv[None, :] == np.arange(RET_HEADS)[:, None]).astype(np.float32)[:, None, :]
    bd = (hk[:, None] == hv[None, :]).astype(np.float32)
    cdec = np.broadcast_to(np.exp(log_g[hk] * T)[:, None], (RET_W, RET_W))
    pmat = (hv[:, None] == hv[None, :]).astype(np.float32) / RET_DK
    f = lambda a: jnp.asarray(np.asarray(a, dtype=np.float32))
    return (f(dmat), f(qd), f(kd), jnp.asarray(hq_mask, dtype=BF16), f(hv_mask), f(cdec), f(bd),
            jnp.asarray(pmat, dtype=BF16))


def _retention(rq, rk, rv, rg, B, S):
    consts = _ret_consts()
    nb = S // T_RET
    tok = lambda b, j: (b * nb + j, 0)
    full = lambda a: pl.BlockSpec(a.shape, lambda b, j: (0,) * a.ndim)
    return pl.pallas_call(
        _ret_kernel,
        grid=(B, nb),
        in_specs=[pl.BlockSpec((T_RET, RET_W), tok)] * 4 + [full(a) for a in consts],
        out_specs=pl.BlockSpec((T_RET, RET_W), tok),
        out_shape=jax.ShapeDtypeStruct(rq.shape, BF16),
        scratch_shapes=[pltpu.VMEM((RET_W, RET_W), F32)],
        compiler_params=_params(("parallel", "arbitrary")),
        name="retention",
    )(rq, rk, rv, rg, *consts)


def _memkv_kernel(mem_ref, g_ref, w_ref, k_ref, v_ref):
    mn = _rms(mem_ref[0], g_ref[...]).astype(BF16)
    kv = _dot(mn, w_ref[...])
    k_ref[0] = kv[:, :MEM_W].astype(BF16)
    v_ref[0] = kv[:, MEM_W:].astype(BF16)


def _mem_kv(mem, g, w):
    B = mem.shape[0]
    return pl.pallas_call(
        _memkv_kernel,
        grid=(B,),
        in_specs=[pl.BlockSpec((1, MEM_LEN, D_MODEL), lambda b: (b, 0, 0)),
                  pl.BlockSpec((1, D_MODEL), lambda b: (0, 0)),
                  pl.BlockSpec(w.shape, lambda b: (0, 0))],
        out_specs=[pl.BlockSpec((1, MEM_LEN, MEM_W), lambda b: (b, 0, 0))] * 2,
        out_shape=[jax.ShapeDtypeStruct((B, MEM_LEN, MEM_W), BF16)] * 2,
        compiler_params=_params(("parallel",)),
        name="mem_kv",
    )(mem, g, w)


def _dsa_prep_tile(z, qn_ref, kvn_ref, wql_ref, wiq_ref, qlat_ref, qidx_ref, wt_ref, ckvn_ref, ckt_ref, kidx_ref):
    cqn = _rms(z[:, :DSA_RQ], qn_ref[...])
    cb = cqn.astype(BF16)
    ql = _dot(cb, wql_ref[...])
    y = _dot(cb, wiq_ref[...])
    yh = y.astype(BF16).astype(F32)
    lane = lax.broadcasted_iota(jnp.int32, y.shape, 1)
    qsel = jnp.where((lane % LANES) < 2 * IDX_DIM, yh, y - yh).astype(BF16)
    for h in range(DSA_HEADS):
        qlat_ref[0, h] = ql[:, h * LANES:(h + 1) * LANES].astype(BF16)
        qidx_ref[0, h] = qsel[:, h * LANES:(h + 1) * LANES]
    ckn = _rms(z[:, DSA_RQ:DSA_RQ + DSA_RKV], kvn_ref[...])
    ckvn_ref[0] = ckn.astype(BF16)
    ckt_ref[0] = ckn.T.astype(BF16)
    ik = z[:, 384:512]
    ikh = ik.astype(BF16).astype(F32)
    lane1 = lax.broadcasted_iota(jnp.int32, ik.shape, 1)
    kidx_ref[0] = jnp.where((lane1 // IDX_DIM) % 2 == 0, ikh, ik - ikh).astype(BF16)
    wfull = z[:, 512:640] * ((IDX_HEADS ** -0.5) * (IDX_DIM ** -0.5))
    wt_ref[...] = wfull.T[:IDX_HEADS, :]


def _key_to_f32(key):
    bits = jnp.where(key >= 0, key, key ^ jnp.int32(0x7FFFFFFF))
    return lax.bitcast_convert_type(bits, F32)


def _f32_to_key(x):
    bits = lax.bitcast_convert_type(x, jnp.int32)
    return jnp.where(bits >= 0, bits, bits ^ jnp.int32(0x7FFFFFFF))


def _sublane_all(a, op):
    for shift in (4, 2, 1):
        a = op(a, pltpu.roll(a, shift, 0))
    return a


def _dsa_kernel(qlat_ref, qidx_ref, wt_ref, ckvn_ref, ckt_ref, kidx_ref, bnear_ref, zq_ref, tri_ref, o_ref,
                sc_ref, m_ref, l_ref, acc_ref, *, k_sel):
    i = pl.program_id(1)
    wt = wt_ref[...]

    def score_tiles(js):
        keys = jnp.concatenate([kidx_ref[j] for j in js], axis=0)
        y_all = _dot_nt(keys, qidx_ref[0].reshape(IDX_HEADS * TQ, LANES))
        tot = jnp.zeros((len(js) * TK, TQ), F32)
        for h in range(IDX_HEADS):
            tot = tot + jnp.maximum(y_all[:, h * TQ:(h + 1) * TQ], 0.0) * wt[h:h + 1, :]
        return tot

    def colsum8(a):
        return jnp.sum(a.reshape(a.shape[0] // 8, 8, TQ), axis=0)

    def ones_where(cond_):
        return jnp.where(cond_, 1.0, 0.0)

    def stats(s_counted, s_summed):
        return (colsum8(s_summed), colsum8(s_summed * s_summed),
                colsum8(ones_where(s_counted >= 0.0)), colsum8(ones_where(s_counted > 0.0)))

    def add_stats(carry, s):
        return tuple(a + b for a, b in zip(carry, stats(s, s)))

    def fill_pair(jj, carry):
        s = score_tiles([2 * jj, 2 * jj + 1])
        sc_ref[2 * jj] = s[:TK]
        sc_ref[2 * jj + 1] = s[TK:]
        return add_stats(carry, s)

    def fill_one(carry):
        s = score_tiles([i - 1])
        sc_ref[i - 1] = s
        return add_stats(carry, s)

    zero8 = jnp.zeros((8, TQ), F32)
    carry = lax.fori_loop(0, i // 2, fill_pair, (zero8, zero8, zero8, zero8))
    carry = lax.cond(i % 2 == 1, fill_one, lambda c: c, carry)
    key_r = lax.broadcasted_iota(jnp.int32, (TK, TQ), 0)
    qry_c = lax.broadcasted_iota(jnp.int32, (TK, TQ), 1)
    admissible = key_r <= (qry_c // CHUNK) * CHUNK + (CHUNK - 1)
    s_diag = score_tiles([i])
    s_masked = jnp.where(admissible, s_diag, -jnp.inf)
    sc_ref[i] = s_masked
    carry = tuple(a + b for a, b in zip(carry, stats(s_masked, jnp.where(admissible, s_diag, 0.0))))
    s1, s2, c_ge0, c_gt0 = (_sublane_all(a, jnp.add)[0:1, :] for a in carry)

    def count_ge(thr):
        def pair(jj, c):
            ca, cb = c
            return (ca + colsum8(ones_where(sc_ref[2 * jj] >= thr)),
                    cb + colsum8(ones_where(sc_ref[2 * jj + 1] >= thr)))
        ca, cb = lax.fori_loop(0, (i + 1) // 2, pair, (zero8, zero8))
        odd = lax.cond((i + 1) % 2 == 1, lambda: colsum8(ones_where(sc_ref[i] >= thr)), lambda: zero8)
        return _sublane_all(ca + cb + odd, jnp.add)[0:1, :]

    qc = lax.broadcasted_iota(jnp.int32, (1, TQ), 1)
    n_adm = (i * TQ + (qc // CHUNK + 1) * CHUNK).astype(F32)
    kf = jnp.float32(k_sel)
    key_min = jnp.int32(np.int32(np.array(-F32_MAX, np.float32).view(np.int32)) ^ np.int32(0x7FFFFFFF))
    key_max = jnp.int32(np.array(np.inf, np.float32).view(np.int32))
    inf = jnp.float32(np.inf)
    zq = zq_ref[0]
    mu = s1 / n_adm
    sd = jnp.sqrt(jnp.maximum(s2 / n_adm - mu * mu, 1e-30))
    dens = zq[1:2, :] / sd
    pos = c_gt0 >= kf
    tie0 = jnp.logical_and(jnp.logical_not(pos), c_ge0 >= kf)
    small = n_adm <= kf
    lo0 = jnp.where(pos | tie0, 0, key_min)
    hi0 = jnp.where(pos, key_max, jnp.where(tie0, 1, 0))
    lo0 = jnp.where(small, key_min, lo0)
    hi0 = jnp.where(small, key_min + 1, hi0)
    lf0 = jnp.where(pos | tie0, 0.0, -inf)
    hf0 = jnp.where(pos, inf, 0.0)
    cl0 = jnp.where(pos | tie0, c_ge0, n_adm)
    ch0 = jnp.where(pos, 0.0, jnp.where(tie0, c_gt0, c_ge0))
    t0 = mu + zq[0:1, :] * sd
    one = jnp.ones((1, TQ), F32)

    def mid_key(lo, hi):
        return (lo & hi) + ((lo ^ hi) >> 1)

    def unfinished(lo, hi, cl):
        return jnp.logical_not((cl == kf) | (mid_key(lo, hi) == lo))

    def cond(st):
        it, lo, hi, cl = st[0], st[1], st[2], st[5]
        return jnp.logical_and(it < SEARCH_CAP, jnp.sum(ones_where(unfinished(lo, hi, cl))) > 0.0)

    def step(st):
        it, lo, hi, lf, hf, cl, ch, t, last, wl, wh = st
        act = unfinished(lo, hi, cl)
        guided = (it < SEARCH_SWITCH).astype(jnp.int32)
        mid = mid_key(lo, hi)
        tk = mid + (_f32_to_key(t) - mid) * guided
        tk = jnp.minimum(jnp.maximum(tk, lo + 1), hi - 1)
        tf = _key_to_f32(tk)
        c = count_ge(tf)
        ge = c >= kf
        up_lo = act & ge
        up_hi = act & jnp.logical_not(ge)
        lo = jnp.where(up_lo, tk, lo)
        lf = jnp.where(up_lo, tf, lf)
        cl = jnp.where(up_lo, c, cl)
        hi = jnp.where(up_hi, tk, hi)
        hf = jnp.where(up_hi, tf, hf)
        ch = jnp.where(up_hi, c, ch)
        side = jnp.where(ge, 1.0, -1.0)
        same = side == last
        wh = jnp.where(ge, jnp.where(same, wh * 0.5, one), one)
        wl = jnp.where(ge, one, jnp.where(same, wl * 0.5, one))
        a = (cl - kf + 0.5) * wl
        b = (kf - 0.5 - ch) * wh
        frac = jnp.where(cl - ch <= SEARCH_FEW, 0.5, a / (a + b))
        t_bracket = lf + (hf - lf) * frac
        t_model = tf + 1.5 * (c - kf + jnp.where(ge, 0.5, -0.5)) / dens
        bracketed = (lf > -inf) & (hf < inf)
        t = jnp.where(bracketed, t_bracket, t_model)
        return it + 1, lo, hi, lf, hf, cl, ch, t, side, wl, wh

    st = (jnp.int32(0), lo0, hi0, lf0, hf0, cl0, ch0, t0, jnp.zeros((1, TQ), F32), one, one)
    st = lax.fori_loop(0, SEARCH_UNCHECKED, lambda _, s: step(s), st)
    st = lax.while_loop(cond, lambda s: step(step(s)), st)
    lo, cl, ch = st[1], st[5], st[6]
    thr = _key_to_f32(lo)
    tied = (cl != kf) & jnp.logical_not(small)
    need = jnp.where(tied, kf - ch, jnp.float32(1e9))

    def mask_plain(j, carry_):
        sc_ref[j] = jnp.where(sc_ref[j] >= thr, 0.0, NEG)
        return carry_

    def mask_ranked(j, before):
        s = sc_ref[j]
        eq = ones_where(s == thr)
        rank = _dot(tri_ref[...], eq.astype(BF16)) + before
        keep = (s > thr) | ((s == thr) & (rank <= need))
        sc_ref[j] = jnp.where(keep, 0.0, NEG)
        return before + jnp.sum(eq, axis=0, keepdims=True)

    any_tied = jnp.sum(ones_where(tied)) > 0.0

    def mask_ranked_pair(jj, before):
        return mask_ranked(2 * jj + 1, mask_ranked(2 * jj, before))

    @pl.when(any_tied)
    def _():
        before = lax.fori_loop(0, (i + 1) // 2, mask_ranked_pair, jnp.zeros((1, TQ), F32))

        @pl.when((i + 1) % 2 == 1)
        def _():
            mask_ranked(i, before)

    @pl.when(jnp.logical_not(any_tied))
    def _():
        lax.fori_loop(0, i + 1, mask_plain, 0)

    def attend(tiles, shifted):
        nk = len(tiles) * TK
        ck = jnp.concatenate([ckvn_ref[j] for j, _ in tiles], axis=0)
        ckt = jnp.concatenate([ckt_ref[j] for j, _ in tiles], axis=1)
        mask_bias = jnp.concatenate([sc_ref[j] for j, _ in tiles], axis=0)
        lg_all = _dot_nt(ck, qlat_ref[0].reshape(DSA_HEADS * TQ, LANES))
        for h in range(DSA_HEADS):
            bias = mask_bias
            if any(slot is not None for _, slot in tiles):
                bias = bias + jnp.concatenate(
                    [jnp.zeros((TK, TQ), F32) if slot is None else bnear_ref[h, slot] for _, slot in tiles], axis=0)
            lg3 = (lg_all[:, h * TQ:(h + 1) * TQ] + bias).reshape(nk // 8, 8, TQ)
            if shifted:
                m_old = m_ref[h]
                m_new = jnp.maximum(m_old, _sublane_all(jnp.max(lg3, axis=0), jnp.maximum))
                alpha = jnp.exp2(m_old - m_new)
                m_ref[h] = m_new
                p3 = jnp.exp2(lg3 - m_new[None])
                l_ref[h] = alpha * l_ref[h] + jnp.sum(p3, axis=0)
                pv = _dot(ckt, p3.reshape(nk, TQ).astype(BF16))
                acc3 = acc_ref[h].reshape(DSA_RKV // 8, 8, TQ) * alpha[None]
                acc_ref[h] = acc3.reshape(DSA_RKV, TQ) + pv
            else:
                p3 = jnp.exp2(lg3)
                l_ref[h] = l_ref[h] + jnp.sum(p3, axis=0)
                acc_ref[h] = acc_ref[h] + _dot(ckt, p3.reshape(nk, TQ).astype(BF16))

    n_far = jnp.maximum(i - 1, 0)

    def attend_all(shifted):
        m_ref[...] = jnp.full(m_ref.shape, NEG, F32)
        l_ref[...] = jnp.zeros(l_ref.shape, F32)
        acc_ref[...] = jnp.zeros(acc_ref.shape, F32)

        @pl.when(i >= 1)
        def _():
            attend([(i - 1, 0), (i, 1)], shifted)

        @pl.when(i == 0)
        def _():
            attend([(i, 1)], shifted)

        def far_pair(jj, carry):
            attend([(2 * jj, None), (2 * jj + 1, None)], shifted)
            return carry

        lax.fori_loop(0, n_far // 2, far_pair, 0)

        @pl.when(n_far % 2 == 1)
        def _():
            attend([(n_far - 1, None)], shifted)

    attend_all(False)

    bad = jnp.zeros((8, TQ), F32)
    for h in range(DSA_HEADS):
        l_tot = _sublane_all(l_ref[h], jnp.add)
        bad = bad + jnp.where((l_tot >= SUM_MIN) & (l_tot <= SUM_MAX), 0.0, 1.0)

    @pl.when(jnp.max(bad) > 0.0)
    def _():
        attend_all(True)


    for h in range(DSA_HEADS):
        l_tot = _sublane_all(l_ref[h], jnp.add)
        o = (acc_ref[h].reshape(DSA_RKV // 8, 8, TQ) / l_tot[None]).reshape(DSA_RKV, TQ)
        o_ref[:, h * LANES:(h + 1) * LANES] = o.T.astype(BF16)


def _t5_bucket(rel):
    nb = REL_BUCKETS // 2
    max_exact = nb // 2
    base = jnp.where(rel > 0, nb, 0)
    n = jnp.abs(rel)
    nf = jnp.maximum(n, 1).astype(jnp.float32)
    large = max_exact + (jnp.log(nf / max_exact) / math.log(REL_MAX_DIST / max_exact)
                         * (nb - max_exact)).astype(jnp.int32)
    large = jnp.minimum(large, nb - 1)
    return base + jnp.where(n < max_exact, n, large)


def _bias_tables(rel_bias):
    s = jnp.arange(TK, dtype=jnp.int32)[:, None]
    t = jnp.arange(TQ, dtype=jnp.int32)[None, :]
    rel = jnp.stack([s - TK - t, s - t])
    table = rel_bias.astype(F32)
    bucket = _t5_bucket(rel)
    far_bucket = _t5_bucket(jnp.full((), -REL_MAX_DIST, jnp.int32))
    near = jnp.zeros((DSA_HEADS,) + rel.shape, F32)
    far = jnp.zeros((DSA_HEADS,), F32)
    for b in range(REL_BUCKETS):
        near = near + jnp.where(bucket[None] == b, table[b][:, None, None, None], 0.0)
        far = far + jnp.where(far_bucket == b, table[b], 0.0)
    return (near - far[:, None, None, None]) * LOG2E


def _search_tables(nq, k_sel):
    nd = statistics.NormalDist()
    tab = np.zeros((nq, 2, TQ), np.float32)
    for i in range(nq):
        for c in range(TQ // CHUNK):
            n = i * TQ + (c + 1) * CHUNK
            z = nd.inv_cdf(1.0 - (k_sel - 0.5) / n) if n > k_sel else 0.0
            tab[i, 0, c * CHUNK:(c + 1) * CHUNK] = z
            tab[i, 1, c * CHUNK:(c + 1) * CHUNK] = n * nd.pdf(z)
    return jnp.asarray(tab)


def _dsa_attention(qlat, qidx, wt, ckvn, ckt, kidx, bnear, B, S):
    nq = S // TQ
    N = B * S
    k_sel = min(TOPK_MAX, S // 4)
    zq = _search_tables(nq, k_sel)
    tri = jnp.asarray(np.tril(np.ones((TK, TK), np.float32)), dtype=BF16)
    stacked = pl.BlockSpec((1, DSA_HEADS, TQ, LANES), lambda b, i: (b * nq + i, 0, 0, 0))
    keys = pl.BlockSpec((nq, TK, LANES), lambda b, i: (b, 0, 0))
    keys_t = pl.BlockSpec((nq, LANES, TK), lambda b, i: (b, 0, 0))
    return pl.pallas_call(
        functools.partial(_dsa_kernel, k_sel=k_sel),
        grid=(B, nq),
        in_specs=[stacked, stacked, pl.BlockSpec((IDX_HEADS, TQ), lambda b, i: (0, b * nq + i)),
                  keys, keys_t, keys,
                  pl.BlockSpec(bnear.shape, lambda b, i: (0, 0, 0, 0)),
                  pl.BlockSpec((1, 2, TQ), lambda b, i: (i, 0, 0)),
                  pl.BlockSpec((TK, TK), lambda b, i: (0, 0))],
        out_specs=pl.BlockSpec((TQ, DSA_HEADS * DSA_RKV), lambda b, i: (b * nq + i, 0)),
        out_shape=jax.ShapeDtypeStruct((N, DSA_HEADS * DSA_RKV), BF16),
        scratch_shapes=[pltpu.VMEM((nq, TK, TQ), F32),
                        pltpu.VMEM((DSA_HEADS, 8, TQ), F32),
                        pltpu.VMEM((DSA_HEADS, 8, TQ), F32),
                        pltpu.VMEM((DSA_HEADS, DSA_RKV, TQ), F32)],
        compiler_params=_params(("parallel", "arbitrary")),
        name="dsa_attention",
    )(qlat, qidx, wt, ckvn, ckt, kidx, bnear, zq, tri)


def _merge_kernel(x_ref, ret_ref, olat_ref, mq_ref, gates_ref, km_ref, vm_ref,
                  wret_ref, wdsa_ref, wmem_ref, wout_ref, o_ref):
    mq = mq_ref[...]
    km = km_ref[0]
    vm = vm_ref[0]
    pvs = []
    for h in range(MEM_HEADS):
        sl = slice(h * MEM_DH, (h + 1) * MEM_DH)
        lg = _dot_nt(mq[:, sl], km[:, sl]) * (MEM_DH ** -0.5)
        p = jnp.exp(lg - jnp.max(lg, axis=1, keepdims=True))
        p = p / jnp.sum(p, axis=1, keepdims=True)
        pvs.append(_dot(p.astype(BF16), vm[:, sl]).astype(BF16))
    mem_b = _dot(jnp.concatenate(pvs, axis=1), wmem_ref[...])
    ret_b = _dot(ret_ref[...], wret_ref[...])
    dsa_b = _dot(olat_ref[...], wdsa_ref[...])
    g = jax.nn.sigmoid(gates_ref[...].astype(F32))
    merged = (g[:, :D_MODEL] * ret_b + g[:, D_MODEL:2 * D_MODEL] * dsa_b + g[:, 2 * D_MODEL:] * mem_b)
    o_ref[...] = x_ref[...] + _dot(merged.astype(BF16), wout_ref[...])


def _merge(x2d, ret, olat, mq, gates, km, vm, wret, wdsa, wmem, wout, S):
    N = x2d.shape[0]
    per_b = S // TM_MERGE
    row = lambda i: (i, 0)
    const = lambda i: (0, 0)
    memb = lambda i: (i // per_b, 0, 0)
    return pl.pallas_call(
        _merge_kernel,
        grid=(N // TM_MERGE,),
        in_specs=[pl.BlockSpec((TM_MERGE, D_MODEL), row), pl.BlockSpec((TM_MERGE, RET_W), row),
                  pl.BlockSpec((TM_MERGE, DSA_HEADS * DSA_RKV), row), pl.BlockSpec((TM_MERGE, MEM_W), row),
                  pl.BlockSpec((TM_MERGE, 3 * D_MODEL), row),
                  pl.BlockSpec((1, MEM_LEN, MEM_W), memb), pl.BlockSpec((1, MEM_LEN, MEM_W), memb),
                  pl.BlockSpec(wret.shape, const), pl.BlockSpec(wdsa.shape, const),
                  pl.BlockSpec(wmem.shape, const), pl.BlockSpec(wout.shape, const)],
        out_specs=pl.BlockSpec((TM_MERGE, D_MODEL), row),
        out_shape=jax.ShapeDtypeStruct(x2d.shape, F32),
        compiler_params=_params(("parallel",)),
        name="merge_out",
    )(x2d, ret, olat, mq, gates, km, vm, wret, wdsa, wmem, wout)


def _mlp_kernel(x_ref, g_ref, w1_ref, w2_ref, gf_ref, o_ref, *, final):
    x = x_ref[...]
    h = _rms(x, g_ref[...]).astype(BF16)
    acc = jnp.zeros(x.shape, F32)
    for c in range(D_FF // D_MODEL):
        sl = slice(c * D_MODEL, (c + 1) * D_MODEL)
        a = jnp.maximum(_dot(h, w1_ref[:, sl]), 0.0)
        acc = acc + _dot((a * a).astype(BF16), w2_ref[sl, :])
    y = x + acc
    if final:
        y = _rms(y, gf_ref[...])
    o_ref[...] = y


def _mlp(x2d, g, w1, w2, gf, final):
    N = x2d.shape[0]
    row = lambda i: (i, 0)
    const = lambda i: (0, 0)
    return pl.pallas_call(
        functools.partial(_mlp_kernel, final=final),
        grid=(N // TM_MLP,),
        in_specs=[pl.BlockSpec((TM_MLP, D_MODEL), row), pl.BlockSpec((1, D_MODEL), const),
                  pl.BlockSpec(w1.shape, const), pl.BlockSpec(w2.shape, const),
                  pl.BlockSpec((1, D_MODEL), const)],
        out_specs=pl.BlockSpec((TM_MLP, D_MODEL), row),
        out_shape=jax.ShapeDtypeStruct(x2d.shape, F32),
        compiler_params=_params(("parallel",)),
        name="mlp",
    )(x2d, g, w1, w2, gf)


def _rope_tables(S):
    half = RET_DK // 2
    pos = jnp.arange(S, dtype=F32)
    freqs = ROPE_BASE ** (-jnp.arange(half, dtype=F32) / half)
    ang = pos[:, None] * freqs[None, :]
    return jnp.tile(jnp.cos(ang), (1, RET_HEADS)), jnp.tile(jnp.sin(ang), (1, RET_HEADS))


def _in_weights(wi, w_iq_l):
    half = RET_DK // 2
    def gather_halves(w):
        return jnp.transpose(w.reshape(D_MODEL, RET_HEADS, 2, half), (0, 2, 1, 3)).reshape(D_MODEL, RET_W)

    offs = np.concatenate([[0], np.cumsum(IN_SIZES)])
    seg = lambda k: wi[:, int(offs[k]):int(offs[k + 1])]
    wa = jnp.concatenate([gather_halves(seg(0)), gather_halves(seg(1)), seg(2), seg(3), seg(8), seg(9)],
                         axis=1).astype(BF16)
    pad = jnp.zeros((D_MODEL, LANES - IDX_HEADS), F32)
    widx = jnp.concatenate([seg(4), seg(5), jnp.tile(seg(6), (1, 4)), seg(7), pad], axis=1)
    wiq = jnp.tile(w_iq_l[:, :, None, :], (1, 1, 4, 1)).reshape(DSA_RQ, IDX_HEADS * LANES)
    return wa, widx.astype(BF16), wiq.astype(BF16)


def kernel(x, mem, norm1, w_in, q_norm, kv_norm, w_uq, w_iq, w_uk, w_uv, mem_norm, w_mem_kv,
           w_ret_o, w_dsa_o, w_mem_o, w_out, norm2, w_ff1, w_ff2, rel_bias, final_norm):
    B, S, D = x.shape
    depth = w_in.shape[0]
    assert D == D_MODEL and S % TM_MLP == 0 and S % TQ == 0 and TQ == TK and TM_IN == TQ
    cos_t, sin_t = _rope_tables(S)
    bnear = _bias_tables(rel_bias)
    wql_all, wdsa_all = _fold_weights(w_uq, w_uk, w_uv, w_dsa_o)
    x2d = x.reshape(B * S, D)
    row = lambda v: v.reshape(1, -1)
    for l in range(depth):
        wa, widx, wiq = _in_weights(w_in[l], w_iq[l])
        (rq, rk, rv, rg, mq, gates, qlat, qidx, wt, ckvn, ckt, kidx) = _in_proj(
            x2d, row(norm1[l]), cos_t, sin_t, wa, widx, row(q_norm[l]), row(kv_norm[l]), wql_all[l], wiq, S)
        ret = _retention(rq, rk, rv, rg, B, S)
        olat = _dsa_attention(qlat, qidx, wt, ckvn, ckt, kidx, bnear, B, S)
        km, vm = _mem_kv(mem, row(mem_norm[l]), w_mem_kv[l].astype(BF16))
        x2d = _merge(x2d, ret, olat, mq, gates, km, vm, w_ret_o[l].astype(BF16), wdsa_all[l],
                     w_mem_o[l].astype(BF16), w_out[l].astype(BF16), S)
        x2d = _mlp(x2d, row(norm2[l]), w_ff1[l].astype(BF16), w_ff2[l].astype(BF16), row(final_norm),
                   final=(l == depth - 1))
    return x2d.reshape(B, S, D)
```

```python
import functools
import math
import statistics

import numpy as np
import jax
import jax.numpy as jnp
from jax import lax
from jax.experimental import pallas as pl
from jax.experimental.pallas import tpu as pltpu

F32 = jnp.float32
BF16 = jnp.bfloat16

D_MODEL = 1024
CHUNK = 64
EPS = 1e-6
GN_EPS = 1e-5
RET_HEADS = 8
RET_DK = 64
RET_W = 512
ROPE_BASE = 10000.0
DSA_HEADS = 8
DSA_DH = 64
DSA_RQ = 256
DSA_RKV = 128
IDX_HEADS = 8
IDX_DIM = 32
TOPK_MAX = 256
MEM_LEN = 256
MEM_HEADS = 4
MEM_DH = 128
MEM_W = 512
REL_BUCKETS = 32
REL_MAX_DIST = 128
D_FF = 4096
IN_SIZES = (512, 512, 512, 512, DSA_RQ, DSA_RKV, IDX_DIM, IDX_HEADS, MEM_W, 3 * D_MODEL)

LANES = 128
IDX_COLS = 640
NEG = -1e30
LOG2E = math.log2(math.e)
F32_MAX = float(np.finfo(np.float32).max)
VMEM_LIMIT = 56 * 1024 * 1024

TM_IN = 256
T_RET = 256
TQ = 256
TK = 256
SUM_MAX = 1e30
SUM_MIN = 2.0 ** -80
SEARCH_UNCHECKED = 8
SEARCH_SWITCH = 16
SEARCH_FEW = 4.0
SEARCH_CAP = SEARCH_SWITCH + 34
TM_MERGE = 512
TM_MLP = 512


def _dot(a, b):
    return jnp.dot(a, b, preferred_element_type=F32)


def _dot_nt(a, b):
    return lax.dot_general(a, b, (((1,), (1,)), ((), ())), preferred_element_type=F32)


def _split(a):
    hi = a.astype(BF16)
    lo = (a - hi.astype(F32)).astype(BF16)
    return hi, lo


def _rms(x, g):
    return x * lax.rsqrt(jnp.mean(x * x, axis=-1, keepdims=True) + EPS) * g


def _params(sem):
    return pltpu.CompilerParams(dimension_semantics=sem, vmem_limit_bytes=VMEM_LIMIT)


def _fold_kernel(uq_ref, uk_ref, uv_ref, wo_ref, wql_ref, wdsa_ref):
    uq_h, uq_l = _split(uq_ref[0, 0])
    uk_h, uk_l = _split(uk_ref[0, 0])
    ql = _dot_nt(uq_h, uk_h) + _dot_nt(uq_h, uk_l) + _dot_nt(uq_l, uk_h)
    wql_ref[0] = (ql * (DSA_DH ** -0.5 * LOG2E)).astype(BF16)
    uv_h, uv_l = _split(uv_ref[0, 0])
    wo_h, wo_l = _split(wo_ref[0, 0])
    wdsa_ref[0] = (_dot(uv_h, wo_h) + _dot(uv_h, wo_l) + _dot(uv_l, wo_h)).astype(BF16)


def _fold_weights(w_uq, w_uk, w_uv, w_dsa_o):
    L = w_uq.shape[0]
    uq = jnp.transpose(w_uq, (0, 2, 1, 3))
    uk = jnp.transpose(w_uk, (0, 2, 1, 3))
    uv = jnp.transpose(w_uv, (0, 2, 1, 3))
    wo = w_dsa_o.reshape(L, DSA_HEADS, DSA_DH, D_MODEL)
    return pl.pallas_call(
        _fold_kernel,
        grid=(L, DSA_HEADS),
        in_specs=[
            pl.BlockSpec((1, 1, DSA_RQ, DSA_DH), lambda l, h: (l, h, 0, 0)),
            pl.BlockSpec((1, 1, DSA_RKV, DSA_DH), lambda l, h: (l, h, 0, 0)),
            pl.BlockSpec((1, 1, DSA_RKV, DSA_DH), lambda l, h: (l, h, 0, 0)),
            pl.BlockSpec((1, 1, DSA_DH, D_MODEL), lambda l, h: (l, h, 0, 0)),
        ],
        out_specs=[
            pl.BlockSpec((1, DSA_RQ, DSA_RKV), lambda l, h: (l, 0, h)),
            pl.BlockSpec((1, DSA_RKV, D_MODEL), lambda l, h: (l, h, 0)),
        ],
        out_shape=[
            jax.ShapeDtypeStruct((L, DSA_RQ, DSA_HEADS * DSA_RKV), BF16),
            jax.ShapeDtypeStruct((L, DSA_HEADS * DSA_RKV, D_MODEL), BF16),
        ],
        compiler_params=_params(("parallel", "parallel")),
        name="fold_weights",
    )(uq, uk, uv, wo)


def _in_kernel(x_ref, g_ref, cos_ref, sin_ref, wa_ref, widx_ref, qn_ref, kvn_ref, wql_ref, wiq_ref,
               rq_ref, rk_ref, rv_ref, rg_ref, mq_ref, gates_ref,
               qlat_ref, qidx_ref, wt_ref, ckvn_ref, ckt_ref, kidx_ref):
    h = _rms(x_ref[...], g_ref[...])
    hb = h.astype(BF16)
    c = cos_ref[...]
    s = sin_ref[...]
    half = RET_W // 2

    def rope_store(ref, z, scale):
        x1 = z[:, :half]
        x2 = z[:, half:]
        ref[:, :half] = ((x1 * c - x2 * s) * scale).astype(BF16)
        ref[:, half:] = ((x2 * c + x1 * s) * scale).astype(BF16)

    rope_store(rq_ref, _dot(hb, wa_ref[:, 0:512]), 1.0)
    rope_store(rk_ref, _dot(hb, wa_ref[:, 512:1024]), RET_DK ** -0.5)
    rv_ref[...] = _dot(hb, wa_ref[:, 1024:1536]).astype(BF16)
    rg_ref[...] = _dot(hb, wa_ref[:, 1536:2048]).astype(BF16)
    mq_ref[...] = _dot(hb, wa_ref[:, 2048:2560]).astype(BF16)
    for j in range(6):
        gates_ref[:, j * 512:(j + 1) * 512] = _dot(hb, wa_ref[:, 2560 + j * 512:3072 + j * 512]).astype(BF16)
    _dsa_prep_tile(_dot(hb, widx_ref[...]), qn_ref, kvn_ref, wql_ref, wiq_ref,
                   qlat_ref, qidx_ref, wt_ref, ckvn_ref, ckt_ref, kidx_ref)


def _in_proj(x2d, g, cos_t, sin_t, wa, widx, qn, kvn, wql, wiq, S):
    N = x2d.shape[0]
    nt = N // TM_IN
    n_pos = S // TM_IN
    row = lambda i: (i, 0)
    const = lambda i: (0, 0)
    pos = lambda i: (i % n_pos, 0)
    tile3 = lambda i: (i, 0, 0)
    stacked = pl.BlockSpec((1, DSA_HEADS, TQ, LANES), lambda i: (i, 0, 0, 0))
    bf = lambda w: jax.ShapeDtypeStruct((N, w), BF16)
    return pl.pallas_call(
        _in_kernel,
        grid=(N // TM_IN,),
        in_specs=[
            pl.BlockSpec((TM_IN, D_MODEL), row),
            pl.BlockSpec((1, D_MODEL), const),
            pl.BlockSpec((TM_IN, RET_W // 2), pos),
            pl.BlockSpec((TM_IN, RET_W // 2), pos),
            pl.BlockSpec(wa.shape, const),
            pl.BlockSpec(widx.shape, const),
            pl.BlockSpec((1, DSA_RQ), const), pl.BlockSpec((1, DSA_RKV), const),
            pl.BlockSpec(wql.shape, const), pl.BlockSpec(wiq.shape, const),
        ],
        out_specs=[
            pl.BlockSpec((TM_IN, 512), row), pl.BlockSpec((TM_IN, 512), row),
            pl.BlockSpec((TM_IN, 512), row), pl.BlockSpec((TM_IN, 512), row),
            pl.BlockSpec((TM_IN, 512), row), pl.BlockSpec((TM_IN, 3 * D_MODEL), row),
            stacked, stacked, pl.BlockSpec((IDX_HEADS, TQ), lambda i: (0, i)),
            pl.BlockSpec((1, TQ, LANES), tile3), pl.BlockSpec((1, LANES, TQ), tile3),
            pl.BlockSpec((1, TQ, LANES), tile3),
        ],
        out_shape=[bf(512), bf(512), bf(512), bf(512), bf(512), bf(3 * D_MODEL),
                   jax.ShapeDtypeStruct((nt, DSA_HEADS, TQ, LANES), BF16),
                   jax.ShapeDtypeStruct((nt, DSA_HEADS, TQ, LANES), BF16),
                   jax.ShapeDtypeStruct((IDX_HEADS, N), F32),
                   jax.ShapeDtypeStruct((nt, TQ, LANES), BF16),
                   jax.ShapeDtypeStruct((nt, LANES, TQ), BF16),
                   jax.ShapeDtypeStruct((nt, TQ, LANES), BF16)],
        compiler_params=_params(("parallel",)),
        name="in_proj",
    )(x2d, g, cos_t, sin_t, wa, widx, qn, kvn, wql, wiq)


def _ret_kernel(rq_ref, rk_ref, rv_ref, rg_ref, dmat_ref, qd_ref, kd_ref, hq_ref, hv_ref,
                cdec_ref, bd_ref, p_ref, o_ref, s_ref):
    @pl.when(pl.program_id(1) == 0)
    def _():
        s_ref[...] = jnp.zeros_like(s_ref)

    q = rq_ref[...]
    k = rk_ref[...]
    v = rv_ref[...]
    def head_lane_tiles(a, h):
        c = h // (LANES // (RET_DK // 2))
        return jnp.concatenate([a[:, c * LANES:(c + 1) * LANES], a[:, (2 + c) * LANES:(3 + c) * LANES]], axis=1)

    o_cols = []
    for pair in range(RET_HEADS // 2):
        cols = slice(pair * LANES, (pair + 1) * LANES)
        o_pair = jnp.zeros((q.shape[0], LANES), F32)
        for h in (2 * pair, 2 * pair + 1):
            sc = _dot_nt(head_lane_tiles(q * hq_ref[h], h), head_lane_tiles(k, h)) * dmat_ref[h]
            o_pair = o_pair + _dot(sc.astype(BF16), v[:, cols]) * hv_ref[h][:, cols]
        o_cols.append(o_pair)
    o = jnp.concatenate(o_cols, axis=1)
    state = s_ref[...]
    qf = q.astype(F32) * qd_ref[...]
    o = o + _dot(qf.astype(BF16), state.astype(BF16))
    kf = k.astype(F32) * kd_ref[...]
    kv = _dot(kf.T.astype(BF16), v)
    s_ref[...] = state * cdec_ref[...] + kv * bd_ref[...]

    p = p_ref[...]

    oh, ol = _split(o)
    d = o - (_dot(oh, p) + _dot(ol, p))
    y = d * lax.rsqrt(_dot((d * d).astype(BF16), p) + GN_EPS)
    g = rg_ref[...].astype(F32)
    o_ref[...] = (g * jax.nn.sigmoid(g) * y).astype(BF16)


def _ret_consts():
    T = T_RET
    hh = np.arange(RET_HEADS, dtype=np.float64)
    log_g = np.log1p(-np.exp2(-5.0 - hh))
    t = np.arange(T)
    ct = t // CHUNK
    diff = (t[:, None] - t[None, :]).astype(np.float64)
    same = ct[:, None] == ct[None, :]
    past = ct[None, :] < ct[:, None]
    expo = np.where(same, np.abs(diff), diff)
    dmat = np.where((same | past)[None], np.exp(log_g[:, None, None] * expo[None]), 0.0)
    lane = np.arange(RET_W)
    hk = (lane % (RET_W // 2)) // (RET_DK // 2)
    hv = lane // RET_DK
    qd = np.exp(log_g[hk][None, :] * (t[:, None] + 1.0))
    kd = np.exp(log_g[hk][None, :] * (T - 1.0 - t[:, None]))
    hq_mask = (hk[None, :] == np.arange(RET_HEADS)[:, None]).astype(np.float32)[:, None, :]
    hv_mask = (hv[None, :] == np.arange(RET_HEADS)[:, None]).astype(np.float32)[:, None, :]
    bd = (hk[:, None] == hv[None, :]).astype(np.float32)
    cdec = np.broadcast_to(np.exp(log_g[hk] * T)[:, None], (RET_W, RET_W))
    pmat = (hv[:, None] == hv[None, :]).astype(np.float32) / RET_DK
    f = lambda a: jnp.asarray(np.asarray(a, dtype=np.float32))
    return (f(dmat), f(qd), f(kd), jnp.asarray(hq_mask, dtype=BF16), f(hv_mask), f(cdec), f(bd),
            jnp.asarray(pmat, dtype=BF16))


def _retention(rq, rk, rv, rg, B, S):
    consts = _ret_consts()
    nb = S // T_RET
    tok = lambda b, j: (b * nb + j, 0)
    full = lambda a: pl.BlockSpec(a.shape, lambda b, j: (0,) * a.ndim)
    return pl.pallas_call(
        _ret_kernel,
        grid=(B, nb),
        in_specs=[pl.BlockSpec((T_RET, RET_W), tok)] * 4 + [full(a) for a in consts],
        out_specs=pl.BlockSpec((T_RET, RET_W), tok),
        out_shape=jax.ShapeDtypeStruct(rq.shape, BF16),
        scratch_shapes=[pltpu.VMEM((RET_W, RET_W), F32)],
        compiler_params=_params(("parallel", "arbitrary")),
        name="retention",
    )(rq, rk, rv, rg, *consts)


def _memkv_kernel(mem_ref, g_ref, w_ref, k_ref, v_ref):
    mn = _rms(mem_ref[0], g_ref[...]).astype(BF16)
    kv = _dot(mn, w_ref[...])
    k_ref[0] = kv[:, :MEM_W].astype(BF16)
    v_ref[0] = kv[:, MEM_W:].astype(BF16)


def _mem_kv(mem, g, w):
    B = mem.shape[0]
    return pl.pallas_call(
        _memkv_kernel,
        grid=(B,),
        in_specs=[pl.BlockSpec((1, MEM_LEN, D_MODEL), lambda b: (b, 0, 0)),
                  pl.BlockSpec((1, D_MODEL), lambda b: (0, 0)),
                  pl.BlockSpec(w.shape, lambda b: (0, 0))],
        out_specs=[pl.BlockSpec((1, MEM_LEN, MEM_W), lambda b: (b, 0, 0))] * 2,
        out_shape=[jax.ShapeDtypeStruct((B, MEM_LEN, MEM_W), BF16)] * 2,
        compiler_params=_params(("parallel",)),
        name="mem_kv",
    )(mem, g, w)


def _dsa_prep_tile(z, qn_ref, kvn_ref, wql_ref, wiq_ref, qlat_ref, qidx_ref, wt_ref, ckvn_ref, ckt_ref, kidx_ref):
    cqn = _rms(z[:, :DSA_RQ], qn_ref[...])
    cb = cqn.astype(BF16)
    ql = _dot(cb, wql_ref[...])
    y = _dot(cb, wiq_ref[...])
    yh = y.astype(BF16).astype(F32)
    lane = lax.broadcasted_iota(jnp.int32, y.shape, 1)
    qsel = jnp.where((lane % LANES) < 2 * IDX_DIM, yh, y - yh).astype(BF16)
    for h in range(DSA_HEADS):
        qlat_ref[0, h] = ql[:, h * LANES:(h + 1) * LANES].astype(BF16)
        qidx_ref[0, h] = qsel[:, h * LANES:(h + 1) * LANES]
    ckn = _rms(z[:, DSA_RQ:DSA_RQ + DSA_RKV], kvn_ref[...])
    ckvn_ref[0] = ckn.astype(BF16)
    ckt_ref[0] = ckn.T.astype(BF16)
    ik = z[:, 384:512]
    ikh = ik.astype(BF16).astype(F32)
    lane1 = lax.broadcasted_iota(jnp.int32, ik.shape, 1)
    kidx_ref[0] = jnp.where((lane1 // IDX_DIM) % 2 == 0, ikh, ik - ikh).astype(BF16)
    wfull = z[:, 512:640] * ((IDX_HEADS ** -0.5) * (IDX_DIM ** -0.5))
    wt_ref[...] = wfull.T[:IDX_HEADS, :]


def _key_to_f32(key):
    bits = jnp.where(key >= 0, key, key ^ jnp.int32(0x7FFFFFFF))
    return lax.bitcast_convert_type(bits, F32)


def _f32_to_key(x):
    bits = lax.bitcast_convert_type(x, jnp.int32)
    return jnp.where(bits >= 0, bits, bits ^ jnp.int32(0x7FFFFFFF))


def _sublane_all(a, op):
    for shift in (4, 2, 1):
        a = op(a, pltpu.roll(a, shift, 0))
    return a


def _dsa_kernel(qlat_ref, qidx_ref, wt_ref, ckvn_ref, ckt_ref, kidx_ref, bnear_ref, zq_ref, tri_ref, o_ref,
                sc_ref, m_ref, l_ref, acc_ref, *, k_sel):
    i = pl.program_id(1)
    wt = wt_ref[...]

    def score_tiles(js):
        keys = jnp.concatenate([kidx_ref[j] for j in js], axis=0)
        y_all = _dot_nt(keys, qidx_ref[0].reshape(IDX_HEADS * TQ, LANES))
        tot = jnp.zeros((len(js) * TK, TQ), F32)
        for h in range(IDX_HEADS):
            tot = tot + jnp.maximum(y_all[:, h * TQ:(h + 1) * TQ], 0.0) * wt[h:h + 1, :]
        return tot

    def colsum8(a):
        return jnp.sum(a.reshape(a.shape[0] // 8, 8, TQ), axis=0)

    def ones_where(cond_):
        return jnp.where(cond_, 1.0, 0.0)

    def stats(s_counted, s_summed):
        return (colsum8(s_summed), colsum8(s_summed * s_summed),
                colsum8(ones_where(s_counted >= 0.0)), colsum8(ones_where(s_counted > 0.0)))

    def add_stats(carry, s):
        return tuple(a + b for a, b in zip(carry, stats(s, s)))

    def fill(j0, count, carry):
        s = score_tiles([j0 + g for g in range(count)])
        for g in range(count):
            sc_ref[j0 + g] = s[g * TK:(g + 1) * TK]
        return add_stats(carry, s)

    zero8 = jnp.zeros((8, TQ), F32)
    carry = lax.fori_loop(0, i // 4, lambda jj, c: fill(4 * jj, 4, c), (zero8, zero8, zero8, zero8))
    carry = lax.cond(i % 4 >= 2, lambda c: fill((i // 4) * 4, 2, c), lambda c: c, carry)
    carry = lax.cond(i % 2 == 1, lambda c: fill(i - 1, 1, c), lambda c: c, carry)
    key_r = lax.broadcasted_iota(jnp.int32, (TK, TQ), 0)
    qry_c = lax.broadcasted_iota(jnp.int32, (TK, TQ), 1)
    admissible = key_r <= (qry_c // CHUNK) * CHUNK + (CHUNK - 1)
    s_diag = score_tiles([i])
    s_masked = jnp.where(admissible, s_diag, -jnp.inf)
    sc_ref[i] = s_masked
    carry = tuple(a + b for a, b in zip(carry, stats(s_masked, jnp.where(admissible, s_diag, 0.0))))
    s1, s2, c_ge0, c_gt0 = (_sublane_all(a, jnp.add)[0:1, :] for a in carry)

    def count_ge(thr):
        def pair(jj, c):
            ca, cb = c
            return (ca + colsum8(ones_where(sc_ref[2 * jj] >= thr)),
                    cb + colsum8(ones_where(sc_ref[2 * jj + 1] >= thr)))
        ca, cb = lax.fori_loop(0, (i + 1) // 2, pair, (zero8, zero8))
        odd = lax.cond((i + 1) % 2 == 1, lambda: colsum8(ones_where(sc_ref[i] >= thr)), lambda: zero8)
        return _sublane_all(ca + cb + odd, jnp.add)[0:1, :]

    qc = lax.broadcasted_iota(jnp.int32, (1, TQ), 1)
    n_adm = (i * TQ + (qc // CHUNK + 1) * CHUNK).astype(F32)
    kf = jnp.float32(k_sel)
    key_min = jnp.int32(np.int32(np.array(-F32_MAX, np.float32).view(np.int32)) ^ np.int32(0x7FFFFFFF))
    key_max = jnp.int32(np.array(np.inf, np.float32).view(np.int32))
    inf = jnp.float32(np.inf)
    zq = zq_ref[0]
    mu = s1 / n_adm
    sd = jnp.sqrt(jnp.maximum(s2 / n_adm - mu * mu, 1e-30))
    dens = zq[1:2, :] / sd
    pos = c_gt0 >= kf
    tie0 = jnp.logical_and(jnp.logical_not(pos), c_ge0 >= kf)
    small = n_adm <= kf
    lo0 = jnp.where(pos | tie0, 0, key_min)
    hi0 = jnp.where(pos, key_max, jnp.where(tie0, 1, 0))
    lo0 = jnp.where(small, key_min, lo0)
    hi0 = jnp.where(small, key_min + 1, hi0)
    lf0 = jnp.where(pos | tie0, 0.0, -inf)
    hf0 = jnp.where(pos, inf, 0.0)
    cl0 = jnp.where(pos | tie0, c_ge0, n_adm)
    ch0 = jnp.where(pos, 0.0, jnp.where(tie0, c_gt0, c_ge0))
    t0 = mu + zq[0:1, :] * sd
    one = jnp.ones((1, TQ), F32)

    def mid_key(lo, hi):
        return (lo & hi) + ((lo ^ hi) >> 1)

    def unfinished(lo, hi, cl):
        return jnp.logical_not((cl == kf) | (mid_key(lo, hi) == lo))

    def cond(st):
        it, lo, hi, cl = st[0], st[1], st[2], st[5]
        return jnp.logical_and(it < SEARCH_CAP, jnp.sum(ones_where(unfinished(lo, hi, cl))) > 0.0)

    def step(st):
        it, lo, hi, lf, hf, cl, ch, t, last, wl, wh = st
        act = unfinished(lo, hi, cl)
        guided = (it < SEARCH_SWITCH).astype(jnp.int32)
        mid = mid_key(lo, hi)
        tk = mid + (_f32_to_key(t) - mid) * guided
        tk = jnp.minimum(jnp.maximum(tk, lo + 1), hi - 1)
        tf = _key_to_f32(tk)
        c = count_ge(tf)
        ge = c >= kf
        up_lo = act & ge
        up_hi = act & jnp.logical_not(ge)
        lo = jnp.where(up_lo, tk, lo)
        lf = jnp.where(up_lo, tf, lf)
        cl = jnp.where(up_lo, c, cl)
        hi = jnp.where(up_hi, tk, hi)
        hf = jnp.where(up_hi, tf, hf)
        ch = jnp.where(up_hi, c, ch)
        side = jnp.where(ge, 1.0, -1.0)
        same = side == last
        wh = jnp.where(ge, jnp.where(same, wh * 0.5, one), one)
        wl = jnp.where(ge, one, jnp.where(same, wl * 0.5, one))
        a = (cl - kf + 0.5) * wl
        b = (kf - 0.5 - ch) * wh
        frac = jnp.where(cl - ch <= SEARCH_FEW, 0.5, a / (a + b))
        t_bracket = lf + (hf - lf) * frac
        t_model = tf + 1.5 * (c - kf + jnp.where(ge, 0.5, -0.5)) / dens
        bracketed = (lf > -inf) & (hf < inf)
        t = jnp.where(bracketed, t_bracket, t_model)
        return it + 1, lo, hi, lf, hf, cl, ch, t, side, wl, wh

    st = (jnp.int32(0), lo0, hi0, lf0, hf0, cl0, ch0, t0, jnp.zeros((1, TQ), F32), one, one)
    st = lax.fori_loop(0, SEARCH_UNCHECKED, lambda _, s: step(s), st)
    st = lax.while_loop(cond, lambda s: step(step(s)), st)
    lo, cl, ch = st[1], st[5], st[6]
    thr = _key_to_f32(lo)
    tied = (cl != kf) & jnp.logical_not(small)
    need = jnp.where(tied, kf - ch, jnp.float32(1e9))

    def mask_plain(j, carry_):
        sc_ref[j] = jnp.where(sc_ref[j] >= thr, 0.0, NEG)
        return carry_

    def mask_ranked(j, before):
        s = sc_ref[j]
        eq = ones_where(s == thr)
        rank = _dot(tri_ref[...], eq.astype(BF16)) + before
        keep = (s > thr) | ((s == thr) & (rank <= need))
        sc_ref[j] = jnp.where(keep, 0.0, NEG)
        return before + jnp.sum(eq, axis=0, keepdims=True)

    any_tied = jnp.sum(ones_where(tied)) > 0.0

    def mask_ranked_group(j0, count, before):
        for g in range(count):
            before = mask_ranked(j0 + g, before)
        return before

    @pl.when(any_tied)
    def _():
        n_all = i + 1
        before = lax.fori_loop(0, n_all // 4, lambda jj, b: mask_ranked_group(4 * jj, 4, b), jnp.zeros((1, TQ), F32))
        before = lax.cond(n_all % 4 >= 2, lambda b: mask_ranked_group((n_all // 4) * 4, 2, b), lambda b: b, before)

        @pl.when(n_all % 2 == 1)
        def _():
            mask_ranked(i, before)

    @pl.when(jnp.logical_not(any_tied))
    def _():
        lax.fori_loop(0, i + 1, mask_plain, 0)

    def attend(tiles, shifted):
        nk = len(tiles) * TK
        ck = jnp.concatenate([ckvn_ref[j] for j, _ in tiles], axis=0)
        ckt = jnp.concatenate([ckt_ref[j] for j, _ in tiles], axis=1)
        mask_bias = jnp.concatenate([sc_ref[j] for j, _ in tiles], axis=0)
        lg_all = _dot_nt(ck, qlat_ref[0].reshape(DSA_HEADS * TQ, LANES))
        for h in range(DSA_HEADS):
            bias = mask_bias
            if any(slot is not None for _, slot in tiles):
                bias = bias + jnp.concatenate(
                    [jnp.zeros((TK, TQ), F32) if slot is None else bnear_ref[h, slot] for _, slot in tiles], axis=0)
            lg3 = (lg_all[:, h * TQ:(h + 1) * TQ] + bias).reshape(nk // 8, 8, TQ)
            if shifted:
                m_old = m_ref[h]
                m_new = jnp.maximum(m_old, _sublane_all(jnp.max(lg3, axis=0), jnp.maximum))
                alpha = jnp.exp2(m_old - m_new)
                m_ref[h] = m_new
                p3 = jnp.exp2(lg3 - m_new[None])
                l_ref[h] = alpha * l_ref[h] + jnp.sum(p3, axis=0)
                pv = _dot(ckt, p3.reshape(nk, TQ).astype(BF16))
                acc3 = acc_ref[h].reshape(DSA_RKV // 8, 8, TQ) * alpha[None]
                acc_ref[h] = acc3.reshape(DSA_RKV, TQ) + pv
            else:
                p3 = jnp.exp2(lg3)
                l_ref[h] = l_ref[h] + jnp.sum(p3, axis=0)
                acc_ref[h] = acc_ref[h] + _dot(ckt, p3.reshape(nk, TQ).astype(BF16))

    n_far = jnp.maximum(i - 1, 0)

    def attend_all(shifted):
        m_ref[...] = jnp.full(m_ref.shape, NEG, F32)
        l_ref[...] = jnp.zeros(l_ref.shape, F32)
        acc_ref[...] = jnp.zeros(acc_ref.shape, F32)

        @pl.when(i >= 1)
        def _():
            attend([(i - 1, 0), (i, 1)], shifted)

        @pl.when(i == 0)
        def _():
            attend([(i, 1)], shifted)

        def far_quad(jj, carry):
            attend([(4 * jj + g, None) for g in range(4)], shifted)
            return carry

        lax.fori_loop(0, n_far // 4, far_quad, 0)

        @pl.when(n_far % 4 >= 2)
        def _():
            attend([((n_far // 4) * 4 + g, None) for g in range(2)], shifted)

        @pl.when(n_far % 2 == 1)
        def _():
            attend([(n_far - 1, None)], shifted)

    attend_all(False)

    bad = jnp.zeros((8, TQ), F32)
    for h in range(DSA_HEADS):
        l_tot = _sublane_all(l_ref[h], jnp.add)
        bad = bad + jnp.where((l_tot >= SUM_MIN) & (l_tot <= SUM_MAX), 0.0, 1.0)

    @pl.when(jnp.max(bad) > 0.0)
    def _():
        attend_all(True)


    for h in range(DSA_HEADS):
        l_tot = _sublane_all(l_ref[h], jnp.add)
        o = (acc_ref[h].reshape(DSA_RKV // 8, 8, TQ) / l_tot[None]).reshape(DSA_RKV, TQ)
        o_ref[:, h * LANES:(h + 1) * LANES] = o.T.astype(BF16)


def _t5_bucket(rel):
    nb = REL_BUCKETS // 2
    max_exact = nb // 2
    base = jnp.where(rel > 0, nb, 0)
    n = jnp.abs(rel)
    nf = jnp.maximum(n, 1).astype(jnp.float32)
    large = max_exact + (jnp.log(nf / max_exact) / math.log(REL_MAX_DIST / max_exact)
                         * (nb - max_exact)).astype(jnp.int32)
    large = jnp.minimum(large, nb - 1)
    return base + jnp.where(n < max_exact, n, large)


def _bias_tables(rel_bias):
    s = jnp.arange(TK, dtype=jnp.int32)[:, None]
    t = jnp.arange(TQ, dtype=jnp.int32)[None, :]
    rel = jnp.stack([s - TK - t, s - t])
    table = rel_bias.astype(F32)
    bucket = _t5_bucket(rel)
    far_bucket = _t5_bucket(jnp.full((), -REL_MAX_DIST, jnp.int32))
    near = jnp.zeros((DSA_HEADS,) + rel.shape, F32)
    far = jnp.zeros((DSA_HEADS,), F32)
    for b in range(REL_BUCKETS):
        near = near + jnp.where(bucket[None] == b, table[b][:, None, None, None], 0.0)
        far = far + jnp.where(far_bucket == b, table[b], 0.0)
    return (near - far[:, None, None, None]) * LOG2E


def _search_tables(nq, k_sel):
    nd = statistics.NormalDist()
    tab = np.zeros((nq, 2, TQ), np.float32)
    for i in range(nq):
        for c in range(TQ // CHUNK):
            n = i * TQ + (c + 1) * CHUNK
            z = nd.inv_cdf(1.0 - (k_sel - 0.5) / n) if n > k_sel else 0.0
            tab[i, 0, c * CHUNK:(c + 1) * CHUNK] = z
            tab[i, 1, c * CHUNK:(c + 1) * CHUNK] = n * nd.pdf(z)
    return jnp.asarray(tab)


def _dsa_attention(qlat, qidx, wt, ckvn, ckt, kidx, bnear, B, S):
    nq = S // TQ
    N = B * S
    k_sel = min(TOPK_MAX, S // 4)
    zq = _search_tables(nq, k_sel)
    tri = jnp.asarray(np.tril(np.ones((TK, TK), np.float32)), dtype=BF16)
    stacked = pl.BlockSpec((1, DSA_HEADS, TQ, LANES), lambda b, i: (b * nq + i, 0, 0, 0))
    keys = pl.BlockSpec((nq, TK, LANES), lambda b, i: (b, 0, 0))
    keys_t = pl.BlockSpec((nq, LANES, TK), lambda b, i: (b, 0, 0))
    return pl.pallas_call(
        functools.partial(_dsa_kernel, k_sel=k_sel),
        grid=(B, nq),
        in_specs=[stacked, stacked, pl.BlockSpec((IDX_HEADS, TQ), lambda b, i: (0, b * nq + i)),
                  keys, keys_t, keys,
                  pl.BlockSpec(bnear.shape, lambda b, i: (0, 0, 0, 0)),
                  pl.BlockSpec((1, 2, TQ), lambda b, i: (i, 0, 0)),
                  pl.BlockSpec((TK, TK), lambda b, i: (0, 0))],
        out_specs=pl.BlockSpec((TQ, DSA_HEADS * DSA_RKV), lambda b, i: (b * nq + i, 0)),
        out_shape=jax.ShapeDtypeStruct((N, DSA_HEADS * DSA_RKV), BF16),
        scratch_shapes=[pltpu.VMEM((nq, TK, TQ), F32),
                        pltpu.VMEM((DSA_HEADS, 8, TQ), F32),
                        pltpu.VMEM((DSA_HEADS, 8, TQ), F32),
                        pltpu.VMEM((DSA_HEADS, DSA_RKV, TQ), F32)],
        compiler_params=_params(("parallel", "arbitrary")),
        name="dsa_attention",
    )(qlat, qidx, wt, ckvn, ckt, kidx, bnear, zq, tri)


def _merge_kernel(x_ref, ret_ref, olat_ref, mq_ref, gates_ref, km_ref, vm_ref,
                  wret_ref, wdsa_ref, wmem_ref, wout_ref, o_ref):
    mq = mq_ref[...]
    km = km_ref[0]
    vm = vm_ref[0]
    pvs = []
    for h in range(MEM_HEADS):
        sl = slice(h * MEM_DH, (h + 1) * MEM_DH)
        lg = _dot_nt(mq[:, sl], km[:, sl]) * (MEM_DH ** -0.5)
        p = jnp.exp(lg - jnp.max(lg, axis=1, keepdims=True))
        p = p / jnp.sum(p, axis=1, keepdims=True)
        pvs.append(_dot(p.astype(BF16), vm[:, sl]).astype(BF16))
    mem_b = _dot(jnp.concatenate(pvs, axis=1), wmem_ref[...])
    ret_b = _dot(ret_ref[...], wret_ref[...])
    dsa_b = _dot(olat_ref[...], wdsa_ref[...])
    g = jax.nn.sigmoid(gates_ref[...].astype(F32))
    merged = (g[:, :D_MODEL] * ret_b + g[:, D_MODEL:2 * D_MODEL] * dsa_b + g[:, 2 * D_MODEL:] * mem_b)
    o_ref[...] = x_ref[...] + _dot(merged.astype(BF16), wout_ref[...])


def _merge(x2d, ret, olat, mq, gates, km, vm, wret, wdsa, wmem, wout, S):
    N = x2d.shape[0]
    per_b = S // TM_MERGE
    row = lambda i: (i, 0)
    const = lambda i: (0, 0)
    memb = lambda i: (i // per_b, 0, 0)
    return pl.pallas_call(
        _merge_kernel,
        grid=(N // TM_MERGE,),
        in_specs=[pl.BlockSpec((TM_MERGE, D_MODEL), row), pl.BlockSpec((TM_MERGE, RET_W), row),
                  pl.BlockSpec((TM_MERGE, DSA_HEADS * DSA_RKV), row), pl.BlockSpec((TM_MERGE, MEM_W), row),
                  pl.BlockSpec((TM_MERGE, 3 * D_MODEL), row),
                  pl.BlockSpec((1, MEM_LEN, MEM_W), memb), pl.BlockSpec((1, MEM_LEN, MEM_W), memb),
                  pl.BlockSpec(wret.shape, const), pl.BlockSpec(wdsa.shape, const),
                  pl.BlockSpec(wmem.shape, const), pl.BlockSpec(wout.shape, const)],
        out_specs=pl.BlockSpec((TM_MERGE, D_MODEL), row),
        out_shape=jax.ShapeDtypeStruct(x2d.shape, F32),
        compiler_params=_params(("parallel",)),
        name="merge_out",
    )(x2d, ret, olat, mq, gates, km, vm, wret, wdsa, wmem, wout)


def _mlp_kernel(x_ref, g_ref, w1_ref, w2_ref, gf_ref, o_ref, *, final):
    x = x_ref[...]
    h = _rms(x, g_ref[...]).astype(BF16)
    acc = jnp.zeros(x.shape, F32)
    for c in range(D_FF // D_MODEL):
        sl = slice(c * D_MODEL, (c + 1) * D_MODEL)
        a = jnp.maximum(_dot(h, w1_ref[:, sl]), 0.0)
        acc = acc + _dot((a * a).astype(BF16), w2_ref[sl, :])
    y = x + acc
    if final:
        y = _rms(y, gf_ref[...])
    o_ref[...] = y


def _mlp(x2d, g, w1, w2, gf, final):
    N = x2d.shape[0]
    row = lambda i: (i, 0)
    const = lambda i: (0, 0)
    return pl.pallas_call(
        functools.partial(_mlp_kernel, final=final),
        grid=(N // TM_MLP,),
        in_specs=[pl.BlockSpec((TM_MLP, D_MODEL), row), pl.BlockSpec((1, D_MODEL), const),
                  pl.BlockSpec(w1.shape, const), pl.BlockSpec(w2.shape, const),
                  pl.BlockSpec((1, D_MODEL), const)],
        out_specs=pl.BlockSpec((TM_MLP, D_MODEL), row),
        out_shape=jax.ShapeDtypeStruct(x2d.shape, F32),
        compiler_params=_params(("parallel",)),
        name="mlp",
    )(x2d, g, w1, w2, gf)


def _rope_tables(S):
    half = RET_DK // 2
    pos = jnp.arange(S, dtype=F32)
    freqs = ROPE_BASE ** (-jnp.arange(half, dtype=F32) / half)
    ang = pos[:, None] * freqs[None, :]
    return jnp.tile(jnp.cos(ang), (1, RET_HEADS)), jnp.tile(jnp.sin(ang), (1, RET_HEADS))


def _in_weights(wi, w_iq_l):
    half = RET_DK // 2
    def gather_halves(w):
        return jnp.transpose(w.reshape(D_MODEL, RET_HEADS, 2, half), (0, 2, 1, 3)).reshape(D_MODEL, RET_W)

    offs = np.concatenate([[0], np.cumsum(IN_SIZES)])
    seg = lambda k: wi[:, int(offs[k]):int(offs[k + 1])]
    wa = jnp.concatenate([gather_halves(seg(0)), gather_halves(seg(1)), seg(2), seg(3), seg(8), seg(9)],
                         axis=1).astype(BF16)
    pad = jnp.zeros((D_MODEL, LANES - IDX_HEADS), F32)
    widx = jnp.concatenate([seg(4), seg(5), jnp.tile(seg(6), (1, 4)), seg(7), pad], axis=1)
    wiq = jnp.tile(w_iq_l[:, :, None, :], (1, 1, 4, 1)).reshape(DSA_RQ, IDX_HEADS * LANES)
    return wa, widx.astype(BF16), wiq.astype(BF16)


def kernel(x, mem, norm1, w_in, q_norm, kv_norm, w_uq, w_iq, w_uk, w_uv, mem_norm, w_mem_kv,
           w_ret_o, w_dsa_o, w_mem_o, w_out, norm2, w_ff1, w_ff2, rel_bias, final_norm):
    B, S, D = x.shape
    depth = w_in.shape[0]
    assert D == D_MODEL and S % TM_MLP == 0 and S % TQ == 0 and TQ == TK and TM_IN == TQ
    cos_t, sin_t = _rope_tables(S)
    bnear = _bias_tables(rel_bias)
    wql_all, wdsa_all = _fold_weights(w_uq, w_uk, w_uv, w_dsa_o)
    x2d = x.reshape(B * S, D)
    row = lambda v: v.reshape(1, -1)
    for l in range(depth):
        wa, widx, wiq = _in_weights(w_in[l], w_iq[l])
        (rq, rk, rv, rg, mq, gates, qlat, qidx, wt, ckvn, ckt, kidx) = _in_proj(
            x2d, row(norm1[l]), cos_t, sin_t, wa, widx, row(q_norm[l]), row(kv_norm[l]), wql_all[l], wiq, S)
        ret = _retention(rq, rk, rv, rg, B, S)
        olat = _dsa_attention(qlat, qidx, wt, ckvn, ckt, kidx, bnear, B, S)
        km, vm = _mem_kv(mem, row(mem_norm[l]), w_mem_kv[l].astype(BF16))
        x2d = _merge(x2d, ret, olat, mq, gates, km, vm, w_ret_o[l].astype(BF16), wdsa_all[l],
                     w_mem_o[l].astype(BF16), w_out[l].astype(BF16), S)
        x2d = _mlp(x2d, row(norm2[l]), w_ff1[l].astype(BF16), w_ff2[l].astype(BF16), row(final_norm),
                   final=(l == depth - 1))
    return x2d.reshape(B, S, D)
```

```python
import functools
import math
import statistics

import numpy as np
import jax
import jax.numpy as jnp
from jax import lax
from jax.experimental import pallas as pl
from jax.experimental.pallas import tpu as pltpu

F32 = jnp.float32
BF16 = jnp.bfloat16

D_MODEL = 1024
CHUNK = 64
EPS = 1e-6
GN_EPS = 1e-5
RET_HEADS = 8
RET_DK = 64
RET_W = 512
ROPE_BASE = 10000.0
DSA_HEADS = 8
DSA_DH = 64
DSA_RQ = 256
DSA_RKV = 128
IDX_HEADS = 8
IDX_DIM = 32
TOPK_MAX = 256
MEM_LEN = 256
MEM_HEADS = 4
MEM_DH = 128
MEM_W = 512
REL_BUCKETS = 32
REL_MAX_DIST = 128
D_FF = 4096
IN_SIZES = (512, 512, 512, 512, DSA_RQ, DSA_RKV, IDX_DIM, IDX_HEADS, MEM_W, 3 * D_MODEL)

LANES = 128
IDX_COLS = 640
NEG = -1e30
LOG2E = math.log2(math.e)
F32_MAX = float(np.finfo(np.float32).max)
VMEM_LIMIT = 56 * 1024 * 1024

TM_IN = 256
T_RET = 256
TQ = 256
TK = 256
SUM_MAX = 1e30
SUM_MIN = 2.0 ** -80
SEARCH_UNCHECKED = 8
SEARCH_SWITCH = 16
SEARCH_FEW = 4.0
SEARCH_CAP = SEARCH_SWITCH + 34
TM_MERGE = 512
TM_MLP = 512


def _dot(a, b):
    return jnp.dot(a, b, preferred_element_type=F32)


def _dot_nt(a, b):
    return lax.dot_general(a, b, (((1,), (1,)), ((), ())), preferred_element_type=F32)


def _split(a):
    hi = a.astype(BF16)
    lo = (a - hi.astype(F32)).astype(BF16)
    return hi, lo


def _rms(x, g):
    return x * lax.rsqrt(jnp.mean(x * x, axis=-1, keepdims=True) + EPS) * g


def _params(sem):
    return pltpu.CompilerParams(dimension_semantics=sem, vmem_limit_bytes=VMEM_LIMIT)


def _fold_kernel(uq_ref, uk_ref, uv_ref, wo_ref, wql_ref, wdsa_ref):
    uq_h, uq_l = _split(uq_ref[0, 0])
    uk_h, uk_l = _split(uk_ref[0, 0])
    ql = _dot_nt(uq_h, uk_h) + _dot_nt(uq_h, uk_l) + _dot_nt(uq_l, uk_h)
    wql_ref[0] = (ql * (DSA_DH ** -0.5 * LOG2E)).astype(BF16)
    uv_h, uv_l = _split(uv_ref[0, 0])
    wo_h, wo_l = _split(wo_ref[0, 0])
    wdsa_ref[0] = (_dot(uv_h, wo_h) + _dot(uv_h, wo_l) + _dot(uv_l, wo_h)).astype(BF16)


def _fold_weights(w_uq, w_uk, w_uv, w_dsa_o):
    L = w_uq.shape[0]
    uq = jnp.transpose(w_uq, (0, 2, 1, 3))
    uk = jnp.transpose(w_uk, (0, 2, 1, 3))
    uv = jnp.transpose(w_uv, (0, 2, 1, 3))
    wo = w_dsa_o.reshape(L, DSA_HEADS, DSA_DH, D_MODEL)
    return pl.pallas_call(
        _fold_kernel,
        grid=(L, DSA_HEADS),
        in_specs=[
            pl.BlockSpec((1, 1, DSA_RQ, DSA_DH), lambda l, h: (l, h, 0, 0)),
            pl.BlockSpec((1, 1, DSA_RKV, DSA_DH), lambda l, h: (l, h, 0, 0)),
            pl.BlockSpec((1, 1, DSA_RKV, DSA_DH), lambda l, h: (l, h, 0, 0)),
            pl.BlockSpec((1, 1, DSA_DH, D_MODEL), lambda l, h: (l, h, 0, 0)),
        ],
        out_specs=[
            pl.BlockSpec((1, DSA_RQ, DSA_RKV), lambda l, h: (l, 0, h)),
            pl.BlockSpec((1, DSA_RKV, D_MODEL), lambda l, h: (l, h, 0)),
        ],
        out_shape=[
            jax.ShapeDtypeStruct((L, DSA_RQ, DSA_HEADS * DSA_RKV), BF16),
            jax.ShapeDtypeStruct((L, DSA_HEADS * DSA_RKV, D_MODEL), BF16),
        ],
        compiler_params=_params(("parallel", "parallel")),
        name="fold_weights",
    )(uq, uk, uv, wo)


def _in_kernel(x_ref, g_ref, cos_ref, sin_ref, wa_ref, widx_ref, qn_ref, kvn_ref, wql_ref, wiq_ref,
               rq_ref, rk_ref, rv_ref, rg_ref, mq_ref, gates_ref,
               qlat_ref, qidx_ref, wt_ref, ckvn_ref, ckt_ref, kidx_ref):
    h = _rms(x_ref[...], g_ref[...])
    hb = h.astype(BF16)
    c = cos_ref[...]
    s = sin_ref[...]
    half = RET_W // 2

    def rope_store(ref, z, scale):
        x1 = z[:, :half]
        x2 = z[:, half:]
        ref[:, :half] = ((x1 * c - x2 * s) * scale).astype(BF16)
        ref[:, half:] = ((x2 * c + x1 * s) * scale).astype(BF16)

    rope_store(rq_ref, _dot(hb, wa_ref[:, 0:512]), 1.0)
    rope_store(rk_ref, _dot(hb, wa_ref[:, 512:1024]), RET_DK ** -0.5)
    rv_ref[...] = _dot(hb, wa_ref[:, 1024:1536]).astype(BF16)
    rg_ref[...] = _dot(hb, wa_ref[:, 1536:2048]).astype(BF16)
    mq_ref[...] = _dot(hb, wa_ref[:, 2048:2560]).astype(BF16)
    for j in range(6):
        gates_ref[:, j * 512:(j + 1) * 512] = _dot(hb, wa_ref[:, 2560 + j * 512:3072 + j * 512]).astype(BF16)
    _dsa_prep_tile(_dot(hb, widx_ref[...]), qn_ref, kvn_ref, wql_ref, wiq_ref,
                   qlat_ref, qidx_ref, wt_ref, ckvn_ref, ckt_ref, kidx_ref)


def _in_proj(x2d, g, cos_t, sin_t, wa, widx, qn, kvn, wql, wiq, S):
    N = x2d.shape[0]
    nt = N // TM_IN
    n_pos = S // TM_IN
    row = lambda i: (i, 0)
    const = lambda i: (0, 0)
    pos = lambda i: (i % n_pos, 0)
    tile3 = lambda i: (i, 0, 0)
    stacked = pl.BlockSpec((1, DSA_HEADS, TQ, LANES), lambda i: (i, 0, 0, 0))
    bf = lambda w: jax.ShapeDtypeStruct((N, w), BF16)
    return pl.pallas_call(
        _in_kernel,
        grid=(N // TM_IN,),
        in_specs=[
            pl.BlockSpec((TM_IN, D_MODEL), row),
            pl.BlockSpec((1, D_MODEL), const),
            pl.BlockSpec((TM_IN, RET_W // 2), pos),
            pl.BlockSpec((TM_IN, RET_W // 2), pos),
            pl.BlockSpec(wa.shape, const),
            pl.BlockSpec(widx.shape, const),
            pl.BlockSpec((1, DSA_RQ), const), pl.BlockSpec((1, DSA_RKV), const),
            pl.BlockSpec(wql.shape, const), pl.BlockSpec(wiq.shape, const),
        ],
        out_specs=[
            pl.BlockSpec((TM_IN, 512), row), pl.BlockSpec((TM_IN, 512), row),
            pl.BlockSpec((TM_IN, 512), row), pl.BlockSpec((TM_IN, 512), row),
            pl.BlockSpec((TM_IN, 512), row), pl.BlockSpec((TM_IN, 3 * D_MODEL), row),
            stacked, stacked, pl.BlockSpec((IDX_HEADS, TQ), lambda i: (0, i)),
            pl.BlockSpec((1, TQ, LANES), tile3), pl.BlockSpec((1, LANES, TQ), tile3),
            pl.BlockSpec((1, TQ, LANES), tile3),
        ],
        out_shape=[bf(512), bf(512), bf(512), bf(512), bf(512), bf(3 * D_MODEL),
                   jax.ShapeDtypeStruct((nt, DSA_HEADS, TQ, LANES), BF16),
                   jax.ShapeDtypeStruct((nt, DSA_HEADS, TQ, LANES), BF16),
                   jax.ShapeDtypeStruct((IDX_HEADS, N), F32),
                   jax.ShapeDtypeStruct((nt, TQ, LANES), BF16),
                   jax.ShapeDtypeStruct((nt, LANES, TQ), BF16),
                   jax.ShapeDtypeStruct((nt, TQ, LANES), BF16)],
        compiler_params=_params(("parallel",)),
        name="in_proj",
    )(x2d, g, cos_t, sin_t, wa, widx, qn, kvn, wql, wiq)


def _ret_kernel(rq_ref, rk_ref, rv_ref, rg_ref, dmat_ref, qd_ref, kd_ref, hq_ref, hv_ref,
                cdec_ref, bd_ref, p_ref, o_ref, s_ref):
    @pl.when(pl.program_id(1) == 0)
    def _():
        s_ref[...] = jnp.zeros_like(s_ref)

    q = rq_ref[...]
    k = rk_ref[...]
    v = rv_ref[...]
    def head_lane_tiles(a, h):
        c = h // (LANES // (RET_DK // 2))
        return jnp.concatenate([a[:, c * LANES:(c + 1) * LANES], a[:, (2 + c) * LANES:(3 + c) * LANES]], axis=1)

    o_cols = []
    for pair in range(RET_HEADS // 2):
        cols = slice(pair * LANES, (pair + 1) * LANES)
        o_pair = jnp.zeros((q.shape[0], LANES), F32)
        for h in (2 * pair, 2 * pair + 1):
            sc = _dot_nt(head_lane_tiles(q * hq_ref[h], h), head_lane_tiles(k, h)) * dmat_ref[h]
            o_pair = o_pair + _dot(sc.astype(BF16), v[:, cols]) * hv_ref[h][:, cols]
        o_cols.append(o_pair)
    o = jnp.concatenate(o_cols, axis=1)
    state = s_ref[...]
    qf = q.astype(F32) * qd_ref[...]
    o = o + _dot(qf.astype(BF16), state.astype(BF16))
    kf = k.astype(F32) * kd_ref[...]
    kv = _dot(kf.T.astype(BF16), v)
    s_ref[...] = state * cdec_ref[...] + kv * bd_ref[...]

    p = p_ref[...]

    oh, ol = _split(o)
    d = o - (_dot(oh, p) + _dot(ol, p))
    y = d * lax.rsqrt(_dot((d * d).astype(BF16), p) + GN_EPS)
    g = rg_ref[...].astype(F32)
    o_ref[...] = (g * jax.nn.sigmoid(g) * y).astype(BF16)


def _ret_consts():
    T = T_RET
    hh = np.arange(RET_HEADS, dtype=np.float64)
    log_g = np.log1p(-np.exp2(-5.0 - hh))
    t = np.arange(T)
    ct = t // CHUNK
    diff = (t[:, None] - t[None, :]).astype(np.float64)
    same = ct[:, None] == ct[None, :]
    past = ct[None, :] < ct[:, None]
    expo = np.where(same, np.abs(diff), diff)
    dmat = np.where((same | past)[None], np.exp(log_g[:, None, None] * expo[None]), 0.0)
    lane = np.arange(RET_W)
    hk = (lane % (RET_W // 2)) // (RET_DK // 2)
    hv = lane // RET_DK
    qd = np.exp(log_g[hk][None, :] * (t[:, None] + 1.0))
    kd = np.exp(log_g[hk][None, :] * (T - 1.0 - t[:, None]))
    hq_mask = (hk[None, :] == np.arange(RET_HEADS)[:, None]).astype(np.float32)[:, None, :]
    hv_mask = (hv[None, :] == np.arange(RET_HEADS)[:, None]).astype(np.float32)[:, None, :]
    bd = (hk[:, None] == hv[None, :]).astype(np.float32)
    cdec = np.broadcast_to(np.exp(log_g[hk] * T)[:, None], (RET_W, RET_W))
    pmat = (hv[:, None] == hv[None, :]).astype(np.float32) / RET_DK
    f = lambda a: jnp.asarray(np.asarray(a, dtype=np.float32))
    return (f(dmat), f(qd), f(kd), jnp.asarray(hq_mask, dtype=BF16), f(hv_mask), f(cdec), f(bd),
            jnp.asarray(pmat, dtype=BF16))


def _retention(rq, rk, rv, rg, B, S):
    consts = _ret_consts()
    nb = S // T_RET
    tok = lambda b, j: (b * nb + j, 0)
    full = lambda a: pl.BlockSpec(a.shape, lambda b, j: (0,) * a.ndim)
    return pl.pallas_call(
        _ret_kernel,
        grid=(B, nb),
        in_specs=[pl.BlockSpec((T_RET, RET_W), tok)] * 4 + [full(a) for a in consts],
        out_specs=pl.BlockSpec((T_RET, RET_W), tok),
        out_shape=jax.ShapeDtypeStruct(rq.shape, BF16),
        scratch_shapes=[pltpu.VMEM((RET_W, RET_W), F32)],
        compiler_params=_params(("parallel", "arbitrary")),
        name="retention",
    )(rq, rk, rv, rg, *consts)


def _memkv_kernel(mem_ref, g_ref, w_ref, k_ref, v_ref):
    mn = _rms(mem_ref[0], g_ref[...]).astype(BF16)
    kv = _dot(mn, w_ref[...])
    k_ref[0] = kv[:, :MEM_W].astype(BF16)
    v_ref[0] = kv[:, MEM_W:].astype(BF16)


def _mem_kv(mem, g, w):
    B = mem.shape[0]
    return pl.pallas_call(
        _memkv_kernel,
        grid=(B,),
        in_specs=[pl.BlockSpec((1, MEM_LEN, D_MODEL), lambda b: (b, 0, 0)),
                  pl.BlockSpec((1, D_MODEL), lambda b: (0, 0)),
                  pl.BlockSpec(w.shape, lambda b: (0, 0))],
        out_specs=[pl.BlockSpec((1, MEM_LEN, MEM_W), lambda b: (b, 0, 0))] * 2,
        out_shape=[jax.ShapeDtypeStruct((B, MEM_LEN, MEM_W), BF16)] * 2,
        compiler_params=_params(("parallel",)),
        name="mem_kv",
    )(mem, g, w)


def _dsa_prep_tile(z, qn_ref, kvn_ref, wql_ref, wiq_ref, qlat_ref, qidx_ref, wt_ref, ckvn_ref, ckt_ref, kidx_ref):
    cqn = _rms(z[:, :DSA_RQ], qn_ref[...])
    cb = cqn.astype(BF16)
    ql = _dot(cb, wql_ref[...])
    y = _dot(cb, wiq_ref[...])
    yh = y.astype(BF16).astype(F32)
    lane = lax.broadcasted_iota(jnp.int32, y.shape, 1)
    qsel = jnp.where((lane % LANES) < 2 * IDX_DIM, yh, y - yh).astype(BF16)
    for h in range(DSA_HEADS):
        qlat_ref[0, h] = ql[:, h * LANES:(h + 1) * LANES].astype(BF16)
        qidx_ref[0, h] = qsel[:, h * LANES:(h + 1) * LANES]
    ckn = _rms(z[:, DSA_RQ:DSA_RQ + DSA_RKV], kvn_ref[...])
    ckvn_ref[0] = ckn.astype(BF16)
    ckt_ref[0] = ckn.T.astype(BF16)
    ik = z[:, 384:512]
    ikh = ik.astype(BF16).astype(F32)
    lane1 = lax.broadcasted_iota(jnp.int32, ik.shape, 1)
    kidx_ref[0] = jnp.where((lane1 // IDX_DIM) % 2 == 0, ikh, ik - ikh).astype(BF16)
    wfull = z[:, 512:640] * ((IDX_HEADS ** -0.5) * (IDX_DIM ** -0.5))
    wt_ref[...] = wfull.T[:IDX_HEADS, :]


def _key_to_f32(key):
    bits = jnp.where(key >= 0, key, key ^ jnp.int32(0x7FFFFFFF))
    return lax.bitcast_convert_type(bits, F32)


def _f32_to_key(x):
    bits = lax.bitcast_convert_type(x, jnp.int32)
    return jnp.where(bits >= 0, bits, bits ^ jnp.int32(0x7FFFFFFF))


def _sublane_all(a, op):
    for shift in (4, 2, 1):
        a = op(a, pltpu.roll(a, shift, 0))
    return a


def _dsa_kernel(qlat_ref, qidx_ref, wt_ref, ckvn_ref, ckt_ref, kidx_ref, bnear_ref, zq_ref, tri_ref, o_ref,
                sc_ref, m_ref, l_ref, acc_ref, *, k_sel):
    i = pl.program_id(1)
    wt = wt_ref[...]

    def score_tiles(js):
        keys = jnp.concatenate([kidx_ref[j] for j in js], axis=0)
        y_all = _dot_nt(keys, qidx_ref[0].reshape(IDX_HEADS * TQ, LANES))
        tot = jnp.zeros((len(js) * TK, TQ), F32)
        for h in range(IDX_HEADS):
            tot = tot + jnp.maximum(y_all[:, h * TQ:(h + 1) * TQ], 0.0) * wt[h:h + 1, :]
        return tot

    def colsum8(a):
        return jnp.sum(a.reshape(a.shape[0] // 8, 8, TQ), axis=0)

    def ones_where(cond_):
        return jnp.where(cond_, 1.0, 0.0)

    def stats(s_counted, s_summed):
        return (colsum8(s_summed), colsum8(s_summed * s_summed),
                colsum8(ones_where(s_counted >= 0.0)), colsum8(ones_where(s_counted > 0.0)))

    def add_stats(carry, s):
        return tuple(a + b for a, b in zip(carry, stats(s, s)))

    def fill(j0, count, carry):
        s = score_tiles([j0 + g for g in range(count)])
        for g in range(count):
            sc_ref[j0 + g] = s[g * TK:(g + 1) * TK]
        return add_stats(carry, s)

    zero8 = jnp.zeros((8, TQ), F32)
    carry = lax.fori_loop(0, i // 4, lambda jj, c: fill(4 * jj, 4, c), (zero8, zero8, zero8, zero8))
    carry = lax.cond(i % 4 >= 2, lambda c: fill((i // 4) * 4, 2, c), lambda c: c, carry)
    carry = lax.cond(i % 2 == 1, lambda c: fill(i - 1, 1, c), lambda c: c, carry)
    key_r = lax.broadcasted_iota(jnp.int32, (TK, TQ), 0)
    qry_c = lax.broadcasted_iota(jnp.int32, (TK, TQ), 1)
    admissible = key_r <= (qry_c // CHUNK) * CHUNK + (CHUNK - 1)
    s_diag = score_tiles([i])
    s_masked = jnp.where(admissible, s_diag, -jnp.inf)
    sc_ref[i] = s_masked
    carry = tuple(a + b for a, b in zip(carry, stats(s_masked, jnp.where(admissible, s_diag, 0.0))))
    s1, s2, c_ge0, c_gt0 = (_sublane_all(a, jnp.add)[0:1, :] for a in carry)

    def count_ge(thr, with_below=False):
        def tile(j):
            s = sc_ref[j]
            ge = s >= thr
            cnt = colsum8(ones_where(ge))
            if not with_below:
                return cnt, ninf8
            return cnt, jnp.max(jnp.where(ge, -jnp.inf, s).reshape(TK // 8, 8, TQ), axis=0)

        def pair(jj, c):
            ca, cb, ba, bb = c
            xa, ya = tile(2 * jj)
            xb, yb = tile(2 * jj + 1)
            return ca + xa, cb + xb, jnp.maximum(ba, ya), jnp.maximum(bb, yb)

        ninf8 = jnp.full((8, TQ), -jnp.inf, F32)
        ca, cb, ba, bb = lax.fori_loop(0, (i + 1) // 2, pair, (zero8, zero8, ninf8, ninf8))
        co, bo = lax.cond((i + 1) % 2 == 1, lambda: tile(i), lambda: (zero8, ninf8))
        count = _sublane_all(ca + cb + co, jnp.add)[0:1, :]
        below = _sublane_all(jnp.maximum(jnp.maximum(ba, bb), bo), jnp.maximum)[0:1, :]
        return count, below

    qc = lax.broadcasted_iota(jnp.int32, (1, TQ), 1)
    n_adm = (i * TQ + (qc // CHUNK + 1) * CHUNK).astype(F32)
    kf = jnp.float32(k_sel)
    key_min = jnp.int32(np.int32(np.array(-F32_MAX, np.float32).view(np.int32)) ^ np.int32(0x7FFFFFFF))
    key_max = jnp.int32(np.array(np.inf, np.float32).view(np.int32))
    inf = jnp.float32(np.inf)
    zq = zq_ref[0]
    mu = s1 / n_adm
    sd = jnp.sqrt(jnp.maximum(s2 / n_adm - mu * mu, 1e-30))
    dens = zq[1:2, :] / sd
    pos = c_gt0 >= kf
    tie0 = jnp.logical_and(jnp.logical_not(pos), c_ge0 >= kf)
    small = n_adm <= kf
    lo0 = jnp.where(pos | tie0, 0, key_min)
    hi0 = jnp.where(pos, key_max, jnp.where(tie0, 1, 0))
    lo0 = jnp.where(small, key_min, lo0)
    hi0 = jnp.where(small, key_min + 1, hi0)
    lf0 = jnp.where(pos | tie0, 0.0, -inf)
    hf0 = jnp.where(pos, inf, 0.0)
    cl0 = jnp.where(pos | tie0, c_ge0, n_adm)
    ch0 = jnp.where(pos, 0.0, jnp.where(tie0, c_gt0, c_ge0))
    t0 = mu + zq[0:1, :] * sd
    one = jnp.ones((1, TQ), F32)

    def mid_key(lo, hi):
        return (lo & hi) + ((lo ^ hi) >> 1)

    def unfinished(lo, hi, cl):
        return jnp.logical_not((cl == kf) | (mid_key(lo, hi) == lo))

    def cond(st):
        it, lo, hi, cl = st[0], st[1], st[2], st[5]
        return jnp.logical_and(it < SEARCH_CAP, jnp.sum(ones_where(unfinished(lo, hi, cl))) > 0.0)

    def step(st, with_below):
        it, lo, hi, lf, hf, cl, ch, t, last, wl, wh = st
        act = unfinished(lo, hi, cl)
        guided = (it < SEARCH_SWITCH).astype(jnp.int32)
        mid = mid_key(lo, hi)
        tk = mid + (_f32_to_key(t) - mid) * guided
        tk = jnp.minimum(jnp.maximum(tk, lo + 1), hi - 1)
        if with_below:
            tk = jnp.where(act & (kf - ch == 1.0) & (hi < key_max), hi, tk)
        tf = _key_to_f32(tk)
        c, below = count_ge(tf, with_below)
        ge = c >= kf
        up_lo = act & ge
        up_hi = act & jnp.logical_not(ge)
        lo = jnp.where(up_lo, tk, lo)
        lf = jnp.where(up_lo, tf, lf)
        cl = jnp.where(up_lo, c, cl)
        if with_below:
            kb = _f32_to_key(below)
            hi = jnp.where(up_hi, kb + 1, hi)
            hf = jnp.where(up_hi, _key_to_f32(kb + 1), hf)
            found = up_hi & (c == kf - 1.0)
            lo = jnp.where(found, kb, lo)
            cl = jnp.where(found, kf + 0.5, cl)
        else:
            hi = jnp.where(up_hi, tk, hi)
            hf = jnp.where(up_hi, tf, hf)
        ch = jnp.where(up_hi, c, ch)
        side = jnp.where(ge, 1.0, -1.0)
        same = side == last
        wh = jnp.where(ge, jnp.where(same, wh * 0.5, one), one)
        wl = jnp.where(ge, one, jnp.where(same, wl * 0.5, one))
        a = (cl - kf + 0.5) * wl
        b = (kf - 0.5 - ch) * wh
        frac = jnp.where(cl - ch <= SEARCH_FEW, 0.5, a / (a + b))
        t_bracket = lf + (hf - lf) * frac
        t_model = tf + 1.5 * (c - kf + jnp.where(ge, 0.5, -0.5)) / dens
        bracketed = (lf > -inf) & (hf < inf)
        t = jnp.where(bracketed, t_bracket, t_model)
        return it + 1, lo, hi, lf, hf, cl, ch, t, side, wl, wh

    st = (jnp.int32(0), lo0, hi0, lf0, hf0, cl0, ch0, t0, jnp.zeros((1, TQ), F32), one, one)
    st = lax.fori_loop(0, SEARCH_UNCHECKED, lambda _, s: step(s, False), st)
    st = lax.while_loop(cond, lambda s: step(step(s, True), True), st)
    lo, cl, ch = st[1], st[5], st[6]
    thr = _key_to_f32(lo)
    tied = (cl != kf) & jnp.logical_not(small)
    need = jnp.where(tied, kf - ch, jnp.float32(1e9))

    def mask_plain(j, carry_):
        sc_ref[j] = jnp.where(sc_ref[j] >= thr, 0.0, NEG)
        return carry_

    def mask_ranked(j, before):
        s = sc_ref[j]
        eq = ones_where(s == thr)
        rank = _dot(tri_ref[...], eq.astype(BF16)) + before
        keep = (s > thr) | ((s == thr) & (rank <= need))
        sc_ref[j] = jnp.where(keep, 0.0, NEG)
        return before + jnp.sum(eq, axis=0, keepdims=True)

    any_tied = jnp.sum(ones_where(tied)) > 0.0

    def mask_ranked_group(j0, count, before):
        for g in range(count):
            before = mask_ranked(j0 + g, before)
        return before

    @pl.when(any_tied)
    def _():
        n_all = i + 1
        before = lax.fori_loop(0, n_all // 4, lambda jj, b: mask_ranked_group(4 * jj, 4, b), jnp.zeros((1, TQ), F32))
        before = lax.cond(n_all % 4 >= 2, lambda b: mask_ranked_group((n_all // 4) * 4, 2, b), lambda b: b, before)

        @pl.when(n_all % 2 == 1)
        def _():
            mask_ranked(i, before)

    @pl.when(jnp.logical_not(any_tied))
    def _():
        lax.fori_loop(0, i + 1, mask_plain, 0)

    def attend(tiles, shifted):
        nk = len(tiles) * TK
        ck = jnp.concatenate([ckvn_ref[j] for j, _ in tiles], axis=0)
        ckt = jnp.concatenate([ckt_ref[j] for j, _ in tiles], axis=1)
        mask_bias = jnp.concatenate([sc_ref[j] for j, _ in tiles], axis=0)
        lg_all = _dot_nt(ck, qlat_ref[0].reshape(DSA_HEADS * TQ, LANES))
        for h in range(DSA_HEADS):
            bias = mask_bias
            if any(slot is not None for _, slot in tiles):
                bias = bias + jnp.concatenate(
                    [jnp.zeros((TK, TQ), F32) if slot is None else bnear_ref[h, slot] for _, slot in tiles], axis=0)
            lg3 = (lg_all[:, h * TQ:(h + 1) * TQ] + bias).reshape(nk // 8, 8, TQ)
            if shifted:
                m_old = m_ref[h]
                m_new = jnp.maximum(m_old, _sublane_all(jnp.max(lg3, axis=0), jnp.maximum))
                alpha = jnp.exp2(m_old - m_new)
                m_ref[h] = m_new
                p3 = jnp.exp2(lg3 - m_new[None])
                l_ref[h] = alpha * l_ref[h] + jnp.sum(p3, axis=0)
                pv = _dot(ckt, p3.reshape(nk, TQ).astype(BF16))
                acc3 = acc_ref[h].reshape(DSA_RKV // 8, 8, TQ) * alpha[None]
                acc_ref[h] = acc3.reshape(DSA_RKV, TQ) + pv
            else:
                p3 = jnp.exp2(lg3)
                l_ref[h] = l_ref[h] + jnp.sum(p3, axis=0)
                acc_ref[h] = acc_ref[h] + _dot(ckt, p3.reshape(nk, TQ).astype(BF16))

    n_far = jnp.maximum(i - 1, 0)

    def attend_all(shifted):
        m_ref[...] = jnp.full(m_ref.shape, NEG, F32)
        l_ref[...] = jnp.zeros(l_ref.shape, F32)
        acc_ref[...] = jnp.zeros(acc_ref.shape, F32)

        @pl.when(i >= 1)
        def _():
            attend([(i - 1, 0), (i, 1)], shifted)

        @pl.when(i == 0)
        def _():
            attend([(i, 1)], shifted)

        def far_quad(jj, carry):
            attend([(4 * jj + g, None) for g in range(4)], shifted)
            return carry

        lax.fori_loop(0, n_far // 4, far_quad, 0)

        @pl.when(n_far % 4 >= 2)
        def _():
            attend([((n_far // 4) * 4 + g, None) for g in range(2)], shifted)

        @pl.when(n_far % 2 == 1)
        def _():
            attend([(n_far - 1, None)], shifted)

    attend_all(False)

    bad = jnp.zeros((8, TQ), F32)
    for h in range(DSA_HEADS):
        l_tot = _sublane_all(l_ref[h], jnp.add)
        bad = bad + jnp.where((l_tot >= SUM_MIN) & (l_tot <= SUM_MAX), 0.0, 1.0)

    @pl.when(jnp.max(bad) > 0.0)
    def _():
        attend_all(True)


    for h in range(DSA_HEADS):
        l_tot = _sublane_all(l_ref[h], jnp.add)
        o = (acc_ref[h].reshape(DSA_RKV // 8, 8, TQ) / l_tot[None]).reshape(DSA_RKV, TQ)
        o_ref[:, h * LANES:(h + 1) * LANES] = o.T.astype(BF16)


def _t5_bucket(rel):
    nb = REL_BUCKETS // 2
    max_exact = nb // 2
    base = jnp.where(rel > 0, nb, 0)
    n = jnp.abs(rel)
    nf = jnp.maximum(n, 1).astype(jnp.float32)
    large = max_exact + (jnp.log(nf / max_exact) / math.log(REL_MAX_DIST / max_exact)
                         * (nb - max_exact)).astype(jnp.int32)
    large = jnp.minimum(large, nb - 1)
    return base + jnp.where(n < max_exact, n, large)


def _bias_tables(rel_bias):
    s = jnp.arange(TK, dtype=jnp.int32)[:, None]
    t = jnp.arange(TQ, dtype=jnp.int32)[None, :]
    rel = jnp.stack([s - TK - t, s - t])
    table = rel_bias.astype(F32)
    bucket = _t5_bucket(rel)
    far_bucket = _t5_bucket(jnp.full((), -REL_MAX_DIST, jnp.int32))
    near = jnp.zeros((DSA_HEADS,) + rel.shape, F32)
    far = jnp.zeros((DSA_HEADS,), F32)
    for b in range(REL_BUCKETS):
        near = near + jnp.where(bucket[None] == b, table[b][:, None, None, None], 0.0)
        far = far + jnp.where(far_bucket == b, table[b], 0.0)
    return (near - far[:, None, None, None]) * LOG2E


def _search_tables(nq, k_sel):
    nd = statistics.NormalDist()
    tab = np.zeros((nq, 2, TQ), np.float32)
    for i in range(nq):
        for c in range(TQ // CHUNK):
            n = i * TQ + (c + 1) * CHUNK
            z = nd.inv_cdf(1.0 - (k_sel - 0.5) / n) if n > k_sel else 0.0
            tab[i, 0, c * CHUNK:(c + 1) * CHUNK] = z
            tab[i, 1, c * CHUNK:(c + 1) * CHUNK] = n * nd.pdf(z)
    return jnp.asarray(tab)


def _dsa_attention(qlat, qidx, wt, ckvn, ckt, kidx, bnear, B, S):
    nq = S // TQ
    N = B * S
    k_sel = min(TOPK_MAX, S // 4)
    zq = _search_tables(nq, k_sel)
    tri = jnp.asarray(np.tril(np.ones((TK, TK), np.float32)), dtype=BF16)
    stacked = pl.BlockSpec((1, DSA_HEADS, TQ, LANES), lambda b, i: (b * nq + i, 0, 0, 0))
    keys = pl.BlockSpec((nq, TK, LANES), lambda b, i: (b, 0, 0))
    keys_t = pl.BlockSpec((nq, LANES, TK), lambda b, i: (b, 0, 0))
    return pl.pallas_call(
        functools.partial(_dsa_kernel, k_sel=k_sel),
        grid=(B, nq),
        in_specs=[stacked, stacked, pl.BlockSpec((IDX_HEADS, TQ), lambda b, i: (0, b * nq + i)),
                  keys, keys_t, keys,
                  pl.BlockSpec(bnear.shape, lambda b, i: (0, 0, 0, 0)),
                  pl.BlockSpec((1, 2, TQ), lambda b, i: (i, 0, 0)),
                  pl.BlockSpec((TK, TK), lambda b, i: (0, 0))],
        out_specs=pl.BlockSpec((TQ, DSA_HEADS * DSA_RKV), lambda b, i: (b * nq + i, 0)),
        out_shape=jax.ShapeDtypeStruct((N, DSA_HEADS * DSA_RKV), BF16),
        scratch_shapes=[pltpu.VMEM((nq, TK, TQ), F32),
                        pltpu.VMEM((DSA_HEADS, 8, TQ), F32),
                        pltpu.VMEM((DSA_HEADS, 8, TQ), F32),
                        pltpu.VMEM((DSA_HEADS, DSA_RKV, TQ), F32)],
        compiler_params=_params(("parallel", "arbitrary")),
        name="dsa_attention",
    )(qlat, qidx, wt, ckvn, ckt, kidx, bnear, zq, tri)


def _merge_kernel(x_ref, ret_ref, olat_ref, mq_ref, gates_ref, km_ref, vm_ref,
                  wret_ref, wdsa_ref, wmem_ref, wout_ref, o_ref):
    mq = mq_ref[...]
    km = km_ref[0]
    vm = vm_ref[0]
    pvs = []
    for h in range(MEM_HEADS):
        sl = slice(h * MEM_DH, (h + 1) * MEM_DH)
        lg = _dot_nt(mq[:, sl], km[:, sl]) * (MEM_DH ** -0.5)
        p = jnp.exp(lg - jnp.max(lg, axis=1, keepdims=True))
        p = p / jnp.sum(p, axis=1, keepdims=True)
        pvs.append(_dot(p.astype(BF16), vm[:, sl]).astype(BF16))
    mem_b = _dot(jnp.concatenate(pvs, axis=1), wmem_ref[...])
    ret_b = _dot(ret_ref[...], wret_ref[...])
    dsa_b = _dot(olat_ref[...], wdsa_ref[...])
    g = jax.nn.sigmoid(gates_ref[...].astype(F32))
    merged = (g[:, :D_MODEL] * ret_b + g[:, D_MODEL:2 * D_MODEL] * dsa_b + g[:, 2 * D_MODEL:] * mem_b)
    o_ref[...] = x_ref[...] + _dot(merged.astype(BF16), wout_ref[...])


def _merge(x2d, ret, olat, mq, gates, km, vm, wret, wdsa, wmem, wout, S):
    N = x2d.shape[0]
    per_b = S // TM_MERGE
    row = lambda i: (i, 0)
    const = lambda i: (0, 0)
    memb = lambda i: (i // per_b, 0, 0)
    return pl.pallas_call(
        _merge_kernel,
        grid=(N // TM_MERGE,),
        in_specs=[pl.BlockSpec((TM_MERGE, D_MODEL), row), pl.BlockSpec((TM_MERGE, RET_W), row),
                  pl.BlockSpec((TM_MERGE, DSA_HEADS * DSA_RKV), row), pl.BlockSpec((TM_MERGE, MEM_W), row),
                  pl.BlockSpec((TM_MERGE, 3 * D_MODEL), row),
                  pl.BlockSpec((1, MEM_LEN, MEM_W), memb), pl.BlockSpec((1, MEM_LEN, MEM_W), memb),
                  pl.BlockSpec(wret.shape, const), pl.BlockSpec(wdsa.shape, const),
                  pl.BlockSpec(wmem.shape, const), pl.BlockSpec(wout.shape, const)],
        out_specs=pl.BlockSpec((TM_MERGE, D_MODEL), row),
        out_shape=jax.ShapeDtypeStruct(x2d.shape, F32),
        compiler_params=_params(("parallel",)),
        name="merge_out",
    )(x2d, ret, olat, mq, gates, km, vm, wret, wdsa, wmem, wout)


def _mlp_kernel(x_ref, g_ref, w1_ref, w2_ref, gf_ref, o_ref, *, final):
    x = x_ref[...]
    h = _rms(x, g_ref[...]).astype(BF16)
    acc = jnp.zeros(x.shape, F32)
    for c in range(D_FF // D_MODEL):
        sl = slice(c * D_MODEL, (c + 1) * D_MODEL)
        a = jnp.maximum(_dot(h, w1_ref[:, sl]), 0.0)
        acc = acc + _dot((a * a).astype(BF16), w2_ref[sl, :])
    y = x + acc
    if final:
        y = _rms(y, gf_ref[...])
    o_ref[...] = y


def _mlp(x2d, g, w1, w2, gf, final):
    N = x2d.shape[0]
    row = lambda i: (i, 0)
    const = lambda i: (0, 0)
    return pl.pallas_call(
        functools.partial(_mlp_kernel, final=final),
        grid=(N // TM_MLP,),
        in_specs=[pl.BlockSpec((TM_MLP, D_MODEL), row), pl.BlockSpec((1, D_MODEL), const),
                  pl.BlockSpec(w1.shape, const), pl.BlockSpec(w2.shape, const),
                  pl.BlockSpec((1, D_MODEL), const)],
        out_specs=pl.BlockSpec((TM_MLP, D_MODEL), row),
        out_shape=jax.ShapeDtypeStruct(x2d.shape, F32),
        compiler_params=_params(("parallel",)),
        name="mlp",
    )(x2d, g, w1, w2, gf)


def _rope_tables(S):
    half = RET_DK // 2
    pos = jnp.arange(S, dtype=F32)
    freqs = ROPE_BASE ** (-jnp.arange(half, dtype=F32) / half)
    ang = pos[:, None] * freqs[None, :]
    return jnp.tile(jnp.cos(ang), (1, RET_HEADS)), jnp.tile(jnp.sin(ang), (1, RET_HEADS))


def _in_weights(wi, w_iq_l):
    half = RET_DK // 2
    def gather_halves(w):
        return jnp.transpose(w.reshape(D_MODEL, RET_HEADS, 2, half), (0, 2, 1, 3)).reshape(D_MODEL, RET_W)

    offs = np.concatenate([[0], np.cumsum(IN_SIZES)])
    seg = lambda k: wi[:, int(offs[k]):int(offs[k + 1])]
    wa = jnp.concatenate([gather_halves(seg(0)), gather_halves(seg(1)), seg(2), seg(3), seg(8), seg(9)],
                         axis=1).astype(BF16)
    pad = jnp.zeros((D_MODEL, LANES - IDX_HEADS), F32)
    widx = jnp.concatenate([seg(4), seg(5), jnp.tile(seg(6), (1, 4)), seg(7), pad], axis=1)
    wiq = jnp.tile(w_iq_l[:, :, None, :], (1, 1, 4, 1)).reshape(DSA_RQ, IDX_HEADS * LANES)
    return wa, widx.astype(BF16), wiq.astype(BF16)


def kernel(x, mem, norm1, w_in, q_norm, kv_norm, w_uq, w_iq, w_uk, w_uv, mem_norm, w_mem_kv,
           w_ret_o, w_dsa_o, w_mem_o, w_out, norm2, w_ff1, w_ff2, rel_bias, final_norm):
    B, S, D = x.shape
    depth = w_in.shape[0]
    assert D == D_MODEL and S % TM_MLP == 0 and S % TQ == 0 and TQ == TK and TM_IN == TQ
    cos_t, sin_t = _rope_tables(S)
    bnear = _bias_tables(rel_bias)
    wql_all, wdsa_all = _fold_weights(w_uq, w_uk, w_uv, w_dsa_o)
    x2d = x.reshape(B * S, D)
    row = lambda v: v.reshape(1, -1)
    for l in range(depth):
        wa, widx, wiq = _in_weights(w_in[l], w_iq[l])
        (rq, rk, rv, rg, mq, gates, qlat, qidx, wt, ckvn, ckt, kidx) = _in_proj(
            x2d, row(norm1[l]), cos_t, sin_t, wa, widx, row(q_norm[l]), row(kv_norm[l]), wql_all[l], wiq, S)
        ret = _retention(rq, rk, rv, rg, B, S)
        olat = _dsa_attention(qlat, qidx, wt, ckvn, ckt, kidx, bnear, B, S)
        km, vm = _mem_kv(mem, row(mem_norm[l]), w_mem_kv[l].astype(BF16))
        x2d = _merge(x2d, ret, olat, mq, gates, km, vm, w_ret_o[l].astype(BF16), wdsa_all[l],
                     w_mem_o[l].astype(BF16), w_out[l].astype(BF16), S)
        x2d = _mlp(x2d, row(norm2[l]), w_ff1[l].astype(BF16), w_ff2[l].astype(BF16), row(final_norm),
                   final=(l == depth - 1))
    return x2d.reshape(B, S, D)
```

```python
import functools
import math
import statistics

import numpy as np
import jax
import jax.numpy as jnp
from jax import lax
from jax.experimental import pallas as pl
from jax.experimental.pallas import tpu as pltpu

F32 = jnp.float32
BF16 = jnp.bfloat16

D_MODEL = 1024
CHUNK = 64
EPS = 1e-6
GN_EPS = 1e-5
RET_HEADS = 8
RET_DK = 64
RET_W = 512
ROPE_BASE = 10000.0
DSA_HEADS = 8
DSA_DH = 64
DSA_RQ = 256
DSA_RKV = 128
IDX_HEADS = 8
IDX_DIM = 32
TOPK_MAX = 256
MEM_LEN = 256
MEM_HEADS = 4
MEM_DH = 128
MEM_W = 512
REL_BUCKETS = 32
REL_MAX_DIST = 128
D_FF = 4096
IN_SIZES = (512, 512, 512, 512, DSA_RQ, DSA_RKV, IDX_DIM, IDX_HEADS, MEM_W, 3 * D_MODEL)

LANES = 128
IDX_IK = DSA_RQ + DSA_RKV
IDX_IW = IDX_IK + 4 * IDX_DIM
IDX_COLS = IDX_IW + LANES
NEG = -1e30
LOG2E = math.log2(math.e)
F32_MAX = float(np.finfo(np.float32).max)
VMEM_LIMIT = 56 * 1024 * 1024

TM_IN = 256
T_RET = 256
TQ = 256
TK = 256
SUM_MAX = 1e30
SUM_MIN = 2.0 ** -80
SEARCH_UNCHECKED = 8
SEARCH_SWITCH = 16
SEARCH_FEW = 4.0
SEARCH_CAP = SEARCH_SWITCH + 34
TM_MERGE = 512
TM_MLP = 512


def _dot(a, b):
    return jnp.dot(a, b, preferred_element_type=F32)


def _dot_nt(a, b):
    return lax.dot_general(a, b, (((1,), (1,)), ((), ())), preferred_element_type=F32)


def _split(a):
    hi = a.astype(BF16)
    lo = (a - hi.astype(F32)).astype(BF16)
    return hi, lo


def _rms(x, g):
    return x * lax.rsqrt(jnp.mean(x * x, axis=-1, keepdims=True) + EPS) * g


def _params(sem):
    return pltpu.CompilerParams(dimension_semantics=sem, vmem_limit_bytes=VMEM_LIMIT)


def _fold_kernel(uq_ref, uk_ref, uv_ref, wo_ref, wql_ref, wdsa_ref):
    uq_h, uq_l = _split(uq_ref[0, 0])
    uk_h, uk_l = _split(uk_ref[0, 0])
    ql = _dot_nt(uq_h, uk_h) + _dot_nt(uq_h, uk_l) + _dot_nt(uq_l, uk_h)
    wql_ref[0] = (ql * (DSA_DH ** -0.5 * LOG2E)).astype(BF16)
    uv_h, uv_l = _split(uv_ref[0, 0])
    wo_h, wo_l = _split(wo_ref[0, 0])
    wdsa_ref[0] = (_dot(uv_h, wo_h) + _dot(uv_h, wo_l) + _dot(uv_l, wo_h)).astype(BF16)


def _fold_weights(w_uq, w_uk, w_uv, w_dsa_o):
    L = w_uq.shape[0]
    uq = jnp.transpose(w_uq, (0, 2, 1, 3))
    uk = jnp.transpose(w_uk, (0, 2, 1, 3))
    uv = jnp.transpose(w_uv, (0, 2, 1, 3))
    wo = w_dsa_o.reshape(L, DSA_HEADS, DSA_DH, D_MODEL)
    return pl.pallas_call(
        _fold_kernel,
        grid=(L, DSA_HEADS),
        in_specs=[
            pl.BlockSpec((1, 1, DSA_RQ, DSA_DH), lambda l, h: (l, h, 0, 0)),
            pl.BlockSpec((1, 1, DSA_RKV, DSA_DH), lambda l, h: (l, h, 0, 0)),
            pl.BlockSpec((1, 1, DSA_RKV, DSA_DH), lambda l, h: (l, h, 0, 0)),
            pl.BlockSpec((1, 1, DSA_DH, D_MODEL), lambda l, h: (l, h, 0, 0)),
        ],
        out_specs=[
            pl.BlockSpec((1, DSA_RQ, DSA_RKV), lambda l, h: (l, 0, h)),
            pl.BlockSpec((1, DSA_RKV, D_MODEL), lambda l, h: (l, h, 0)),
        ],
        out_shape=[
            jax.ShapeDtypeStruct((L, DSA_RQ, DSA_HEADS * DSA_RKV), BF16),
            jax.ShapeDtypeStruct((L, DSA_HEADS * DSA_RKV, D_MODEL), BF16),
        ],
        compiler_params=_params(("parallel", "parallel")),
        name="fold_weights",
    )(uq, uk, uv, wo)


def _in_kernel(x_ref, g_ref, cos_ref, sin_ref, wa_ref, widx_ref, qn_ref, kvn_ref, wql_ref, wiq_ref,
               rq_ref, rk_ref, rv_ref, rg_ref, mq_ref, gates_ref,
               qlat_ref, qidx_ref, wt_ref, ckvn_ref, ckt_ref, kidx_ref):
    h = _rms(x_ref[...], g_ref[...])
    hb = h.astype(BF16)
    c = cos_ref[...]
    s = sin_ref[...]
    half = RET_W // 2

    def rope_store(ref, z, scale):
        x1 = z[:, :half]
        x2 = z[:, half:]
        ref[:, :half] = ((x1 * c - x2 * s) * scale).astype(BF16)
        ref[:, half:] = ((x2 * c + x1 * s) * scale).astype(BF16)

    rope_store(rq_ref, _dot(hb, wa_ref[:, 0:512]), 1.0)
    rope_store(rk_ref, _dot(hb, wa_ref[:, 512:1024]), RET_DK ** -0.5)
    rv_ref[...] = _dot(hb, wa_ref[:, 1024:1536]).astype(BF16)
    rg_ref[...] = _dot(hb, wa_ref[:, 1536:2048]).astype(BF16)
    mq_ref[...] = _dot(hb, wa_ref[:, 2048:2560]).astype(BF16)
    for j in range(6):
        gates_ref[:, j * 512:(j + 1) * 512] = _dot(hb, wa_ref[:, 2560 + j * 512:3072 + j * 512]).astype(BF16)
    _dsa_prep_tile(_dot(hb, widx_ref[...]), qn_ref, kvn_ref, wql_ref, wiq_ref,
                   qlat_ref, qidx_ref, wt_ref, ckvn_ref, ckt_ref, kidx_ref)


def _in_proj(x2d, g, cos_t, sin_t, wa, widx, qn, kvn, wql, wiq, S):
    N = x2d.shape[0]
    nt = N // TM_IN
    n_pos = S // TM_IN
    row = lambda i: (i, 0)
    const = lambda i: (0, 0)
    pos = lambda i: (i % n_pos, 0)
    tile3 = lambda i: (i, 0, 0)
    stacked = pl.BlockSpec((1, DSA_HEADS, TQ, LANES), lambda i: (i, 0, 0, 0))
    bf = lambda w: jax.ShapeDtypeStruct((N, w), BF16)
    return pl.pallas_call(
        _in_kernel,
        grid=(N // TM_IN,),
        in_specs=[
            pl.BlockSpec((TM_IN, D_MODEL), row),
            pl.BlockSpec((1, D_MODEL), const),
            pl.BlockSpec((TM_IN, RET_W // 2), pos),
            pl.BlockSpec((TM_IN, RET_W // 2), pos),
            pl.BlockSpec(wa.shape, const),
            pl.BlockSpec(widx.shape, const),
            pl.BlockSpec((1, DSA_RQ), const), pl.BlockSpec((1, DSA_RKV), const),
            pl.BlockSpec(wql.shape, const), pl.BlockSpec(wiq.shape, const),
        ],
        out_specs=[
            pl.BlockSpec((TM_IN, 512), row), pl.BlockSpec((TM_IN, 512), row),
            pl.BlockSpec((TM_IN, 512), row), pl.BlockSpec((TM_IN, 512), row),
            pl.BlockSpec((TM_IN, 512), row), pl.BlockSpec((TM_IN, 3 * D_MODEL), row),
            stacked, stacked, pl.BlockSpec((IDX_HEADS, TQ), lambda i: (0, i)),
            pl.BlockSpec((1, TQ, LANES), tile3), pl.BlockSpec((1, LANES, TQ), tile3),
            pl.BlockSpec((1, TQ, LANES), tile3),
        ],
        out_shape=[bf(512), bf(512), bf(512), bf(512), bf(512), bf(3 * D_MODEL),
                   jax.ShapeDtypeStruct((nt, DSA_HEADS, TQ, LANES), BF16),
                   jax.ShapeDtypeStruct((nt, DSA_HEADS, TQ, LANES), BF16),
                   jax.ShapeDtypeStruct((IDX_HEADS, N), F32),
                   jax.ShapeDtypeStruct((nt, TQ, LANES), BF16),
                   jax.ShapeDtypeStruct((nt, LANES, TQ), BF16),
                   jax.ShapeDtypeStruct((nt, TQ, LANES), BF16)],
        compiler_params=_params(("parallel",)),
        name="in_proj",
    )(x2d, g, cos_t, sin_t, wa, widx, qn, kvn, wql, wiq)


def _ret_kernel(rq_ref, rk_ref, rv_ref, rg_ref, dmat_ref, qd_ref, kd_ref, hq_ref, hv_ref,
                cdec_ref, bd_ref, p_ref, o_ref, s_ref):
    @pl.when(pl.program_id(1) == 0)
    def _():
        s_ref[...] = jnp.zeros_like(s_ref)

    q = rq_ref[...]
    k = rk_ref[...]
    v = rv_ref[...]
    def head_lane_tiles(a, h):
        c = h // (LANES // (RET_DK // 2))
        return jnp.concatenate([a[:, c * LANES:(c + 1) * LANES], a[:, (2 + c) * LANES:(3 + c) * LANES]], axis=1)

    o_cols = []
    for pair in range(RET_HEADS // 2):
        cols = slice(pair * LANES, (pair + 1) * LANES)
        o_pair = jnp.zeros((q.shape[0], LANES), F32)
        for h in (2 * pair, 2 * pair + 1):
            sc = _dot_nt(head_lane_tiles(q * hq_ref[h], h), head_lane_tiles(k, h)) * dmat_ref[h]
            o_pair = o_pair + _dot(sc.astype(BF16), v[:, cols]) * hv_ref[h][:, cols]
        o_cols.append(o_pair)
    o = jnp.concatenate(o_cols, axis=1)
    state = s_ref[...]
    qf = q.astype(F32) * qd_ref[...]
    o = o + _dot(qf.astype(BF16), state.astype(BF16))
    kf = k.astype(F32) * kd_ref[...]
    kv = _dot(kf.T.astype(BF16), v)
    s_ref[...] = state * cdec_ref[...] + kv * bd_ref[...]

    p = p_ref[...]

    oh, ol = _split(o)
    d = o - (_dot(oh, p) + _dot(ol, p))
    y = d * lax.rsqrt(_dot((d * d).astype(BF16), p) + GN_EPS)
    g = rg_ref[...].astype(F32)
    o_ref[...] = (g * jax.nn.sigmoid(g) * y).astype(BF16)


def _ret_consts():
    T = T_RET
    hh = np.arange(RET_HEADS, dtype=np.float64)
    log_g = np.log1p(-np.exp2(-5.0 - hh))
    t = np.arange(T)
    ct = t // CHUNK
    diff = (t[:, None] - t[None, :]).astype(np.float64)
    same = ct[:, None] == ct[None, :]
    past = ct[None, :] < ct[:, None]
    expo = np.where(same, np.abs(diff), diff)
    dmat = np.where((same | past)[None], np.exp(log_g[:, None, None] * expo[None]), 0.0)
    lane = np.arange(RET_W)
    hk = (lane % (RET_W // 2)) // (RET_DK // 2)
    hv = lane // RET_DK
    qd = np.exp(log_g[hk][None, :] * (t[:, None] + 1.0))
    kd = np.exp(log_g[hk][None, :] * (T - 1.0 - t[:, None]))
    hq_mask = (hk[None, :] == np.arange(RET_HEADS)[:, None]).astype(np.float32)[:, None, :]
    hv_mask = (hv[None, :] == np.arange(RET_HEADS)[:, None]).astype(np.float32)[:, None, :]
    bd = (hk[:, None] == hv[None, :]).astype(np.float32)
    cdec = np.broadcast_to(np.exp(log_g[hk] * T)[:, None], (RET_W, RET_W))
    pmat = (hv[:, None] == hv[None, :]).astype(np.float32) / RET_DK
    f = lambda a: jnp.asarray(np.asarray(a, dtype=np.float32))
    return (f(dmat), f(qd), f(kd), jnp.asarray(hq_mask, dtype=BF16), f(hv_mask), f(cdec), f(bd),
            jnp.asarray(pmat, dtype=BF16))


def _retention(rq, rk, rv, rg, B, S):
    consts = _ret_consts()
    nb = S // T_RET
    tok = lambda b, j: (b * nb + j, 0)
    full = lambda a: pl.BlockSpec(a.shape, lambda b, j: (0,) * a.ndim)
    return pl.pallas_call(
        _ret_kernel,
        grid=(B, nb),
        in_specs=[pl.BlockSpec((T_RET, RET_W), tok)] * 4 + [full(a) for a in consts],
        out_specs=pl.BlockSpec((T_RET, RET_W), tok),
        out_shape=jax.ShapeDtypeStruct(rq.shape, BF16),
        scratch_shapes=[pltpu.VMEM((RET_W, RET_W), F32)],
        compiler_params=_params(("parallel", "arbitrary")),
        name="retention",
    )(rq, rk, rv, rg, *consts)


def _memkv_kernel(mem_ref, g_ref, w_ref, k_ref, v_ref):
    mn = _rms(mem_ref[0], g_ref[...]).astype(BF16)
    kv = _dot(mn, w_ref[...])
    k_ref[0] = kv[:, :MEM_W].astype(BF16)
    v_ref[0] = kv[:, MEM_W:].astype(BF16)


def _mem_kv(mem, g, w):
    B = mem.shape[0]
    return pl.pallas_call(
        _memkv_kernel,
        grid=(B,),
        in_specs=[pl.BlockSpec((1, MEM_LEN, D_MODEL), lambda b: (b, 0, 0)),
                  pl.BlockSpec((1, D_MODEL), lambda b: (0, 0)),
                  pl.BlockSpec(w.shape, lambda b: (0, 0))],
        out_specs=[pl.BlockSpec((1, MEM_LEN, MEM_W), lambda b: (b, 0, 0))] * 2,
        out_shape=[jax.ShapeDtypeStruct((B, MEM_LEN, MEM_W), BF16)] * 2,
        compiler_params=_params(("parallel",)),
        name="mem_kv",
    )(mem, g, w)


def _dsa_prep_tile(z, qn_ref, kvn_ref, wql_ref, wiq_ref, qlat_ref, qidx_ref, wt_ref, ckvn_ref, ckt_ref, kidx_ref):
    cqn = _rms(z[:, :DSA_RQ], qn_ref[...])
    cb = cqn.astype(BF16)
    ql = _dot(cb, wql_ref[...])
    y = _dot(cb, wiq_ref[...])
    yh = y.astype(BF16).astype(F32)
    lane = lax.broadcasted_iota(jnp.int32, y.shape, 1)
    qsel = jnp.where((lane % LANES) < 2 * IDX_DIM, yh, y - yh).astype(BF16)
    for h in range(DSA_HEADS):
        qlat_ref[0, h] = ql[:, h * LANES:(h + 1) * LANES].astype(BF16)
        qidx_ref[0, h] = qsel[:, h * LANES:(h + 1) * LANES]
    ckn = _rms(z[:, DSA_RQ:DSA_RQ + DSA_RKV], kvn_ref[...])
    ckvn_ref[0] = ckn.astype(BF16)
    ckt_ref[0] = ckn.T.astype(BF16)
    ik = z[:, IDX_IK:IDX_IW]
    ikh = ik.astype(BF16).astype(F32)
    lane1 = lax.broadcasted_iota(jnp.int32, ik.shape, 1)
    kidx_ref[0] = jnp.where((lane1 // IDX_DIM) % 2 == 0, ikh, ik - ikh).astype(BF16)
    wfull = z[:, IDX_IW:IDX_COLS] * ((IDX_HEADS ** -0.5) * (IDX_DIM ** -0.5))
    wt_ref[...] = wfull.T[:IDX_HEADS, :]


def _key_to_f32(key):
    bits = jnp.where(key >= 0, key, key ^ jnp.int32(0x7FFFFFFF))
    return lax.bitcast_convert_type(bits, F32)


def _f32_to_key(x):
    bits = lax.bitcast_convert_type(x, jnp.int32)
    return jnp.where(bits >= 0, bits, bits ^ jnp.int32(0x7FFFFFFF))


def _sublane_all(a, op):
    for shift in (4, 2, 1):
        a = op(a, pltpu.roll(a, shift, 0))
    return a


def _dsa_kernel(qlat_ref, qidx_ref, wt_ref, ckvn_ref, ckt_ref, kidx_ref, bnear_ref, zq_ref, tri_ref, o_ref,
                sc_ref, m_ref, l_ref, acc_ref, *, k_sel):
    i = pl.program_id(1)
    wt = wt_ref[...]

    def score_tiles(js):
        keys = jnp.concatenate([kidx_ref[j] for j in js], axis=0)
        y_all = _dot_nt(keys, qidx_ref[0].reshape(IDX_HEADS * TQ, LANES))
        tot = jnp.zeros((len(js) * TK, TQ), F32)
        for h in range(IDX_HEADS):
            tot = tot + jnp.maximum(y_all[:, h * TQ:(h + 1) * TQ], 0.0) * wt[h:h + 1, :]
        return tot

    def colsum8(a):
        return jnp.sum(a.reshape(a.shape[0] // 8, 8, TQ), axis=0)

    def ones_where(cond_):
        return jnp.where(cond_, 1.0, 0.0)

    def stats(s_counted, s_summed):
        return (colsum8(s_summed), colsum8(s_summed * s_summed),
                colsum8(ones_where(s_counted >= 0.0)), colsum8(ones_where(s_counted > 0.0)))

    def add_stats(carry, s):
        return tuple(a + b for a, b in zip(carry, stats(s, s)))

    def fill(j0, count, carry):
        s = score_tiles([j0 + g for g in range(count)])
        for g in range(count):
            sc_ref[j0 + g] = s[g * TK:(g + 1) * TK]
        return add_stats(carry, s)

    zero8 = jnp.zeros((8, TQ), F32)
    carry = lax.fori_loop(0, i // 4, lambda jj, c: fill(4 * jj, 4, c), (zero8, zero8, zero8, zero8))
    carry = lax.cond(i % 4 >= 2, lambda c: fill((i // 4) * 4, 2, c), lambda c: c, carry)
    carry = lax.cond(i % 2 == 1, lambda c: fill(i - 1, 1, c), lambda c: c, carry)
    key_r = lax.broadcasted_iota(jnp.int32, (TK, TQ), 0)
    qry_c = lax.broadcasted_iota(jnp.int32, (TK, TQ), 1)
    admissible = key_r <= (qry_c // CHUNK) * CHUNK + (CHUNK - 1)
    s_diag = score_tiles([i])
    s_masked = jnp.where(admissible, s_diag, -jnp.inf)
    sc_ref[i] = s_masked
    carry = tuple(a + b for a, b in zip(carry, stats(s_masked, jnp.where(admissible, s_diag, 0.0))))
    s1, s2, c_ge0, c_gt0 = (_sublane_all(a, jnp.add)[0:1, :] for a in carry)

    def count_ge(thr, with_below=False):
        def tile(j):
            s = sc_ref[j]
            ge = s >= thr
            cnt = colsum8(ones_where(ge))
            if not with_below:
                return cnt, ninf8
            return cnt, jnp.max(jnp.where(ge, -jnp.inf, s).reshape(TK // 8, 8, TQ), axis=0)

        def pair(jj, c):
            ca, cb, ba, bb = c
            xa, ya = tile(2 * jj)
            xb, yb = tile(2 * jj + 1)
            return ca + xa, cb + xb, jnp.maximum(ba, ya), jnp.maximum(bb, yb)

        ninf8 = jnp.full((8, TQ), -jnp.inf, F32)
        ca, cb, ba, bb = lax.fori_loop(0, (i + 1) // 2, pair, (zero8, zero8, ninf8, ninf8))
        co, bo = lax.cond((i + 1) % 2 == 1, lambda: tile(i), lambda: (zero8, ninf8))
        count = _sublane_all(ca + cb + co, jnp.add)[0:1, :]
        below = _sublane_all(jnp.maximum(jnp.maximum(ba, bb), bo), jnp.maximum)[0:1, :]
        return count, below

    qc = lax.broadcasted_iota(jnp.int32, (1, TQ), 1)
    n_adm = (i * TQ + (qc // CHUNK + 1) * CHUNK).astype(F32)
    kf = jnp.float32(k_sel)
    key_min = jnp.int32(np.int32(np.array(-F32_MAX, np.float32).view(np.int32)) ^ np.int32(0x7FFFFFFF))
    key_max = jnp.int32(np.array(np.inf, np.float32).view(np.int32))
    inf = jnp.float32(np.inf)
    zq = zq_ref[0]
    mu = s1 / n_adm
    sd = jnp.sqrt(jnp.maximum(s2 / n_adm - mu * mu, 1e-30))
    dens = zq[1:2, :] / sd
    pos = c_gt0 >= kf
    tie0 = jnp.logical_and(jnp.logical_not(pos), c_ge0 >= kf)
    small = n_adm <= kf
    lo0 = jnp.where(pos | tie0, 0, key_min)
    hi0 = jnp.where(pos, key_max, jnp.where(tie0, 1, 0))
    lo0 = jnp.where(small, key_min, lo0)
    hi0 = jnp.where(small, key_min + 1, hi0)
    lf0 = jnp.where(pos | tie0, 0.0, -inf)
    hf0 = jnp.where(pos, inf, 0.0)
    cl0 = jnp.where(pos | tie0, c_ge0, n_adm)
    ch0 = jnp.where(pos, 0.0, jnp.where(tie0, c_gt0, c_ge0))
    t0 = mu + zq[0:1, :] * sd
    one = jnp.ones((1, TQ), F32)

    def mid_key(lo, hi):
        return (lo & hi) + ((lo ^ hi) >> 1)

    def unfinished(lo, hi, cl):
        return jnp.logical_not((cl == kf) | (mid_key(lo, hi) == lo))

    def cond(st):
        it, lo, hi, cl = st[0], st[1], st[2], st[5]
        return jnp.logical_and(it < SEARCH_CAP, jnp.sum(ones_where(unfinished(lo, hi, cl))) > 0.0)

    def step(st, with_below):
        it, lo, hi, lf, hf, cl, ch, t, last, wl, wh = st
        act = unfinished(lo, hi, cl)
        guided = (it < SEARCH_SWITCH).astype(jnp.int32)
        mid = mid_key(lo, hi)
        tk = mid + (_f32_to_key(t) - mid) * guided
        tk = jnp.minimum(jnp.maximum(tk, lo + 1), hi - 1)
        if with_below:
            tk = jnp.where(act & (kf - ch == 1.0) & (hi < key_max), hi, tk)
        tf = _key_to_f32(tk)
        c, below = count_ge(tf, with_below)
        ge = c >= kf
        up_lo = act & ge
        up_hi = act & jnp.logical_not(ge)
        lo = jnp.where(up_lo, tk, lo)
        lf = jnp.where(up_lo, tf, lf)
        cl = jnp.where(up_lo, c, cl)
        if with_below:
            kb = _f32_to_key(below)
            hi = jnp.where(up_hi, kb + 1, hi)
            hf = jnp.where(up_hi, _key_to_f32(kb + 1), hf)
            found = up_hi & (c == kf - 1.0)
            lo = jnp.where(found, kb, lo)
            cl = jnp.where(found, kf + 0.5, cl)
        else:
            hi = jnp.where(up_hi, tk, hi)
            hf = jnp.where(up_hi, tf, hf)
        ch = jnp.where(up_hi, c, ch)
        side = jnp.where(ge, 1.0, -1.0)
        same = side == last
        wh = jnp.where(ge, jnp.where(same, wh * 0.5, one), one)
        wl = jnp.where(ge, one, jnp.where(same, wl * 0.5, one))
        a = (cl - kf + 0.5) * wl
        b = (kf - 0.5 - ch) * wh
        frac = jnp.where(cl - ch <= SEARCH_FEW, 0.5, a / (a + b))
        t_bracket = lf + (hf - lf) * frac
        t_model = tf + 1.5 * (c - kf + jnp.where(ge, 0.5, -0.5)) / dens
        bracketed = (lf > -inf) & (hf < inf)
        t = jnp.where(bracketed, t_bracket, t_model)
        return it + 1, lo, hi, lf, hf, cl, ch, t, side, wl, wh

    st = (jnp.int32(0), lo0, hi0, lf0, hf0, cl0, ch0, t0, jnp.zeros((1, TQ), F32), one, one)
    st = lax.fori_loop(0, SEARCH_UNCHECKED, lambda _, s: step(s, False), st)
    st = lax.while_loop(cond, lambda s: step(s, True), st)
    lo, cl, ch = st[1], st[5], st[6]
    thr = _key_to_f32(lo)
    tied = (cl != kf) & jnp.logical_not(small)
    need = jnp.where(tied, kf - ch, jnp.float32(1e9))

    def mask_plain(j, carry_):
        sc_ref[j] = jnp.where(sc_ref[j] >= thr, 0.0, NEG)
        return carry_

    def mask_ranked(j, before):
        s = sc_ref[j]
        eq = ones_where(s == thr)
        rank = _dot(tri_ref[...], eq.astype(BF16)) + before
        keep = (s > thr) | ((s == thr) & (rank <= need))
        sc_ref[j] = jnp.where(keep, 0.0, NEG)
        return before + jnp.sum(eq, axis=0, keepdims=True)

    any_tied = jnp.sum(ones_where(tied)) > 0.0

    def mask_ranked_group(j0, count, before):
        for g in range(count):
            before = mask_ranked(j0 + g, before)
        return before

    @pl.when(any_tied)
    def _():
        n_all = i + 1
        before = lax.fori_loop(0, n_all // 4, lambda jj, b: mask_ranked_group(4 * jj, 4, b), jnp.zeros((1, TQ), F32))
        before = lax.cond(n_all % 4 >= 2, lambda b: mask_ranked_group((n_all // 4) * 4, 2, b), lambda b: b, before)

        @pl.when(n_all % 2 == 1)
        def _():
            mask_ranked(i, before)

    @pl.when(jnp.logical_not(any_tied))
    def _():
        lax.fori_loop(0, i + 1, mask_plain, 0)

    def attend(tiles, shifted):
        nk = len(tiles) * TK
        ck = jnp.concatenate([ckvn_ref[j] for j, _ in tiles], axis=0)
        ckt = jnp.concatenate([ckt_ref[j] for j, _ in tiles], axis=1)
        mask_bias = jnp.concatenate([sc_ref[j] for j, _ in tiles], axis=0)
        lg_all = _dot_nt(ck, qlat_ref[0].reshape(DSA_HEADS * TQ, LANES))
        for h in range(DSA_HEADS):
            bias = mask_bias
            if any(slot is not None for _, slot in tiles):
                bias = bias + jnp.concatenate(
                    [jnp.zeros((TK, TQ), F32) if slot is None else bnear_ref[h, slot] for _, slot in tiles], axis=0)
            lg3 = (lg_all[:, h * TQ:(h + 1) * TQ] + bias).reshape(nk // 8, 8, TQ)
            if shifted:
                m_old = m_ref[h]
                m_new = jnp.maximum(m_old, _sublane_all(jnp.max(lg3, axis=0), jnp.maximum))
                alpha = jnp.exp2(m_old - m_new)
                m_ref[h] = m_new
                p3 = jnp.exp2(lg3 - m_new[None])
                l_ref[h] = alpha * l_ref[h] + jnp.sum(p3, axis=0)
                pv = _dot(ckt, p3.reshape(nk, TQ).astype(BF16))
                acc3 = acc_ref[h].reshape(DSA_RKV // 8, 8, TQ) * alpha[None]
                acc_ref[h] = acc3.reshape(DSA_RKV, TQ) + pv
            else:
                p3 = jnp.exp2(lg3)
                l_ref[h] = l_ref[h] + jnp.sum(p3, axis=0)
                acc_ref[h] = acc_ref[h] + _dot(ckt, p3.reshape(nk, TQ).astype(BF16))

    n_far = jnp.maximum(i - 1, 0)

    def attend_all(shifted):
        m_ref[...] = jnp.full(m_ref.shape, NEG, F32)
        l_ref[...] = jnp.zeros(l_ref.shape, F32)
        acc_ref[...] = jnp.zeros(acc_ref.shape, F32)

        @pl.when(i >= 1)
        def _():
            attend([(i - 1, 0), (i, 1)], shifted)

        @pl.when(i == 0)
        def _():
            attend([(i, 1)], shifted)

        def far_quad(jj, carry):
            attend([(4 * jj + g, None) for g in range(4)], shifted)
            return carry

        lax.fori_loop(0, n_far // 4, far_quad, 0)

        @pl.when(n_far % 4 >= 2)
        def _():
            attend([((n_far // 4) * 4 + g, None) for g in range(2)], shifted)

        @pl.when(n_far % 2 == 1)
        def _():
            attend([(n_far - 1, None)], shifted)

    attend_all(False)

    bad = jnp.zeros((8, TQ), F32)
    for h in range(DSA_HEADS):
        l_tot = _sublane_all(l_ref[h], jnp.add)
        bad = bad + jnp.where((l_tot >= SUM_MIN) & (l_tot <= SUM_MAX), 0.0, 1.0)

    @pl.when(jnp.max(bad) > 0.0)
    def _():
        attend_all(True)


    for h in range(DSA_HEADS):
        l_tot = _sublane_all(l_ref[h], jnp.add)
        o = (acc_ref[h].reshape(DSA_RKV // 8, 8, TQ) / l_tot[None]).reshape(DSA_RKV, TQ)
        o_ref[:, h * LANES:(h + 1) * LANES] = o.T.astype(BF16)


def _t5_bucket(rel):
    nb = REL_BUCKETS // 2
    max_exact = nb // 2
    base = jnp.where(rel > 0, nb, 0)
    n = jnp.abs(rel)
    nf = jnp.maximum(n, 1).astype(jnp.float32)
    large = max_exact + (jnp.log(nf / max_exact) / math.log(REL_MAX_DIST / max_exact)
                         * (nb - max_exact)).astype(jnp.int32)
    large = jnp.minimum(large, nb - 1)
    return base + jnp.where(n < max_exact, n, large)


def _bias_tables(rel_bias):
    s = jnp.arange(TK, dtype=jnp.int32)[:, None]
    t = jnp.arange(TQ, dtype=jnp.int32)[None, :]
    rel = jnp.stack([s - TK - t, s - t])
    table = rel_bias.astype(F32)
    bucket = _t5_bucket(rel)
    far_bucket = _t5_bucket(jnp.full((), -REL_MAX_DIST, jnp.int32))
    near = jnp.zeros((DSA_HEADS,) + rel.shape, F32)
    far = jnp.zeros((DSA_HEADS,), F32)
    for b in range(REL_BUCKETS):
        near = near + jnp.where(bucket[None] == b, table[b][:, None, None, None], 0.0)
        far = far + jnp.where(far_bucket == b, table[b], 0.0)
    return (near - far[:, None, None, None]) * LOG2E


def _search_tables(nq, k_sel):
    nd = statistics.NormalDist()
    tab = np.zeros((nq, 2, TQ), np.float32)
    for i in range(nq):
        for c in range(TQ // CHUNK):
            n = i * TQ + (c + 1) * CHUNK
            z = nd.inv_cdf(1.0 - (k_sel - 0.5) / n) if n > k_sel else 0.0
            tab[i, 0, c * CHUNK:(c + 1) * CHUNK] = z
            tab[i, 1, c * CHUNK:(c + 1) * CHUNK] = n * nd.pdf(z)
    return jnp.asarray(tab)


def _dsa_attention(qlat, qidx, wt, ckvn, ckt, kidx, bnear, B, S):
    nq = S // TQ
    N = B * S
    k_sel = min(TOPK_MAX, S // 4)
    zq = _search_tables(nq, k_sel)
    tri = jnp.asarray(np.tril(np.ones((TK, TK), np.float32)), dtype=BF16)
    stacked = pl.BlockSpec((1, DSA_HEADS, TQ, LANES), lambda b, i: (b * nq + i, 0, 0, 0))
    keys = pl.BlockSpec((nq, TK, LANES), lambda b, i: (b, 0, 0))
    keys_t = pl.BlockSpec((nq, LANES, TK), lambda b, i: (b, 0, 0))
    return pl.pallas_call(
        functools.partial(_dsa_kernel, k_sel=k_sel),
        grid=(B, nq),
        in_specs=[stacked, stacked, pl.BlockSpec((IDX_HEADS, TQ), lambda b, i: (0, b * nq + i)),
                  keys, keys_t, keys,
                  pl.BlockSpec(bnear.shape, lambda b, i: (0, 0, 0, 0)),
                  pl.BlockSpec((1, 2, TQ), lambda b, i: (i, 0, 0)),
                  pl.BlockSpec((TK, TK), lambda b, i: (0, 0))],
        out_specs=pl.BlockSpec((TQ, DSA_HEADS * DSA_RKV), lambda b, i: (b * nq + i, 0)),
        out_shape=jax.ShapeDtypeStruct((N, DSA_HEADS * DSA_RKV), BF16),
        scratch_shapes=[pltpu.VMEM((nq, TK, TQ), F32),
                        pltpu.VMEM((DSA_HEADS, 8, TQ), F32),
                        pltpu.VMEM((DSA_HEADS, 8, TQ), F32),
                        pltpu.VMEM((DSA_HEADS, DSA_RKV, TQ), F32)],
        compiler_params=_params(("parallel", "arbitrary")),
        name="dsa_attention",
    )(qlat, qidx, wt, ckvn, ckt, kidx, bnear, zq, tri)


def _merge_kernel(x_ref, ret_ref, olat_ref, mq_ref, gates_ref, km_ref, vm_ref,
                  wret_ref, wdsa_ref, wmem_ref, wout_ref, o_ref):
    mq = mq_ref[...]
    km = km_ref[0]
    vm = vm_ref[0]
    pvs = []
    for h in range(MEM_HEADS):
        sl = slice(h * MEM_DH, (h + 1) * MEM_DH)
        lg = _dot_nt(mq[:, sl], km[:, sl]) * (MEM_DH ** -0.5)
        p = jnp.exp(lg - jnp.max(lg, axis=1, keepdims=True))
        p = p / jnp.sum(p, axis=1, keepdims=True)
        pvs.append(_dot(p.astype(BF16), vm[:, sl]).astype(BF16))
    mem_b = _dot(jnp.concatenate(pvs, axis=1), wmem_ref[...])
    ret_b = _dot(ret_ref[...], wret_ref[...])
    dsa_b = _dot(olat_ref[...], wdsa_ref[...])
    g = jax.nn.sigmoid(gates_ref[...].astype(F32))
    merged = (g[:, :D_MODEL] * ret_b + g[:, D_MODEL:2 * D_MODEL] * dsa_b + g[:, 2 * D_MODEL:] * mem_b)
    o_ref[...] = x_ref[...] + _dot(merged.astype(BF16), wout_ref[...])


def _merge(x2d, ret, olat, mq, gates, km, vm, wret, wdsa, wmem, wout, S):
    N = x2d.shape[0]
    per_b = S // TM_MERGE
    row = lambda i: (i, 0)
    const = lambda i: (0, 0)
    memb = lambda i: (i // per_b, 0, 0)
    return pl.pallas_call(
        _merge_kernel,
        grid=(N // TM_MERGE,),
        in_specs=[pl.BlockSpec((TM_MERGE, D_MODEL), row), pl.BlockSpec((TM_MERGE, RET_W), row),
                  pl.BlockSpec((TM_MERGE, DSA_HEADS * DSA_RKV), row), pl.BlockSpec((TM_MERGE, MEM_W), row),
                  pl.BlockSpec((TM_MERGE, 3 * D_MODEL), row),
                  pl.BlockSpec((1, MEM_LEN, MEM_W), memb), pl.BlockSpec((1, MEM_LEN, MEM_W), memb),
                  pl.BlockSpec(wret.shape, const), pl.BlockSpec(wdsa.shape, const),
                  pl.BlockSpec(wmem.shape, const), pl.BlockSpec(wout.shape, const)],
        out_specs=pl.BlockSpec((TM_MERGE, D_MODEL), row),
        out_shape=jax.ShapeDtypeStruct(x2d.shape, F32),
        compiler_params=_params(("parallel",)),
        name="merge_out",
    )(x2d, ret, olat, mq, gates, km, vm, wret, wdsa, wmem, wout)


def _mlp_kernel(x_ref, g_ref, w1_ref, w2_ref, gf_ref, o_ref, *, final):
    x = x_ref[...]
    h = _rms(x, g_ref[...]).astype(BF16)
    acc = jnp.zeros(x.shape, F32)
    for c in range(D_FF // D_MODEL):
        sl = slice(c * D_MODEL, (c + 1) * D_MODEL)
        a = jnp.maximum(_dot(h, w1_ref[:, sl]), 0.0)
        acc = acc + _dot((a * a).astype(BF16), w2_ref[sl, :])
    y = x + acc
    if final:
        y = _rms(y, gf_ref[...])
    o_ref[...] = y


def _mlp(x2d, g, w1, w2, gf, final):
    N = x2d.shape[0]
    row = lambda i: (i, 0)
    const = lambda i: (0, 0)
    return pl.pallas_call(
        functools.partial(_mlp_kernel, final=final),
        grid=(N // TM_MLP,),
        in_specs=[pl.BlockSpec((TM_MLP, D_MODEL), row), pl.BlockSpec((1, D_MODEL), const),
                  pl.BlockSpec(w1.shape, const), pl.BlockSpec(w2.shape, const),
                  pl.BlockSpec((1, D_MODEL), const)],
        out_specs=pl.BlockSpec((TM_MLP, D_MODEL), row),
        out_shape=jax.ShapeDtypeStruct(x2d.shape, F32),
        compiler_params=_params(("parallel",)),
        name="mlp",
    )(x2d, g, w1, w2, gf)


def _rope_tables(S):
    half = RET_DK // 2
    pos = jnp.arange(S, dtype=F32)
    freqs = ROPE_BASE ** (-jnp.arange(half, dtype=F32) / half)
    ang = pos[:, None] * freqs[None, :]
    return jnp.tile(jnp.cos(ang), (1, RET_HEADS)), jnp.tile(jnp.sin(ang), (1, RET_HEADS))


def _in_weights(wi, w_iq_l):
    half = RET_DK // 2
    def gather_halves(w):
        return jnp.transpose(w.reshape(D_MODEL, RET_HEADS, 2, half), (0, 2, 1, 3)).reshape(D_MODEL, RET_W)

    offs = np.concatenate([[0], np.cumsum(IN_SIZES)])
    seg = lambda k: wi[:, int(offs[k]):int(offs[k + 1])]
    wa = jnp.concatenate([gather_halves(seg(0)), gather_halves(seg(1)), seg(2), seg(3), seg(8), seg(9)],
                         axis=1).astype(BF16)
    pad = jnp.zeros((D_MODEL, LANES - IDX_HEADS), F32)
    widx = jnp.concatenate([seg(4), seg(5), jnp.tile(seg(6), (1, 4)), seg(7), pad], axis=1)
    wiq = jnp.tile(w_iq_l[:, :, None, :], (1, 1, 4, 1)).reshape(DSA_RQ, IDX_HEADS * LANES)
    return wa, widx.astype(BF16), wiq.astype(BF16)


def kernel(x, mem, norm1, w_in, q_norm, kv_norm, w_uq, w_iq, w_uk, w_uv, mem_norm, w_mem_kv,
           w_ret_o, w_dsa_o, w_mem_o, w_out, norm2, w_ff1, w_ff2, rel_bias, final_norm):
    B, S, D = x.shape
    depth = w_in.shape[0]
    assert D == D_MODEL and S % TM_MLP == 0 and S % TQ == 0 and TQ == TK and TM_IN == TQ
    cos_t, sin_t = _rope_tables(S)
    bnear = _bias_tables(rel_bias)
    wql_all, wdsa_all = _fold_weights(w_uq, w_uk, w_uv, w_dsa_o)
    x2d = x.reshape(B * S, D)
    row = lambda v: v.reshape(1, -1)
    for l in range(depth):
        wa, widx, wiq = _in_weights(w_in[l], w_iq[l])
        (rq, rk, rv, rg, mq, gates, qlat, qidx, wt, ckvn, ckt, kidx) = _in_proj(
            x2d, row(norm1[l]), cos_t, sin_t, wa, widx, row(q_norm[l]), row(kv_norm[l]), wql_all[l], wiq, S)
        ret = _retention(rq, rk, rv, rg, B, S)
        olat = _dsa_attention(qlat, qidx, wt, ckvn, ckt, kidx, bnear, B, S)
        km, vm = _mem_kv(mem, row(mem_norm[l]), w_mem_kv[l].astype(BF16))
        x2d = _merge(x2d, ret, olat, mq, gates, km, vm, w_ret_o[l].astype(BF16), wdsa_all[l],
                     w_mem_o[l].astype(BF16), w_out[l].astype(BF16), S)
        x2d = _mlp(x2d, row(norm2[l]), w_ff1[l].astype(BF16), w_ff2[l].astype(BF16), row(final_norm),
                   final=(l == depth - 1))
    return x2d.reshape(B, S, D)
```

```python
import functools
import math
import statistics

import numpy as np
import jax
import jax.numpy as jnp
from jax import lax
from jax.experimental import pallas as pl
from jax.experimental.pallas import tpu as pltpu

F32 = jnp.float32
BF16 = jnp.bfloat16

D_MODEL = 1024
CHUNK = 64
EPS = 1e-6
GN_EPS = 1e-5
RET_HEADS = 8
RET_DK = 64
RET_W = 512
ROPE_BASE = 10000.0
DSA_HEADS = 8
DSA_DH = 64
DSA_RQ = 256
DSA_RKV = 128
IDX_HEADS = 8
IDX_DIM = 32
TOPK_MAX = 256
MEM_LEN = 256
MEM_HEADS = 4
MEM_DH = 128
MEM_W = 512
REL_BUCKETS = 32
REL_MAX_DIST = 128
D_FF = 4096
IN_SIZES = (512, 512, 512, 512, DSA_RQ, DSA_RKV, IDX_DIM, IDX_HEADS, MEM_W, 3 * D_MODEL)

LANES = 128
IDX_IK = DSA_RQ + DSA_RKV
IDX_IW = IDX_IK + 4 * IDX_DIM
IDX_COLS = IDX_IW + LANES
NEG = -1e30
LOG2E = math.log2(math.e)
F32_MAX = float(np.finfo(np.float32).max)
VMEM_LIMIT = 56 * 1024 * 1024

TM_IN = 256
T_RET = 256
TQ = 256
TK = 256
SUM_MAX = 1e30
SUM_MIN = 2.0 ** -80
SEARCH_UNCHECKED = 8
SEARCH_SWITCH = 16
SEARCH_FEW = 4.0
SEARCH_CAP = SEARCH_SWITCH + 34
TM_MERGE = 512
TM_MLP = 512


def _dot(a, b):
    return jnp.dot(a, b, preferred_element_type=F32)


def _dot_nt(a, b):
    return lax.dot_general(a, b, (((1,), (1,)), ((), ())), preferred_element_type=F32)


def _split(a):
    hi = a.astype(BF16)
    lo = (a - hi.astype(F32)).astype(BF16)
    return hi, lo


def _rms(x, g):
    return x * lax.rsqrt(jnp.mean(x * x, axis=-1, keepdims=True) + EPS) * g


def _params(sem):
    return pltpu.CompilerParams(dimension_semantics=sem, vmem_limit_bytes=VMEM_LIMIT)


def _fold_kernel(uq_ref, uk_ref, uv_ref, wo_ref, wql_ref, wdsa_ref):
    uq_h, uq_l = _split(uq_ref[0, 0])
    uk_h, uk_l = _split(uk_ref[0, 0])
    ql = _dot_nt(uq_h, uk_h) + _dot_nt(uq_h, uk_l) + _dot_nt(uq_l, uk_h)
    wql_ref[0] = (ql * (DSA_DH ** -0.5 * LOG2E)).astype(BF16)
    uv_h, uv_l = _split(uv_ref[0, 0])
    wo_h, wo_l = _split(wo_ref[0, 0])
    wdsa_ref[0] = (_dot(uv_h, wo_h) + _dot(uv_h, wo_l) + _dot(uv_l, wo_h)).astype(BF16)


def _fold_weights(w_uq, w_uk, w_uv, w_dsa_o):
    L = w_uq.shape[0]
    uq = jnp.transpose(w_uq, (0, 2, 1, 3))
    uk = jnp.transpose(w_uk, (0, 2, 1, 3))
    uv = jnp.transpose(w_uv, (0, 2, 1, 3))
    wo = w_dsa_o.reshape(L, DSA_HEADS, DSA_DH, D_MODEL)
    return pl.pallas_call(
        _fold_kernel,
        grid=(L, DSA_HEADS),
        in_specs=[
            pl.BlockSpec((1, 1, DSA_RQ, DSA_DH), lambda l, h: (l, h, 0, 0)),
            pl.BlockSpec((1, 1, DSA_RKV, DSA_DH), lambda l, h: (l, h, 0, 0)),
            pl.BlockSpec((1, 1, DSA_RKV, DSA_DH), lambda l, h: (l, h, 0, 0)),
            pl.BlockSpec((1, 1, DSA_DH, D_MODEL), lambda l, h: (l, h, 0, 0)),
        ],
        out_specs=[
            pl.BlockSpec((1, DSA_RQ, DSA_RKV), lambda l, h: (l, 0, h)),
            pl.BlockSpec((1, DSA_RKV, D_MODEL), lambda l, h: (l, h, 0)),
        ],
        out_shape=[
            jax.ShapeDtypeStruct((L, DSA_RQ, DSA_HEADS * DSA_RKV), BF16),
            jax.ShapeDtypeStruct((L, DSA_HEADS * DSA_RKV, D_MODEL), BF16),
        ],
        compiler_params=_params(("parallel", "parallel")),
        name="fold_weights",
    )(uq, uk, uv, wo)


def _in_kernel(x_ref, g_ref, cos_ref, sin_ref, wa_ref, widx_ref, qn_ref, kvn_ref, wql_ref, wiq_ref,
               rq_ref, rk_ref, rv_ref, rg_ref, mq_ref, gates_ref,
               qlat_ref, qidx_ref, wt_ref, ckvn_ref, ckt_ref, kidx_ref):
    h = _rms(x_ref[...], g_ref[...])
    hb = h.astype(BF16)
    c = cos_ref[...]
    s = sin_ref[...]
    half = RET_W // 2

    def rope_store(ref, z, scale):
        x1 = z[:, :half]
        x2 = z[:, half:]
        ref[:, :half] = ((x1 * c - x2 * s) * scale).astype(BF16)
        ref[:, half:] = ((x2 * c + x1 * s) * scale).astype(BF16)

    rope_store(rq_ref, _dot(hb, wa_ref[:, 0:512]), 1.0)
    rope_store(rk_ref, _dot(hb, wa_ref[:, 512:1024]), RET_DK ** -0.5)
    rv_ref[...] = _dot(hb, wa_ref[:, 1024:1536]).astype(BF16)
    rg_ref[...] = _dot(hb, wa_ref[:, 1536:2048]).astype(BF16)
    mq_ref[...] = _dot(hb, wa_ref[:, 2048:2560]).astype(BF16)
    for j in range(6):
        gates_ref[:, j * 512:(j + 1) * 512] = _dot(hb, wa_ref[:, 2560 + j * 512:3072 + j * 512]).astype(BF16)
    _dsa_prep_tile(_dot(hb, widx_ref[...]), qn_ref, kvn_ref, wql_ref, wiq_ref,
                   qlat_ref, qidx_ref, wt_ref, ckvn_ref, ckt_ref, kidx_ref)


def _in_proj(x2d, g, cos_t, sin_t, wa, widx, qn, kvn, wql, wiq, S):
    N = x2d.shape[0]
    nt = N // TM_IN
    n_pos = S // TM_IN
    row = lambda i: (i, 0)
    const = lambda i: (0, 0)
    pos = lambda i: (i % n_pos, 0)
    tile3 = lambda i: (i, 0, 0)
    stacked = pl.BlockSpec((1, DSA_HEADS, TQ, LANES), lambda i: (i, 0, 0, 0))
    bf = lambda w: jax.ShapeDtypeStruct((N, w), BF16)
    return pl.pallas_call(
        _in_kernel,
        grid=(N // TM_IN,),
        in_specs=[
            pl.BlockSpec((TM_IN, D_MODEL), row),
            pl.BlockSpec((1, D_MODEL), const),
            pl.BlockSpec((TM_IN, RET_W // 2), pos),
            pl.BlockSpec((TM_IN, RET_W // 2), pos),
            pl.BlockSpec(wa.shape, const),
            pl.BlockSpec(widx.shape, const),
            pl.BlockSpec((1, DSA_RQ), const), pl.BlockSpec((1, DSA_RKV), const),
            pl.BlockSpec(wql.shape, const), pl.BlockSpec(wiq.shape, const),
        ],
        out_specs=[
            pl.BlockSpec((TM_IN, 512), row), pl.BlockSpec((TM_IN, 512), row),
            pl.BlockSpec((TM_IN, 512), row), pl.BlockSpec((TM_IN, 512), row),
            pl.BlockSpec((TM_IN, 512), row), pl.BlockSpec((TM_IN, 3 * D_MODEL), row),
            stacked, stacked, pl.BlockSpec((IDX_HEADS, TQ), lambda i: (0, i)),
            pl.BlockSpec((1, TQ, LANES), tile3), pl.BlockSpec((1, LANES, TQ), tile3),
            pl.BlockSpec((1, TQ, LANES), tile3),
        ],
        out_shape=[bf(512), bf(512), bf(512), bf(512), bf(512), bf(3 * D_MODEL),
                   jax.ShapeDtypeStruct((nt, DSA_HEADS, TQ, LANES), BF16),
                   jax.ShapeDtypeStruct((nt, DSA_HEADS, TQ, LANES), BF16),
                   jax.ShapeDtypeStruct((IDX_HEADS, N), F32),
                   jax.ShapeDtypeStruct((nt, TQ, LANES), BF16),
                   jax.ShapeDtypeStruct((nt, LANES, TQ), BF16),
                   jax.ShapeDtypeStruct((nt, TQ, LANES), BF16)],
        compiler_params=_params(("parallel",)),
        name="in_proj",
    )(x2d, g, cos_t, sin_t, wa, widx, qn, kvn, wql, wiq)


def _ret_kernel(rq_ref, rk_ref, rv_ref, rg_ref, dmat_ref, qd_ref, kd_ref, hq_ref, hv_ref,
                cdec_ref, bd_ref, p_ref, o_ref, s_ref):
    @pl.when(pl.program_id(1) == 0)
    def _():
        s_ref[...] = jnp.zeros_like(s_ref)

    q = rq_ref[...]
    k = rk_ref[...]
    v = rv_ref[...]
    def head_lane_tiles(a, h):
        c = h // (LANES // (RET_DK // 2))
        return jnp.concatenate([a[:, c * LANES:(c + 1) * LANES], a[:, (2 + c) * LANES:(3 + c) * LANES]], axis=1)

    o_cols = []
    for pair in range(RET_HEADS // 2):
        cols = slice(pair * LANES, (pair + 1) * LANES)
        o_pair = jnp.zeros((q.shape[0], LANES), F32)
        for h in (2 * pair, 2 * pair + 1):
            sc = _dot_nt(head_lane_tiles(q * hq_ref[h], h), head_lane_tiles(k, h)) * dmat_ref[h]
            o_pair = o_pair + _dot(sc.astype(BF16), v[:, cols]) * hv_ref[h][:, cols]
        o_cols.append(o_pair)
    o = jnp.concatenate(o_cols, axis=1)
    state = s_ref[...]
    qf = q.astype(F32) * qd_ref[...]
    o = o + _dot(qf.astype(BF16), state.astype(BF16))
    kf = k.astype(F32) * kd_ref[...]
    kv = _dot(kf.T.astype(BF16), v)
    s_ref[...] = state * cdec_ref[...] + kv * bd_ref[...]

    p = p_ref[...]

    oh, ol = _split(o)
    d = o - (_dot(oh, p) + _dot(ol, p))
    y = d * lax.rsqrt(_dot((d * d).astype(BF16), p) + GN_EPS)
    g = rg_ref[...].astype(F32)
    o_ref[...] = (g * jax.nn.sigmoid(g) * y).astype(BF16)


def _ret_consts():
    T = T_RET
    hh = np.arange(RET_HEADS, dtype=np.float64)
    log_g = np.log1p(-np.exp2(-5.0 - hh))
    t = np.arange(T)
    ct = t // CHUNK
    diff = (t[:, None] - t[None, :]).astype(np.float64)
    same = ct[:, None] == ct[None, :]
    past = ct[None, :] < ct[:, None]
    expo = np.where(same, np.abs(diff), diff)
    dmat = np.where((same | past)[None], np.exp(log_g[:, None, None] * expo[None]), 0.0)
    lane = np.arange(RET_W)
    hk = (lane % (RET_W // 2)) // (RET_DK // 2)
    hv = lane // RET_DK
    qd = np.exp(log_g[hk][None, :] * (t[:, None] + 1.0))
    kd = np.exp(log_g[hk][None, :] * (T - 1.0 - t[:, None]))
    hq_mask = (hk[None, :] == np.arange(RET_HEADS)[:, None]).astype(np.float32)[:, None, :]
    hv_mask = (hv[None, :] == np.arange(RET_HEADS)[:, None]).astype(np.float32)[:, None, :]
    bd = (hk[:, None] == hv[None, :]).astype(np.float32)
    cdec = np.broadcast_to(np.exp(log_g[hk] * T)[:, None], (RET_W, RET_W))
    pmat = (hv[:, None] == hv[None, :]).astype(np.float32) / RET_DK
    f = lambda a: jnp.asarray(np.asarray(a, dtype=np.float32))
    return (f(dmat), f(qd), f(kd), jnp.asarray(hq_mask, dtype=BF16), f(hv_mask), f(cdec), f(bd),
            jnp.asarray(pmat, dtype=BF16))


def _retention(rq, rk, rv, rg, B, S):
    consts = _ret_consts()
    nb = S // T_RET
    tok = lambda b, j: (b * nb + j, 0)
    full = lambda a: pl.BlockSpec(a.shape, lambda b, j: (0,) * a.ndim)
    return pl.pallas_call(
        _ret_kernel,
        grid=(B, nb),
        in_specs=[pl.BlockSpec((T_RET, RET_W), tok)] * 4 + [full(a) for a in consts],
        out_specs=pl.BlockSpec((T_RET, RET_W), tok),
        out_shape=jax.ShapeDtypeStruct(rq.shape, BF16),
        scratch_shapes=[pltpu.VMEM((RET_W, RET_W), F32)],
        compiler_params=_params(("parallel", "arbitrary")),
        name="retention",
    )(rq, rk, rv, rg, *consts)


def _memkv_kernel(mem_ref, g_ref, w_ref, k_ref, v_ref):
    mn = _rms(mem_ref[0], g_ref[...]).astype(BF16)
    kv = _dot(mn, w_ref[...])
    k_ref[0] = kv[:, :MEM_W].astype(BF16)
    v_ref[0] = kv[:, MEM_W:].astype(BF16)


def _mem_kv(mem, g, w):
    B = mem.shape[0]
    return pl.pallas_call(
        _memkv_kernel,
        grid=(B,),
        in_specs=[pl.BlockSpec((1, MEM_LEN, D_MODEL), lambda b: (b, 0, 0)),
                  pl.BlockSpec((1, D_MODEL), lambda b: (0, 0)),
                  pl.BlockSpec(w.shape, lambda b: (0, 0))],
        out_specs=[pl.BlockSpec((1, MEM_LEN, MEM_W), lambda b: (b, 0, 0))] * 2,
        out_shape=[jax.ShapeDtypeStruct((B, MEM_LEN, MEM_W), BF16)] * 2,
        compiler_params=_params(("parallel",)),
        name="mem_kv",
    )(mem, g, w)


def _dsa_prep_tile(z, qn_ref, kvn_ref, wql_ref, wiq_ref, qlat_ref, qidx_ref, wt_ref, ckvn_ref, ckt_ref, kidx_ref):
    cqn = _rms(z[:, :DSA_RQ], qn_ref[...])
    cb = cqn.astype(BF16)
    ql = _dot(cb, wql_ref[...])
    y = _dot(cb, wiq_ref[...])
    yh = y.astype(BF16).astype(F32)
    lane = lax.broadcasted_iota(jnp.int32, y.shape, 1)
    qsel = jnp.where((lane % LANES) < 2 * IDX_DIM, yh, y - yh).astype(BF16)
    for h in range(DSA_HEADS):
        qlat_ref[0, h] = ql[:, h * LANES:(h + 1) * LANES].astype(BF16)
        qidx_ref[0, h] = qsel[:, h * LANES:(h + 1) * LANES]
    ckn = _rms(z[:, DSA_RQ:DSA_RQ + DSA_RKV], kvn_ref[...])
    ckvn_ref[0] = ckn.astype(BF16)
    ckt_ref[0] = ckn.T.astype(BF16)
    ik = z[:, IDX_IK:IDX_IW]
    ikh = ik.astype(BF16).astype(F32)
    lane1 = lax.broadcasted_iota(jnp.int32, ik.shape, 1)
    kidx_ref[0] = jnp.where((lane1 // IDX_DIM) % 2 == 0, ikh, ik - ikh).astype(BF16)
    wfull = z[:, IDX_IW:IDX_COLS] * ((IDX_HEADS ** -0.5) * (IDX_DIM ** -0.5))
    wt_ref[...] = wfull.T[:IDX_HEADS, :]


def _key_to_f32(key):
    bits = jnp.where(key >= 0, key, key ^ jnp.int32(0x7FFFFFFF))
    return lax.bitcast_convert_type(bits, F32)


def _f32_to_key(x):
    bits = lax.bitcast_convert_type(x, jnp.int32)
    return jnp.where(bits >= 0, bits, bits ^ jnp.int32(0x7FFFFFFF))


def _sublane_all(a, op):
    for shift in (4, 2, 1):
        a = op(a, pltpu.roll(a, shift, 0))
    return a


def _dsa_kernel(qlat_ref, qidx_ref, wt_ref, ckvn_ref, ckt_ref, kidx_ref, bnear_ref, zq_ref, tri_ref, o_ref,
                sc_ref, m_ref, l_ref, acc_ref, *, k_sel):
    i = pl.program_id(1)
    wt = wt_ref[...]

    def score_tiles(js):
        keys = jnp.concatenate([kidx_ref[j] for j in js], axis=0)
        y_all = _dot_nt(keys, qidx_ref[0].reshape(IDX_HEADS * TQ, LANES))
        tot = jnp.zeros((len(js) * TK, TQ), F32)
        for h in range(IDX_HEADS):
            tot = tot + jnp.maximum(y_all[:, h * TQ:(h + 1) * TQ], 0.0) * wt[h:h + 1, :]
        return tot

    def colsum8(a):
        return jnp.sum(a.reshape(a.shape[0] // 8, 8, TQ), axis=0)

    def ones_where(cond_):
        return jnp.where(cond_, 1.0, 0.0)

    def stats(s_counted, s_summed):
        return (colsum8(s_summed), colsum8(s_summed * s_summed),
                colsum8(ones_where(s_counted >= 0.0)), colsum8(ones_where(s_counted > 0.0)))

    def add_stats(carry, s):
        return tuple(a + b for a, b in zip(carry, stats(s, s)))

    def fill(j0, count, carry):
        s = score_tiles([j0 + g for g in range(count)])
        for g in range(count):
            sc_ref[j0 + g] = s[g * TK:(g + 1) * TK]
        return add_stats(carry, s)

    zero8 = jnp.zeros((8, TQ), F32)
    carry = lax.fori_loop(0, i // 4, lambda jj, c: fill(4 * jj, 4, c), (zero8, zero8, zero8, zero8))
    carry = lax.cond(i % 4 >= 2, lambda c: fill((i // 4) * 4, 2, c), lambda c: c, carry)
    carry = lax.cond(i % 2 == 1, lambda c: fill(i - 1, 1, c), lambda c: c, carry)
    key_r = lax.broadcasted_iota(jnp.int32, (TK, TQ), 0)
    qry_c = lax.broadcasted_iota(jnp.int32, (TK, TQ), 1)
    admissible = key_r <= (qry_c // CHUNK) * CHUNK + (CHUNK - 1)
    s_diag = score_tiles([i])
    s_masked = jnp.where(admissible, s_diag, -jnp.inf)
    sc_ref[i] = s_masked
    carry = tuple(a + b for a, b in zip(carry, stats(s_masked, jnp.where(admissible, s_diag, 0.0))))
    s1, s2, c_ge0, c_gt0 = (_sublane_all(a, jnp.add)[0:1, :] for a in carry)

    def count_ge(thr, with_below=False):
        def tile(j):
            s = sc_ref[j]
            ge = s >= thr
            cnt = colsum8(ones_where(ge))
            if not with_below:
                return cnt, ninf8
            return cnt, jnp.max(jnp.where(ge, -jnp.inf, s).reshape(TK // 8, 8, TQ), axis=0)

        def pair(jj, c):
            ca, cb, ba, bb = c
            xa, ya = tile(2 * jj)
            xb, yb = tile(2 * jj + 1)
            return ca + xa, cb + xb, jnp.maximum(ba, ya), jnp.maximum(bb, yb)

        ninf8 = jnp.full((8, TQ), -jnp.inf, F32)
        ca, cb, ba, bb = lax.fori_loop(0, (i + 1) // 2, pair, (zero8, zero8, ninf8, ninf8))
        co, bo = lax.cond((i + 1) % 2 == 1, lambda: tile(i), lambda: (zero8, ninf8))
        count = _sublane_all(ca + cb + co, jnp.add)[0:1, :]
        below = _sublane_all(jnp.maximum(jnp.maximum(ba, bb), bo), jnp.maximum)[0:1, :]
        return count, below

    qc = lax.broadcasted_iota(jnp.int32, (1, TQ), 1)
    n_adm = (i * TQ + (qc // CHUNK + 1) * CHUNK).astype(F32)
    kf = jnp.float32(k_sel)
    key_min = jnp.int32(np.int32(np.array(-F32_MAX, np.float32).view(np.int32)) ^ np.int32(0x7FFFFFFF))
    key_max = jnp.int32(np.array(np.inf, np.float32).view(np.int32))
    inf = jnp.float32(np.inf)
    zq = zq_ref[0]
    mu = s1 / n_adm
    sd = jnp.sqrt(jnp.maximum(s2 / n_adm - mu * mu, 1e-30))
    dens = zq[1:2, :] / sd
    pos = c_gt0 >= kf
    tie0 = jnp.logical_and(jnp.logical_not(pos), c_ge0 >= kf)
    small = n_adm <= kf
    lo0 = jnp.where(pos | tie0, 0, key_min)
    hi0 = jnp.where(pos, key_max, jnp.where(tie0, 1, 0))
    lo0 = jnp.where(small, key_min, lo0)
    hi0 = jnp.where(small, key_min + 1, hi0)
    lf0 = jnp.where(pos | tie0, 0.0, -inf)
    hf0 = jnp.where(pos, inf, 0.0)
    cl0 = jnp.where(pos | tie0, c_ge0, n_adm)
    ch0 = jnp.where(pos, 0.0, jnp.where(tie0, c_gt0, c_ge0))
    t0 = mu + zq[0:1, :] * sd
    one = jnp.ones((1, TQ), F32)

    def mid_key(lo, hi):
        return (lo & hi) + ((lo ^ hi) >> 1)

    def unfinished(lo, hi, cl):
        return jnp.logical_not((cl == kf) | (mid_key(lo, hi) == lo))

    def cond(st):
        it, lo, hi, cl = st[0], st[1], st[2], st[5]
        return jnp.logical_and(it < SEARCH_CAP, jnp.sum(ones_where(unfinished(lo, hi, cl))) > 0.0)

    def step(st, with_below):
        it, lo, hi, lf, hf, cl, ch, t, last, wl, wh = st
        act = unfinished(lo, hi, cl)
        guided = (it < SEARCH_SWITCH).astype(jnp.int32)
        mid = mid_key(lo, hi)
        tk = mid + (_f32_to_key(t) - mid) * guided
        tk = jnp.minimum(jnp.maximum(tk, lo + 1), hi - 1)
        if with_below:
            tk = jnp.where(act & (kf - ch == 1.0) & (hi < key_max), hi, tk)
        tf = _key_to_f32(tk)
        c, below = count_ge(tf, with_below)
        ge = c >= kf
        up_lo = act & ge
        up_hi = act & jnp.logical_not(ge)
        lo = jnp.where(up_lo, tk, lo)
        lf = jnp.where(up_lo, tf, lf)
        cl = jnp.where(up_lo, c, cl)
        if with_below:
            kb = _f32_to_key(below)
            hi = jnp.where(up_hi, kb + 1, hi)
            hf = jnp.where(up_hi, _key_to_f32(kb + 1), hf)
            found = up_hi & (c == kf - 1.0)
            lo = jnp.where(found, kb, lo)
            cl = jnp.where(found, kf + 0.5, cl)
        else:
            hi = jnp.where(up_hi, tk, hi)
            hf = jnp.where(up_hi, tf, hf)
        ch = jnp.where(up_hi, c, ch)
        side = jnp.where(ge, 1.0, -1.0)
        same = side == last
        wh = jnp.where(ge, jnp.where(same, wh * 0.5, one), one)
        wl = jnp.where(ge, one, jnp.where(same, wl * 0.5, one))
        a = (cl - kf + 0.5) * wl
        b = (kf - 0.5 - ch) * wh
        frac = jnp.where(cl - ch <= SEARCH_FEW, 0.5, a / (a + b))
        t_bracket = lf + (hf - lf) * frac
        t_model = tf + 1.5 * (c - kf + jnp.where(ge, 0.5, -0.5)) / dens
        bracketed = (lf > -inf) & (hf < inf)
        t = jnp.where(bracketed, t_bracket, t_model)
        return it + 1, lo, hi, lf, hf, cl, ch, t, side, wl, wh

    st = (jnp.int32(0), lo0, hi0, lf0, hf0, cl0, ch0, t0, jnp.zeros((1, TQ), F32), one, one)
    st = lax.fori_loop(0, SEARCH_UNCHECKED, lambda _, s: step(s, False), st)
    st = step(st, True)
    st = lax.while_loop(cond, lambda s: step(s, True), st)
    lo, cl, ch = st[1], st[5], st[6]
    thr = _key_to_f32(lo)
    tied = (cl != kf) & jnp.logical_not(small)
    need = jnp.where(tied, kf - ch, jnp.float32(1e9))

    def mask_plain(j, carry_):
        sc_ref[j] = jnp.where(sc_ref[j] >= thr, 0.0, NEG)
        return carry_

    def mask_ranked(j, before):
        s = sc_ref[j]
        eq = ones_where(s == thr)
        rank = _dot(tri_ref[...], eq.astype(BF16)) + before
        keep = (s > thr) | ((s == thr) & (rank <= need))
        sc_ref[j] = jnp.where(keep, 0.0, NEG)
        return before + jnp.sum(eq, axis=0, keepdims=True)

    any_tied = jnp.sum(ones_where(tied)) > 0.0

    def mask_ranked_group(j0, count, before):
        for g in range(count):
            before = mask_ranked(j0 + g, before)
        return before

    @pl.when(any_tied)
    def _():
        n_all = i + 1
        before = lax.fori_loop(0, n_all // 4, lambda jj, b: mask_ranked_group(4 * jj, 4, b), jnp.zeros((1, TQ), F32))
        before = lax.cond(n_all % 4 >= 2, lambda b: mask_ranked_group((n_all // 4) * 4, 2, b), lambda b: b, before)

        @pl.when(n_all % 2 == 1)
        def _():
            mask_ranked(i, before)

    @pl.when(jnp.logical_not(any_tied))
    def _():
        lax.fori_loop(0, i + 1, mask_plain, 0)

    def attend(tiles, shifted):
        nk = len(tiles) * TK
        ck = jnp.concatenate([ckvn_ref[j] for j, _ in tiles], axis=0)
        ckt = jnp.concatenate([ckt_ref[j] for j, _ in tiles], axis=1)
        mask_bias = jnp.concatenate([sc_ref[j] for j, _ in tiles], axis=0)
        lg_all = _dot_nt(ck, qlat_ref[0].reshape(DSA_HEADS * TQ, LANES))
        for h in range(DSA_HEADS):
            bias = mask_bias
            if any(slot is not None for _, slot in tiles):
                bias = bias + jnp.concatenate(
                    [jnp.zeros((TK, TQ), F32) if slot is None else bnear_ref[h, slot] for _, slot in tiles], axis=0)
            lg3 = (lg_all[:, h * TQ:(h + 1) * TQ] + bias).reshape(nk // 8, 8, TQ)
            if shifted:
                m_old = m_ref[h]
                m_new = jnp.maximum(m_old, _sublane_all(jnp.max(lg3, axis=0), jnp.maximum))
                alpha = jnp.exp2(m_old - m_new)
                m_ref[h] = m_new
                p3 = jnp.exp2(lg3 - m_new[None])
                l_ref[h] = alpha * l_ref[h] + jnp.sum(p3, axis=0)
                pv = _dot(ckt, p3.reshape(nk, TQ).astype(BF16))
                acc3 = acc_ref[h].reshape(DSA_RKV // 8, 8, TQ) * alpha[None]
                acc_ref[h] = acc3.reshape(DSA_RKV, TQ) + pv
            else:
                p3 = jnp.exp2(lg3)
                l_ref[h] = l_ref[h] + jnp.sum(p3, axis=0)
                acc_ref[h] = acc_ref[h] + _dot(ckt, p3.reshape(nk, TQ).astype(BF16))

    n_far = jnp.maximum(i - 1, 0)

    def attend_all(shifted):
        m_ref[...] = jnp.full(m_ref.shape, NEG, F32)
        l_ref[...] = jnp.zeros(l_ref.shape, F32)
        acc_ref[...] = jnp.zeros(acc_ref.shape, F32)

        @pl.when(i >= 1)
        def _():
            attend([(i - 1, 0), (i, 1)], shifted)

        @pl.when(i == 0)
        def _():
            attend([(i, 1)], shifted)

        def far_quad(jj, carry):
            attend([(4 * jj + g, None) for g in range(4)], shifted)
            return carry

        lax.fori_loop(0, n_far // 4, far_quad, 0)

        @pl.when(n_far % 4 >= 2)
        def _():
            attend([((n_far // 4) * 4 + g, None) for g in range(2)], shifted)

        @pl.when(n_far % 2 == 1)
        def _():
            attend([(n_far - 1, None)], shifted)

    attend_all(False)

    bad = jnp.zeros((8, TQ), F32)
    for h in range(DSA_HEADS):
        l_tot = _sublane_all(l_ref[h], jnp.add)
        bad = bad + jnp.where((l_tot >= SUM_MIN) & (l_tot <= SUM_MAX), 0.0, 1.0)

    @pl.when(jnp.max(bad) > 0.0)
    def _():
        attend_all(True)


    for h in range(DSA_HEADS):
        l_tot = _sublane_all(l_ref[h], jnp.add)
        o = (acc_ref[h].reshape(DSA_RKV // 8, 8, TQ) / l_tot[None]).reshape(DSA_RKV, TQ)
        o_ref[:, h * LANES:(h + 1) * LANES] = o.T.astype(BF16)


def _t5_bucket(rel):
    nb = REL_BUCKETS // 2
    max_exact = nb // 2
    base = jnp.where(rel > 0, nb, 0)
    n = jnp.abs(rel)
    nf = jnp.maximum(n, 1).astype(jnp.float32)
    large = max_exact + (jnp.log(nf / max_exact) / math.log(REL_MAX_DIST / max_exact)
                         * (nb - max_exact)).astype(jnp.int32)
    large = jnp.minimum(large, nb - 1)
    return base + jnp.where(n < max_exact, n, large)


def _bias_tables(rel_bias):
    s = jnp.arange(TK, dtype=jnp.int32)[:, None]
    t = jnp.arange(TQ, dtype=jnp.int32)[None, :]
    rel = jnp.stack([s - TK - t, s - t])
    table = rel_bias.astype(F32)
    bucket = _t5_bucket(rel)
    far_bucket = _t5_bucket(jnp.full((), -REL_MAX_DIST, jnp.int32))
    near = jnp.zeros((DSA_HEADS,) + rel.shape, F32)
    far = jnp.zeros((DSA_HEADS,), F32)
    for b in range(REL_BUCKETS):
        near = near + jnp.where(bucket[None] == b, table[b][:, None, None, None], 0.0)
        far = far + jnp.where(far_bucket == b, table[b], 0.0)
    return (near - far[:, None, None, None]) * LOG2E


def _search_tables(nq, k_sel):
    nd = statistics.NormalDist()
    tab = np.zeros((nq, 2, TQ), np.float32)
    for i in range(nq):
        for c in range(TQ // CHUNK):
            n = i * TQ + (c + 1) * CHUNK
            z = nd.inv_cdf(1.0 - (k_sel - 0.5) / n) if n > k_sel else 0.0
            tab[i, 0, c * CHUNK:(c + 1) * CHUNK] = z
            tab[i, 1, c * CHUNK:(c + 1) * CHUNK] = n * nd.pdf(z)
    return jnp.asarray(tab)


def _dsa_attention(qlat, qidx, wt, ckvn, ckt, kidx, bnear, B, S):
    nq = S // TQ
    N = B * S
    k_sel = min(TOPK_MAX, S // 4)
    zq = _search_tables(nq, k_sel)
    tri = jnp.asarray(np.tril(np.ones((TK, TK), np.float32)), dtype=BF16)
    stacked = pl.BlockSpec((1, DSA_HEADS, TQ, LANES), lambda b, i: (b * nq + i, 0, 0, 0))
    keys = pl.BlockSpec((nq, TK, LANES), lambda b, i: (b, 0, 0))
    keys_t = pl.BlockSpec((nq, LANES, TK), lambda b, i: (b, 0, 0))
    return pl.pallas_call(
        functools.partial(_dsa_kernel, k_sel=k_sel),
        grid=(B, nq),
        in_specs=[stacked, stacked, pl.BlockSpec((IDX_HEADS, TQ), lambda b, i: (0, b * nq + i)),
                  keys, keys_t, keys,
                  pl.BlockSpec(bnear.shape, lambda b, i: (0, 0, 0, 0)),
                  pl.BlockSpec((1, 2, TQ), lambda b, i: (i, 0, 0)),
                  pl.BlockSpec((TK, TK), lambda b, i: (0, 0))],
        out_specs=pl.BlockSpec((TQ, DSA_HEADS * DSA_RKV), lambda b, i: (b * nq + i, 0)),
        out_shape=jax.ShapeDtypeStruct((N, DSA_HEADS * DSA_RKV), BF16),
        scratch_shapes=[pltpu.VMEM((nq, TK, TQ), F32),
                        pltpu.VMEM((DSA_HEADS, 8, TQ), F32),
                        pltpu.VMEM((DSA_HEADS, 8, TQ), F32),
                        pltpu.VMEM((DSA_HEADS, DSA_RKV, TQ), F32)],
        compiler_params=_params(("parallel", "arbitrary")),
        name="dsa_attention",
    )(qlat, qidx, wt, ckvn, ckt, kidx, bnear, zq, tri)


def _merge_kernel(x_ref, ret_ref, olat_ref, mq_ref, gates_ref, km_ref, vm_ref,
                  wret_ref, wdsa_ref, wmem_ref, wout_ref, o_ref):
    mq = mq_ref[...]
    km = km_ref[0]
    vm = vm_ref[0]
    pvs = []
    for h in range(MEM_HEADS):
        sl = slice(h * MEM_DH, (h + 1) * MEM_DH)
        lg = _dot_nt(mq[:, sl], km[:, sl]) * (MEM_DH ** -0.5)
        p = jnp.exp(lg - jnp.max(lg, axis=1, keepdims=True))
        p = p / jnp.sum(p, axis=1, keepdims=True)
        pvs.append(_dot(p.astype(BF16), vm[:, sl]).astype(BF16))
    mem_b = _dot(jnp.concatenate(pvs, axis=1), wmem_ref[...])
    ret_b = _dot(ret_ref[...], wret_ref[...])
    dsa_b = _dot(olat_ref[...], wdsa_ref[...])
    g = jax.nn.sigmoid(gates_ref[...].astype(F32))
    merged = (g[:, :D_MODEL] * ret_b + g[:, D_MODEL:2 * D_MODEL] * dsa_b + g[:, 2 * D_MODEL:] * mem_b)
    o_ref[...] = x_ref[...] + _dot(merged.astype(BF16), wout_ref[...])


def _merge(x2d, ret, olat, mq, gates, km, vm, wret, wdsa, wmem, wout, S):
    N = x2d.shape[0]
    per_b = S // TM_MERGE
    row = lambda i: (i, 0)
    const = lambda i: (0, 0)
    memb = lambda i: (i // per_b, 0, 0)
    return pl.pallas_call(
        _merge_kernel,
        grid=(N // TM_MERGE,),
        in_specs=[pl.BlockSpec((TM_MERGE, D_MODEL), row), pl.BlockSpec((TM_MERGE, RET_W), row),
                  pl.BlockSpec((TM_MERGE, DSA_HEADS * DSA_RKV), row), pl.BlockSpec((TM_MERGE, MEM_W), row),
                  pl.BlockSpec((TM_MERGE, 3 * D_MODEL), row),
                  pl.BlockSpec((1, MEM_LEN, MEM_W), memb), pl.BlockSpec((1, MEM_LEN, MEM_W), memb),
                  pl.BlockSpec(wret.shape, const), pl.BlockSpec(wdsa.shape, const),
                  pl.BlockSpec(wmem.shape, const), pl.BlockSpec(wout.shape, const)],
        out_specs=pl.BlockSpec((TM_MERGE, D_MODEL), row),
        out_shape=jax.ShapeDtypeStruct(x2d.shape, F32),
        compiler_params=_params(("parallel",)),
        name="merge_out",
    )(x2d, ret, olat, mq, gates, km, vm, wret, wdsa, wmem, wout)


def _mlp_kernel(x_ref, g_ref, w1_ref, w2_ref, gf_ref, o_ref, *, final):
    x = x_ref[...]
    h = _rms(x, g_ref[...]).astype(BF16)
    acc = jnp.zeros(x.shape, F32)
    for c in range(D_FF // D_MODEL):
        sl = slice(c * D_MODEL, (c + 1) * D_MODEL)
        a = jnp.maximum(_dot(h, w1_ref[:, sl]), 0.0)
        acc = acc + _dot((a * a).astype(BF16), w2_ref[sl, :])
    y = x + acc
    if final:
        y = _rms(y, gf_ref[...])
    o_ref[...] = y


def _mlp(x2d, g, w1, w2, gf, final):
    N = x2d.shape[0]
    row = lambda i: (i, 0)
    const = lambda i: (0, 0)
    return pl.pallas_call(
        functools.partial(_mlp_kernel, final=final),
        grid=(N // TM_MLP,),
        in_specs=[pl.BlockSpec((TM_MLP, D_MODEL), row), pl.BlockSpec((1, D_MODEL), const),
                  pl.BlockSpec(w1.shape, const), pl.BlockSpec(w2.shape, const),
                  pl.BlockSpec((1, D_MODEL), const)],
        out_specs=pl.BlockSpec((TM_MLP, D_MODEL), row),
        out_shape=jax.ShapeDtypeStruct(x2d.shape, F32),
        compiler_params=_params(("parallel",)),
        name="mlp",
    )(x2d, g, w1, w2, gf)


def _rope_tables(S):
    half = RET_DK // 2
    pos = jnp.arange(S, dtype=F32)
    freqs = ROPE_BASE ** (-jnp.arange(half, dtype=F32) / half)
    ang = pos[:, None] * freqs[None, :]
    return jnp.tile(jnp.cos(ang), (1, RET_HEADS)), jnp.tile(jnp.sin(ang), (1, RET_HEADS))


def _in_weights(wi, w_iq_l):
    half = RET_DK // 2
    def gather_halves(w):
        return jnp.transpose(w.reshape(D_MODEL, RET_HEADS, 2, half), (0, 2, 1, 3)).reshape(D_MODEL, RET_W)

    offs = np.concatenate([[0], np.cumsum(IN_SIZES)])
    seg = lambda k: wi[:, int(offs[k]):int(offs[k + 1])]
    wa = jnp.concatenate([gather_halves(seg(0)), gather_halves(seg(1)), seg(2), seg(3), seg(8), seg(9)],
                         axis=1).astype(BF16)
    pad = jnp.zeros((D_MODEL, LANES - IDX_HEADS), F32)
    widx = jnp.concatenate([seg(4), seg(5), jnp.tile(seg(6), (1, 4)), seg(7), pad], axis=1)
    wiq = jnp.tile(w_iq_l[:, :, None, :], (1, 1, 4, 1)).reshape(DSA_RQ, IDX_HEADS * LANES)
    return wa, widx.astype(BF16), wiq.astype(BF16)


def kernel(x, mem, norm1, w_in, q_norm, kv_norm, w_uq, w_iq, w_uk, w_uv, mem_norm, w_mem_kv,
           w_ret_o, w_dsa_o, w_mem_o, w_out, norm2, w_ff1, w_ff2, rel_bias, final_norm):
    B, S, D = x.shape
    depth = w_in.shape[0]
    assert D == D_MODEL and S % TM_MLP == 0 and S % TQ == 0 and TQ == TK and TM_IN == TQ
    cos_t, sin_t = _rope_tables(S)
    bnear = _bias_tables(rel_bias)
    wql_all, wdsa_all = _fold_weights(w_uq, w_uk, w_uv, w_dsa_o)
    x2d = x.reshape(B * S, D)
    row = lambda v: v.reshape(1, -1)
    for l in range(depth):
        wa, widx, wiq = _in_weights(w_in[l], w_iq[l])
        (rq, rk, rv, rg, mq, gates, qlat, qidx, wt, ckvn, ckt, kidx) = _in_proj(
            x2d, row(norm1[l]), cos_t, sin_t, wa, widx, row(q_norm[l]), row(kv_norm[l]), wql_all[l], wiq, S)
        ret = _retention(rq, rk, rv, rg, B, S)
        olat = _dsa_attention(qlat, qidx, wt, ckvn, ckt, kidx, bnear, B, S)
        km, vm = _mem_kv(mem, row(mem_norm[l]), w_mem_kv[l].astype(BF16))
        x2d = _merge(x2d, ret, olat, mq, gates, km, vm, w_ret_o[l].astype(BF16), wdsa_all[l],
                     w_mem_o[l].astype(BF16), w_out[l].astype(BF16), S)
        x2d = _mlp(x2d, row(norm2[l]), w_ff1[l].astype(BF16), w_ff2[l].astype(BF16), row(final_norm),
                   final=(l == depth - 1))
    return x2d.reshape(B, S, D)
```

```python
import functools
import math
import statistics

import numpy as np
import jax
import jax.numpy as jnp
from jax import lax
from jax.experimental import pallas as pl
from jax.experimental.pallas import tpu as pltpu

F32 = jnp.float32
BF16 = jnp.bfloat16

D_MODEL = 1024
CHUNK = 64
EPS = 1e-6
GN_EPS = 1e-5
RET_HEADS = 8
RET_DK = 64
RET_W = 512
ROPE_BASE = 10000.0
DSA_HEADS = 8
DSA_DH = 64
DSA_RQ = 256
DSA_RKV = 128
IDX_HEADS = 8
IDX_DIM = 32
TOPK_MAX = 256
MEM_LEN = 256
MEM_HEADS = 4
MEM_DH = 128
MEM_W = 512
REL_BUCKETS = 32
REL_MAX_DIST = 128
D_FF = 4096
IN_SIZES = (512, 512, 512, 512, DSA_RQ, DSA_RKV, IDX_DIM, IDX_HEADS, MEM_W, 3 * D_MODEL)

LANES = 128
IDX_IK = DSA_RQ + DSA_RKV
IDX_IW = IDX_IK + 4 * IDX_DIM
IDX_COLS = IDX_IW + LANES
NEG = -1e30
LOG2E = math.log2(math.e)
F32_MAX = float(np.finfo(np.float32).max)
VMEM_LIMIT = 56 * 1024 * 1024

TM_IN = 256
T_RET = 256
TQ = 256
TK = 256
SUM_MAX = 1e30
SUM_MIN = 2.0 ** -80
SEARCH_UNCHECKED = 8
SEARCH_SWITCH = 16
SEARCH_FEW = 4.0
SEARCH_CAP = SEARCH_SWITCH + 34
TM_MERGE = 512
TM_MLP = 512


def _dot(a, b):
    return jnp.dot(a, b, preferred_element_type=F32)


def _dot_nt(a, b):
    return lax.dot_general(a, b, (((1,), (1,)), ((), ())), preferred_element_type=F32)


def _split(a):
    hi = a.astype(BF16)
    lo = (a - hi.astype(F32)).astype(BF16)
    return hi, lo


def _rms(x, g):
    return x * lax.rsqrt(jnp.mean(x * x, axis=-1, keepdims=True) + EPS) * g


def _params(sem):
    return pltpu.CompilerParams(dimension_semantics=sem, vmem_limit_bytes=VMEM_LIMIT)


def _fold_kernel(uq_ref, uk_ref, uv_ref, wo_ref, wql_ref, wdsa_ref):
    uq_h, uq_l = _split(uq_ref[0, 0])
    uk_h, uk_l = _split(uk_ref[0, 0])
    ql = _dot_nt(uq_h, uk_h) + _dot_nt(uq_h, uk_l) + _dot_nt(uq_l, uk_h)
    wql_ref[0] = (ql * (DSA_DH ** -0.5 * LOG2E)).astype(BF16)
    uv_h, uv_l = _split(uv_ref[0, 0])
    wo_h, wo_l = _split(wo_ref[0, 0])
    wdsa_ref[0] = (_dot(uv_h, wo_h) + _dot(uv_h, wo_l) + _dot(uv_l, wo_h)).astype(BF16)


def _fold_weights(w_uq, w_uk, w_uv, w_dsa_o):
    L = w_uq.shape[0]
    uq = jnp.transpose(w_uq, (0, 2, 1, 3))
    uk = jnp.transpose(w_uk, (0, 2, 1, 3))
    uv = jnp.transpose(w_uv, (0, 2, 1, 3))
    wo = w_dsa_o.reshape(L, DSA_HEADS, DSA_DH, D_MODEL)
    return pl.pallas_call(
        _fold_kernel,
        grid=(L, DSA_HEADS),
        in_specs=[
            pl.BlockSpec((1, 1, DSA_RQ, DSA_DH), lambda l, h: (l, h, 0, 0)),
            pl.BlockSpec((1, 1, DSA_RKV, DSA_DH), lambda l, h: (l, h, 0, 0)),
            pl.BlockSpec((1, 1, DSA_RKV, DSA_DH), lambda l, h: (l, h, 0, 0)),
            pl.BlockSpec((1, 1, DSA_DH, D_MODEL), lambda l, h: (l, h, 0, 0)),
        ],
        out_specs=[
            pl.BlockSpec((1, DSA_RQ, DSA_RKV), lambda l, h: (l, 0, h)),
            pl.BlockSpec((1, DSA_RKV, D_MODEL), lambda l, h: (l, h, 0)),
        ],
        out_shape=[
            jax.ShapeDtypeStruct((L, DSA_RQ, DSA_HEADS * DSA_RKV), BF16),
            jax.ShapeDtypeStruct((L, DSA_HEADS * DSA_RKV, D_MODEL), BF16),
        ],
        compiler_params=_params(("parallel", "parallel")),
        name="fold_weights",
    )(uq, uk, uv, wo)


def _in_kernel(x_ref, g_ref, cos_ref, sin_ref, wa_ref, widx_ref, qn_ref, kvn_ref, wql_ref, wiq_ref,
               rq_ref, rk_ref, rv_ref, rg_ref, mq_ref, gates_ref,
               qlat_ref, qidx_ref, wt_ref, ckvn_ref, ckt_ref, kidx_ref):
    h = _rms(x_ref[...], g_ref[...])
    hb = h.astype(BF16)
    c = cos_ref[...]
    s = sin_ref[...]
    half = RET_W // 2

    def rope_store(ref, z, scale):
        x1 = z[:, :half]
        x2 = z[:, half:]
        ref[:, :half] = ((x1 * c - x2 * s) * scale).astype(BF16)
        ref[:, half:] = ((x2 * c + x1 * s) * scale).astype(BF16)

    rope_store(rq_ref, _dot(hb, wa_ref[:, 0:512]), 1.0)
    rope_store(rk_ref, _dot(hb, wa_ref[:, 512:1024]), RET_DK ** -0.5)
    rv_ref[...] = _dot(hb, wa_ref[:, 1024:1536]).astype(BF16)
    rg_ref[...] = _dot(hb, wa_ref[:, 1536:2048]).astype(BF16)
    mq_ref[...] = _dot(hb, wa_ref[:, 2048:2560]).astype(BF16)
    for j in range(6):
        gates_ref[:, j * 512:(j + 1) * 512] = _dot(hb, wa_ref[:, 2560 + j * 512:3072 + j * 512]).astype(BF16)
    _dsa_prep_tile(_dot(hb, widx_ref[...]), qn_ref, kvn_ref, wql_ref, wiq_ref,
                   qlat_ref, qidx_ref, wt_ref, ckvn_ref, ckt_ref, kidx_ref)


def _in_proj(x2d, g, cos_t, sin_t, wa, widx, qn, kvn, wql, wiq, S):
    N = x2d.shape[0]
    nt = N // TM_IN
    n_pos = S // TM_IN
    row = lambda i: (i, 0)
    const = lambda i: (0, 0)
    pos = lambda i: (i % n_pos, 0)
    tile3 = lambda i: (i, 0, 0)
    stacked = pl.BlockSpec((1, DSA_HEADS, TQ, LANES), lambda i: (i, 0, 0, 0))
    bf = lambda w: jax.ShapeDtypeStruct((N, w), BF16)
    return pl.pallas_call(
        _in_kernel,
        grid=(N // TM_IN,),
        in_specs=[
            pl.BlockSpec((TM_IN, D_MODEL), row),
            pl.BlockSpec((1, D_MODEL), const),
            pl.BlockSpec((TM_IN, RET_W // 2), pos),
            pl.BlockSpec((TM_IN, RET_W // 2), pos),
            pl.BlockSpec(wa.shape, const),
            pl.BlockSpec(widx.shape, const),
            pl.BlockSpec((1, DSA_RQ), const), pl.BlockSpec((1, DSA_RKV), const),
            pl.BlockSpec(wql.shape, const), pl.BlockSpec(wiq.shape, const),
        ],
        out_specs=[
            pl.BlockSpec((TM_IN, 512), row), pl.BlockSpec((TM_IN, 512), row),
            pl.BlockSpec((TM_IN, 512), row), pl.BlockSpec((TM_IN, 512), row),
            pl.BlockSpec((TM_IN, 512), row), pl.BlockSpec((TM_IN, 3 * D_MODEL), row),
            stacked, stacked, pl.BlockSpec((IDX_HEADS, TQ), lambda i: (0, i)),
            pl.BlockSpec((1, TQ, LANES), tile3), pl.BlockSpec((1, LANES, TQ), tile3),
            pl.BlockSpec((1, TQ, LANES), tile3),
        ],
        out_shape=[bf(512), bf(512), bf(512), bf(512), bf(512), bf(3 * D_MODEL),
                   jax.ShapeDtypeStruct((nt, DSA_HEADS, TQ, LANES), BF16),
                   jax.ShapeDtypeStruct((nt, DSA_HEADS, TQ, LANES), BF16),
                   jax.ShapeDtypeStruct((IDX_HEADS, N), F32),
                   jax.ShapeDtypeStruct((nt, TQ, LANES), BF16),
                   jax.ShapeDtypeStruct((nt, LANES, TQ), BF16),
                   jax.ShapeDtypeStruct((nt, TQ, LANES), BF16)],
        compiler_params=_params(("parallel",)),
        name="in_proj",
    )(x2d, g, cos_t, sin_t, wa, widx, qn, kvn, wql, wiq)


def _ret_kernel(rq_ref, rk_ref, rv_ref, rg_ref, dmat_ref, qd_ref, kd_ref, hq_ref, hv_ref,
                cdec_ref, bd_ref, p_ref, o_ref, s_ref):
    @pl.when(pl.program_id(1) == 0)
    def _():
        s_ref[...] = jnp.zeros_like(s_ref)

    q = rq_ref[...]
    k = rk_ref[...]
    v = rv_ref[...]
    def head_lane_tiles(a, h):
        c = h // (LANES // (RET_DK // 2))
        return jnp.concatenate([a[:, c * LANES:(c + 1) * LANES], a[:, (2 + c) * LANES:(3 + c) * LANES]], axis=1)

    o_cols = []
    for pair in range(RET_HEADS // 2):
        cols = slice(pair * LANES, (pair + 1) * LANES)
        o_pair = jnp.zeros((q.shape[0], LANES), F32)
        for h in (2 * pair, 2 * pair + 1):
            sc = _dot_nt(head_lane_tiles(q * hq_ref[h], h), head_lane_tiles(k, h)) * dmat_ref[h]
            o_pair = o_pair + _dot(sc.astype(BF16), v[:, cols]) * hv_ref[h][:, cols]
        o_cols.append(o_pair)
    o = jnp.concatenate(o_cols, axis=1)
    state = s_ref[...]
    qf = q.astype(F32) * qd_ref[...]
    o = o + _dot(qf.astype(BF16), state.astype(BF16))
    kf = k.astype(F32) * kd_ref[...]
    kv = _dot(kf.T.astype(BF16), v)
    s_ref[...] = state * cdec_ref[...] + kv * bd_ref[...]

    p = p_ref[...]

    oh, ol = _split(o)
    d = o - (_dot(oh, p) + _dot(ol, p))
    y = d * lax.rsqrt(_dot((d * d).astype(BF16), p) + GN_EPS)
    g = rg_ref[...].astype(F32)
    o_ref[...] = (g * jax.nn.sigmoid(g) * y).astype(BF16)


def _ret_consts():
    T = T_RET
    hh = np.arange(RET_HEADS, dtype=np.float64)
    log_g = np.log1p(-np.exp2(-5.0 - hh))
    t = np.arange(T)
    ct = t // CHUNK
    diff = (t[:, None] - t[None, :]).astype(np.float64)
    same = ct[:, None] == ct[None, :]
    past = ct[None, :] < ct[:, None]
    expo = np.where(same, np.abs(diff), diff)
    dmat = np.where((same | past)[None], np.exp(log_g[:, None, None] * expo[None]), 0.0)
    lane = np.arange(RET_W)
    hk = (lane % (RET_W // 2)) // (RET_DK // 2)
    hv = lane // RET_DK
    qd = np.exp(log_g[hk][None, :] * (t[:, None] + 1.0))
    kd = np.exp(log_g[hk][None, :] * (T - 1.0 - t[:, None]))
    hq_mask = (hk[None, :] == np.arange(RET_HEADS)[:, None]).astype(np.float32)[:, None, :]
    hv_mask = (hv[None, :] == np.arange(RET_HEADS)[:, None]).astype(np.float32)[:, None, :]
    bd = (hk[:, None] == hv[None, :]).astype(np.float32)
    cdec = np.broadcast_to(np.exp(log_g[hk] * T)[:, None], (RET_W, RET_W))
    pmat = (hv[:, None] == hv[None, :]).astype(np.float32) / RET_DK
    f = lambda a: jnp.asarray(np.asarray(a, dtype=np.float32))
    return (f(dmat), f(qd), f(kd), jnp.asarray(hq_mask, dtype=BF16), f(hv_mask), f(cdec), f(bd),
            jnp.asarray(pmat, dtype=BF16))


def _retention(rq, rk, rv, rg, B, S):
    consts = _ret_consts()
    nb = S // T_RET
    tok = lambda b, j: (b * nb + j, 0)
    full = lambda a: pl.BlockSpec(a.shape, lambda b, j: (0,) * a.ndim)
    return pl.pallas_call(
        _ret_kernel,
        grid=(B, nb),
        in_specs=[pl.BlockSpec((T_RET, RET_W), tok)] * 4 + [full(a) for a in consts],
        out_specs=pl.BlockSpec((T_RET, RET_W), tok),
        out_shape=jax.ShapeDtypeStruct(rq.shape, BF16),
        scratch_shapes=[pltpu.VMEM((RET_W, RET_W), F32)],
        compiler_params=_params(("parallel", "arbitrary")),
        name="retention",
    )(rq, rk, rv, rg, *consts)


def _memkv_kernel(mem_ref, g_ref, w_ref, k_ref, v_ref):
    mn = _rms(mem_ref[0], g_ref[...]).astype(BF16)
    kv = _dot(mn, w_ref[...])
    k_ref[0] = kv[:, :MEM_W].astype(BF16)
    v_ref[0] = kv[:, MEM_W:].astype(BF16)


def _mem_kv(mem, g, w):
    B = mem.shape[0]
    return pl.pallas_call(
        _memkv_kernel,
        grid=(B,),
        in_specs=[pl.BlockSpec((1, MEM_LEN, D_MODEL), lambda b: (b, 0, 0)),
                  pl.BlockSpec((1, D_MODEL), lambda b: (0, 0)),
                  pl.BlockSpec(w.shape, lambda b: (0, 0))],
        out_specs=[pl.BlockSpec((1, MEM_LEN, MEM_W), lambda b: (b, 0, 0))] * 2,
        out_shape=[jax.ShapeDtypeStruct((B, MEM_LEN, MEM_W), BF16)] * 2,
        compiler_params=_params(("parallel",)),
        name="mem_kv",
    )(mem, g, w)


def _dsa_prep_tile(z, qn_ref, kvn_ref, wql_ref, wiq_ref, qlat_ref, qidx_ref, wt_ref, ckvn_ref, ckt_ref, kidx_ref):
    cqn = _rms(z[:, :DSA_RQ], qn_ref[...])
    cb = cqn.astype(BF16)
    ql = _dot(cb, wql_ref[...])
    y = _dot(cb, wiq_ref[...])
    yh = y.astype(BF16).astype(F32)
    lane = lax.broadcasted_iota(jnp.int32, y.shape, 1)
    qsel = jnp.where((lane % LANES) < 2 * IDX_DIM, yh, y - yh).astype(BF16)
    for h in range(DSA_HEADS):
        qlat_ref[0, h] = ql[:, h * LANES:(h + 1) * LANES].astype(BF16)
        qidx_ref[0, h] = qsel[:, h * LANES:(h + 1) * LANES]
    ckn = _rms(z[:, DSA_RQ:DSA_RQ + DSA_RKV], kvn_ref[...])
    ckvn_ref[0] = ckn.astype(BF16)
    ckt_ref[0] = ckn.T.astype(BF16)
    ik = z[:, IDX_IK:IDX_IW]
    ikh = ik.astype(BF16).astype(F32)
    lane1 = lax.broadcasted_iota(jnp.int32, ik.shape, 1)
    kidx_ref[0] = jnp.where((lane1 // IDX_DIM) % 2 == 0, ikh, ik - ikh).astype(BF16)
    wfull = z[:, IDX_IW:IDX_COLS] * ((IDX_HEADS ** -0.5) * (IDX_DIM ** -0.5))
    wt_ref[...] = wfull.T[:IDX_HEADS, :]


def _key_to_f32(key):
    bits = jnp.where(key >= 0, key, key ^ jnp.int32(0x7FFFFFFF))
    return lax.bitcast_convert_type(bits, F32)


def _f32_to_key(x):
    bits = lax.bitcast_convert_type(x, jnp.int32)
    return jnp.where(bits >= 0, bits, bits ^ jnp.int32(0x7FFFFFFF))


def _sublane_all(a, op):
    for shift in (4, 2, 1):
        a = op(a, pltpu.roll(a, shift, 0))
    return a


def _dsa_kernel(qlat_ref, qidx_ref, wt_ref, ckvn_ref, ckt_ref, kidx_ref, bnear_ref, zq_ref, tri_ref, o_ref,
                sc_ref, m_ref, l_ref, acc_ref, *, k_sel):
    i = pl.program_id(1)
    wt = wt_ref[...]

    def score_tiles(js):
        keys = jnp.concatenate([kidx_ref[j] for j in js], axis=0)
        y_all = _dot_nt(keys, qidx_ref[0].reshape(IDX_HEADS * TQ, LANES))
        tot = jnp.zeros((len(js) * TK, TQ), F32)
        for h in range(IDX_HEADS):
            tot = tot + jnp.maximum(y_all[:, h * TQ:(h + 1) * TQ], 0.0) * wt[h:h + 1, :]
        return tot

    def colsum8(a):
        return jnp.sum(a.reshape(a.shape[0] // 8, 8, TQ), axis=0)

    def ones_where(cond_):
        return jnp.where(cond_, 1.0, 0.0)

    def stats(s_counted, s_summed):
        return (colsum8(s_summed), colsum8(s_summed * s_summed),
                colsum8(ones_where(s_counted >= 0.0)), colsum8(ones_where(s_counted > 0.0)))

    def add_stats(carry, s):
        return tuple(a + b for a, b in zip(carry, stats(s, s)))

    def fill(j0, count, carry):
        s = score_tiles([j0 + g for g in range(count)])
        for g in range(count):
            sc_ref[j0 + g] = s[g * TK:(g + 1) * TK]
        return add_stats(carry, s)

    zero8 = jnp.zeros((8, TQ), F32)
    carry = lax.fori_loop(0, i // 4, lambda jj, c: fill(4 * jj, 4, c), (zero8, zero8, zero8, zero8))
    carry = lax.cond(i % 4 >= 2, lambda c: fill((i // 4) * 4, 2, c), lambda c: c, carry)
    carry = lax.cond(i % 2 == 1, lambda c: fill(i - 1, 1, c), lambda c: c, carry)
    key_r = lax.broadcasted_iota(jnp.int32, (TK, TQ), 0)
    qry_c = lax.broadcasted_iota(jnp.int32, (TK, TQ), 1)
    admissible = key_r <= (qry_c // CHUNK) * CHUNK + (CHUNK - 1)
    s_diag = score_tiles([i])
    s_masked = jnp.where(admissible, s_diag, -jnp.inf)
    sc_ref[i] = s_masked
    carry = tuple(a + b for a, b in zip(carry, stats(s_masked, jnp.where(admissible, s_diag, 0.0))))
    s1, s2, c_ge0, c_gt0 = (_sublane_all(a, jnp.add)[0:1, :] for a in carry)

    def count_ge(thr, with_below=False):
        def tile(j):
            s = sc_ref[j]
            ge = s >= thr
            cnt = colsum8(ones_where(ge))
            if not with_below:
                return cnt, ninf8
            return cnt, jnp.max(jnp.where(ge, -jnp.inf, s).reshape(TK // 8, 8, TQ), axis=0)

        def pair(jj, c):
            ca, cb, ba, bb = c
            xa, ya = tile(2 * jj)
            xb, yb = tile(2 * jj + 1)
            return ca + xa, cb + xb, jnp.maximum(ba, ya), jnp.maximum(bb, yb)

        ninf8 = jnp.full((8, TQ), -jnp.inf, F32)
        ca, cb, ba, bb = lax.fori_loop(0, (i + 1) // 2, pair, (zero8, zero8, ninf8, ninf8))
        co, bo = lax.cond((i + 1) % 2 == 1, lambda: tile(i), lambda: (zero8, ninf8))
        count = _sublane_all(ca + cb + co, jnp.add)[0:1, :]
        below = _sublane_all(jnp.maximum(jnp.maximum(ba, bb), bo), jnp.maximum)[0:1, :]
        return count, below

    qc = lax.broadcasted_iota(jnp.int32, (1, TQ), 1)
    n_adm = (i * TQ + (qc // CHUNK + 1) * CHUNK).astype(F32)
    kf = jnp.float32(k_sel)
    key_min = jnp.int32(np.int32(np.array(-F32_MAX, np.float32).view(np.int32)) ^ np.int32(0x7FFFFFFF))
    key_max = jnp.int32(np.array(np.inf, np.float32).view(np.int32))
    inf = jnp.float32(np.inf)
    zq = zq_ref[0]
    mu = s1 / n_adm
    sd = jnp.sqrt(jnp.maximum(s2 / n_adm - mu * mu, 1e-30))
    dens = zq[1:2, :] / sd
    pos = c_gt0 >= kf
    tie0 = jnp.logical_and(jnp.logical_not(pos), c_ge0 >= kf)
    small = n_adm <= kf
    lo0 = jnp.where(pos | tie0, 0, key_min)
    hi0 = jnp.where(pos, key_max, jnp.where(tie0, 1, 0))
    lo0 = jnp.where(small, key_min, lo0)
    hi0 = jnp.where(small, key_min + 1, hi0)
    lf0 = jnp.where(pos | tie0, 0.0, -inf)
    hf0 = jnp.where(pos, inf, 0.0)
    cl0 = jnp.where(pos | tie0, c_ge0, n_adm)
    ch0 = jnp.where(pos, 0.0, jnp.where(tie0, c_gt0, c_ge0))
    t0 = mu + zq[0:1, :] * sd
    one = jnp.ones((1, TQ), F32)

    def mid_key(lo, hi):
        return (lo & hi) + ((lo ^ hi) >> 1)

    def unfinished(lo, hi, cl):
        return jnp.logical_not((cl == kf) | (mid_key(lo, hi) == lo))

    def cond(st):
        it, lo, hi, cl = st[0], st[1], st[2], st[5]
        return jnp.logical_and(it < SEARCH_CAP, jnp.sum(ones_where(unfinished(lo, hi, cl))) > 0.0)

    def step(st, with_below):
        it, lo, hi, lf, hf, cl, ch, t, last, wl, wh = st
        act = unfinished(lo, hi, cl)
        guided = (it < SEARCH_SWITCH).astype(jnp.int32)
        mid = mid_key(lo, hi)
        tk = mid + (_f32_to_key(t) - mid) * guided
        tk = jnp.minimum(jnp.maximum(tk, lo + 1), hi - 1)
        if with_below:
            tk = jnp.where(act & (kf - ch == 1.0) & (hi < key_max), hi, tk)
        tf = _key_to_f32(tk)
        c, below = count_ge(tf, with_below)
        ge = c >= kf
        up_lo = act & ge
        up_hi = act & jnp.logical_not(ge)
        lo = jnp.where(up_lo, tk, lo)
        lf = jnp.where(up_lo, tf, lf)
        cl = jnp.where(up_lo, c, cl)
        if with_below:
            kb = _f32_to_key(below)
            hi = jnp.where(up_hi, kb + 1, hi)
            hf = jnp.where(up_hi, _key_to_f32(kb + 1), hf)
            found = up_hi & (c == kf - 1.0)
            lo = jnp.where(found, kb, lo)
            cl = jnp.where(found, kf + 0.5, cl)
        else:
            hi = jnp.where(up_hi, tk, hi)
            hf = jnp.where(up_hi, tf, hf)
        ch = jnp.where(up_hi, c, ch)
        side = jnp.where(ge, 1.0, -1.0)
        same = side == last
        wh = jnp.where(ge, jnp.where(same, wh * 0.5, one), one)
        wl = jnp.where(ge, one, jnp.where(same, wl * 0.5, one))
        a = (cl - kf + 0.5) * wl
        b = (kf - 0.5 - ch) * wh
        frac = jnp.where(cl - ch <= SEARCH_FEW, 0.5, a / (a + b))
        t_bracket = lf + (hf - lf) * frac
        t_model = tf + 1.5 * (c - kf + jnp.where(ge, 0.5, -0.5)) / dens
        bracketed = (lf > -inf) & (hf < inf)
        t = jnp.where(bracketed, t_bracket, t_model)
        return it + 1, lo, hi, lf, hf, cl, ch, t, side, wl, wh

    st = (jnp.int32(0), lo0, hi0, lf0, hf0, cl0, ch0, t0, jnp.zeros((1, TQ), F32), one, one)
    st = lax.fori_loop(0, SEARCH_UNCHECKED, lambda _, s: step(s, False), st)
    st = step(st, True)
    st = lax.while_loop(cond, lambda s: step(s, True), st)
    lo, cl, ch = st[1], st[5], st[6]
    thr = _key_to_f32(lo)
    tied = (cl != kf) & jnp.logical_not(small)
    need = jnp.where(tied, kf - ch, jnp.float32(1e9))

    def mask_plain(j, carry_):
        sc_ref[j] = jnp.where(sc_ref[j] >= thr, 0.0, NEG)
        return carry_

    def mask_ranked(j, before):
        s = sc_ref[j]
        eq = ones_where(s == thr)
        rank = _dot(tri_ref[...], eq.astype(BF16)) + before
        keep = (s > thr) | ((s == thr) & (rank <= need))
        sc_ref[j] = jnp.where(keep, 0.0, NEG)
        return before + jnp.sum(eq, axis=0, keepdims=True)

    any_tied = jnp.sum(ones_where(tied)) > 0.0

    def mask_ranked_group(j0, count, before):
        for g in range(count):
            before = mask_ranked(j0 + g, before)
        return before

    @pl.when(any_tied)
    def _():
        n_all = i + 1
        before = lax.fori_loop(0, n_all // 4, lambda jj, b: mask_ranked_group(4 * jj, 4, b), jnp.zeros((1, TQ), F32))
        before = lax.cond(n_all % 4 >= 2, lambda b: mask_ranked_group((n_all // 4) * 4, 2, b), lambda b: b, before)

        @pl.when(n_all % 2 == 1)
        def _():
            mask_ranked(i, before)

    @pl.when(jnp.logical_not(any_tied))
    def _():
        lax.fori_loop(0, i + 1, mask_plain, 0)

    def attend(tiles, shifted):
        nk = len(tiles) * TK
        ck = jnp.concatenate([ckvn_ref[j] for j, _ in tiles], axis=0)
        ckt = jnp.concatenate([ckt_ref[j] for j, _ in tiles], axis=1)
        mask_bias = jnp.concatenate([sc_ref[j] for j, _ in tiles], axis=0)
        lg_all = _dot_nt(ck, qlat_ref[0].reshape(DSA_HEADS * TQ, LANES))
        for h in range(DSA_HEADS):
            bias = mask_bias
            if any(slot is not None for _, slot in tiles):
                bias = bias + jnp.concatenate(
                    [jnp.zeros((TK, TQ), F32) if slot is None else bnear_ref[h, slot] for _, slot in tiles], axis=0)
            lg3 = (lg_all[:, h * TQ:(h + 1) * TQ] + bias).reshape(nk // 8, 8, TQ)
            if shifted:
                m_old = m_ref[h]
                m_new = jnp.maximum(m_old, _sublane_all(jnp.max(lg3, axis=0), jnp.maximum))
                alpha = jnp.exp2(m_old - m_new)
                m_ref[h] = m_new
                p3 = jnp.exp2(lg3 - m_new[None])
                l_ref[h] = alpha * l_ref[h] + jnp.sum(p3, axis=0)
                pv = _dot(ckt, p3.reshape(nk, TQ).astype(BF16))
                acc3 = acc_ref[h].reshape(DSA_RKV // 8, 8, TQ) * alpha[None]
                acc_ref[h] = acc3.reshape(DSA_RKV, TQ) + pv
            else:
                p3 = jnp.exp2(lg3)
                l_ref[h] = l_ref[h] + jnp.sum(p3, axis=0)
                acc_ref[h] = acc_ref[h] + _dot(ckt, p3.reshape(nk, TQ).astype(BF16))

    n_far = jnp.maximum(i - 1, 0)

    def attend_all(shifted):
        m_ref[...] = jnp.full(m_ref.shape, NEG, F32)
        l_ref[...] = jnp.zeros(l_ref.shape, F32)
        acc_ref[...] = jnp.zeros(acc_ref.shape, F32)

        @pl.when(i >= 1)
        def _():
            attend([(i - 1, 0), (i, 1)], shifted)

        @pl.when(i == 0)
        def _():
            attend([(i, 1)], shifted)

        def far_quad(jj, carry):
            attend([(4 * jj + g, None) for g in range(4)], shifted)
            return carry

        lax.fori_loop(0, n_far // 4, far_quad, 0)

        @pl.when(n_far % 4 >= 2)
        def _():
            attend([((n_far // 4) * 4 + g, None) for g in range(2)], shifted)

        @pl.when(n_far % 2 == 1)
        def _():
            attend([(n_far - 1, None)], shifted)

    attend_all(False)

    bad = jnp.zeros((8, TQ), F32)
    for h in range(DSA_HEADS):
        l_tot = _sublane_all(l_ref[h], jnp.add)
        bad = bad + jnp.where((l_tot >= SUM_MIN) & (l_tot <= SUM_MAX), 0.0, 1.0)

    @pl.when(jnp.max(bad) > 0.0)
    def _():
        attend_all(True)


    for h in range(DSA_HEADS):
        l_tot = _sublane_all(l_ref[h], jnp.add)
        o = (acc_ref[h].reshape(DSA_RKV // 8, 8, TQ) / l_tot[None]).reshape(DSA_RKV, TQ)
        o_ref[:, h * LANES:(h + 1) * LANES] = o.T.astype(BF16)


def _t5_bucket(rel):
    nb = REL_BUCKETS // 2
    max_exact = nb // 2
    base = jnp.where(rel > 0, nb, 0)
    n = jnp.abs(rel)
    nf = jnp.maximum(n, 1).astype(jnp.float32)
    large = max_exact + (jnp.log(nf / max_exact) / math.log(REL_MAX_DIST / max_exact)
                         * (nb - max_exact)).astype(jnp.int32)
    large = jnp.minimum(large, nb - 1)
    return base + jnp.where(n < max_exact, n, large)


def _bias_tables(rel_bias):
    s = jnp.arange(TK, dtype=jnp.int32)[:, None]
    t = jnp.arange(TQ, dtype=jnp.int32)[None, :]
    rel = jnp.stack([s - TK - t, s - t])
    table = rel_bias.astype(F32)
    bucket = _t5_bucket(rel)
    far_bucket = _t5_bucket(jnp.full((), -REL_MAX_DIST, jnp.int32))
    near = jnp.zeros((DSA_HEADS,) + rel.shape, F32)
    far = jnp.zeros((DSA_HEADS,), F32)
    for b in range(REL_BUCKETS):
        near = near + jnp.where(bucket[None] == b, table[b][:, None, None, None], 0.0)
        far = far + jnp.where(far_bucket == b, table[b], 0.0)
    return (near - far[:, None, None, None]) * LOG2E


def _search_tables(nq, k_sel):
    nd = statistics.NormalDist()
    tab = np.zeros((nq, 2, TQ), np.float32)
    for i in range(nq):
        for c in range(TQ // CHUNK):
            n = i * TQ + (c + 1) * CHUNK
            z = nd.inv_cdf(1.0 - (k_sel - 0.5) / n) if n > k_sel else 0.0
            tab[i, 0, c * CHUNK:(c + 1) * CHUNK] = z
            tab[i, 1, c * CHUNK:(c + 1) * CHUNK] = n * nd.pdf(z)
    return jnp.asarray(tab)


def _dsa_attention(qlat, qidx, wt, ckvn, ckt, kidx, bnear, B, S):
    nq = S // TQ
    N = B * S
    k_sel = min(TOPK_MAX, S // 4)
    zq = _search_tables(nq, k_sel)
    tri = jnp.asarray(np.tril(np.ones((TK, TK), np.float32)), dtype=BF16)
    stacked = pl.BlockSpec((1, DSA_HEADS, TQ, LANES), lambda b, i: (b * nq + i, 0, 0, 0))
    keys = pl.BlockSpec((nq, TK, LANES), lambda b, i: (b, 0, 0))
    keys_t = pl.BlockSpec((nq, LANES, TK), lambda b, i: (b, 0, 0))
    return pl.pallas_call(
        functools.partial(_dsa_kernel, k_sel=k_sel),
        grid=(B, nq),
        in_specs=[stacked, stacked, pl.BlockSpec((IDX_HEADS, TQ), lambda b, i: (0, b * nq + i)),
                  keys, keys_t, keys,
                  pl.BlockSpec(bnear.shape, lambda b, i: (0, 0, 0, 0)),
                  pl.BlockSpec((1, 2, TQ), lambda b, i: (i, 0, 0)),
                  pl.BlockSpec((TK, TK), lambda b, i: (0, 0))],
        out_specs=pl.BlockSpec((TQ, DSA_HEADS * DSA_RKV), lambda b, i: (b * nq + i, 0)),
        out_shape=jax.ShapeDtypeStruct((N, DSA_HEADS * DSA_RKV), BF16),
        scratch_shapes=[pltpu.VMEM((nq, TK, TQ), F32),
                        pltpu.VMEM((DSA_HEADS, 8, TQ), F32),
                        pltpu.VMEM((DSA_HEADS, 8, TQ), F32),
                        pltpu.VMEM((DSA_HEADS, DSA_RKV, TQ), F32)],
        compiler_params=_params(("parallel", "arbitrary")),
        name="dsa_attention",
    )(qlat, qidx, wt, ckvn, ckt, kidx, bnear, zq, tri)


def _merge_kernel(x_ref, ret_ref, olat_ref, mq_ref, gates_ref, km_ref, vm_ref,
                  wret_ref, wdsa_ref, wmem_ref, wout_ref, o_ref):
    mq = mq_ref[...]
    km = km_ref[0]
    vm = vm_ref[0]
    pvs = []
    for h in range(MEM_HEADS):
        sl = slice(h * MEM_DH, (h + 1) * MEM_DH)
        lg = _dot_nt(mq[:, sl], km[:, sl]) * (MEM_DH ** -0.5)
        p = jnp.exp(lg - jnp.max(lg, axis=1, keepdims=True))
        pv = _dot(p.astype(BF16), vm[:, sl]) / jnp.sum(p, axis=1, keepdims=True)
        pvs.append(pv.astype(BF16))
    mem_b = _dot(jnp.concatenate(pvs, axis=1), wmem_ref[...])
    ret_b = _dot(ret_ref[...], wret_ref[...])
    dsa_b = _dot(olat_ref[...], wdsa_ref[...])
    g = jax.nn.sigmoid(gates_ref[...].astype(F32))
    merged = (g[:, :D_MODEL] * ret_b + g[:, D_MODEL:2 * D_MODEL] * dsa_b + g[:, 2 * D_MODEL:] * mem_b)
    o_ref[...] = x_ref[...] + _dot(merged.astype(BF16), wout_ref[...])


def _merge(x2d, ret, olat, mq, gates, km, vm, wret, wdsa, wmem, wout, S):
    N = x2d.shape[0]
    per_b = S // TM_MERGE
    row = lambda i: (i, 0)
    const = lambda i: (0, 0)
    memb = lambda i: (i // per_b, 0, 0)
    return pl.pallas_call(
        _merge_kernel,
        grid=(N // TM_MERGE,),
        in_specs=[pl.BlockSpec((TM_MERGE, D_MODEL), row), pl.BlockSpec((TM_MERGE, RET_W), row),
                  pl.BlockSpec((TM_MERGE, DSA_HEADS * DSA_RKV), row), pl.BlockSpec((TM_MERGE, MEM_W), row),
                  pl.BlockSpec((TM_MERGE, 3 * D_MODEL), row),
                  pl.BlockSpec((1, MEM_LEN, MEM_W), memb), pl.BlockSpec((1, MEM_LEN, MEM_W), memb),
                  pl.BlockSpec(wret.shape, const), pl.BlockSpec(wdsa.shape, const),
                  pl.BlockSpec(wmem.shape, const), pl.BlockSpec(wout.shape, const)],
        out_specs=pl.BlockSpec((TM_MERGE, D_MODEL), row),
        out_shape=jax.ShapeDtypeStruct(x2d.shape, F32),
        compiler_params=_params(("parallel",)),
        name="merge_out",
    )(x2d, ret, olat, mq, gates, km, vm, wret, wdsa, wmem, wout)


def _mlp_kernel(x_ref, g_ref, w1_ref, w2_ref, gf_ref, o_ref, *, final):
    x = x_ref[...]
    h = _rms(x, g_ref[...]).astype(BF16)
    acc = jnp.zeros(x.shape, F32)
    for c in range(D_FF // D_MODEL):
        sl = slice(c * D_MODEL, (c + 1) * D_MODEL)
        a = jnp.maximum(_dot(h, w1_ref[:, sl]), 0.0)
        acc = acc + _dot((a * a).astype(BF16), w2_ref[sl, :])
    y = x + acc
    if final:
        y = _rms(y, gf_ref[...])
    o_ref[...] = y


def _mlp(x2d, g, w1, w2, gf, final):
    N = x2d.shape[0]
    row = lambda i: (i, 0)
    const = lambda i: (0, 0)
    return pl.pallas_call(
        functools.partial(_mlp_kernel, final=final),
        grid=(N // TM_MLP,),
        in_specs=[pl.BlockSpec((TM_MLP, D_MODEL), row), pl.BlockSpec((1, D_MODEL), const),
                  pl.BlockSpec(w1.shape, const), pl.BlockSpec(w2.shape, const),
                  pl.BlockSpec((1, D_MODEL), const)],
        out_specs=pl.BlockSpec((TM_MLP, D_MODEL), row),
        out_shape=jax.ShapeDtypeStruct(x2d.shape, F32),
        compiler_params=_params(("parallel",)),
        name="mlp",
    )(x2d, g, w1, w2, gf)


def _rope_tables(S):
    half = RET_DK // 2
    pos = jnp.arange(S, dtype=F32)
    freqs = ROPE_BASE ** (-jnp.arange(half, dtype=F32) / half)
    ang = pos[:, None] * freqs[None, :]
    return jnp.tile(jnp.cos(ang), (1, RET_HEADS)), jnp.tile(jnp.sin(ang), (1, RET_HEADS))


def _in_weights(wi, w_iq_l):
    half = RET_DK // 2
    def gather_halves(w):
        return jnp.transpose(w.reshape(D_MODEL, RET_HEADS, 2, half), (0, 2, 1, 3)).reshape(D_MODEL, RET_W)

    offs = np.concatenate([[0], np.cumsum(IN_SIZES)])
    seg = lambda k: wi[:, int(offs[k]):int(offs[k + 1])]
    wa = jnp.concatenate([gather_halves(seg(0)), gather_halves(seg(1)), seg(2), seg(3), seg(8), seg(9)],
                         axis=1).astype(BF16)
    pad = jnp.zeros((D_MODEL, LANES - IDX_HEADS), F32)
    widx = jnp.concatenate([seg(4), seg(5), jnp.tile(seg(6), (1, 4)), seg(7), pad], axis=1)
    wiq = jnp.tile(w_iq_l[:, :, None, :], (1, 1, 4, 1)).reshape(DSA_RQ, IDX_HEADS * LANES)
    return wa, widx.astype(BF16), wiq.astype(BF16)


def kernel(x, mem, norm1, w_in, q_norm, kv_norm, w_uq, w_iq, w_uk, w_uv, mem_norm, w_mem_kv,
           w_ret_o, w_dsa_o, w_mem_o, w_out, norm2, w_ff1, w_ff2, rel_bias, final_norm):
    B, S, D = x.shape
    depth = w_in.shape[0]
    assert D == D_MODEL and S % TM_MLP == 0 and S % TQ == 0 and TQ == TK and TM_IN == TQ
    cos_t, sin_t = _rope_tables(S)
    bnear = _bias_tables(rel_bias)
    wql_all, wdsa_all = _fold_weights(w_uq, w_uk, w_uv, w_dsa_o)
    x2d = x.reshape(B * S, D)
    row = lambda v: v.reshape(1, -1)
    for l in range(depth):
        wa, widx, wiq = _in_weights(w_in[l], w_iq[l])
        (rq, rk, rv, rg, mq, gates, qlat, qidx, wt, ckvn, ckt, kidx) = _in_proj(
            x2d, row(norm1[l]), cos_t, sin_t, wa, widx, row(q_norm[l]), row(kv_norm[l]), wql_all[l], wiq, S)
        ret = _retention(rq, rk, rv, rg, B, S)
        olat = _dsa_attention(qlat, qidx, wt, ckvn, ckt, kidx, bnear, B, S)
        km, vm = _mem_kv(mem, row(mem_norm[l]), w_mem_kv[l].astype(BF16))
        x2d = _merge(x2d, ret, olat, mq, gates, km, vm, w_ret_o[l].astype(BF16), wdsa_all[l],
                     w_mem_o[l].astype(BF16), w_out[l].astype(BF16), S)
        x2d = _mlp(x2d, row(norm2[l]), w_ff1[l].astype(BF16), w_ff2[l].astype(BF16), row(final_norm),
                   final=(l == depth - 1))
    return x2d.reshape(B, S, D)
```

```python
import functools
import math
import statistics

import numpy as np
import jax
import jax.numpy as jnp
from jax import lax
from jax.experimental import pallas as pl
from jax.experimental.pallas import tpu as pltpu

F32 = jnp.float32
BF16 = jnp.bfloat16

D_MODEL = 1024
CHUNK = 64
EPS = 1e-6
GN_EPS = 1e-5
RET_HEADS = 8
RET_DK = 64
RET_W = 512
ROPE_BASE = 10000.0
DSA_HEADS = 8
DSA_DH = 64
DSA_RQ = 256
DSA_RKV = 128
IDX_HEADS = 8
IDX_DIM = 32
TOPK_MAX = 256
MEM_LEN = 256
MEM_HEADS = 4
MEM_DH = 128
MEM_W = 512
REL_BUCKETS = 32
REL_MAX_DIST = 128
D_FF = 4096
IN_SIZES = (512, 512, 512, 512, DSA_RQ, DSA_RKV, IDX_DIM, IDX_HEADS, MEM_W, 3 * D_MODEL)

LANES = 128
IDX_IK = DSA_RQ + DSA_RKV
IDX_IW = IDX_IK + 4 * IDX_DIM
IDX_COLS = IDX_IW + LANES
NEG = -1e30
LOG2E = math.log2(math.e)
F32_MAX = float(np.finfo(np.float32).max)
VMEM_LIMIT = 56 * 1024 * 1024

TM_IN = 256
T_RET = 256
TQ = 256
TK = 256
SUM_MAX = 1e30
SUM_MIN = 2.0 ** -80
SEARCH_UNCHECKED = 8
SEARCH_SWITCH = 16
SEARCH_FEW = 4.0
SEARCH_CAP = SEARCH_SWITCH + 34
TM_MERGE = 512
TM_MLP = 512


def _dot(a, b):
    return jnp.dot(a, b, preferred_element_type=F32)


def _dot_nt(a, b):
    return lax.dot_general(a, b, (((1,), (1,)), ((), ())), preferred_element_type=F32)


def _split(a):
    hi = a.astype(BF16)
    lo = (a - hi.astype(F32)).astype(BF16)
    return hi, lo


def _rms(x, g):
    return x * lax.rsqrt(jnp.mean(x * x, axis=-1, keepdims=True) + EPS) * g


def _params(sem):
    return pltpu.CompilerParams(dimension_semantics=sem, vmem_limit_bytes=VMEM_LIMIT)


def _fold_kernel(uq_ref, uk_ref, uv_ref, wo_ref, wql_ref, wdsa_ref):
    uq_h, uq_l = _split(uq_ref[0, 0])
    uk_h, uk_l = _split(uk_ref[0, 0])
    ql = _dot_nt(uq_h, uk_h) + _dot_nt(uq_h, uk_l) + _dot_nt(uq_l, uk_h)
    wql_ref[0] = (ql * (DSA_DH ** -0.5 * LOG2E)).astype(BF16)
    uv_h, uv_l = _split(uv_ref[0, 0])
    wo_h, wo_l = _split(wo_ref[0, 0])
    wdsa_ref[0] = (_dot(uv_h, wo_h) + _dot(uv_h, wo_l) + _dot(uv_l, wo_h)).astype(BF16)


def _fold_weights(w_uq, w_uk, w_uv, w_dsa_o):
    L = w_uq.shape[0]
    uq = jnp.transpose(w_uq, (0, 2, 1, 3))
    uk = jnp.transpose(w_uk, (0, 2, 1, 3))
    uv = jnp.transpose(w_uv, (0, 2, 1, 3))
    wo = w_dsa_o.reshape(L, DSA_HEADS, DSA_DH, D_MODEL)
    return pl.pallas_call(
        _fold_kernel,
        grid=(L, DSA_HEADS),
        in_specs=[
            pl.BlockSpec((1, 1, DSA_RQ, DSA_DH), lambda l, h: (l, h, 0, 0)),
            pl.BlockSpec((1, 1, DSA_RKV, DSA_DH), lambda l, h: (l, h, 0, 0)),
            pl.BlockSpec((1, 1, DSA_RKV, DSA_DH), lambda l, h: (l, h, 0, 0)),
            pl.BlockSpec((1, 1, DSA_DH, D_MODEL), lambda l, h: (l, h, 0, 0)),
        ],
        out_specs=[
            pl.BlockSpec((1, DSA_RQ, DSA_RKV), lambda l, h: (l, 0, h)),
            pl.BlockSpec((1, DSA_RKV, D_MODEL), lambda l, h: (l, h, 0)),
        ],
        out_shape=[
            jax.ShapeDtypeStruct((L, DSA_RQ, DSA_HEADS * DSA_RKV), BF16),
            jax.ShapeDtypeStruct((L, DSA_HEADS * DSA_RKV, D_MODEL), BF16),
        ],
        compiler_params=_params(("parallel", "parallel")),
        name="fold_weights",
    )(uq, uk, uv, wo)


def _in_kernel(x_ref, g_ref, cos_ref, sin_ref, wa_ref, widx_ref, qn_ref, kvn_ref, wql_ref, wiq_ref,
               rq_ref, rk_ref, rv_ref, rg_ref, mq_ref, gates_ref,
               qlat_ref, qidx_ref, wt_ref, ckvn_ref, ckt_ref, kidx_ref):
    h = _rms(x_ref[...], g_ref[...])
    hb = h.astype(BF16)
    c = cos_ref[...]
    s = sin_ref[...]
    half = RET_W // 2

    def rope_store(ref, z, scale):
        x1 = z[:, :half]
        x2 = z[:, half:]
        ref[:, :half] = ((x1 * c - x2 * s) * scale).astype(BF16)
        ref[:, half:] = ((x2 * c + x1 * s) * scale).astype(BF16)

    rope_store(rq_ref, _dot(hb, wa_ref[:, 0:512]), 1.0)
    rope_store(rk_ref, _dot(hb, wa_ref[:, 512:1024]), RET_DK ** -0.5)
    rv_ref[...] = _dot(hb, wa_ref[:, 1024:1536]).astype(BF16)
    rg_ref[...] = _dot(hb, wa_ref[:, 1536:2048]).astype(BF16)
    mq_ref[...] = _dot(hb, wa_ref[:, 2048:2560]).astype(BF16)
    for j in range(6):
        gates_ref[:, j * 512:(j + 1) * 512] = _dot(hb, wa_ref[:, 2560 + j * 512:3072 + j * 512]).astype(BF16)
    _dsa_prep_tile(_dot(hb, widx_ref[...]), qn_ref, kvn_ref, wql_ref, wiq_ref,
                   qlat_ref, qidx_ref, wt_ref, ckvn_ref, ckt_ref, kidx_ref)


def _in_proj(x2d, g, cos_t, sin_t, wa, widx, qn, kvn, wql, wiq, S):
    N = x2d.shape[0]
    nt = N // TM_IN
    n_pos = S // TM_IN
    row = lambda i: (i, 0)
    const = lambda i: (0, 0)
    pos = lambda i: (i % n_pos, 0)
    tile3 = lambda i: (i, 0, 0)
    stacked = pl.BlockSpec((1, DSA_HEADS, TQ, LANES), lambda i: (i, 0, 0, 0))
    bf = lambda w: jax.ShapeDtypeStruct((N, w), BF16)
    return pl.pallas_call(
        _in_kernel,
        grid=(N // TM_IN,),
        in_specs=[
            pl.BlockSpec((TM_IN, D_MODEL), row),
            pl.BlockSpec((1, D_MODEL), const),
            pl.BlockSpec((TM_IN, RET_W // 2), pos),
            pl.BlockSpec((TM_IN, RET_W // 2), pos),
            pl.BlockSpec(wa.shape, const),
            pl.BlockSpec(widx.shape, const),
            pl.BlockSpec((1, DSA_RQ), const), pl.BlockSpec((1, DSA_RKV), const),
            pl.BlockSpec(wql.shape, const), pl.BlockSpec(wiq.shape, const),
        ],
        out_specs=[
            pl.BlockSpec((TM_IN, 512), row), pl.BlockSpec((TM_IN, 512), row),
            pl.BlockSpec((TM_IN, 512), row), pl.BlockSpec((TM_IN, 512), row),
            pl.BlockSpec((TM_IN, 512), row), pl.BlockSpec((TM_IN, 3 * D_MODEL), row),
            stacked, stacked, pl.BlockSpec((IDX_HEADS, TQ), lambda i: (0, i)),
            pl.BlockSpec((1, TQ, LANES), tile3), pl.BlockSpec((1, LANES, TQ), tile3),
            pl.BlockSpec((1, TQ, LANES), tile3),
        ],
        out_shape=[bf(512), bf(512), bf(512), bf(512), bf(512), bf(3 * D_MODEL),
                   jax.ShapeDtypeStruct((nt, DSA_HEADS, TQ, LANES), BF16),
                   jax.ShapeDtypeStruct((nt, DSA_HEADS, TQ, LANES), BF16),
                   jax.ShapeDtypeStruct((IDX_HEADS, N), F32),
                   jax.ShapeDtypeStruct((nt, TQ, LANES), BF16),
                   jax.ShapeDtypeStruct((nt, LANES, TQ), BF16),
                   jax.ShapeDtypeStruct((nt, TQ, LANES), BF16)],
        compiler_params=_params(("parallel",)),
        name="in_proj",
    )(x2d, g, cos_t, sin_t, wa, widx, qn, kvn, wql, wiq)


def _ret_kernel(rq_ref, rk_ref, rv_ref, rg_ref, dmat_ref, qd_ref, kd_ref, hq_ref, hv_ref,
                cdec_ref, bd_ref, p_ref, o_ref, s_ref):
    @pl.when(pl.program_id(1) == 0)
    def _():
        s_ref[...] = jnp.zeros_like(s_ref)

    q = rq_ref[...]
    k = rk_ref[...]
    v = rv_ref[...]
    def head_lane_tiles(a, h):
        c = h // (LANES // (RET_DK // 2))
        return jnp.concatenate([a[:, c * LANES:(c + 1) * LANES], a[:, (2 + c) * LANES:(3 + c) * LANES]], axis=1)

    o_cols = []
    for pair in range(RET_HEADS // 2):
        cols = slice(pair * LANES, (pair + 1) * LANES)
        o_pair = jnp.zeros((q.shape[0], LANES), F32)
        for h in (2 * pair, 2 * pair + 1):
            sc = _dot_nt(head_lane_tiles(q * hq_ref[h], h), head_lane_tiles(k, h)) * dmat_ref[h]
            o_pair = o_pair + _dot(sc.astype(BF16), v[:, cols]) * hv_ref[h][:, cols]
        o_cols.append(o_pair)
    o = jnp.concatenate(o_cols, axis=1)
    state = s_ref[...]
    qf = q.astype(F32) * qd_ref[...]
    o = o + _dot(qf.astype(BF16), state.astype(BF16))
    kf = k.astype(F32) * kd_ref[...]
    kv = _dot(kf.T.astype(BF16), v)
    s_ref[...] = state * cdec_ref[...] + kv * bd_ref[...]

    p = p_ref[...]

    oh, ol = _split(o)
    d = o - (_dot(oh, p) + _dot(ol, p))
    y = d * lax.rsqrt(_dot((d * d).astype(BF16), p) + GN_EPS)
    g = rg_ref[...].astype(F32)
    o_ref[...] = (g * jax.nn.sigmoid(g) * y).astype(BF16)


def _ret_consts():
    T = T_RET
    hh = np.arange(RET_HEADS, dtype=np.float64)
    log_g = np.log1p(-np.exp2(-5.0 - hh))
    t = np.arange(T)
    ct = t // CHUNK
    diff = (t[:, None] - t[None, :]).astype(np.float64)
    same = ct[:, None] == ct[None, :]
    past = ct[None, :] < ct[:, None]
    expo = np.where(same, np.abs(diff), diff)
    dmat = np.where((same | past)[None], np.exp(log_g[:, None, None] * expo[None]), 0.0)
    lane = np.arange(RET_W)
    hk = (lane % (RET_W // 2)) // (RET_DK // 2)
    hv = lane // RET_DK
    qd = np.exp(log_g[hk][None, :] * (t[:, None] + 1.0))
    kd = np.exp(log_g[hk][None, :] * (T - 1.0 - t[:, None]))
    hq_mask = (hk[None, :] == np.arange(RET_HEADS)[:, None]).astype(np.float32)[:, None, :]
    hv_mask = (hv[None, :] == np.arange(RET_HEADS)[:, None]).astype(np.float32)[:, None, :]
    bd = (hk[:, None] == hv[None, :]).astype(np.float32)
    cdec = np.broadcast_to(np.exp(log_g[hk] * T)[:, None], (RET_W, RET_W))
    pmat = (hv[:, None] == hv[None, :]).astype(np.float32) / RET_DK
    f = lambda a: jnp.asarray(np.asarray(a, dtype=np.float32))
    return (f(dmat), f(qd), f(kd), jnp.asarray(hq_mask, dtype=BF16), f(hv_mask), f(cdec), f(bd),
            jnp.asarray(pmat, dtype=BF16))


def _retention(rq, rk, rv, rg, B, S):
    consts = _ret_consts()
    nb = S // T_RET
    tok = lambda b, j: (b * nb + j, 0)
    full = lambda a: pl.BlockSpec(a.shape, lambda b, j: (0,) * a.ndim)
    return pl.pallas_call(
        _ret_kernel,
        grid=(B, nb),
        in_specs=[pl.BlockSpec((T_RET, RET_W), tok)] * 4 + [full(a) for a in consts],
        out_specs=pl.BlockSpec((T_RET, RET_W), tok),
        out_shape=jax.ShapeDtypeStruct(rq.shape, BF16),
        scratch_shapes=[pltpu.VMEM((RET_W, RET_W), F32)],
        compiler_params=_params(("parallel", "arbitrary")),
        name="retention",
    )(rq, rk, rv, rg, *consts)


def _memkv_kernel(mem_ref, g_ref, w_ref, k_ref, v_ref):
    mn = _rms(mem_ref[0], g_ref[...]).astype(BF16)
    kv = _dot(mn, w_ref[...])
    k_ref[0] = kv[:, :MEM_W].astype(BF16)
    v_ref[0] = kv[:, MEM_W:].astype(BF16)


def _mem_kv(mem, g, w):
    B = mem.shape[0]
    return pl.pallas_call(
        _memkv_kernel,
        grid=(B,),
        in_specs=[pl.BlockSpec((1, MEM_LEN, D_MODEL), lambda b: (b, 0, 0)),
                  pl.BlockSpec((1, D_MODEL), lambda b: (0, 0)),
                  pl.BlockSpec(w.shape, lambda b: (0, 0))],
        out_specs=[pl.BlockSpec((1, MEM_LEN, MEM_W), lambda b: (b, 0, 0))] * 2,
        out_shape=[jax.ShapeDtypeStruct((B, MEM_LEN, MEM_W), BF16)] * 2,
        compiler_params=_params(("parallel",)),
        name="mem_kv",
    )(mem, g, w)


def _dsa_prep_tile(z, qn_ref, kvn_ref, wql_ref, wiq_ref, qlat_ref, qidx_ref, wt_ref, ckvn_ref, ckt_ref, kidx_ref):
    cqn = _rms(z[:, :DSA_RQ], qn_ref[...])
    cb = cqn.astype(BF16)
    ql = _dot(cb, wql_ref[...])
    y = _dot(cb, wiq_ref[...])
    yh = y.astype(BF16).astype(F32)
    lane = lax.broadcasted_iota(jnp.int32, y.shape, 1)
    qsel = jnp.where((lane % LANES) < 2 * IDX_DIM, yh, y - yh).astype(BF16)
    for h in range(DSA_HEADS):
        qlat_ref[0, h] = ql[:, h * LANES:(h + 1) * LANES].astype(BF16)
        qidx_ref[0, h] = qsel[:, h * LANES:(h + 1) * LANES]
    ckn = _rms(z[:, DSA_RQ:DSA_RQ + DSA_RKV], kvn_ref[...])
    ckvn_ref[0] = ckn.astype(BF16)
    ckt_ref[0] = ckn.T.astype(BF16)
    ik = z[:, IDX_IK:IDX_IW]
    ikh = ik.astype(BF16).astype(F32)
    lane1 = lax.broadcasted_iota(jnp.int32, ik.shape, 1)
    kidx_ref[0] = jnp.where((lane1 // IDX_DIM) % 2 == 0, ikh, ik - ikh).astype(BF16)
    wfull = z[:, IDX_IW:IDX_COLS] * ((IDX_HEADS ** -0.5) * (IDX_DIM ** -0.5))
    wt_ref[...] = wfull.T[:IDX_HEADS, :]


def _key_to_f32(key):
    bits = jnp.where(key >= 0, key, key ^ jnp.int32(0x7FFFFFFF))
    return lax.bitcast_convert_type(bits, F32)


def _f32_to_key(x):
    bits = lax.bitcast_convert_type(x, jnp.int32)
    return jnp.where(bits >= 0, bits, bits ^ jnp.int32(0x7FFFFFFF))


def _sublane_all(a, op):
    for shift in (4, 2, 1):
        a = op(a, pltpu.roll(a, shift, 0))
    return a


def _dsa_kernel(qlat_ref, qidx_ref, wt_ref, ckvn_ref, ckt_ref, kidx_ref, bnear_ref, zq_ref, tri_ref, o_ref,
                sc_ref, m_ref, l_ref, acc_ref, *, k_sel):
    i = pl.program_id(1)
    wt = wt_ref[...]

    def score_tiles(js):
        keys = jnp.concatenate([kidx_ref[j] for j in js], axis=0)
        y_all = _dot_nt(keys, qidx_ref[0].reshape(IDX_HEADS * TQ, LANES))
        tot = jnp.zeros((len(js) * TK, TQ), F32)
        for h in range(IDX_HEADS):
            tot = tot + jnp.maximum(y_all[:, h * TQ:(h + 1) * TQ], 0.0) * wt[h:h + 1, :]
        return tot

    def colsum8(a):
        return jnp.sum(a.reshape(a.shape[0] // 8, 8, TQ), axis=0)

    def ones_where(cond_):
        return jnp.where(cond_, 1.0, 0.0)

    def stats(s_counted, s_summed):
        return (colsum8(s_summed), colsum8(s_summed * s_summed),
                colsum8(ones_where(s_counted >= 0.0)), colsum8(ones_where(s_counted > 0.0)))

    def add_stats(carry, s):
        return tuple(a + b for a, b in zip(carry, stats(s, s)))

    def fill(j0, count, carry):
        s = score_tiles([j0 + g for g in range(count)])
        for g in range(count):
            sc_ref[j0 + g] = s[g * TK:(g + 1) * TK]
        return add_stats(carry, s)

    zero8 = jnp.zeros((8, TQ), F32)
    carry = lax.fori_loop(0, i // 4, lambda jj, c: fill(4 * jj, 4, c), (zero8, zero8, zero8, zero8))
    carry = lax.cond(i % 4 >= 2, lambda c: fill((i // 4) * 4, 2, c), lambda c: c, carry)
    carry = lax.cond(i % 2 == 1, lambda c: fill(i - 1, 1, c), lambda c: c, carry)
    key_r = lax.broadcasted_iota(jnp.int32, (TK, TQ), 0)
    qry_c = lax.broadcasted_iota(jnp.int32, (TK, TQ), 1)
    admissible = key_r <= (qry_c // CHUNK) * CHUNK + (CHUNK - 1)
    s_diag = score_tiles([i])
    s_masked = jnp.where(admissible, s_diag, -jnp.inf)
    sc_ref[i] = s_masked
    carry = tuple(a + b for a, b in zip(carry, stats(s_masked, jnp.where(admissible, s_diag, 0.0))))
    s1, s2, c_ge0, c_gt0 = (_sublane_all(a, jnp.add)[0:1, :] for a in carry)

    def count_ge(thr, with_below=False):
        def tile(j):
            s = sc_ref[j]
            ge = s >= thr
            cnt = colsum8(ones_where(ge))
            if not with_below:
                return cnt, ninf8
            return cnt, jnp.max(jnp.where(ge, -jnp.inf, s).reshape(TK // 8, 8, TQ), axis=0)

        def pair(jj, c):
            ca, cb, ba, bb = c
            xa, ya = tile(2 * jj)
            xb, yb = tile(2 * jj + 1)
            return ca + xa, cb + xb, jnp.maximum(ba, ya), jnp.maximum(bb, yb)

        ninf8 = jnp.full((8, TQ), -jnp.inf, F32)
        ca, cb, ba, bb = lax.fori_loop(0, (i + 1) // 2, pair, (zero8, zero8, ninf8, ninf8))
        co, bo = lax.cond((i + 1) % 2 == 1, lambda: tile(i), lambda: (zero8, ninf8))
        count = _sublane_all(ca + cb + co, jnp.add)[0:1, :]
        below = _sublane_all(jnp.maximum(jnp.maximum(ba, bb), bo), jnp.maximum)[0:1, :]
        return count, below

    qc = lax.broadcasted_iota(jnp.int32, (1, TQ), 1)
    n_adm = (i * TQ + (qc // CHUNK + 1) * CHUNK).astype(F32)
    kf = jnp.float32(k_sel)
    key_min = jnp.int32(np.int32(np.array(-F32_MAX, np.float32).view(np.int32)) ^ np.int32(0x7FFFFFFF))
    key_max = jnp.int32(np.array(np.inf, np.float32).view(np.int32))
    inf = jnp.float32(np.inf)
    zq = zq_ref[0]
    mu = s1 / n_adm
    sd = jnp.sqrt(jnp.maximum(s2 / n_adm - mu * mu, 1e-30))
    dens = zq[1:2, :] / sd
    pos = c_gt0 >= kf
    tie0 = jnp.logical_and(jnp.logical_not(pos), c_ge0 >= kf)
    small = n_adm <= kf
    lo0 = jnp.where(pos | tie0, 0, key_min)
    hi0 = jnp.where(pos, key_max, jnp.where(tie0, 1, 0))
    lo0 = jnp.where(small, key_min, lo0)
    hi0 = jnp.where(small, key_min + 1, hi0)
    lf0 = jnp.where(pos | tie0, 0.0, -inf)
    hf0 = jnp.where(pos, inf, 0.0)
    cl0 = jnp.where(pos | tie0, c_ge0, n_adm)
    ch0 = jnp.where(pos, 0.0, jnp.where(tie0, c_gt0, c_ge0))
    t0 = mu + zq[0:1, :] * sd
    one = jnp.ones((1, TQ), F32)

    def mid_key(lo, hi):
        return (lo & hi) + ((lo ^ hi) >> 1)

    def unfinished(lo, hi, cl):
        return jnp.logical_not((cl == kf) | (mid_key(lo, hi) == lo))

    def cond(st):
        it, lo, hi, cl = st[0], st[1], st[2], st[5]
        return jnp.logical_and(it < SEARCH_CAP, jnp.sum(ones_where(unfinished(lo, hi, cl))) > 0.0)

    def step(st, with_below):
        it, lo, hi, lf, hf, cl, ch, t, last, wl, wh = st
        act = unfinished(lo, hi, cl)
        guided = (it < SEARCH_SWITCH).astype(jnp.int32)
        mid = mid_key(lo, hi)
        tk = mid + (_f32_to_key(t) - mid) * guided
        tk = jnp.minimum(jnp.maximum(tk, lo + 1), hi - 1)
        if with_below:
            tk = jnp.where(act & (kf - ch == 1.0) & (hi < key_max), hi, tk)
        tf = _key_to_f32(tk)
        c, below = count_ge(tf, with_below)
        ge = c >= kf
        up_lo = act & ge
        up_hi = act & jnp.logical_not(ge)
        lo = jnp.where(up_lo, tk, lo)
        lf = jnp.where(up_lo, tf, lf)
        cl = jnp.where(up_lo, c, cl)
        if with_below:
            kb = _f32_to_key(below)
            hi = jnp.where(up_hi, kb + 1, hi)
            hf = jnp.where(up_hi, _key_to_f32(kb + 1), hf)
            found = up_hi & (c == kf - 1.0)
            lo = jnp.where(found, kb, lo)
            cl = jnp.where(found, kf + 0.5, cl)
        else:
            hi = jnp.where(up_hi, tk, hi)
            hf = jnp.where(up_hi, tf, hf)
        ch = jnp.where(up_hi, c, ch)
        side = jnp.where(ge, 1.0, -1.0)
        same = side == last
        wh = jnp.where(ge, jnp.where(same, wh * 0.5, one), one)
        wl = jnp.where(ge, one, jnp.where(same, wl * 0.5, one))
        a = (cl - kf + 0.5) * wl
        b = (kf - 0.5 - ch) * wh
        frac = jnp.where(cl - ch <= SEARCH_FEW, 0.5, a / (a + b))
        t_bracket = lf + (hf - lf) * frac
        t_model = tf + 1.5 * (c - kf + jnp.where(ge, 0.5, -0.5)) / dens
        bracketed = (lf > -inf) & (hf < inf)
        t = jnp.where(bracketed, t_bracket, t_model)
        return it + 1, lo, hi, lf, hf, cl, ch, t, side, wl, wh

    st = (jnp.int32(0), lo0, hi0, lf0, hf0, cl0, ch0, t0, jnp.zeros((1, TQ), F32), one, one)
    st = lax.fori_loop(0, SEARCH_UNCHECKED, lambda _, s: step(s, False), st)
    st = step(st, True)
    st = lax.while_loop(cond, lambda s: step(s, True), st)
    lo, cl, ch = st[1], st[5], st[6]
    thr = _key_to_f32(lo)
    tied = (cl != kf) & jnp.logical_not(small)
    need = jnp.where(tied, kf - ch, jnp.float32(1e9))

    def mask_plain(j, carry_):
        sc_ref[j] = jnp.where(sc_ref[j] >= thr, 0.0, NEG)
        return carry_

    def mask_ranked(j, before):
        s = sc_ref[j]
        eq = ones_where(s == thr)
        rank = _dot(tri_ref[...], eq.astype(BF16)) + before
        keep = (s > thr) | ((s == thr) & (rank <= need))
        sc_ref[j] = jnp.where(keep, 0.0, NEG)
        return before + jnp.sum(eq, axis=0, keepdims=True)

    any_tied = jnp.sum(ones_where(tied)) > 0.0

    def mask_ranked_group(j0, count, before):
        for g in range(count):
            before = mask_ranked(j0 + g, before)
        return before

    @pl.when(any_tied)
    def _():
        n_all = i + 1
        before = lax.fori_loop(0, n_all // 4, lambda jj, b: mask_ranked_group(4 * jj, 4, b), jnp.zeros((1, TQ), F32))
        before = lax.cond(n_all % 4 >= 2, lambda b: mask_ranked_group((n_all // 4) * 4, 2, b), lambda b: b, before)

        @pl.when(n_all % 2 == 1)
        def _():
            mask_ranked(i, before)

    @pl.when(jnp.logical_not(any_tied))
    def _():
        lax.fori_loop(0, i + 1, mask_plain, 0)

    def attend(tiles, shifted):
        nk = len(tiles) * TK
        ck = jnp.concatenate([ckvn_ref[j] for j, _ in tiles], axis=0)
        ckt = jnp.concatenate([ckt_ref[j] for j, _ in tiles], axis=1)
        mask_bias = jnp.concatenate([sc_ref[j] for j, _ in tiles], axis=0)
        lg_all = _dot_nt(ck, qlat_ref[0].reshape(DSA_HEADS * TQ, LANES))
        for h in range(DSA_HEADS):
            bias = mask_bias
            if any(slot is not None for _, slot in tiles):
                bias = bias + jnp.concatenate(
                    [jnp.zeros((TK, TQ), F32) if slot is None else bnear_ref[h, slot] for _, slot in tiles], axis=0)
            lg3 = (lg_all[:, h * TQ:(h + 1) * TQ] + bias).reshape(nk // 8, 8, TQ)
            if shifted:
                m_old = m_ref[h]
                m_new = jnp.maximum(m_old, _sublane_all(jnp.max(lg3, axis=0), jnp.maximum))
                alpha = jnp.exp2(m_old - m_new)
                m_ref[h] = m_new
                p3 = jnp.exp2(lg3 - m_new[None])
                l_ref[h] = alpha * l_ref[h] + jnp.sum(p3, axis=0)
                pv = _dot(ckt, p3.reshape(nk, TQ).astype(BF16))
                acc3 = acc_ref[h].reshape(DSA_RKV // 8, 8, TQ) * alpha[None]
                acc_ref[h] = acc3.reshape(DSA_RKV, TQ) + pv
            else:
                p3 = jnp.exp2(lg3)
                l_ref[h] = l_ref[h] + jnp.sum(p3, axis=0)
                acc_ref[h] = acc_ref[h] + _dot(ckt, p3.reshape(nk, TQ).astype(BF16))

    n_far = jnp.maximum(i - 1, 0)

    def attend_all(shifted):
        m_ref[...] = jnp.full(m_ref.shape, NEG, F32)
        l_ref[...] = jnp.zeros(l_ref.shape, F32)
        acc_ref[...] = jnp.zeros(acc_ref.shape, F32)

        @pl.when(i >= 1)
        def _():
            attend([(i - 1, 0), (i, 1)], shifted)

        @pl.when(i == 0)
        def _():
            attend([(i, 1)], shifted)

        def far_quad(jj, carry):
            attend([(4 * jj + g, None) for g in range(4)], shifted)
            return carry

        lax.fori_loop(0, n_far // 4, far_quad, 0)

        @pl.when(n_far % 4 >= 2)
        def _():
            attend([((n_far // 4) * 4 + g, None) for g in range(2)], shifted)

        @pl.when(n_far % 2 == 1)
        def _():
            attend([(n_far - 1, None)], shifted)

    attend_all(False)

    bad = jnp.zeros((8, TQ), F32)
    for h in range(DSA_HEADS):
        l_tot = _sublane_all(l_ref[h], jnp.add)
        bad = bad + jnp.where((l_tot >= SUM_MIN) & (l_tot <= SUM_MAX), 0.0, 1.0)

    @pl.when(jnp.max(bad) > 0.0)
    def _():
        attend_all(True)


    for h in range(DSA_HEADS):
        l_tot = _sublane_all(l_ref[h], jnp.add)
        o = (acc_ref[h].reshape(DSA_RKV // 8, 8, TQ) / l_tot[None]).reshape(DSA_RKV, TQ)
        o_ref[:, h * LANES:(h + 1) * LANES] = o.T.astype(BF16)


def _t5_bucket(rel):
    nb = REL_BUCKETS // 2
    max_exact = nb // 2
    base = jnp.where(rel > 0, nb, 0)
    n = jnp.abs(rel)
    nf = jnp.maximum(n, 1).astype(jnp.float32)
    large = max_exact + (jnp.log(nf / max_exact) / math.log(REL_MAX_DIST / max_exact)
                         * (nb - max_exact)).astype(jnp.int32)
    large = jnp.minimum(large, nb - 1)
    return base + jnp.where(n < max_exact, n, large)


def _bias_tables(rel_bias):
    s = jnp.arange(TK, dtype=jnp.int32)[:, None]
    t = jnp.arange(TQ, dtype=jnp.int32)[None, :]
    rel = jnp.stack([s - TK - t, s - t])
    table = rel_bias.astype(F32)
    bucket = _t5_bucket(rel)
    far_bucket = _t5_bucket(jnp.full((), -REL_MAX_DIST, jnp.int32))
    near = jnp.zeros((DSA_HEADS,) + rel.shape, F32)
    far = jnp.zeros((DSA_HEADS,), F32)
    for b in range(REL_BUCKETS):
        near = near + jnp.where(bucket[None] == b, table[b][:, None, None, None], 0.0)
        far = far + jnp.where(far_bucket == b, table[b], 0.0)
    return (near - far[:, None, None, None]) * LOG2E


def _search_tables(nq, k_sel):
    nd = statistics.NormalDist()
    tab = np.zeros((nq, 2, TQ), np.float32)
    for i in range(nq):
        for c in range(TQ // CHUNK):
            n = i * TQ + (c + 1) * CHUNK
            z = nd.inv_cdf(1.0 - (k_sel - 0.5) / n) if n > k_sel else 0.0
            tab[i, 0, c * CHUNK:(c + 1) * CHUNK] = z
            tab[i, 1, c * CHUNK:(c + 1) * CHUNK] = n * nd.pdf(z)
    return jnp.asarray(tab)


def _dsa_attention(qlat, qidx, wt, ckvn, ckt, kidx, bnear, B, S):
    nq = S // TQ
    N = B * S
    k_sel = min(TOPK_MAX, S // 4)
    zq = _search_tables(nq, k_sel)
    tri = jnp.asarray(np.tril(np.ones((TK, TK), np.float32)), dtype=BF16)
    stacked = pl.BlockSpec((1, DSA_HEADS, TQ, LANES), lambda b, i: (b * nq + i, 0, 0, 0))
    keys = pl.BlockSpec((nq, TK, LANES), lambda b, i: (b, 0, 0))
    keys_t = pl.BlockSpec((nq, LANES, TK), lambda b, i: (b, 0, 0))
    return pl.pallas_call(
        functools.partial(_dsa_kernel, k_sel=k_sel),
        grid=(B, nq),
        in_specs=[stacked, stacked, pl.BlockSpec((IDX_HEADS, TQ), lambda b, i: (0, b * nq + i)),
                  keys, keys_t, keys,
                  pl.BlockSpec(bnear.shape, lambda b, i: (0, 0, 0, 0)),
                  pl.BlockSpec((1, 2, TQ), lambda b, i: (i, 0, 0)),
                  pl.BlockSpec((TK, TK), lambda b, i: (0, 0))],
        out_specs=pl.BlockSpec((TQ, DSA_HEADS * DSA_RKV), lambda b, i: (b * nq + i, 0)),
        out_shape=jax.ShapeDtypeStruct((N, DSA_HEADS * DSA_RKV), BF16),
        scratch_shapes=[pltpu.VMEM((nq, TK, TQ), F32),
                        pltpu.VMEM((DSA_HEADS, 8, TQ), F32),
                        pltpu.VMEM((DSA_HEADS, 8, TQ), F32),
                        pltpu.VMEM((DSA_HEADS, DSA_RKV, TQ), F32)],
        compiler_params=_params(("parallel", "arbitrary")),
        name="dsa_attention",
    )(qlat, qidx, wt, ckvn, ckt, kidx, bnear, zq, tri)


def _merge_kernel(x_ref, ret_ref, olat_ref, mq_ref, gates_ref, km_ref, vm_ref,
                  wret_ref, wdsa_ref, wmem_ref, wout_ref, o_ref):
    mq = mq_ref[...]
    km = km_ref[0]
    vm = vm_ref[0]
    pvs = []
    for h in range(MEM_HEADS):
        sl = slice(h * MEM_DH, (h + 1) * MEM_DH)
        lg = _dot_nt(mq[:, sl], km[:, sl])
        p = jnp.exp(lg - jnp.max(lg, axis=1, keepdims=True))
        pv = _dot(p.astype(BF16), vm[:, sl]) / jnp.sum(p, axis=1, keepdims=True)
        pvs.append(pv.astype(BF16))
    mem_b = _dot(jnp.concatenate(pvs, axis=1), wmem_ref[...])
    ret_b = _dot(ret_ref[...], wret_ref[...])
    dsa_b = _dot(olat_ref[...], wdsa_ref[...])
    g = jax.nn.sigmoid(gates_ref[...].astype(F32))
    merged = (g[:, :D_MODEL] * ret_b + g[:, D_MODEL:2 * D_MODEL] * dsa_b + g[:, 2 * D_MODEL:] * mem_b)
    o_ref[...] = x_ref[...] + _dot(merged.astype(BF16), wout_ref[...])


def _merge(x2d, ret, olat, mq, gates, km, vm, wret, wdsa, wmem, wout, S):
    N = x2d.shape[0]
    per_b = S // TM_MERGE
    row = lambda i: (i, 0)
    const = lambda i: (0, 0)
    memb = lambda i: (i // per_b, 0, 0)
    return pl.pallas_call(
        _merge_kernel,
        grid=(N // TM_MERGE,),
        in_specs=[pl.BlockSpec((TM_MERGE, D_MODEL), row), pl.BlockSpec((TM_MERGE, RET_W), row),
                  pl.BlockSpec((TM_MERGE, DSA_HEADS * DSA_RKV), row), pl.BlockSpec((TM_MERGE, MEM_W), row),
                  pl.BlockSpec((TM_MERGE, 3 * D_MODEL), row),
                  pl.BlockSpec((1, MEM_LEN, MEM_W), memb), pl.BlockSpec((1, MEM_LEN, MEM_W), memb),
                  pl.BlockSpec(wret.shape, const), pl.BlockSpec(wdsa.shape, const),
                  pl.BlockSpec(wmem.shape, const), pl.BlockSpec(wout.shape, const)],
        out_specs=pl.BlockSpec((TM_MERGE, D_MODEL), row),
        out_shape=jax.ShapeDtypeStruct(x2d.shape, F32),
        compiler_params=_params(("parallel",)),
        name="merge_out",
    )(x2d, ret, olat, mq, gates, km, vm, wret, wdsa, wmem, wout)


def _mlp_kernel(x_ref, g_ref, w1_ref, w2_ref, gf_ref, o_ref, *, final):
    x = x_ref[...]
    h = _rms(x, g_ref[...]).astype(BF16)
    acc = jnp.zeros(x.shape, F32)
    for c in range(D_FF // D_MODEL):
        sl = slice(c * D_MODEL, (c + 1) * D_MODEL)
        a = jnp.maximum(_dot(h, w1_ref[:, sl]), 0.0)
        acc = acc + _dot((a * a).astype(BF16), w2_ref[sl, :])
    y = x + acc
    if final:
        y = _rms(y, gf_ref[...])
    o_ref[...] = y


def _mlp(x2d, g, w1, w2, gf, final):
    N = x2d.shape[0]
    row = lambda i: (i, 0)
    const = lambda i: (0, 0)
    return pl.pallas_call(
        functools.partial(_mlp_kernel, final=final),
        grid=(N // TM_MLP,),
        in_specs=[pl.BlockSpec((TM_MLP, D_MODEL), row), pl.BlockSpec((1, D_MODEL), const),
                  pl.BlockSpec(w1.shape, const), pl.BlockSpec(w2.shape, const),
                  pl.BlockSpec((1, D_MODEL), const)],
        out_specs=pl.BlockSpec((TM_MLP, D_MODEL), row),
        out_shape=jax.ShapeDtypeStruct(x2d.shape, F32),
        compiler_params=_params(("parallel",)),
        name="mlp",
    )(x2d, g, w1, w2, gf)


def _rope_tables(S):
    half = RET_DK // 2
    pos = jnp.arange(S, dtype=F32)
    freqs = ROPE_BASE ** (-jnp.arange(half, dtype=F32) / half)
    ang = pos[:, None] * freqs[None, :]
    return jnp.tile(jnp.cos(ang), (1, RET_HEADS)), jnp.tile(jnp.sin(ang), (1, RET_HEADS))


def _in_weights(wi, w_iq_l):
    half = RET_DK // 2
    def gather_halves(w):
        return jnp.transpose(w.reshape(D_MODEL, RET_HEADS, 2, half), (0, 2, 1, 3)).reshape(D_MODEL, RET_W)

    offs = np.concatenate([[0], np.cumsum(IN_SIZES)])
    seg = lambda k: wi[:, int(offs[k]):int(offs[k + 1])]
    wa = jnp.concatenate([gather_halves(seg(0)), gather_halves(seg(1)), seg(2), seg(3),
                          seg(8) * (MEM_DH ** -0.5), seg(9)],
                         axis=1).astype(BF16)
    pad = jnp.zeros((D_MODEL, LANES - IDX_HEADS), F32)
    widx = jnp.concatenate([seg(4), seg(5), jnp.tile(seg(6), (1, 4)), seg(7), pad], axis=1)
    wiq = jnp.tile(w_iq_l[:, :, None, :], (1, 1, 4, 1)).reshape(DSA_RQ, IDX_HEADS * LANES)
    return wa, widx.astype(BF16), wiq.astype(BF16)


def kernel(x, mem, norm1, w_in, q_norm, kv_norm, w_uq, w_iq, w_uk, w_uv, mem_norm, w_mem_kv,
           w_ret_o, w_dsa_o, w_mem_o, w_out, norm2, w_ff1, w_ff2, rel_bias, final_norm):
    B, S, D = x.shape
    depth = w_in.shape[0]
    assert D == D_MODEL and S % TM_MLP == 0 and S % TQ == 0 and TQ == TK and TM_IN == TQ
    cos_t, sin_t = _rope_tables(S)
    bnear = _bias_tables(rel_bias)
    wql_all, wdsa_all = _fold_weights(w_uq, w_uk, w_uv, w_dsa_o)
    x2d = x.reshape(B * S, D)
    row = lambda v: v.reshape(1, -1)
    for l in range(depth):
        wa, widx, wiq = _in_weights(w_in[l], w_iq[l])
        (rq, rk, rv, rg, mq, gates, qlat, qidx, wt, ckvn, ckt, kidx) = _in_proj(
            x2d, row(norm1[l]), cos_t, sin_t, wa, widx, row(q_norm[l]), row(kv_norm[l]), wql_all[l], wiq, S)
        ret = _retention(rq, rk, rv, rg, B, S)
        olat = _dsa_attention(qlat, qidx, wt, ckvn, ckt, kidx, bnear, B, S)
        km, vm = _mem_kv(mem, row(mem_norm[l]), w_mem_kv[l].astype(BF16))
        x2d = _merge(x2d, ret, olat, mq, gates, km, vm, w_ret_o[l].astype(BF16), wdsa_all[l],
                     w_mem_o[l].astype(BF16), w_out[l].astype(BF16), S)
        x2d = _mlp(x2d, row(norm2[l]), w_ff1[l].astype(BF16), w_ff2[l].astype(BF16), row(final_norm),
                   final=(l == depth - 1))
    return x2d.reshape(B, S, D)
```
